```python
import jax
import jax.numpy as jnp
from jax import lax
import numpy as np

D_MODEL = 2048
BATCH = 1
SEQ = 8192
DEPTH = 4

GRID_W = 64
CTX_LEN = 256
EPS = 1e-6

GROUP_W = D_MODEL // 4
D_MIX = 4 * GROUP_W

MLA_HEADS = 4
MLA_NOPE = 128
MLA_ROPE = 64
MLA_V = 128
MLA_Q_LORA = 384
MLA_KV_LORA = 256
ROPE_BASE = 10000.0
Q_BLOCK = 128

RWKV_HEADS = 8
RWKV_HEAD = GROUP_W // RWKV_HEADS
RWKV_W_LORA = 64
RWKV_A_LORA = 64
RWKV_G_LORA = 128
RWKV_LN_EPS = 64e-5

GLA_HEADS = 4
GLA_DK = 64
GLA_DV = GROUP_W // GLA_HEADS
GLA_GATE_RANK = 16
GLA_GATE_NORM = 16.0

HGRN_HEADS = 4
HGRN_EXPAND = GROUP_W // HGRN_HEADS
HGRN_DV = GROUP_W // HGRN_HEADS

CHUNK = 64

D_FF = 5632
N_EXPERTS = 8
TOP_K = 2

MLA_COLS = MLA_Q_LORA + MLA_KV_LORA + MLA_ROPE
RWKV_COLS = 3 * GROUP_W + 2 * RWKV_W_LORA + 2 * RWKV_A_LORA + RWKV_G_LORA
GLA_KD = GLA_HEADS * GLA_DK
GLA_COLS = 2 * GLA_KD + GROUP_W + 2 * GLA_GATE_RANK + GROUP_W
HGRN_COLS = 5 * GROUP_W
N_IN = MLA_COLS + RWKV_COLS + GLA_COLS + HGRN_COLS
IN_SPLITS = (MLA_COLS, MLA_COLS + RWKV_COLS, MLA_COLS + RWKV_COLS + GLA_COLS)
RWKV_SPLITS = (GROUP_W, 2 * GROUP_W, 3 * GROUP_W, 3 * GROUP_W + 2 * RWKV_W_LORA,
               3 * GROUP_W + 2 * RWKV_W_LORA + 2 * RWKV_A_LORA)
GLA_SPLITS = (GLA_KD, 2 * GLA_KD, 2 * GLA_KD + GROUP_W, 2 * GLA_KD + GROUP_W + 2 * GLA_GATE_RANK)

kernel_name = 'hybrid_flow_backbone'


def rmsnorm(x, g, eps=EPS):
    xf = x.astype(jnp.float32)
    y = xf * lax.rsqrt(jnp.mean(xf * xf, axis=-1, keepdims=True) + eps)
    return (y * g.astype(jnp.float32)).astype(x.dtype)


def modulate(x, g, shift, scale):
    return rmsnorm(x, g) * (1 + scale) + shift


def flip_parts(z, m):
    return jnp.concatenate([jnp.flip(z[:, :m], axis=1), jnp.flip(z[:, m:], axis=1)], axis=1)


def run_direction(scan_fn, seq, m, reverse):
    if reverse:
        return flip_parts(scan_fn(*[flip_parts(t, m) for t in seq]), m)
    return scan_fn(*seq)


def axial_rope_tables(n):
    rows = n // GRID_W
    row = jnp.repeat(jnp.arange(rows, dtype=jnp.float32), GRID_W)
    col = jnp.tile(jnp.arange(GRID_W, dtype=jnp.float32), rows)
    n_freq = MLA_ROPE // 4
    freqs = ROPE_BASE ** (-jnp.arange(n_freq, dtype=jnp.float32) / n_freq)
    ang = jnp.stack([row[:, None] * freqs, col[:, None] * freqs], axis=1)
    return jnp.cos(ang), jnp.sin(ang)


def apply_axial_rope(x, cos, sin):
    xs = x.reshape(x.shape[:-1] + (2, 2, MLA_ROPE // 4))
    x1, x2 = xs[..., 0, :], xs[..., 1, :]
    out = jnp.stack([x1 * cos - x2 * sin, x1 * sin + x2 * cos], axis=-2)
    return out.reshape(x.shape).astype(x.dtype)


def mla_mixer(p, m, q_norm, w_q_up, kv_norm, w_kv_up):
    b, L, _ = p.shape
    n = L - m
    cq, ckv, k_rope = jnp.split(p, (MLA_Q_LORA, MLA_Q_LORA + MLA_KV_LORA), axis=-1)
    q = (rmsnorm(cq, q_norm) @ w_q_up).reshape(b, L, MLA_HEADS, MLA_NOPE + MLA_ROPE)
    kv = (rmsnorm(ckv, kv_norm) @ w_kv_up).reshape(b, L, MLA_HEADS, MLA_NOPE + MLA_V)
    k_nope, v = jnp.split(kv, (MLA_NOPE,), axis=-1)
    cos, sin = axial_rope_tables(n)
    q_nope, q_rope = jnp.split(q, (MLA_NOPE,), axis=-1)
    q_rope = jnp.concatenate([q_rope[:, :m], apply_axial_rope(q_rope[:, m:], cos[:, None], sin[:, None])], axis=1)
    k_rope = jnp.concatenate([k_rope[:, :m], apply_axial_rope(k_rope[:, m:], cos, sin)], axis=1)
    q = jnp.concatenate([q_nope, q_rope], axis=-1)
    k = jnp.concatenate([k_nope, jnp.broadcast_to(k_rope[:, :, None, :], (b, L, MLA_HEADS, MLA_ROPE))], axis=-1)
    scale = (MLA_NOPE + MLA_ROPE) ** -0.5

    def attend(qb, kb, vb):
        s = jnp.einsum('bqhd,bkhd->bhqk', qb, kb).astype(jnp.float32) * scale
        w = jax.nn.softmax(s, axis=-1).astype(vb.dtype)
        return jnp.einsum('bhqk,bkhd->bqhd', w, vb)

    o_ctx = attend(q[:, :m], k[:, :m], v[:, :m])
    q_blocks = q[:, m:].reshape(b, n // Q_BLOCK, Q_BLOCK, MLA_HEADS, MLA_NOPE + MLA_ROPE).swapaxes(0, 1)
    o_lat = lax.map(lambda qb: attend(qb, k, v), q_blocks)
    o_lat = o_lat.swapaxes(0, 1).reshape(b, n, MLA_HEADS, MLA_V)
    return jnp.concatenate([o_ctx, o_lat], axis=1).reshape(b, L, MLA_HEADS * MLA_V)


def shift_mix(z, m, mu_prev, mu_next):
    def one(zp):
        zprev = jnp.pad(zp[:, :-1], ((0, 0), (1, 0), (0, 0)))
        znext = jnp.pad(zp[:, 1:], ((0, 0), (0, 1), (0, 0)))
        return zp + mu_prev * (zprev - zp) + mu_next * (znext - zp)
    return jnp.concatenate([one(z[:, :m]), one(z[:, m:])], axis=1)


def rwkv7_scan(r, w, k, v, kk, a):
    b, L, h, d = r.shape

    def step(S, inp):
        r_t, w_t, k_t, v_t, kk_t, a_t = inp
        sa = jnp.einsum('bhvk,bhk->bhv', S, -kk_t)
        S = S * w_t[:, :, None, :] + sa[..., None] * (kk_t * a_t)[:, :, None, :] + v_t[..., None] * k_t[:, :, None, :]
        return S, jnp.einsum('bhvk,bhk->bhv', S, r_t)

    xs = tuple(jnp.moveaxis(t, 1, 0) for t in (r, w, k, v, kk, a))
    _, o = lax.scan(step, jnp.zeros((b, h, d, d), jnp.float32), xs)
    return jnp.moveaxis(o, 0, 1)


def rwkv7_mixer(p, m, mu, w0, w_up, a0, a_up, k_k, k_a, u, g_up, ln_g, ln_b):
    b, L, _ = p.shape
    p = shift_mix(p, m, mu[0], mu[1])
    r, k, v, w_dn, a_dn, g_dn = jnp.split(p, RWKV_SPLITS, axis=-1)
    heads = lambda t: t.reshape(b, L, RWKV_HEADS, RWKV_HEAD).astype(jnp.float32)
    r_h, v_h = heads(r), heads(v)
    outs, bonus = [], []
    for d in range(2):
        w = w0[d] + jnp.tanh(w_dn[..., d * RWKV_W_LORA:(d + 1) * RWKV_W_LORA]) @ w_up[d]
        decay = jnp.exp(-jnp.exp(-jax.nn.softplus(-w.astype(jnp.float32)) - 0.5))
        a = jax.nn.sigmoid(a0[d] + a_dn[..., d * RWKV_A_LORA:(d + 1) * RWKV_A_LORA] @ a_up[d])
        kk = heads(k * k_k[d])
        kk = kk * lax.rsqrt(jnp.maximum(jnp.sum(kk * kk, axis=-1, keepdims=True), 1e-24))
        k_mod = heads(k * (1 + (a - 1) * k_a[d]))
        seq = (r_h, heads(decay), k_mod, v_h, kk, heads(a))
        outs.append(run_direction(rwkv7_scan, seq, m, reverse=(d == 1)))
        u_h = u[d].reshape(RWKV_HEADS, RWKV_HEAD)
        bonus.append(jnp.sum(r_h * k_mod * u_h, axis=-1, keepdims=True) * v_h)
    o = outs[0] + outs[1]
    mean = jnp.mean(o, axis=-1, keepdims=True)
    var = jnp.mean(jnp.square(o - mean), axis=-1, keepdims=True)
    o = ((o - mean) * lax.rsqrt(var + RWKV_LN_EPS)).reshape(b, L, GROUP_W) * ln_g + ln_b
    o = o + (bonus[0] + bonus[1]).reshape(b, L, GROUP_W)
    g = jax.nn.sigmoid(g_dn) @ g_up
    return (o * g).astype(p.dtype)


def chunked_gated_linear_attention(q, k, v, log_g):
    b, L, h, dk = q.shape
    dv = v.shape[-1]
    nc = L // CHUNK

    def chunks(t):
        return t.astype(jnp.float32).reshape(b, nc, CHUNK, h, t.shape[-1]).transpose(1, 0, 3, 2, 4)

    mask = jnp.tril(jnp.ones((CHUNK, CHUNK), dtype=bool))[:, :, None]

    def step(S, inp):
        qc, kc, vc, gc = inp
        bcum = jnp.cumsum(gc, axis=2)
        inter = jnp.einsum('bhcd,bhde->bhce', qc * jnp.exp(bcum), S)
        diff = bcum[:, :, :, None, :] - bcum[:, :, None, :, :]
        decay = jnp.where(mask, jnp.exp(jnp.where(mask, diff, 0.0)), 0.0)
        att = jnp.einsum('bhid,bhjd,bhijd->bhij', qc, kc, decay)
        out = inter + jnp.einsum('bhij,bhje->bhie', att, vc)
        b_last = bcum[:, :, -1:, :]
        S = jnp.exp(b_last[:, :, 0, :])[..., None] * S + jnp.einsum('bhjd,bhje->bhde', kc * jnp.exp(b_last - bcum), vc)
        return S, out

    S0 = jnp.zeros((b, h, dk, dv), jnp.float32)
    _, o = lax.scan(step, S0, (chunks(q), chunks(k), chunks(v), chunks(log_g)))
    return o.transpose(1, 0, 3, 2, 4).reshape(b, L, h, dv)


def gla_mixer(p, m, a_up, a_bias, g_norm):
    b, L, _ = p.shape
    q, k, v, a_dn, r = jnp.split(p, GLA_SPLITS, axis=-1)
    heads = lambda t, d: t.reshape(b, L, GLA_HEADS, d).astype(jnp.float32)
    qh, kh, vh = heads(q, GLA_DK) * GLA_DK ** -0.5, heads(k, GLA_DK), heads(v, GLA_DV)
    outs = []
    for d in range(2):
        logit = a_dn[..., d * GLA_GATE_RANK:(d + 1) * GLA_GATE_RANK] @ a_up[d] + a_bias[d]
        log_a = jax.nn.log_sigmoid(logit.astype(jnp.float32)) / GLA_GATE_NORM
        seq = (qh, kh, vh, heads(log_a, GLA_DK))
        outs.append(run_direction(chunked_gated_linear_attention, seq, m, reverse=(d == 1)))
    o = rmsnorm(outs[0] + outs[1], g_norm).reshape(b, L, GROUP_W)
    return (o * jax.nn.silu(r)).astype(p.dtype)


def hgrn2_mixer(p, m, lb, g_norm):
    b, L, _ = p.shape
    q, f_fwd, f_bwd, i, g = jnp.split(p, 5, axis=-1)
    heads = lambda t, d: t.reshape(b, L, HGRN_HEADS, d).astype(jnp.float32)
    qh, ih = heads(jax.nn.silu(q), HGRN_EXPAND), heads(i, HGRN_DV)
    outs = []
    for d, f_raw in enumerate((f_fwd, f_bwd)):
        f_raw = f_raw.astype(jnp.float32)
        lb_d = lb[d].astype(jnp.float32)
        log_f = jnp.logaddexp(jnp.log(lb_d), jnp.log1p(-lb_d) + jax.nn.log_sigmoid(f_raw))
        k = (1.0 - lb_d) * jax.nn.sigmoid(-f_raw)
        seq = (qh, heads(k, HGRN_EXPAND), ih, heads(log_f, HGRN_EXPAND))
        outs.append(run_direction(chunked_gated_linear_attention, seq, m, reverse=(d == 1)))
    o = rmsnorm(outs[0] + outs[1], g_norm).reshape(b, L, GROUP_W)
    return (o * jax.nn.silu(g)).astype(p.dtype)


def swiglu(h, w1, w3, w2):
    return (jax.nn.silu(h @ w1) * (h @ w3)) @ w2


def moe_swiglu(h, router, w1, w3, w2):
    logits = (h @ router).astype(jnp.float32)
    top_vals, top_idx = lax.top_k(logits, TOP_K)
    weights = jax.nn.softmax(top_vals, axis=-1)
    gates = jnp.sum(jax.nn.one_hot(top_idx, N_EXPERTS, dtype=jnp.float32) * weights[..., None], axis=-2).astype(h.dtype)
    out = jnp.zeros_like(h)
    for e in range(N_EXPERTS):
        out = out + gates[..., e:e + 1] * swiglu(h, w1[e], w3[e], w2[e])
    return out


def channel_mixer(l, h, ffn_w1, ffn_w3, ffn_w2, moe_router, moe_w1, moe_w3, moe_w2):
    j = l // 2
    if l % 2 == 0:
        return swiglu(h, ffn_w1[j], ffn_w3[j], ffn_w2[j])
    return moe_swiglu(h, moe_router[j], moe_w1[j], moe_w3[j], moe_w2[j])


def setup_inputs(seed: int = 0) -> dict:
    key = jax.random.key(seed)
    ks = iter(jax.random.split(key, 48))
    D, G = D_MODEL, GROUP_W
    n_dense, n_moe = (DEPTH + 1) // 2, DEPTH // 2

    def nrm(shape, scale):
        return jax.random.normal(next(ks), shape, jnp.float32) * scale

    return {
        'x': nrm((BATCH, SEQ, D), 1.0),
        'c': nrm((BATCH, D), 1.0),
        'ctx': nrm((BATCH, CTX_LEN, D), 1.0),
        'c_ctx': nrm((D,), 1.0),
        'norm1_g': 1.0 + nrm((DEPTH, D), 0.02),
        'norm2_g': 1.0 + nrm((DEPTH, D), 0.02),
        'w_mod': nrm((DEPTH, D, 6 * D), 0.5 * D ** -0.5),
        'b_mod': nrm((DEPTH, 6 * D), 0.02),
        'w_in': nrm((DEPTH, D, N_IN), D ** -0.5),
        'w_out': nrm((DEPTH, D_MIX, D), D_MIX ** -0.5),
        'mla_q_norm': 1.0 + nrm((DEPTH, MLA_Q_LORA), 0.02),
        'mla_w_q_up': nrm((DEPTH, MLA_Q_LORA, MLA_HEADS * (MLA_NOPE + MLA_ROPE)), MLA_Q_LORA ** -0.5),
        'mla_kv_norm': 1.0 + nrm((DEPTH, MLA_KV_LORA), 0.02),
        'mla_w_kv_up': nrm((DEPTH, MLA_KV_LORA, MLA_HEADS * (MLA_NOPE + MLA_V)), MLA_KV_LORA ** -0.5),
        'rwkv_mu': jax.random.uniform(next(ks), (DEPTH, 2, RWKV_COLS), jnp.float32, 0.0, 0.5),
        'rwkv_w0': nrm((DEPTH, 2, G), 0.5),
        'rwkv_w_up': nrm((DEPTH, 2, RWKV_W_LORA, G), 0.1),
        'rwkv_a0': nrm((DEPTH, 2, G), 0.5),
        'rwkv_a_up': nrm((DEPTH, 2, RWKV_A_LORA, G), RWKV_A_LORA ** -0.5),
        'rwkv_k_k': 0.85 + nrm((DEPTH, 2, G), 0.05),
        'rwkv_k_a': 1.0 + nrm((DEPTH, 2, G), 0.05),
        'rwkv_u': nrm((DEPTH, 2, G), 0.1),
        'rwkv_g_up': nrm((DEPTH, RWKV_G_LORA, G), RWKV_G_LORA ** -0.5),
        'rwkv_ln_g': 1.0 + nrm((DEPTH, G), 0.02),
        'rwkv_ln_b': nrm((DEPTH, G), 0.02),
        'gla_a_up': nrm((DEPTH, 2, GLA_GATE_RANK, GLA_KD), GLA_GATE_RANK ** -0.5),
        'gla_a_bias': nrm((DEPTH, 2, GLA_KD), 0.5),
        'gla_norm': 1.0 + nrm((DEPTH, GLA_DV), 0.02),
        'hgrn_lb': 1.0 + nrm((2, DEPTH, G), 0.1),
        'hgrn_norm': 1.0 + nrm((DEPTH, HGRN_DV), 0.02),
        'ffn_w1': nrm((n_dense, D, D_FF), D ** -0.5),
        'ffn_w3': nrm((n_dense, D, D_FF), D ** -0.5),
        'ffn_w2': nrm((n_dense, D_FF, D), D_FF ** -0.5),
        'moe_router': nrm((n_moe, D, N_EXPERTS), D ** -0.5),
        'moe_w1': nrm((n_moe, N_EXPERTS, D, D_FF), D ** -0.5),
        'moe_w3': nrm((n_moe, N_EXPERTS, D, D_FF), D ** -0.5),
        'moe_w2': nrm((n_moe, N_EXPERTS, D_FF, D), D_FF ** -0.5),
        'final_norm_g': 1.0 + nrm((D,), 0.02),
    }


def reference(x, c, ctx, c_ctx, norm1_g, norm2_g, w_mod, b_mod, w_in, w_out,
              mla_q_norm, mla_w_q_up, mla_kv_norm, mla_w_kv_up,
              rwkv_mu, rwkv_w0, rwkv_w_up, rwkv_a0, rwkv_a_up, rwkv_k_k, rwkv_k_a, rwkv_u,
              rwkv_g_up, rwkv_ln_g, rwkv_ln_b,
              gla_a_up, gla_a_bias, gla_norm, hgrn_lb, hgrn_norm,
              ffn_w1, ffn_w3, ffn_w2, moe_router, moe_w1, moe_w3, moe_w2, final_norm_g):
    m = ctx.shape[1]
    lb_all = jnp.cumsum(jax.nn.softmax(hgrn_lb.astype(jnp.float32), axis=1), axis=1)
    lb_all = lb_all - lb_all[:, :1]
    ffn_args = (ffn_w1, ffn_w3, ffn_w2, moe_router, moe_w1, moe_w3, moe_w2)
    xc, xl = ctx, x
    for l in range(DEPTH):
        last = l == DEPTH - 1
        mod_lat = (jax.nn.silu(c) @ w_mod[l] + b_mod[l])[:, None, :]
        mod_ctx = (jax.nn.silu(c_ctx) @ w_mod[l] + b_mod[l])[None, None, :]
        sh1_l, sc1_l, gt1_l, sh2_l, sc2_l, gt2_l = jnp.split(mod_lat, 6, axis=-1)
        sh1_c, sc1_c, gt1_c, sh2_c, sc2_c, gt2_c = jnp.split(mod_ctx, 6, axis=-1)

        h = jnp.concatenate([modulate(xc, norm1_g[l], sh1_c, sc1_c),
                             modulate(xl, norm1_g[l], sh1_l, sc1_l)], axis=1)
        p_mla, p_rwkv, p_gla, p_hgrn = jnp.split(h @ w_in[l], IN_SPLITS, axis=-1)
        o = jnp.concatenate([
            mla_mixer(p_mla, m, mla_q_norm[l], mla_w_q_up[l], mla_kv_norm[l], mla_w_kv_up[l]),
            rwkv7_mixer(p_rwkv, m, rwkv_mu[l], rwkv_w0[l], rwkv_w_up[l], rwkv_a0[l], rwkv_a_up[l],
                        rwkv_k_k[l], rwkv_k_a[l], rwkv_u[l], rwkv_g_up[l], rwkv_ln_g[l], rwkv_ln_b[l]),
            gla_mixer(p_gla, m, gla_a_up[l], gla_a_bias[l], gla_norm[l]),
            hgrn2_mixer(p_hgrn, m, lb_all[:, l], hgrn_norm[l]),
        ], axis=-1)

        if last:
            xl = xl + gt1_l * (o[:, m:] @ w_out[l])
            h2 = modulate(xl, norm2_g[l], sh2_l, sc2_l)
            xl = xl + gt2_l * channel_mixer(l, h2, *ffn_args)
        else:
            y = o @ w_out[l]
            xc = xc + gt1_c * y[:, :m]
            xl = xl + gt1_l * y[:, m:]
            h2 = jnp.concatenate([modulate(xc, norm2_g[l], sh2_c, sc2_c),
                                  modulate(xl, norm2_g[l], sh2_l, sc2_l)], axis=1)
            f = channel_mixer(l, h2, *ffn_args)
            xc = xc + gt2_c * f[:, :m]
            xl = xl + gt2_l * f[:, m:]
    return rmsnorm(xl, final_norm_g)
```

```python
import functools
import math

import numpy as np
import jax
import jax.numpy as jnp
from jax import lax
from jax.experimental import pallas as pl
from jax.experimental.pallas import tpu as pltpu

F32 = jnp.float32
BF16 = jnp.bfloat16

DEPTH = 4
GRID_W = 64
EPS = 1e-6
GROUP_W = 512
MLA_HEADS, MLA_NOPE, MLA_ROPE, MLA_V = 4, 128, 64, 128
MLA_Q_LORA, MLA_KV_LORA = 384, 256
ROPE_BASE = 10000.0
RWKV_HEADS, RWKV_HEAD = 8, 64
RWKV_W_LORA, RWKV_A_LORA, RWKV_G_LORA = 64, 64, 128
RWKV_LN_EPS = 64e-5
GLA_HEADS, GLA_DK, GLA_DV = 4, 64, 128
GLA_GATE_RANK, GLA_GATE_NORM = 16, 16.0
HGRN_HEADS, HGRN_EXPAND, HGRN_DV = 4, 128, 128
CHUNK = 64
N_EXPERTS, TOP_K = 8, 2
MLA_COLS = MLA_Q_LORA + MLA_KV_LORA + MLA_ROPE
RWKV_COLS = 3 * GROUP_W + 2 * RWKV_W_LORA + 2 * RWKV_A_LORA + RWKV_G_LORA
GLA_KD = GLA_HEADS * GLA_DK
GLA_COLS = 2 * GLA_KD + GROUP_W + 2 * GLA_GATE_RANK + GROUP_W
IN_SPLITS = (MLA_COLS, MLA_COLS + RWKV_COLS, MLA_COLS + RWKV_COLS + GLA_COLS)
RWKV_SPLITS = (GROUP_W, 2 * GROUP_W, 3 * GROUP_W, 3 * GROUP_W + 2 * RWKV_W_LORA,
               3 * GROUP_W + 2 * RWKV_W_LORA + 2 * RWKV_A_LORA)
GLA_SPLITS = (GLA_KD, 2 * GLA_KD, 2 * GLA_KD + GROUP_W, 2 * GLA_KD + GROUP_W + 2 * GLA_GATE_RANK)

V7X_VMEM_LIMIT = 56 * 1024 * 1024
N_LEVELS = 6


def _cparams(sem, vmem=V7X_VMEM_LIMIT):
    return pltpu.CompilerParams(dimension_semantics=sem, vmem_limit_bytes=vmem)


def _dot(a, b):
    return jnp.dot(a.astype(BF16), b.astype(BF16), preferred_element_type=F32)


def _dot_nt(a, b):
    return lax.dot_general(a.astype(BF16), b.astype(BF16), (((1,), (1,)), ((), ())), preferred_element_type=F32)


def _split2(x):
    hi = x.astype(BF16)
    lo = (x - hi.astype(F32)).astype(BF16)
    return hi, lo


def _dot_exact_lhs(m_bf16, x):
    hi, lo = _split2(x)
    return (jnp.dot(m_bf16, hi, preferred_element_type=F32) + jnp.dot(m_bf16, lo, preferred_element_type=F32))


def _mm_kernel(*refs, n_w, has_res, has_rowscale, m_ctx, tm, cast_w):
    it = iter(refs)
    x_ref = next(it)
    w_refs = [next(it) for _ in range(n_w)]
    res_ref = next(it) if has_res else None
    gate_ref = next(it) if has_res else None
    rs_ref = next(it) if has_rowscale else None
    o_ref = next(it)
    wb_refs = [next(it) for _ in range(n_w)] if cast_w else w_refs
    i = pl.program_id(1)

    if cast_w:
        @pl.when(i == 0)
        def _():
            for w_ref, wb_ref in zip(w_refs, wb_refs):
                wb_ref[...] = w_ref[...].astype(BF16)

    x = x_ref[...].astype(BF16)
    acc = jnp.dot(x, wb_refs[0][...], preferred_element_type=F32)
    if n_w == 2:
        acc3 = jnp.dot(x, wb_refs[1][...], preferred_element_type=F32)
        acc = acc * jax.nn.sigmoid(acc) * acc3
    if has_rowscale:
        acc = acc * rs_ref[...]
    if has_res:
        rows = i * tm + lax.broadcasted_iota(jnp.int32, acc.shape, 0)
        g = jnp.where(rows < m_ctx, gate_ref[0:1, :], gate_ref[1:2, :])
        acc = res_ref[...] + g * acc
    o_ref[...] = acc.astype(o_ref.dtype)


def pmatmul(x, w, widx=(), *, w3=None, tm, tn, out_dtype=F32, res=None, gates=None, rowscale=None, m_ctx=0, name="mm"):
    m, k = x.shape
    n = w.shape[-1]
    tm = math.gcd(m, tm)
    assert w.shape[-2] == k and tm % 16 == 0
    nj, ni = pl.cdiv(n, tn), m // tm
    lead = (None,) * len(widx)
    w_spec = pl.BlockSpec(lead + (k, tn), lambda j, i: tuple(widx) + (0, j))
    ws = [w] if w3 is None else [w, w3]
    cast_w = w.dtype != BF16
    in_specs = [pl.BlockSpec((tm, k), lambda j, i: (i, 0))] + [w_spec] * len(ws)
    args = [x] + ws
    if res is not None:
        in_specs += [pl.BlockSpec((tm, tn), lambda j, i: (i, j)), pl.BlockSpec((2, tn), lambda j, i: (0, j))]
        args += [res, gates]
    if rowscale is not None:
        in_specs.append(pl.BlockSpec((tm, 1), lambda j, i: (i, 0)))
        args.append(rowscale)
    kern = functools.partial(_mm_kernel, n_w=len(ws), has_res=res is not None, has_rowscale=rowscale is not None,
                             m_ctx=m_ctx, tm=tm, cast_w=cast_w)
    return pl.pallas_call(
        kern,
        out_shape=jax.ShapeDtypeStruct((m, n), out_dtype),
        grid=(nj, ni),
        in_specs=in_specs,
        out_specs=pl.BlockSpec((tm, tn), lambda j, i: (i, j)),
        scratch_shapes=[pltpu.VMEM((k, tn), BF16) for _ in ws] if cast_w else [],
        compiler_params=_cparams(("arbitrary", "arbitrary")),
        name=name,
    )(*args)


def _norm_kernel(x_ref, g_ref, sh_ref, sc_ref, o_ref, *, m_ctx, tm):
    i = pl.program_id(0)
    x = x_ref[...]
    y = x * lax.rsqrt(jnp.mean(x * x, axis=-1, keepdims=True) + EPS) * g_ref[...]
    rows = i * tm + lax.broadcasted_iota(jnp.int32, x.shape, 0)
    is_ctx = rows < m_ctx
    sc = jnp.where(is_ctx, sc_ref[0:1, :], sc_ref[1:2, :])
    sh = jnp.where(is_ctx, sh_ref[0:1, :], sh_ref[1:2, :])
    o_ref[...] = (y * (1.0 + sc) + sh).astype(o_ref.dtype)


def norm_mod(x, g, shift2, scale2, *, m_ctx, out_dtype, tm=256, name="norm_mod"):
    m, d = x.shape
    return pl.pallas_call(
        functools.partial(_norm_kernel, m_ctx=m_ctx, tm=tm),
        out_shape=jax.ShapeDtypeStruct((m, d), out_dtype),
        grid=(m // tm,),
        in_specs=[pl.BlockSpec((tm, d), lambda i: (i, 0)), pl.BlockSpec((1, d), lambda i: (0, 0)),
                  pl.BlockSpec((2, d), lambda i: (0, 0)), pl.BlockSpec((2, d), lambda i: (0, 0))],
        out_specs=pl.BlockSpec((tm, d), lambda i: (i, 0)),
        compiler_params=_cparams(("parallel",)),
        name=name,
    )(x, g.reshape(1, d), shift2, scale2)


def _attn_kernel(q_ref, k_ref, v_ref, o_ref, m_ref, l_ref, acc_ref, *, tk, n_kv):
    q = q_ref[...]
    m_ref[...] = jnp.full(m_ref.shape, -jnp.inf, F32)
    l_ref[...] = jnp.zeros(l_ref.shape, F32)
    acc_ref[...] = jnp.zeros(acc_ref.shape, F32)

    def body(j, carry):
        off = pl.multiple_of(j * tk, tk)
        kb = k_ref[pl.ds(off, tk), :]
        vb = v_ref[pl.ds(off, tk), :]
        s = lax.dot_general(q, kb, (((1,), (1,)), ((), ())), preferred_element_type=F32)
        m_old = m_ref[...]
        m_new = jnp.maximum(m_old, jnp.max(s, axis=-1, keepdims=True))
        alpha = jnp.exp(m_old - m_new)
        p = jnp.exp(s - m_new)
        l_ref[...] = alpha * l_ref[...] + jnp.sum(p, axis=-1, keepdims=True)
        acc_ref[...] = alpha * acc_ref[...] + jnp.dot(p.astype(BF16), vb, preferred_element_type=F32)
        m_ref[...] = m_new
        return carry

    lax.fori_loop(0, n_kv, body, 0)
    o_ref[...] = (acc_ref[...] / l_ref[...]).astype(o_ref.dtype)


def flash_attention(q, k, v, *, tq, tk, name="mla_attn"):
    h, lq, dqk = q.shape
    lk, dv = k.shape[1], v.shape[2]
    assert lq % tq == 0 and lk % tk == 0
    return pl.pallas_call(
        functools.partial(_attn_kernel, tk=tk, n_kv=lk // tk),
        out_shape=jax.ShapeDtypeStruct((lq, h * dv), F32),
        grid=(h, lq // tq),
        in_specs=[pl.BlockSpec((None, tq, dqk), lambda hh, i: (hh, i, 0)),
                  pl.BlockSpec((None, lk, dqk), lambda hh, i: (hh, 0, 0)),
                  pl.BlockSpec((None, lk, dv), lambda hh, i: (hh, 0, 0))],
        out_specs=pl.BlockSpec((tq, dv), lambda hh, i: (i, hh)),
        scratch_shapes=[pltpu.VMEM((tq, 1), F32), pltpu.VMEM((tq, 1), F32), pltpu.VMEM((tq, dv), F32)],
        compiler_params=_cparams(("parallel", "parallel")),
        name=name,
    )(q, k, v)


def _chunk_constants():
    c = CHUNK
    t = np.arange(c)
    tri = (t[None, :] <= t[:, None]).astype(np.float32)
    strict = (t[None, :] < t[:, None]).astype(np.float32)
    eye = np.eye(c, dtype=np.float32)
    seg, off = [], []
    for lv in range(N_LEVELS):
        s = c >> (lv + 1)
        blk = t // s
        same = blk[:, None] == blk[None, :]
        odd = (blk % 2 == 1)[:, None]
        seg.append(np.where(odd, same & (t[None, :] <= t[:, None]), same & (t[None, :] > t[:, None])).astype(np.float32))
        off.append((odd & (blk[None, :] == blk[:, None] - 1)).astype(np.float32))
    seg, off = np.stack(seg), np.stack(off)

    def both(a):
        return np.stack([a, a[..., ::-1, ::-1]])

    return {k: both(v) for k, v in dict(tri=tri, strict=strict, eye=eye, seg=seg, off=off).items()}


_CC = _chunk_constants()


def _chunk_pos(d, c, n_ctx_chunks, n_chunks):
    back = jnp.where(c < n_ctx_chunks, n_ctx_chunks - 1 - c, n_chunks + n_ctx_chunks - 1 - c)
    return jnp.where(d == 0, c, back)


def _gla_kernel(q_ref, k_ref, v_ref, g_ref, mall_ref, off_ref, eye_ref, o_ref, st_ref, *, heads):
    c = pl.program_id(1)

    @pl.when(c == 0)
    def _():
        st_ref[...] = jnp.zeros(st_ref.shape, F32)

    mall = mall_ref[...]
    eye = eye_ref[...]
    for h in range(heads):
        q, k, v, g = q_ref[h], k_ref[h], v_ref[h], g_ref[h]
        e_all = _dot_exact_lhs(mall, g)
        bc = e_all[0:CHUNK]
        btot = jnp.sum(g, axis=0, keepdims=True)
        att = eye * _dot_nt(q, k)
        for lv in range(N_LEVELS):
            w = jnp.exp(e_all[(lv + 1) * CHUNK:(lv + 2) * CHUNK])
            att = att + off_ref[lv] * _dot_nt(q * w, k * w)
        st = st_ref[h]
        out = _dot_nt(q * jnp.exp(bc), st) + _dot(att, v)
        kt = k * jnp.exp(btot - bc)
        st_ref[h] = st * jnp.exp(btot) + _dot(v.T, kt)
        o_ref[h] = out


def gla_scan(q, k, v, g, *, n_ctx, name):
    _, h, l, dk = g.shape
    dv = v.shape[-1]
    nch, ncc = l // CHUNK, n_ctx // CHUNK

    def spec(arr, d_last):
        nd = arr.shape[0]
        return pl.BlockSpec((None, h, CHUNK, d_last),
                            lambda d, c: (d if nd == 2 else 0, 0, _chunk_pos(d, c, ncc, nch), 0))

    mall = jnp.asarray(np.concatenate([_CC["tri"][:, None], _CC["seg"]], axis=1).reshape(2, -1, CHUNK), BF16)
    off = jnp.asarray(_CC["off"], F32)
    eye = jnp.asarray(_CC["eye"], F32)
    nl = 1 + N_LEVELS
    return pl.pallas_call(
        functools.partial(_gla_kernel, heads=h),
        out_shape=jax.ShapeDtypeStruct((2, h, l, dv), F32),
        grid=(2, nch),
        in_specs=[spec(q, dk), spec(k, dk), spec(v, dv), spec(g, dk),
                  pl.BlockSpec((None, nl * CHUNK, CHUNK), lambda d, c: (d, 0, 0)),
                  pl.BlockSpec((None, N_LEVELS, CHUNK, CHUNK), lambda d, c: (d, 0, 0, 0)),
                  pl.BlockSpec((None, CHUNK, CHUNK), lambda d, c: (d, 0, 0))],
        out_specs=pl.BlockSpec((None, h, CHUNK, dv), lambda d, c: (d, 0, _chunk_pos(d, c, ncc, nch), 0)),
        scratch_shapes=[pltpu.VMEM((h, dv, dk), F32)],
        compiler_params=_cparams(("arbitrary", "arbitrary")),
        name=name,
    )(q, k, v, g, mall, off, eye)


def _rwkv_kernel(r_ref, v_ref, lw_ref, k_ref, kk_ref, b_ref, tri_ref, strict_ref, off_ref, eye_ref, o_ref, st_ref,
                 *, heads):
    c = pl.program_id(1)

    @pl.when(c == 0)
    def _():
        st_ref[...] = jnp.zeros(st_ref.shape, F32)

    tri = tri_ref[...]
    strict = strict_ref[...]
    eye = eye_ref[...]
    incl = strict + eye
    for h in range(heads):
        r, v, lw, k, kk, b = r_ref[h], v_ref[h], lw_ref[h], k_ref[h], kk_ref[h], b_ref[h]
        bc = _dot_exact_lhs(tri, lw)
        btot = jnp.sum(lw, axis=0, keepdims=True)
        einv = jnp.exp(-bc)
        khat = kk * jnp.exp(bc - lw)
        rhat = r * jnp.exp(bc)
        ks = k * einv
        bs = b * einv
        a_kb = strict * _dot_nt(khat, bs)
        a_kk = strict * _dot_nt(khat, ks)
        a_rb = incl * _dot_nt(rhat, bs)
        a_rk = incl * _dot_nt(rhat, ks)
        minv = eye - off_ref[N_LEVELS - 1] * a_kb
        for lv in range(N_LEVELS - 2, -1, -1):
            minv = minv - _dot(minv, _dot(off_ref[lv] * a_kb, minv))
        st = st_ref[h]
        u = _dot(minv, _dot_nt(khat, st) + _dot(a_kk, v))
        o_ref[h] = _dot_nt(rhat, st) + _dot(a_rk, v) - _dot(a_rb, u)
        st_ref[h] = (st + _dot(v.T, ks) - _dot(u.T, bs)) * jnp.exp(btot)


def rwkv_scan(r, v, lw, k, kk, b, *, n_ctx, name="rwkv_scan"):
    _, h, l, dk = lw.shape
    nch, ncc = l // CHUNK, n_ctx // CHUNK

    def spec(arr):
        nd = arr.shape[0]
        return pl.BlockSpec((None, h, CHUNK, dk),
                            lambda d, c: (d if nd == 2 else 0, 0, _chunk_pos(d, c, ncc, nch), 0))

    cc = lambda a: pl.BlockSpec((None,) + a.shape[1:], lambda d, c: (d,) + (0,) * (a.ndim - 1))
    tri = jnp.asarray(_CC["tri"], BF16)
    strict = jnp.asarray(_CC["strict"], F32)
    off = jnp.asarray(_CC["off"], F32)
    eye = jnp.asarray(_CC["eye"], F32)
    return pl.pallas_call(
        functools.partial(_rwkv_kernel, heads=h),
        out_shape=jax.ShapeDtypeStruct((2, h, l, dk), F32),
        grid=(2, nch),
        in_specs=[spec(r), spec(v), spec(lw), spec(k), spec(kk), spec(b), cc(tri), cc(strict), cc(off), cc(eye)],
        out_specs=pl.BlockSpec((None, h, CHUNK, dk), lambda d, c: (d, 0, _chunk_pos(d, c, ncc, nch), 0)),
        scratch_shapes=[pltpu.VMEM((h, dk, dk), F32)],
        compiler_params=_cparams(("arbitrary", "arbitrary")),
        name=name,
    )(r, v, lw, k, kk, b, tri, strict, off, eye)


def _rms(x, g, eps=EPS):
    return x * lax.rsqrt(jnp.mean(x * x, axis=-1, keepdims=True) + eps) * g


def _heads_major(t, h):
    l = t.shape[0]
    return t.reshape(l, h, -1).transpose(1, 0, 2)


def _rope_tables(n):
    rows = n // GRID_W
    row = jnp.repeat(jnp.arange(rows, dtype=F32), GRID_W)
    col = jnp.tile(jnp.arange(GRID_W, dtype=F32), rows)
    n_freq = MLA_ROPE // 4
    freqs = ROPE_BASE ** (-jnp.arange(n_freq, dtype=F32) / n_freq)
    ang = jnp.stack([row[:, None] * freqs, col[:, None] * freqs], axis=1)
    return jnp.cos(ang), jnp.sin(ang)


def _rope(x, cos, sin):
    xs = x.reshape(x.shape[:-1] + (2, 2, MLA_ROPE // 4))
    x1, x2 = xs[..., 0, :], xs[..., 1, :]
    return jnp.stack([x1 * cos - x2 * sin, x1 * sin + x2 * cos], axis=-2).reshape(x.shape)


def mla_mixer(p, m, q_norm, w_q_up, kv_norm, w_kv_up):
    l = p.shape[0]
    n = l - m
    cq, ckv, k_rope = p[:, :MLA_Q_LORA], p[:, MLA_Q_LORA:MLA_Q_LORA + MLA_KV_LORA], p[:, MLA_Q_LORA + MLA_KV_LORA:]
    q = pmatmul(_rms(cq, q_norm), w_q_up, tm=1408, tn=768, name="mla_q_up").reshape(l, MLA_HEADS, MLA_NOPE + MLA_ROPE)
    kv = pmatmul(_rms(ckv, kv_norm), w_kv_up, tm=1408, tn=1024, name="mla_kv_up").reshape(l, MLA_HEADS, MLA_NOPE + MLA_V)
    k_nope, v = kv[..., :MLA_NOPE], kv[..., MLA_NOPE:]
    cos, sin = _rope_tables(n)
    q_nope, q_rope = q[..., :MLA_NOPE], q[..., MLA_NOPE:]
    q_rope = jnp.concatenate([q_rope[:m], _rope(q_rope[m:], cos[:, None], sin[:, None])], axis=0)
    k_rope = jnp.concatenate([k_rope[:m], _rope(k_rope[m:], cos, sin)], axis=0)
    scale = (MLA_NOPE + MLA_ROPE) ** -0.5
    qh = (jnp.concatenate([q_nope, q_rope], axis=-1) * scale).astype(BF16).transpose(1, 0, 2)
    kh = jnp.concatenate([k_nope, jnp.broadcast_to(k_rope[:, None, :], (l, MLA_HEADS, MLA_ROPE))], axis=-1)
    kh = kh.astype(BF16).transpose(1, 0, 2)
    vh = v.astype(BF16).transpose(1, 0, 2)
    o_ctx = flash_attention(qh[:, :m], kh[:, :m], vh[:, :m], tq=m, tk=m, name="mla_attn_ctx")
    o_lat = flash_attention(qh[:, m:], kh, vh, tq=512, tk=256, name="mla_attn_lat")
    return jnp.concatenate([o_ctx, o_lat], axis=0)


def _shift_mix(z, m, mu_prev, mu_next):
    def one(zp):
        zprev = jnp.pad(zp[:-1], ((1, 0), (0, 0)))
        znext = jnp.pad(zp[1:], ((0, 1), (0, 0)))
        return zp + mu_prev * (zprev - zp) + mu_next * (znext - zp)
    return jnp.concatenate([one(z[:m]), one(z[m:])], axis=0)


def rwkv7_mixer(p, m, mu, w0, w_up, a0, a_up, k_k, k_a, u, g_up, ln_g, ln_b):
    l = p.shape[0]
    p = _shift_mix(p, m, mu[0], mu[1])
    r, k, v, w_dn, a_dn, g_dn = jnp.split(p, RWKV_SPLITS, axis=-1)
    hd = lambda t: t.reshape(l, RWKV_HEADS, RWKV_HEAD)
    r_h, v_h = hd(r), hd(v)
    lws, kmods, kks, bs, bonus = [], [], [], [], 0.0
    for d in range(2):
        w_lora = pmatmul(jnp.tanh(w_dn[:, d * RWKV_W_LORA:(d + 1) * RWKV_W_LORA]).astype(BF16), w_up, (d,),
                         tm=1408, tn=GROUP_W, name="rwkv_w_up")
        w = w0[d] + w_lora
        lws.append(-jnp.exp(-jax.nn.softplus(-w) - 0.5))
        a_lora = pmatmul(a_dn[:, d * RWKV_A_LORA:(d + 1) * RWKV_A_LORA].astype(BF16), a_up, (d,),
                         tm=1408, tn=GROUP_W, name="rwkv_a_up")
        a = jax.nn.sigmoid(a0[d] + a_lora)
        kk = hd(k * k_k[d])
        kk = kk * lax.rsqrt(jnp.maximum(jnp.sum(kk * kk, axis=-1, keepdims=True), 1e-24))
        k_mod = hd(k * (1 + (a - 1) * k_a[d]))
        kmods.append(k_mod)
        kks.append(kk)
        bs.append(kk * hd(a))
        bonus = bonus + jnp.sum(r_h * k_mod * u[d].reshape(RWKV_HEADS, RWKV_HEAD), axis=-1, keepdims=True) * v_h
    hm = lambda t: t.transpose(1, 0, 2)
    st2 = lambda ts: jnp.stack([hm(t) for t in ts])
    o2 = rwkv_scan(hm(r_h)[None], hm(v_h)[None], st2([hd(t) for t in lws]), st2(kmods), st2(kks), st2(bs), n_ctx=m)
    o = (o2[0] + o2[1]).transpose(1, 0, 2)
    mean = jnp.mean(o, axis=-1, keepdims=True)
    var = jnp.mean(jnp.square(o - mean), axis=-1, keepdims=True)
    o = ((o - mean) * lax.rsqrt(var + RWKV_LN_EPS)).reshape(l, GROUP_W) * ln_g + ln_b
    o = o + bonus.reshape(l, GROUP_W)
    g = pmatmul(jax.nn.sigmoid(g_dn).astype(BF16), g_up, tm=1408, tn=GROUP_W, name="rwkv_g_up")
    return o * g


def gla_mixer(p, m, a_up, a_bias, g_norm):
    l = p.shape[0]
    q, k, v, a_dn, r = jnp.split(p, GLA_SPLITS, axis=-1)
    qh = _heads_major(q * GLA_DK ** -0.5, GLA_HEADS)[None]
    kh = _heads_major(k, GLA_HEADS)[None]
    vh = _heads_major(v, GLA_HEADS)[None]
    gs = []
    for d in range(2):
        logit = pmatmul(a_dn[:, d * GLA_GATE_RANK:(d + 1) * GLA_GATE_RANK].astype(BF16), a_up, (d,),
                        tm=1408, tn=GLA_KD, name="gla_gate_up") + a_bias[d]
        gs.append(_heads_major(jax.nn.log_sigmoid(logit) / GLA_GATE_NORM, GLA_HEADS))
    o2 = gla_scan(qh, kh, vh, jnp.stack(gs), n_ctx=m, name="gla_scan")
    o = (o2[0] + o2[1]).transpose(1, 0, 2)
    o = _rms(o, g_norm).reshape(l, GROUP_W)
    return o * jax.nn.silu(r)


def hgrn2_mixer(p, m, lb, g_norm):
    l = p.shape[0]
    q, f_fwd, f_bwd, i, g = jnp.split(p, 5, axis=-1)
    qh = _heads_major(jax.nn.silu(q), HGRN_HEADS)[None]
    ih = _heads_major(i, HGRN_HEADS)[None]
    ks, gs = [], []
    for d, f_raw in enumerate((f_fwd, f_bwd)):
        lb_d = lb[d]
        log_f = jnp.logaddexp(jnp.log(lb_d), jnp.log1p(-lb_d) + jax.nn.log_sigmoid(f_raw))
        ks.append(_heads_major((1.0 - lb_d) * jax.nn.sigmoid(-f_raw), HGRN_HEADS))
        gs.append(_heads_major(log_f, HGRN_HEADS))
    o2 = gla_scan(qh, jnp.stack(ks), ih, jnp.stack(gs), n_ctx=m, name="hgrn_scan")
    o = (o2[0] + o2[1]).transpose(1, 0, 2)
    o = _rms(o, g_norm).reshape(l, GROUP_W)
    return o * jax.nn.silu(g)


def dense_ffn(xs, h2, w1, w3, w2, j, gates2, m):
    act = pmatmul(h2, w1, (j,), w3=w3, tm=1408, tn=512, out_dtype=BF16, name="ffn_up")
    return pmatmul(act, w2, (j,), tm=768, tn=256, res=xs, gates=gates2, m_ctx=m, name="ffn_down")


def moe_ffn(xs, h2f, h2, router, w1, w3, w2, j, gates2, m):
    logits = jnp.dot(h2f, router[j], precision=lax.Precision.HIGHEST)
    top_vals, top_idx = lax.top_k(logits, TOP_K)
    weights = jax.nn.softmax(top_vals, axis=-1)
    gates = jnp.sum(jax.nn.one_hot(top_idx, N_EXPERTS, dtype=F32) * weights[..., None], axis=-2)
    out = xs
    for e in range(N_EXPERTS):
        act = pmatmul(h2, w1, (j, e), w3=w3, tm=1408, tn=512, out_dtype=BF16, rowscale=gates[:, e:e + 1],
                      name="moe_up")
        out = pmatmul(act, w2, (j, e), tm=768, tn=256, res=out, gates=gates2, m_ctx=m, name="moe_down")
    return out


def kernel(x, c, ctx, c_ctx, norm1_g, norm2_g, w_mod, b_mod, w_in, w_out, mla_q_norm, mla_w_q_up, mla_kv_norm, mla_w_kv_up, rwkv_mu, rwkv_w0, rwkv_w_up, rwkv_a0, rwkv_a_up, rwkv_k_k, rwkv_k_a, rwkv_u, rwkv_g_up, rwkv_ln_g, rwkv_ln_b, gla_a_up, gla_a_bias, gla_norm, hgrn_lb, hgrn_norm, ffn_w1, ffn_w3, ffn_w2, moe_router, moe_w1, moe_w3, moe_w2, final_norm_g):
    m, n, d = ctx.shape[1], x.shape[1], x.shape[2]
    lb_all = jnp.cumsum(jax.nn.softmax(hgrn_lb.astype(F32), axis=1), axis=1)
    lb_all = lb_all - lb_all[:, :1]
    xs = jnp.concatenate([ctx[0], x[0]], axis=0)
    cvec = jnp.zeros((16, d), F32).at[0].set(jax.nn.silu(c[0])).at[1].set(jax.nn.silu(c_ctx))
    for l in range(DEPTH):
        mod = pmatmul(cvec, w_mod, (l,), tm=16, tn=1024, name="mod") + b_mod[l]
        mods = jnp.stack([mod[1].reshape(6, d), mod[0].reshape(6, d)], axis=1)
        sh1, sc1, gt1, sh2, sc2, gt2 = (mods[i] for i in range(6))

        h = norm_mod(xs, norm1_g[l], sh1, sc1, m_ctx=m, out_dtype=BF16)
        p = pmatmul(h, w_in, (l,), tm=1408, tn=1024, name="w_in")
        p_mla, p_rwkv, p_gla, p_hgrn = jnp.split(p, IN_SPLITS, axis=-1)
        o = jnp.concatenate([
            mla_mixer(p_mla, m, mla_q_norm[l], mla_w_q_up[l], mla_kv_norm[l], mla_w_kv_up[l]),
            rwkv7_mixer(p_rwkv, m, rwkv_mu[l], rwkv_w0[l], rwkv_w_up[l], rwkv_a0[l], rwkv_a_up[l],
                        rwkv_k_k[l], rwkv_k_a[l], rwkv_u[l], rwkv_g_up[l], rwkv_ln_g[l], rwkv_ln_b[l]),
            gla_mixer(p_gla, m, gla_a_up[l], gla_a_bias[l], gla_norm[l]),
            hgrn2_mixer(p_hgrn, m, lb_all[:, l], hgrn_norm[l]),
        ], axis=-1).astype(BF16)
        xs = pmatmul(o, w_out, (l,), tm=1408, tn=512, res=xs, gates=gt1, m_ctx=m, name="w_out")

        j = l // 2
        if l % 2 == 0:
            h2 = norm_mod(xs, norm2_g[l], sh2, sc2, m_ctx=m, out_dtype=BF16)
            xs = dense_ffn(xs, h2, ffn_w1, ffn_w3, ffn_w2, j, gt2, m)
        else:
            h2f = norm_mod(xs, norm2_g[l], sh2, sc2, m_ctx=m, out_dtype=F32)
            xs = moe_ffn(xs, h2f, h2f.astype(BF16), moe_router, moe_w1, moe_w3, moe_w2, j, gt2, m)
    zeros2 = jnp.zeros((2, d), F32)
    out = norm_mod(xs, final_norm_g, zeros2, zeros2, m_ctx=m, out_dtype=F32, name="final_norm")
    return out[m:][None]
```

```python
import functools
import math

import numpy as np
import jax
import jax.numpy as jnp
from jax import lax
from jax.experimental import pallas as pl
from jax.experimental.pallas import tpu as pltpu

F32 = jnp.float32
BF16 = jnp.bfloat16

DEPTH = 4
GRID_W = 64
EPS = 1e-6
GROUP_W = 512
MLA_HEADS, MLA_NOPE, MLA_ROPE, MLA_V = 4, 128, 64, 128
MLA_Q_LORA, MLA_KV_LORA = 384, 256
ROPE_BASE = 10000.0
RWKV_HEADS, RWKV_HEAD = 8, 64
RWKV_W_LORA, RWKV_A_LORA, RWKV_G_LORA = 64, 64, 128
RWKV_LN_EPS = 64e-5
GLA_HEADS, GLA_DK, GLA_DV = 4, 64, 128
GLA_GATE_RANK, GLA_GATE_NORM = 16, 16.0
HGRN_HEADS, HGRN_EXPAND, HGRN_DV = 4, 128, 128
CHUNK = 64
N_EXPERTS, TOP_K = 8, 2
MLA_COLS = MLA_Q_LORA + MLA_KV_LORA + MLA_ROPE
RWKV_COLS = 3 * GROUP_W + 2 * RWKV_W_LORA + 2 * RWKV_A_LORA + RWKV_G_LORA
GLA_KD = GLA_HEADS * GLA_DK
GLA_COLS = 2 * GLA_KD + GROUP_W + 2 * GLA_GATE_RANK + GROUP_W
IN_SPLITS = (MLA_COLS, MLA_COLS + RWKV_COLS, MLA_COLS + RWKV_COLS + GLA_COLS)
RWKV_SPLITS = (GROUP_W, 2 * GROUP_W, 3 * GROUP_W, 3 * GROUP_W + 2 * RWKV_W_LORA,
               3 * GROUP_W + 2 * RWKV_W_LORA + 2 * RWKV_A_LORA)
GLA_SPLITS = (GLA_KD, 2 * GLA_KD, 2 * GLA_KD + GROUP_W, 2 * GLA_KD + GROUP_W + 2 * GLA_GATE_RANK)

V7X_VMEM_LIMIT = 56 * 1024 * 1024
V7X_MXU = 256
N_LEVELS = 6
RWKV_PACK = V7X_MXU // RWKV_HEAD
MOE_TILE = 512


def _cparams(sem, vmem=V7X_VMEM_LIMIT):
    return pltpu.CompilerParams(dimension_semantics=sem, vmem_limit_bytes=vmem)


def _dot(a, b):
    return jnp.dot(a.astype(BF16), b.astype(BF16), preferred_element_type=F32)


def _dot_nt(a, b):
    return lax.dot_general(a.astype(BF16), b.astype(BF16), (((1,), (1,)), ((), ())), preferred_element_type=F32)


def _split2(x):
    hi = x.astype(BF16)
    lo = (x - hi.astype(F32)).astype(BF16)
    return hi, lo


def _dot_exact_lhs(m_bf16, x):
    hi, lo = _split2(x)
    return (jnp.dot(m_bf16, hi, preferred_element_type=F32) + jnp.dot(m_bf16, lo, preferred_element_type=F32))


def _mm_kernel(*refs, n_w, has_res, m_ctx, tm, precision):
    it = iter(refs)
    x_ref = next(it)
    w_refs = [next(it) for _ in range(n_w)]
    res_ref = next(it) if has_res else None
    gate_ref = next(it) if has_res else None
    o_ref = next(it)
    wb_refs = [next(it) for _ in range(n_w)] if precision is None else w_refs
    i = pl.program_id(1)

    if precision is None:
        @pl.when(i == 0)
        def _():
            for w_ref, wb_ref in zip(w_refs, wb_refs):
                wb_ref[...] = w_ref[...].astype(BF16)
        x = x_ref[...].astype(BF16)
    else:
        x = x_ref[...]

    acc = jnp.dot(x, wb_refs[0][...], preferred_element_type=F32, precision=precision)
    if n_w == 2:
        acc3 = jnp.dot(x, wb_refs[1][...], preferred_element_type=F32)
        acc = acc * jax.nn.sigmoid(acc) * acc3
    if has_res:
        rows = i * tm + lax.broadcasted_iota(jnp.int32, acc.shape, 0)
        g = jnp.where(rows < m_ctx, gate_ref[0:1, :], gate_ref[1:2, :])
        acc = res_ref[...] + g * acc
    o_ref[...] = acc.astype(o_ref.dtype)


def pmatmul(x, w, widx=(), *, w3=None, tm, tn, out_dtype=F32, res=None, gates=None, m_ctx=0, precision=None,
            name="mm"):
    m, k = x.shape
    n = w.shape[-1]
    tm = math.gcd(m, tm)
    assert w.shape[-2] == k and tm % 16 == 0
    nj, ni = pl.cdiv(n, tn), m // tm
    lead = (None,) * len(widx)
    w_spec = pl.BlockSpec(lead + (k, tn), lambda j, i: tuple(widx) + (0, j))
    ws = [w] if w3 is None else [w, w3]
    in_specs = [pl.BlockSpec((tm, k), lambda j, i: (i, 0))] + [w_spec] * len(ws)
    args = [x] + ws
    if res is not None:
        in_specs += [pl.BlockSpec((tm, tn), lambda j, i: (i, j)), pl.BlockSpec((2, tn), lambda j, i: (0, j))]
        args += [res, gates]
    kern = functools.partial(_mm_kernel, n_w=len(ws), has_res=res is not None, m_ctx=m_ctx, tm=tm, precision=precision)
    return pl.pallas_call(
        kern,
        out_shape=jax.ShapeDtypeStruct((m, n), out_dtype),
        grid=(nj, ni),
        in_specs=in_specs,
        out_specs=pl.BlockSpec((tm, tn), lambda j, i: (i, j)),
        scratch_shapes=[pltpu.VMEM((k, tn), BF16) for _ in ws] if precision is None else [],
        compiler_params=_cparams(("arbitrary", "arbitrary")),
        name=name,
    )(*args)


def _gmm_kernel(meta_ref, x_ref, *refs, n_w, has_rowscale, n_tiles):
    it = iter(refs)
    w_refs = [next(it) for _ in range(n_w)]
    rs_ref = next(it) if has_rowscale else None
    o_ref = next(it)
    wb_refs = [next(it) for _ in range(n_w)]
    t = pl.program_id(1)
    e = meta_ref[t]
    e_prev = meta_ref[jnp.maximum(t - 1, 0)]

    @pl.when((t == 0) | (e != e_prev))
    def _():
        for w_ref, wb_ref in zip(w_refs, wb_refs):
            wb_ref[...] = w_ref[...].astype(BF16)

    @pl.when(t < meta_ref[n_tiles])
    def _():
        x = x_ref[...].astype(BF16)
        acc = jnp.dot(x, wb_refs[0][...], preferred_element_type=F32)
        if n_w == 2:
            acc3 = jnp.dot(x, wb_refs[1][...], preferred_element_type=F32)
            acc = acc * jax.nn.sigmoid(acc) * acc3
        if has_rowscale:
            acc = acc * rs_ref[...]
        o_ref[...] = acc.astype(o_ref.dtype)

    @pl.when(t >= meta_ref[n_tiles])
    def _():
        o_ref[...] = jnp.zeros(o_ref.shape, o_ref.dtype)


def gmatmul(meta, x, w, jl, *, w3=None, tn, out_dtype, rowscale=None, name):
    r, k = x.shape
    n = w.shape[-1]
    n_tiles = r // MOE_TILE
    ws = [w] if w3 is None else [w, w3]
    w_spec = pl.BlockSpec((None, None, k, tn), lambda j, t, mr: (jl, mr[t], 0, j))
    in_specs = [pl.BlockSpec((MOE_TILE, k), lambda j, t, mr: (t, 0))] + [w_spec] * len(ws)
    args = [x] + ws
    if rowscale is not None:
        in_specs.append(pl.BlockSpec((MOE_TILE, 1), lambda j, t, mr: (t, 0)))
        args.append(rowscale)
    return pl.pallas_call(
        functools.partial(_gmm_kernel, n_w=len(ws), has_rowscale=rowscale is not None, n_tiles=n_tiles),
        out_shape=jax.ShapeDtypeStruct((r, n), out_dtype),
        grid_spec=pltpu.PrefetchScalarGridSpec(
            num_scalar_prefetch=1,
            grid=(n // tn, n_tiles),
            in_specs=in_specs,
            out_specs=pl.BlockSpec((MOE_TILE, tn), lambda j, t, mr: (t, j)),
            scratch_shapes=[pltpu.VMEM((k, tn), BF16) for _ in ws]),
        compiler_params=_cparams(("arbitrary", "arbitrary")),
        name=name,
    )(meta, *args)


def _norm_kernel(x_ref, g_ref, sh_ref, sc_ref, o_ref, *, m_ctx, tm):
    i = pl.program_id(0)
    x = x_ref[...]
    y = x * lax.rsqrt(jnp.mean(x * x, axis=-1, keepdims=True) + EPS) * g_ref[...]
    rows = i * tm + lax.broadcasted_iota(jnp.int32, x.shape, 0)
    is_ctx = rows < m_ctx
    sc = jnp.where(is_ctx, sc_ref[0:1, :], sc_ref[1:2, :])
    sh = jnp.where(is_ctx, sh_ref[0:1, :], sh_ref[1:2, :])
    o_ref[...] = (y * (1.0 + sc) + sh).astype(o_ref.dtype)


def norm_mod(x, g, shift2, scale2, *, m_ctx, out_dtype, tm=256, name="norm_mod"):
    m, d = x.shape
    return pl.pallas_call(
        functools.partial(_norm_kernel, m_ctx=m_ctx, tm=tm),
        out_shape=jax.ShapeDtypeStruct((m, d), out_dtype),
        grid=(m // tm,),
        in_specs=[pl.BlockSpec((tm, d), lambda i: (i, 0)), pl.BlockSpec((1, d), lambda i: (0, 0)),
                  pl.BlockSpec((2, d), lambda i: (0, 0)), pl.BlockSpec((2, d), lambda i: (0, 0))],
        out_specs=pl.BlockSpec((tm, d), lambda i: (i, 0)),
        compiler_params=_cparams(("parallel",)),
        name=name,
    )(x, g.reshape(1, d), shift2, scale2)


def _attn_kernel(q_ref, k_ref, v_ref, o_ref, *, tk, n_kv, dv):
    q = q_ref[...]
    m = acc = None
    for j in range(n_kv):
        kb = k_ref[j * tk:(j + 1) * tk, :]
        vb = v_ref[j * tk:(j + 1) * tk, :]
        s = lax.dot_general(q, kb, (((1,), (1,)), ((), ())), preferred_element_type=F32)
        m_blk = jnp.max(s, axis=-1, keepdims=True)
        m_new = m_blk if j == 0 else jnp.maximum(m, m_blk)
        pv = jnp.dot(jnp.exp(s - m_new).astype(BF16), vb, preferred_element_type=F32)
        acc = pv if j == 0 else jnp.exp(m - m_new) * acc + pv
        m = m_new
    o_ref[...] = (acc[:, :dv] / acc[:, dv:dv + 1]).astype(o_ref.dtype)


def _attn_kv_tile(lk, cap=1408):
    return max(t for t in range(128, cap + 1, 128) if lk % t == 0)


def flash_attention(q, k, v_ext, *, dv, tq, name="mla_attn"):
    h, lq, dqk = q.shape
    lk, dve = k.shape[1], v_ext.shape[2]
    tk = _attn_kv_tile(lk)
    assert lq % tq == 0
    return pl.pallas_call(
        functools.partial(_attn_kernel, tk=tk, n_kv=lk // tk, dv=dv),
        out_shape=jax.ShapeDtypeStruct((lq, h * dv), F32),
        grid=(h, lq // tq),
        in_specs=[pl.BlockSpec((None, tq, dqk), lambda hh, i: (hh, i, 0)),
                  pl.BlockSpec((None, lk, dqk), lambda hh, i: (hh, 0, 0)),
                  pl.BlockSpec((None, lk, dve), lambda hh, i: (hh, 0, 0))],
        out_specs=pl.BlockSpec((tq, dv), lambda hh, i: (i, hh)),
        compiler_params=_cparams(("parallel", "parallel")),
        name=name,
    )(q, k, v_ext)


def _chunk_constants():
    c = CHUNK
    t = np.arange(c)
    tri = (t[None, :] <= t[:, None]).astype(np.float32)
    strict = (t[None, :] < t[:, None]).astype(np.float32)
    eye = np.eye(c, dtype=np.float32)
    seg, off = [], []
    for lv in range(N_LEVELS):
        s = c >> (lv + 1)
        blk = t // s
        same = blk[:, None] == blk[None, :]
        odd = (blk % 2 == 1)[:, None]
        seg.append(np.where(odd, same & (t[None, :] <= t[:, None]), same & (t[None, :] > t[:, None])).astype(np.float32))
        off.append((odd & (blk[None, :] == blk[:, None] - 1)).astype(np.float32))
    seg, off = np.stack(seg), np.stack(off)

    def both(a):
        return np.stack([a, a[..., ::-1, ::-1]])

    return {k: both(v) for k, v in dict(tri=tri, strict=strict, eye=eye, seg=seg, off=off).items()}


_CC = _chunk_constants()


def _chunk_pos(d, c, n_ctx_chunks, n_chunks):
    back = jnp.where(c < n_ctx_chunks, n_ctx_chunks - 1 - c, n_chunks + n_ctx_chunks - 1 - c)
    return jnp.where(d == 0, c, back)


def _gla_kernel(q_ref, k_ref, v_ref, g_ref, mall_ref, off_ref, eye_ref, o_ref, st_ref, *, heads):
    c = pl.program_id(1)

    @pl.when(c == 0)
    def _():
        st_ref[...] = jnp.zeros(st_ref.shape, F32)

    mall = mall_ref[...]
    eye = eye_ref[...]
    for h in range(heads):
        q, k, v, g = q_ref[h], k_ref[h], v_ref[h], g_ref[h]
        e_all = _dot_exact_lhs(mall, g)
        bc = e_all[0:CHUNK]
        btot = jnp.sum(g, axis=0, keepdims=True)
        att = eye * _dot_nt(q, k)
        for lv in range(N_LEVELS):
            w = jnp.exp(e_all[(lv + 1) * CHUNK:(lv + 2) * CHUNK])
            att = att + off_ref[lv] * _dot_nt(q * w, k * w)
        st = st_ref[h]
        out = _dot_nt(q * jnp.exp(bc), st) + _dot(att, v)
        kt = k * jnp.exp(btot - bc)
        st_ref[h] = st * jnp.exp(btot) + _dot(v.T, kt)
        o_ref[h] = out


def gla_scan(q, k, v, g, *, n_ctx, name):
    _, h, l, dk = g.shape
    dv = v.shape[-1]
    nch, ncc = l // CHUNK, n_ctx // CHUNK

    def spec(arr, d_last):
        nd = arr.shape[0]
        return pl.BlockSpec((None, h, CHUNK, d_last),
                            lambda d, c: (d if nd == 2 else 0, 0, _chunk_pos(d, c, ncc, nch), 0))

    mall = jnp.asarray(np.concatenate([_CC["tri"][:, None], _CC["seg"]], axis=1).reshape(2, -1, CHUNK), BF16)
    off = jnp.asarray(_CC["off"], F32)
    eye = jnp.asarray(_CC["eye"], F32)
    nl = 1 + N_LEVELS
    return pl.pallas_call(
        functools.partial(_gla_kernel, heads=h),
        out_shape=jax.ShapeDtypeStruct((2, h, l, dv), F32),
        grid=(2, nch),
        in_specs=[spec(q, dk), spec(k, dk), spec(v, dv), spec(g, dk),
                  pl.BlockSpec((None, nl * CHUNK, CHUNK), lambda d, c: (d, 0, 0)),
                  pl.BlockSpec((None, N_LEVELS, CHUNK, CHUNK), lambda d, c: (d, 0, 0, 0)),
                  pl.BlockSpec((None, CHUNK, CHUNK), lambda d, c: (d, 0, 0))],
        out_specs=pl.BlockSpec((None, h, CHUNK, dv), lambda d, c: (d, 0, _chunk_pos(d, c, ncc, nch), 0)),
        scratch_shapes=[pltpu.VMEM((h, dv, dk), F32)],
        compiler_params=_cparams(("arbitrary", "arbitrary")),
        name=name,
    )(q, k, v, g, mall, off, eye)


def _rwkv_kernel(*refs, groups):
    (r0, v0, lw0, k0, kk0, b0, r1, v1, lw1, k1, kk1, b1,
     tri_ref, strict_ref, incl_ref, off_ref, eye_ref, hm_ref, o0_ref, o1_ref, st_ref) = refs
    c = pl.program_id(0)

    @pl.when(c == 0)
    def _():
        st_ref[...] = jnp.zeros(st_ref.shape, F32)

    hm = hm_ref[...]
    eye = eye_ref[...]
    w = RWKV_PACK * RWKV_HEAD
    nb = RWKV_PACK * CHUNK

    def stack(x):
        return jnp.concatenate([x] * RWKV_PACK, axis=0) * hm

    streams = []
    for d, (r_ref, v_ref, lw_ref, k_ref, kk_ref, b_ref, o_ref) in enumerate(
            ((r0, v0, lw0, k0, kk0, b0, o0_ref), (r1, v1, lw1, k1, kk1, b1, o1_ref))):
        lw = lw_ref[...]
        bc = _dot_exact_lhs(tri_ref[d], lw)
        ebt = jnp.exp(jnp.sum(lw, axis=0, keepdims=True))
        einv = jnp.exp(-bc)
        khat = kk_ref[...] * jnp.exp(bc - lw)
        rhat = r_ref[...] * jnp.exp(bc)
        ks = k_ref[...] * einv
        bs = b_ref[...] * einv
        v = v_ref[...]
        for g in range(groups):
            sl = slice(g * w, (g + 1) * w)
            streams.append(dict(
                d=d, g=g, sl=sl, o_ref=o_ref, ebt=ebt[:, sl],
                kr=jnp.concatenate([stack(khat[:, sl]), stack(rhat[:, sl])], axis=0).astype(BF16),
                bk=jnp.concatenate([stack(bs[:, sl]), stack(ks[:, sl])], axis=0).astype(BF16),
                vs=stack(v[:, sl])))

    for s in streams:
        aa = _dot_nt(s["kr"], s["bk"])
        strict, incl = strict_ref[s["d"]], incl_ref[s["d"]]
        s["akb"] = strict * aa[:nb, :nb]
        s["arb"] = (incl * aa[nb:, :nb]).astype(BF16)
        s["ak_v"] = jnp.concatenate([strict * aa[:nb, nb:], incl * aa[nb:, nb:]], axis=0).astype(BF16)
        s["minv"] = eye - off_ref[s["d"], N_LEVELS - 1] * s["akb"]
    for lv in range(N_LEVELS - 2, -1, -1):
        for s in streams:
            s["t1"] = _dot(off_ref[s["d"], lv] * s["akb"], s["minv"])
        for s in streams:
            s["minv"] = s["minv"] - _dot(s["minv"], s["t1"])
    for s in streams:
        st = st_ref[s["d"], s["g"]]
        from_state = _dot_nt(s["kr"], st)
        from_v = _dot(s["ak_v"], s["vs"])
        u = _dot(s["minv"], from_state[:nb] + from_v[:nb])
        o = from_state[nb:] + from_v[nb:] - _dot(s["arb"], u)
        s["o_ref"][:, s["sl"]] = sum(o[h * CHUNK:(h + 1) * CHUNK] for h in range(RWKV_PACK))
        upd = _dot(s["vs"].T, s["bk"][nb:]) - _dot(u.T, s["bk"][:nb])
        st_ref[s["d"], s["g"]] = (st + upd) * s["ebt"]


def rwkv_scan(r, v, lw, k, kk, b, *, n_ctx, name="rwkv_scan"):
    _, l, hw = lw.shape
    w = RWKV_PACK * RWKV_HEAD
    groups = hw // w
    nch, ncc = l // CHUNK, n_ctx // CHUNK
    nb = RWKV_PACK * CHUNK

    def shared(d):
        return pl.BlockSpec((CHUNK, hw), lambda c: (_chunk_pos(d, c, ncc, nch), 0))

    def perdir(d):
        return pl.BlockSpec((None, CHUNK, hw), lambda c: (d, _chunk_pos(d, c, ncc, nch), 0))

    def const(a):
        return pl.BlockSpec(a.shape, lambda c: (0,) * a.ndim)

    bd = lambda a: np.kron(np.eye(RWKV_PACK, dtype=np.float32), a)
    strict = jnp.asarray(np.stack([bd(_CC["strict"][d]) for d in range(2)]), F32)
    incl = jnp.asarray(np.stack([bd(_CC["strict"][d] + _CC["eye"][d]) for d in range(2)]), F32)
    off = jnp.asarray(np.stack([np.stack([bd(_CC["off"][d, lv]) for lv in range(N_LEVELS)]) for d in range(2)]), F32)
    eye = jnp.asarray(np.eye(nb, dtype=np.float32))
    hm = jnp.asarray(np.kron(np.eye(RWKV_PACK, dtype=np.float32), np.ones((CHUNK, RWKV_HEAD), np.float32)))
    tri = jnp.asarray(_CC["tri"], BF16)
    consts = [tri, strict, incl, off, eye, hm]
    in_specs = []
    args = []
    for d in range(2):
        in_specs += [shared(d), shared(d), perdir(d), perdir(d), perdir(d), perdir(d)]
        args += [r, v, lw, k, kk, b]
    return pl.pallas_call(
        functools.partial(_rwkv_kernel, groups=groups),
        out_shape=[jax.ShapeDtypeStruct((l, hw), F32)] * 2,
        grid=(nch,),
        in_specs=in_specs + [const(a) for a in consts],
        out_specs=[shared(0), shared(1)],
        scratch_shapes=[pltpu.VMEM((2, groups, nb, nb), F32)],
        compiler_params=_cparams(("arbitrary",)),
        name=name,
    )(*args, *consts)


def _rms(x, g, eps=EPS):
    return x * lax.rsqrt(jnp.mean(x * x, axis=-1, keepdims=True) + eps) * g


def _heads_major(t, h):
    l = t.shape[0]
    return t.reshape(l, h, -1).transpose(1, 0, 2)


def _rope_tables(n):
    rows = n // GRID_W
    row = jnp.repeat(jnp.arange(rows, dtype=F32), GRID_W)
    col = jnp.tile(jnp.arange(GRID_W, dtype=F32), rows)
    n_freq = MLA_ROPE // 4
    freqs = ROPE_BASE ** (-jnp.arange(n_freq, dtype=F32) / n_freq)
    ang = jnp.stack([row[:, None] * freqs, col[:, None] * freqs], axis=1)
    return jnp.cos(ang), jnp.sin(ang)


def _rope(x, cos, sin):
    xs = x.reshape(x.shape[:-1] + (2, 2, MLA_ROPE // 4))
    x1, x2 = xs[..., 0, :], xs[..., 1, :]
    return jnp.stack([x1 * cos - x2 * sin, x1 * sin + x2 * cos], axis=-2).reshape(x.shape)


def mla_mixer(p, m, q_norm, w_q_up, kv_norm, w_kv_up):
    l = p.shape[0]
    n = l - m
    cq, ckv, k_rope = p[:, :MLA_Q_LORA], p[:, MLA_Q_LORA:MLA_Q_LORA + MLA_KV_LORA], p[:, MLA_Q_LORA + MLA_KV_LORA:]
    q = pmatmul(_rms(cq, q_norm), w_q_up, tm=1408, tn=768, name="mla_q_up").reshape(l, MLA_HEADS, MLA_NOPE + MLA_ROPE)
    kv = pmatmul(_rms(ckv, kv_norm), w_kv_up, tm=1408, tn=1024, name="mla_kv_up").reshape(l, MLA_HEADS, MLA_NOPE + MLA_V)
    k_nope, v = kv[..., :MLA_NOPE], kv[..., MLA_NOPE:]
    cos, sin = _rope_tables(n)
    q_nope, q_rope = q[..., :MLA_NOPE], q[..., MLA_NOPE:]
    q_rope = jnp.concatenate([q_rope[:m], _rope(q_rope[m:], cos[:, None], sin[:, None])], axis=0)
    k_rope = jnp.concatenate([k_rope[:m], _rope(k_rope[m:], cos, sin)], axis=0)
    scale = (MLA_NOPE + MLA_ROPE) ** -0.5
    qh = (jnp.concatenate([q_nope, q_rope], axis=-1) * scale).astype(BF16).transpose(1, 0, 2)
    kh = jnp.concatenate([k_nope, jnp.broadcast_to(k_rope[:, None, :], (l, MLA_HEADS, MLA_ROPE))], axis=-1)
    kh = kh.astype(BF16).transpose(1, 0, 2)
    v_ext = jnp.concatenate([v, jnp.ones((l, MLA_HEADS, 1), F32), jnp.zeros((l, MLA_HEADS, MLA_V - 1), F32)], axis=-1)
    vh = v_ext.astype(BF16).transpose(1, 0, 2)
    o_ctx = flash_attention(qh[:, :m], kh[:, :m], vh[:, :m], dv=MLA_V, tq=m, name="mla_attn_ctx")
    o_lat = flash_attention(qh[:, m:], kh, vh, dv=MLA_V, tq=512, name="mla_attn_lat")
    return jnp.concatenate([o_ctx, o_lat], axis=0)


def _shift_mix(z, m, mu_prev, mu_next):
    def one(zp):
        zprev = jnp.pad(zp[:-1], ((1, 0), (0, 0)))
        znext = jnp.pad(zp[1:], ((0, 1), (0, 0)))
        return zp + mu_prev * (zprev - zp) + mu_next * (znext - zp)
    return jnp.concatenate([one(z[:m]), one(z[m:])], axis=0)


def rwkv7_mixer(p, m, mu, w0, w_up, a0, a_up, k_k, k_a, u, g_up, ln_g, ln_b):
    l = p.shape[0]
    p = _shift_mix(p, m, mu[0], mu[1])
    r, k, v, w_dn, a_dn, g_dn = jnp.split(p, RWKV_SPLITS, axis=-1)
    hd = lambda t: t.reshape(l, RWKV_HEADS, RWKV_HEAD)
    r_h, v_h = hd(r), hd(v)
    lws, kmods, kks, bs, bonus = [], [], [], [], 0.0
    for d in range(2):
        w_lora = pmatmul(jnp.tanh(w_dn[:, d * RWKV_W_LORA:(d + 1) * RWKV_W_LORA]).astype(BF16), w_up, (d,),
                         tm=1408, tn=GROUP_W, name="rwkv_w_up")
        w = w0[d] + w_lora
        lws.append(-jnp.exp(-jax.nn.softplus(-w) - 0.5))
        a_lora = pmatmul(a_dn[:, d * RWKV_A_LORA:(d + 1) * RWKV_A_LORA].astype(BF16), a_up, (d,),
                         tm=1408, tn=GROUP_W, name="rwkv_a_up")
        a = jax.nn.sigmoid(a0[d] + a_lora)
        kk = hd(k * k_k[d])
        kk = kk * lax.rsqrt(jnp.maximum(jnp.sum(kk * kk, axis=-1, keepdims=True), 1e-24))
        k_mod = k * (1 + (a - 1) * k_a[d])
        kmods.append(k_mod)
        kks.append(kk.reshape(l, GROUP_W))
        bs.append(kk.reshape(l, GROUP_W) * a)
        bonus = bonus + jnp.sum(r_h * hd(k_mod) * u[d].reshape(RWKV_HEADS, RWKV_HEAD), axis=-1, keepdims=True) * v_h
    o_f, o_b = rwkv_scan(r, v, jnp.stack(lws), jnp.stack(kmods), jnp.stack(kks), jnp.stack(bs), n_ctx=m)
    o = hd(o_f + o_b)
    mean = jnp.mean(o, axis=-1, keepdims=True)
    var = jnp.mean(jnp.square(o - mean), axis=-1, keepdims=True)
    o = ((o - mean) * lax.rsqrt(var + RWKV_LN_EPS)).reshape(l, GROUP_W) * ln_g + ln_b
    o = o + bonus.reshape(l, GROUP_W)
    g = pmatmul(jax.nn.sigmoid(g_dn).astype(BF16), g_up, tm=1408, tn=GROUP_W, name="rwkv_g_up")
    return o * g


def gla_mixer(p, m, a_up, a_bias, g_norm):
    l = p.shape[0]
    q, k, v, a_dn, r = jnp.split(p, GLA_SPLITS, axis=-1)
    qh = _heads_major(q * GLA_DK ** -0.5, GLA_HEADS)[None]
    kh = _heads_major(k, GLA_HEADS)[None]
    vh = _heads_major(v, GLA_HEADS)[None]
    gs = []
    for d in range(2):
        logit = pmatmul(a_dn[:, d * GLA_GATE_RANK:(d + 1) * GLA_GATE_RANK].astype(BF16), a_up, (d,),
                        tm=1408, tn=GLA_KD, name="gla_gate_up") + a_bias[d]
        gs.append(_heads_major(jax.nn.log_sigmoid(logit) / GLA_GATE_NORM, GLA_HEADS))
    o2 = gla_scan(qh, kh, vh, jnp.stack(gs), n_ctx=m, name="gla_scan")
    o = (o2[0] + o2[1]).transpose(1, 0, 2)
    o = _rms(o, g_norm).reshape(l, GROUP_W)
    return o * jax.nn.silu(r)


def hgrn2_mixer(p, m, lb, g_norm):
    l = p.shape[0]
    q, f_fwd, f_bwd, i, g = jnp.split(p, 5, axis=-1)
    qh = _heads_major(jax.nn.silu(q), HGRN_HEADS)[None]
    ih = _heads_major(i, HGRN_HEADS)[None]
    ks, gs = [], []
    for d, f_raw in enumerate((f_fwd, f_bwd)):
        lb_d = lb[d]
        log_f = jnp.logaddexp(jnp.log(lb_d), jnp.log1p(-lb_d) + jax.nn.log_sigmoid(f_raw))
        ks.append(_heads_major((1.0 - lb_d) * jax.nn.sigmoid(-f_raw), HGRN_HEADS))
        gs.append(_heads_major(log_f, HGRN_HEADS))
    o2 = gla_scan(qh, jnp.stack(ks), ih, jnp.stack(gs), n_ctx=m, name="hgrn_scan")
    o = (o2[0] + o2[1]).transpose(1, 0, 2)
    o = _rms(o, g_norm).reshape(l, GROUP_W)
    return o * jax.nn.silu(g)


def dense_ffn(xs, h2, w1, w3, w2, j, gates2, m):
    act = pmatmul(h2, w1, (j,), w3=w3, tm=1408, tn=512, out_dtype=BF16, name="ffn_up")
    return pmatmul(act, w2, (j,), tm=768, tn=256, res=xs, gates=gates2, m_ctx=m, name="ffn_down")


def _route(top_idx, weights):
    n_pairs = top_idx.size
    r = n_pairs + N_EXPERTS * MOE_TILE
    n_tiles = r // MOE_TILE
    e_flat = top_idx.reshape(-1)
    onehot = (e_flat[:, None] == jnp.arange(N_EXPERTS)[None, :]).astype(jnp.int32)
    rank = jnp.sum((jnp.cumsum(onehot, axis=0) - onehot) * onehot, axis=1)
    counts = jnp.sum(onehot, axis=0)
    padded = (counts + MOE_TILE - 1) // MOE_TILE * MOE_TILE
    ends = jnp.cumsum(padded)
    dest = (ends - padded)[e_flat] + rank
    src_token = jnp.zeros((r,), jnp.int32).at[dest].set(jnp.arange(n_pairs, dtype=jnp.int32) // TOP_K)
    row_gate = jnp.zeros((r,), F32).at[dest].set(weights.reshape(-1))
    tile_start = jnp.arange(n_tiles, dtype=jnp.int32) * MOE_TILE
    tile_expert = jnp.minimum(jnp.sum(tile_start[:, None] >= ends[None, :], axis=1), N_EXPERTS - 1)
    meta = jnp.concatenate([tile_expert.astype(jnp.int32), (ends[-1:] // MOE_TILE).astype(jnp.int32)])
    return src_token, row_gate, dest.reshape(top_idx.shape), meta


def moe_ffn(xs, h2f, router, w1, w3, w2, j, gates2, m):
    l, d = h2f.shape
    router_p = jnp.pad(router[j], ((0, 0), (0, 128 - N_EXPERTS)))
    logits = pmatmul(h2f, router_p, tm=768, tn=128, precision=lax.Precision.HIGHEST, name="moe_router")[:, :N_EXPERTS]
    top_vals, top_idx = lax.top_k(logits, TOP_K)
    weights = jax.nn.softmax(top_vals, axis=-1)
    src_token, row_gate, dest, meta = _route(top_idx, weights)
    x_sorted = jnp.take(h2f.astype(BF16), src_token, axis=0)
    act = gmatmul(meta, x_sorted, w1, j, w3=w3, tn=512, out_dtype=BF16, rowscale=row_gate[:, None], name="moe_up")
    y = gmatmul(meta, act, w2, j, tn=512, out_dtype=F32, name="moe_down")
    f = jnp.take(y, dest[:, 0], axis=0) + jnp.take(y, dest[:, 1], axis=0)
    gate_rows = jnp.where(jnp.arange(l)[:, None] < m, gates2[0:1], gates2[1:2])
    return xs + gate_rows * f


def kernel(x, c, ctx, c_ctx, norm1_g, norm2_g, w_mod, b_mod, w_in, w_out, mla_q_norm, mla_w_q_up, mla_kv_norm, mla_w_kv_up, rwkv_mu, rwkv_w0, rwkv_w_up, rwkv_a0, rwkv_a_up, rwkv_k_k, rwkv_k_a, rwkv_u, rwkv_g_up, rwkv_ln_g, rwkv_ln_b, gla_a_up, gla_a_bias, gla_norm, hgrn_lb, hgrn_norm, ffn_w1, ffn_w3, ffn_w2, moe_router, moe_w1, moe_w3, moe_w2, final_norm_g):
    m, n, d = ctx.shape[1], x.shape[1], x.shape[2]
    lb_all = jnp.cumsum(jax.nn.softmax(hgrn_lb.astype(F32), axis=1), axis=1)
    lb_all = lb_all - lb_all[:, :1]
    xs = jnp.concatenate([ctx[0], x[0]], axis=0)
    cvec = jnp.zeros((16, d), F32).at[0].set(jax.nn.silu(c[0])).at[1].set(jax.nn.silu(c_ctx))
    for l in range(DEPTH):
        mod = pmatmul(cvec, w_mod, (l,), tm=16, tn=1024, name="mod") + b_mod[l]
        mods = jnp.stack([mod[1].reshape(6, d), mod[0].reshape(6, d)], axis=1)
        sh1, sc1, gt1, sh2, sc2, gt2 = (mods[i] for i in range(6))

        h = norm_mod(xs, norm1_g[l], sh1, sc1, m_ctx=m, out_dtype=BF16)
        p = pmatmul(h, w_in, (l,), tm=1408, tn=1024, name="w_in")
        p_mla, p_rwkv, p_gla, p_hgrn = jnp.split(p, IN_SPLITS, axis=-1)
        o = jnp.concatenate([
            mla_mixer(p_mla, m, mla_q_norm[l], mla_w_q_up[l], mla_kv_norm[l], mla_w_kv_up[l]),
            rwkv7_mixer(p_rwkv, m, rwkv_mu[l], rwkv_w0[l], rwkv_w_up[l], rwkv_a0[l], rwkv_a_up[l],
                        rwkv_k_k[l], rwkv_k_a[l], rwkv_u[l], rwkv_g_up[l], rwkv_ln_g[l], rwkv_ln_b[l]),
            gla_mixer(p_gla, m, gla_a_up[l], gla_a_bias[l], gla_norm[l]),
            hgrn2_mixer(p_hgrn, m, lb_all[:, l], hgrn_norm[l]),
        ], axis=-1).astype(BF16)
        xs = pmatmul(o, w_out, (l,), tm=1408, tn=512, res=xs, gates=gt1, m_ctx=m, name="w_out")

        j = l // 2
        if l % 2 == 0:
            h2 = norm_mod(xs, norm2_g[l], sh2, sc2, m_ctx=m, out_dtype=BF16)
            xs = dense_ffn(xs, h2, ffn_w1, ffn_w3, ffn_w2, j, gt2, m)
        else:
            h2f = norm_mod(xs, norm2_g[l], sh2, sc2, m_ctx=m, out_dtype=F32)
            xs = moe_ffn(xs, h2f, moe_router, moe_w1, moe_w3, moe_w2, j, gt2, m)
    zeros2 = jnp.zeros((2, d), F32)
    out = norm_mod(xs, final_norm_g, zeros2, zeros2, m_ctx=m, out_dtype=F32, name="final_norm")
    return out[m:][None]
```

```python
import functools
import math

import numpy as np
import jax
import jax.numpy as jnp
from jax import lax
from jax.experimental import pallas as pl
from jax.experimental.pallas import tpu as pltpu

F32 = jnp.float32
BF16 = jnp.bfloat16

DEPTH = 4
GRID_W = 64
EPS = 1e-6
GROUP_W = 512
MLA_HEADS, MLA_NOPE, MLA_ROPE, MLA_V = 4, 128, 64, 128
MLA_Q_LORA, MLA_KV_LORA = 384, 256
ROPE_BASE = 10000.0
RWKV_HEADS, RWKV_HEAD = 8, 64
RWKV_W_LORA, RWKV_A_LORA, RWKV_G_LORA = 64, 64, 128
RWKV_LN_EPS = 64e-5
GLA_HEADS, GLA_DK, GLA_DV = 4, 64, 128
GLA_GATE_RANK, GLA_GATE_NORM = 16, 16.0
HGRN_HEADS, HGRN_EXPAND, HGRN_DV = 4, 128, 128
CHUNK = 64
N_EXPERTS, TOP_K = 8, 2
MLA_COLS = MLA_Q_LORA + MLA_KV_LORA + MLA_ROPE
RWKV_COLS = 3 * GROUP_W + 2 * RWKV_W_LORA + 2 * RWKV_A_LORA + RWKV_G_LORA
GLA_KD = GLA_HEADS * GLA_DK
GLA_COLS = 2 * GLA_KD + GROUP_W + 2 * GLA_GATE_RANK + GROUP_W
IN_SPLITS = (MLA_COLS, MLA_COLS + RWKV_COLS, MLA_COLS + RWKV_COLS + GLA_COLS)
RWKV_SPLITS = (GROUP_W, 2 * GROUP_W, 3 * GROUP_W, 3 * GROUP_W + 2 * RWKV_W_LORA,
               3 * GROUP_W + 2 * RWKV_W_LORA + 2 * RWKV_A_LORA)
GLA_SPLITS = (GLA_KD, 2 * GLA_KD, 2 * GLA_KD + GROUP_W, 2 * GLA_KD + GROUP_W + 2 * GLA_GATE_RANK)

V7X_VMEM_LIMIT = 56 * 1024 * 1024
V7X_MXU = 256
N_LEVELS = 6
RWKV_PACK = V7X_MXU // RWKV_HEAD
MOE_TILE = 512


def _cparams(sem, vmem=V7X_VMEM_LIMIT):
    return pltpu.CompilerParams(dimension_semantics=sem, vmem_limit_bytes=vmem)


def _dot(a, b):
    return jnp.dot(a.astype(BF16), b.astype(BF16), preferred_element_type=F32)


def _dot_nt(a, b):
    return lax.dot_general(a.astype(BF16), b.astype(BF16), (((1,), (1,)), ((), ())), preferred_element_type=F32)


def _split2(x):
    hi = x.astype(BF16)
    lo = (x - hi.astype(F32)).astype(BF16)
    return hi, lo


def _dot_exact_lhs(m_bf16, x):
    hi, lo = _split2(x)
    return (jnp.dot(m_bf16, hi, preferred_element_type=F32) + jnp.dot(m_bf16, lo, preferred_element_type=F32))


def _mm_kernel(*refs, n_x, n_w, has_res, m_ctx, tm, precision):
    it = iter(refs)
    x_refs = [next(it) for _ in range(n_x)]
    w_refs = [next(it) for _ in range(n_w)]
    res_ref = next(it) if has_res else None
    gate_ref = next(it) if has_res else None
    o_ref = next(it)
    wb_refs = [next(it) for _ in range(n_w)] if precision is None else w_refs
    i = pl.program_id(1)

    if precision is None:
        @pl.when(i == 0)
        def _():
            for w_ref, wb_ref in zip(w_refs, wb_refs):
                wb_ref[...] = w_ref[...].astype(BF16)
        x = jnp.concatenate([x_ref[...].astype(BF16) for x_ref in x_refs], axis=1)
    else:
        x = x_refs[0][...]

    acc = jnp.dot(x, wb_refs[0][...], preferred_element_type=F32, precision=precision)
    if n_w == 2:
        acc3 = jnp.dot(x, wb_refs[1][...], preferred_element_type=F32)
        acc = acc * jax.nn.sigmoid(acc) * acc3
    if has_res:
        rows = i * tm + lax.broadcasted_iota(jnp.int32, acc.shape, 0)
        g = jnp.where(rows < m_ctx, gate_ref[0:1, :], gate_ref[1:2, :])
        acc = res_ref[...] + g * acc
    o_ref[...] = acc.astype(o_ref.dtype)


def pmatmul(x, w, widx=(), *, w3=None, tm, tn, out_dtype=F32, res=None, gates=None, m_ctx=0, precision=None,
            name="mm"):
    xs = list(x) if isinstance(x, (list, tuple)) else [x]
    m = xs[0].shape[0]
    k = sum(xi.shape[1] for xi in xs)
    n = w.shape[-1]
    tm = math.gcd(m, tm)
    assert w.shape[-2] == k and tm % 16 == 0
    nj, ni = pl.cdiv(n, tn), m // tm
    lead = (None,) * len(widx)
    w_spec = pl.BlockSpec(lead + (k, tn), lambda j, i: tuple(widx) + (0, j))
    ws = [w] if w3 is None else [w, w3]
    in_specs = [pl.BlockSpec((tm, xi.shape[1]), lambda j, i: (i, 0)) for xi in xs] + [w_spec] * len(ws)
    args = xs + ws
    if res is not None:
        in_specs += [pl.BlockSpec((tm, tn), lambda j, i: (i, j)), pl.BlockSpec((2, tn), lambda j, i: (0, j))]
        args += [res, gates]
    kern = functools.partial(_mm_kernel, n_x=len(xs), n_w=len(ws), has_res=res is not None, m_ctx=m_ctx, tm=tm,
                             precision=precision)
    return pl.pallas_call(
        kern,
        out_shape=jax.ShapeDtypeStruct((m, n), out_dtype),
        grid=(nj, ni),
        in_specs=in_specs,
        out_specs=pl.BlockSpec((tm, tn), lambda j, i: (i, j)),
        scratch_shapes=[pltpu.VMEM((k, tn), BF16) for _ in ws] if precision is None else [],
        compiler_params=_cparams(("arbitrary", "arbitrary")),
        name=name,
    )(*args)


def _gmm_kernel(meta_ref, x_ref, *refs, n_w, has_rowscale, n_tiles):
    it = iter(refs)
    w_refs = [next(it) for _ in range(n_w)]
    rs_ref = next(it) if has_rowscale else None
    o_ref = next(it)
    wb_refs = [next(it) for _ in range(n_w)]
    t = pl.program_id(1)
    e = meta_ref[t]
    e_prev = meta_ref[jnp.maximum(t - 1, 0)]

    @pl.when((t == 0) | (e != e_prev))
    def _():
        for w_ref, wb_ref in zip(w_refs, wb_refs):
            wb_ref[...] = w_ref[...].astype(BF16)

    @pl.when(t < meta_ref[n_tiles])
    def _():
        x = x_ref[...].astype(BF16)
        acc = jnp.dot(x, wb_refs[0][...], preferred_element_type=F32)
        if n_w == 2:
            acc3 = jnp.dot(x, wb_refs[1][...], preferred_element_type=F32)
            acc = acc * jax.nn.sigmoid(acc) * acc3
        if has_rowscale:
            acc = acc * rs_ref[...]
        o_ref[...] = acc.astype(o_ref.dtype)

    @pl.when(t >= meta_ref[n_tiles])
    def _():
        o_ref[...] = jnp.zeros(o_ref.shape, o_ref.dtype)


def gmatmul(meta, x, w, jl, *, w3=None, tn, out_dtype, rowscale=None, name):
    r, k = x.shape
    n = w.shape[-1]
    n_tiles = r // MOE_TILE
    ws = [w] if w3 is None else [w, w3]
    w_spec = pl.BlockSpec((None, None, k, tn), lambda j, t, mr: (jl, mr[t], 0, j))
    in_specs = [pl.BlockSpec((MOE_TILE, k), lambda j, t, mr: (t, 0))] + [w_spec] * len(ws)
    args = [x] + ws
    if rowscale is not None:
        in_specs.append(pl.BlockSpec((MOE_TILE, 1), lambda j, t, mr: (t, 0)))
        args.append(rowscale)
    return pl.pallas_call(
        functools.partial(_gmm_kernel, n_w=len(ws), has_rowscale=rowscale is not None, n_tiles=n_tiles),
        out_shape=jax.ShapeDtypeStruct((r, n), out_dtype),
        grid_spec=pltpu.PrefetchScalarGridSpec(
            num_scalar_prefetch=1,
            grid=(n // tn, n_tiles),
            in_specs=in_specs,
            out_specs=pl.BlockSpec((MOE_TILE, tn), lambda j, t, mr: (t, j)),
            scratch_shapes=[pltpu.VMEM((k, tn), BF16) for _ in ws]),
        compiler_params=_cparams(("arbitrary", "arbitrary")),
        name=name,
    )(meta, *args)


def _norm_kernel(x_ref, g_ref, sh_ref, sc_ref, o_ref, *, m_ctx, tm):
    i = pl.program_id(0)
    x = x_ref[...]
    y = x * lax.rsqrt(jnp.mean(x * x, axis=-1, keepdims=True) + EPS) * g_ref[...]
    rows = i * tm + lax.broadcasted_iota(jnp.int32, x.shape, 0)
    is_ctx = rows < m_ctx
    sc = jnp.where(is_ctx, sc_ref[0:1, :], sc_ref[1:2, :])
    sh = jnp.where(is_ctx, sh_ref[0:1, :], sh_ref[1:2, :])
    o_ref[...] = (y * (1.0 + sc) + sh).astype(o_ref.dtype)


def norm_mod(x, g, shift2, scale2, *, m_ctx, out_dtype, tm=256, name="norm_mod"):
    m, d = x.shape
    return pl.pallas_call(
        functools.partial(_norm_kernel, m_ctx=m_ctx, tm=tm),
        out_shape=jax.ShapeDtypeStruct((m, d), out_dtype),
        grid=(m // tm,),
        in_specs=[pl.BlockSpec((tm, d), lambda i: (i, 0)), pl.BlockSpec((1, d), lambda i: (0, 0)),
                  pl.BlockSpec((2, d), lambda i: (0, 0)), pl.BlockSpec((2, d), lambda i: (0, 0))],
        out_specs=pl.BlockSpec((tm, d), lambda i: (i, 0)),
        compiler_params=_cparams(("parallel",)),
        name=name,
    )(x, g.reshape(1, d), shift2, scale2)


def _attn_kernel(q_ref, k_ref, v_ref, o_ref, *, tk, n_kv, dv):
    q = q_ref[...]
    m = acc = None
    for j in range(n_kv):
        kb = k_ref[j * tk:(j + 1) * tk, :]
        vb = v_ref[j * tk:(j + 1) * tk, :]
        s = lax.dot_general(q, kb, (((1,), (1,)), ((), ())), preferred_element_type=F32)
        m_blk = jnp.max(s, axis=-1, keepdims=True)
        m_new = m_blk if j == 0 else jnp.maximum(m, m_blk)
        pv = jnp.dot(jnp.exp(s - m_new).astype(BF16), vb, preferred_element_type=F32)
        acc = pv if j == 0 else jnp.exp(m - m_new) * acc + pv
        m = m_new
    o_ref[...] = (acc[:, :dv] / acc[:, dv:dv + 1]).astype(o_ref.dtype)


def _attn_kv_tile(lk, cap=1408):
    return max(t for t in range(128, cap + 1, 128) if lk % t == 0)


def flash_attention(q, k, v_ext, *, dv, tq, name="mla_attn"):
    h, lq, dqk = q.shape
    lk, dve = k.shape[1], v_ext.shape[2]
    tk = _attn_kv_tile(lk)
    assert lq % tq == 0
    return pl.pallas_call(
        functools.partial(_attn_kernel, tk=tk, n_kv=lk // tk, dv=dv),
        out_shape=jax.ShapeDtypeStruct((lq, h * dv), BF16),
        grid=(h, lq // tq),
        in_specs=[pl.BlockSpec((None, tq, dqk), lambda hh, i: (hh, i, 0)),
                  pl.BlockSpec((None, lk, dqk), lambda hh, i: (hh, 0, 0)),
                  pl.BlockSpec((None, lk, dve), lambda hh, i: (hh, 0, 0))],
        out_specs=pl.BlockSpec((tq, dv), lambda hh, i: (i, hh)),
        compiler_params=_cparams(("parallel", "parallel")),
        name=name,
    )(q, k, v_ext)


def _chunk_constants():
    c = CHUNK
    t = np.arange(c)
    tri = (t[None, :] <= t[:, None]).astype(np.float32)
    strict = (t[None, :] < t[:, None]).astype(np.float32)
    eye = np.eye(c, dtype=np.float32)
    seg, off = [], []
    for lv in range(N_LEVELS):
        s = c >> (lv + 1)
        blk = t // s
        same = blk[:, None] == blk[None, :]
        odd = (blk % 2 == 1)[:, None]
        seg.append(np.where(odd, same & (t[None, :] <= t[:, None]), same & (t[None, :] > t[:, None])).astype(np.float32))
        off.append((odd & (blk[None, :] == blk[:, None] - 1)).astype(np.float32))
    seg, off = np.stack(seg), np.stack(off)

    def both(a):
        return np.stack([a, a[..., ::-1, ::-1]])

    return {k: both(v) for k, v in dict(tri=tri, strict=strict, eye=eye, seg=seg, off=off).items()}


_CC = _chunk_constants()


def _chunk_pos(d, c, n_ctx_chunks, n_chunks):
    back = jnp.where(c < n_ctx_chunks, n_ctx_chunks - 1 - c, n_chunks + n_ctx_chunks - 1 - c)
    return jnp.where(d == 0, c, back)


def _log_sigmoid(x):
    return jnp.minimum(x, 0.0) - jnp.log1p(jnp.exp(-jnp.abs(x)))


def _gla_kernel(*refs, mode, dk, dv, pack, groups, n_in):
    ins = [refs[:n_in], refs[n_in:2 * n_in]]
    par_ref, mall_ref, off_ref, eye_ref, hmk_ref, hmv_ref, o0_ref, o1_ref, st_ref = refs[2 * n_in:]
    c = pl.program_id(0)

    @pl.when(c == 0)
    def _():
        st_ref[...] = jnp.zeros(st_ref.shape, F32)

    eye = eye_ref[...]
    hmk = hmk_ref[...]
    hmv = hmv_ref[...]
    wk, wv = pack * dk, pack * dv
    for d, o_ref in enumerate((o0_ref, o1_ref)):
        if mode == "hgrn":
            q_raw, f_raw, v = (r[...] for r in ins[d])
            par = par_ref[d]
            q = q_raw * jax.nn.sigmoid(q_raw)
            g = jnp.logaddexp(par[0:1], par[1:2] + _log_sigmoid(f_raw))
            k = par[2:3] * jax.nn.sigmoid(-f_raw)
        else:
            q_raw, k, v, a_dn = (r[...] for r in ins[d])
            par = par_ref[d]
            q = q_raw * dk ** -0.5
            g = _log_sigmoid(_dot(a_dn, par[0:128]) + par[128:129]) / GLA_GATE_NORM
        e_all = _dot_exact_lhs(mall_ref[d], g)
        bc = e_all[0:CHUNK]
        btot = jnp.sum(g, axis=0, keepdims=True)
        qhat = q * jnp.exp(bc)
        kt = k * jnp.exp(btot - bc)
        ebt = jnp.exp(btot)
        qw, kw = [q], [k]
        for lv in range(N_LEVELS):
            w = jnp.exp(e_all[(lv + 1) * CHUNK:(lv + 2) * CHUNK])
            qw.append(q * w)
            kw.append(k * w)
        for gi in range(groups):
            slk = slice(gi * wk, (gi + 1) * wk)
            slv = slice(gi * wv, (gi + 1) * wv)

            def stk(x):
                return (jnp.concatenate([x[:, slk]] * pack, axis=0) * hmk).astype(BF16)

            att = eye * _dot_nt(stk(qw[0]), stk(kw[0]))
            for lv in range(N_LEVELS):
                att = att + off_ref[d, lv] * _dot_nt(stk(qw[lv + 1]), stk(kw[lv + 1]))
            vs = jnp.concatenate([v[:, slv]] * pack, axis=0) * hmv
            st = st_ref[d, gi]
            o = _dot_nt(stk(qhat), st) + _dot(att, vs)
            o_ref[:, slv] = sum(o[h * CHUNK:(h + 1) * CHUNK] for h in range(pack))
            st_ref[d, gi] = st * ebt[:, slk] + _dot(vs.T, stk(kt))


def gla_scan(p, cols, par, *, mode, heads, dk, dv, pack, n_ctx, name):
    l = p.shape[0]
    groups = heads // pack
    nch, ncc = l // CHUNK, n_ctx // CHUNK
    nb = pack * CHUNK

    def col_spec(d, off, width):
        assert off % width == 0
        return pl.BlockSpec((CHUNK, width), lambda c: (_chunk_pos(d, c, ncc, nch), off // width))

    def const(a):
        return pl.BlockSpec(a.shape, lambda c: (0,) * a.ndim)

    bd = lambda a: np.kron(np.eye(pack, dtype=np.float32), a)
    mall = jnp.asarray(np.concatenate([_CC["tri"][:, None], _CC["seg"]], axis=1).reshape(2, -1, CHUNK), BF16)
    off = jnp.asarray(np.stack([np.stack([bd(_CC["off"][d, lv]) for lv in range(N_LEVELS)]) for d in range(2)]), F32)
    eye = jnp.asarray(np.eye(nb, dtype=np.float32))
    hmk = jnp.asarray(np.kron(np.eye(pack, dtype=np.float32), np.ones((CHUNK, dk), np.float32)))
    hmv = jnp.asarray(np.kron(np.eye(pack, dtype=np.float32), np.ones((CHUNK, dv), np.float32)))
    consts = [par, mall, off, eye, hmk, hmv]
    in_specs = [col_spec(d, o, w) for d in range(2) for (o, w) in cols[d]]
    out_spec = lambda d: pl.BlockSpec((CHUNK, heads * dv), lambda c: (_chunk_pos(d, c, ncc, nch), 0))
    return pl.pallas_call(
        functools.partial(_gla_kernel, mode=mode, dk=dk, dv=dv, pack=pack, groups=groups, n_in=len(cols[0])),
        out_shape=[jax.ShapeDtypeStruct((l, heads * dv), F32)] * 2,
        grid=(nch,),
        in_specs=in_specs + [const(a) for a in consts],
        out_specs=[out_spec(0), out_spec(1)],
        scratch_shapes=[pltpu.VMEM((2, groups, pack * dv, pack * dk), F32)],
        compiler_params=_cparams(("arbitrary",)),
        name=name,
    )(*([p] * len(in_specs)), *consts)


def _post_kernel(of_ref, ob_ref, gate_ref, gn_ref, o_ref, *, heads, dv):
    o = of_ref[...] + ob_ref[...]
    gate = gate_ref[...]
    gn = gn_ref[...]
    for h in range(heads):
        sl = slice(h * dv, (h + 1) * dv)
        oh = o[:, sl]
        y = oh * lax.rsqrt(jnp.mean(oh * oh, axis=-1, keepdims=True) + EPS) * gn
        gh = gate[:, sl]
        o_ref[:, sl] = (y * gh * jax.nn.sigmoid(gh)).astype(o_ref.dtype)


def mix_post(o_f, o_b, p, gate_off, g_norm, *, heads, dv, tm=768, name):
    l, w = o_f.shape
    tm = math.gcd(l, tm)
    assert gate_off % w == 0
    row = pl.BlockSpec((tm, w), lambda i: (i, 0))
    return pl.pallas_call(
        functools.partial(_post_kernel, heads=heads, dv=dv),
        out_shape=jax.ShapeDtypeStruct((l, w), BF16),
        grid=(l // tm,),
        in_specs=[row, row, pl.BlockSpec((tm, w), lambda i: (i, gate_off // w)), pl.BlockSpec((1, dv), lambda i: (0, 0))],
        out_specs=row,
        compiler_params=_cparams(("parallel",)),
        name=name,
    )(o_f, o_b, p, g_norm.reshape(1, dv))


def _rwkv_kernel(*refs, groups):
    (r0, v0, lw0, k0, kk0, b0, r1, v1, lw1, k1, kk1, b1,
     tri_ref, strict_ref, incl_ref, off_ref, eye_ref, hm_ref, o0_ref, o1_ref, st_ref) = refs
    c = pl.program_id(0)

    @pl.when(c == 0)
    def _():
        st_ref[...] = jnp.zeros(st_ref.shape, F32)

    hm = hm_ref[...]
    eye = eye_ref[...]
    w = RWKV_PACK * RWKV_HEAD
    nb = RWKV_PACK * CHUNK

    def stack(x):
        return jnp.concatenate([x] * RWKV_PACK, axis=0) * hm

    streams = []
    for d, (r_ref, v_ref, lw_ref, k_ref, kk_ref, b_ref, o_ref) in enumerate(
            ((r0, v0, lw0, k0, kk0, b0, o0_ref), (r1, v1, lw1, k1, kk1, b1, o1_ref))):
        lw = lw_ref[...]
        bc = _dot_exact_lhs(tri_ref[d], lw)
        ebt = jnp.exp(jnp.sum(lw, axis=0, keepdims=True))
        einv = jnp.exp(-bc)
        khat = kk_ref[...] * jnp.exp(bc - lw)
        rhat = r_ref[...] * jnp.exp(bc)
        ks = k_ref[...] * einv
        bs = b_ref[...] * einv
        v = v_ref[...]
        for g in range(groups):
            sl = slice(g * w, (g + 1) * w)
            streams.append(dict(
                d=d, g=g, sl=sl, o_ref=o_ref, ebt=ebt[:, sl],
                kr=jnp.concatenate([stack(khat[:, sl]), stack(rhat[:, sl])], axis=0).astype(BF16),
                bk=jnp.concatenate([stack(bs[:, sl]), stack(ks[:, sl])], axis=0).astype(BF16),
                vs=stack(v[:, sl])))

    for s in streams:
        aa = _dot_nt(s["kr"], s["bk"])
        strict, incl = strict_ref[s["d"]], incl_ref[s["d"]]
        s["akb"] = strict * aa[:nb, :nb]
        s["arb"] = (incl * aa[nb:, :nb]).astype(BF16)
        s["ak_v"] = jnp.concatenate([strict * aa[:nb, nb:], incl * aa[nb:, nb:]], axis=0).astype(BF16)
        s["minv"] = eye - off_ref[s["d"], N_LEVELS - 1] * s["akb"]
    for lv in range(N_LEVELS - 2, -1, -1):
        for s in streams:
            s["t1"] = _dot(off_ref[s["d"], lv] * s["akb"], s["minv"])
        for s in streams:
            s["minv"] = s["minv"] - _dot(s["minv"], s["t1"])
    for s in streams:
        st = st_ref[s["d"], s["g"]]
        from_state = _dot_nt(s["kr"], st)
        from_v = _dot(s["ak_v"], s["vs"])
        u = _dot(s["minv"], from_state[:nb] + from_v[:nb])
        o = from_state[nb:] + from_v[nb:] - _dot(s["arb"], u)
        s["o_ref"][:, s["sl"]] = sum(o[h * CHUNK:(h + 1) * CHUNK] for h in range(RWKV_PACK))
        upd = _dot(s["vs"].T, s["bk"][nb:]) - _dot(u.T, s["bk"][:nb])
        st_ref[s["d"], s["g"]] = (st + upd) * s["ebt"]


def rwkv_scan(r, v, lw, k, kk, b, *, n_ctx, name="rwkv_scan"):
    _, l, hw = lw.shape
    w = RWKV_PACK * RWKV_HEAD
    groups = hw // w
    nch, ncc = l // CHUNK, n_ctx // CHUNK
    nb = RWKV_PACK * CHUNK

    def shared(d):
        return pl.BlockSpec((CHUNK, hw), lambda c: (_chunk_pos(d, c, ncc, nch), 0))

    def perdir(d):
        return pl.BlockSpec((None, CHUNK, hw), lambda c: (d, _chunk_pos(d, c, ncc, nch), 0))

    def const(a):
        return pl.BlockSpec(a.shape, lambda c: (0,) * a.ndim)

    bd = lambda a: np.kron(np.eye(RWKV_PACK, dtype=np.float32), a)
    strict = jnp.asarray(np.stack([bd(_CC["strict"][d]) for d in range(2)]), F32)
    incl = jnp.asarray(np.stack([bd(_CC["strict"][d] + _CC["eye"][d]) for d in range(2)]), F32)
    off = jnp.asarray(np.stack([np.stack([bd(_CC["off"][d, lv]) for lv in range(N_LEVELS)]) for d in range(2)]), F32)
    eye = jnp.asarray(np.eye(nb, dtype=np.float32))
    hm = jnp.asarray(np.kron(np.eye(RWKV_PACK, dtype=np.float32), np.ones((CHUNK, RWKV_HEAD), np.float32)))
    tri = jnp.asarray(_CC["tri"], BF16)
    consts = [tri, strict, incl, off, eye, hm]
    in_specs = []
    args = []
    for d in range(2):
        in_specs += [shared(d), shared(d), perdir(d), perdir(d), perdir(d), perdir(d)]
        args += [r, v, lw, k, kk, b]
    return pl.pallas_call(
        functools.partial(_rwkv_kernel, groups=groups),
        out_shape=[jax.ShapeDtypeStruct((l, hw), F32)] * 2,
        grid=(nch,),
        in_specs=in_specs + [const(a) for a in consts],
        out_specs=[shared(0), shared(1)],
        scratch_shapes=[pltpu.VMEM((2, groups, nb, nb), F32)],
        compiler_params=_cparams(("arbitrary",)),
        name=name,
    )(*args, *consts)


def _rms(x, g, eps=EPS):
    return x * lax.rsqrt(jnp.mean(x * x, axis=-1, keepdims=True) + eps) * g


def _heads_major(t, h):
    l = t.shape[0]
    return t.reshape(l, h, -1).transpose(1, 0, 2)


def _rope_tables(n):
    rows = n // GRID_W
    row = jnp.repeat(jnp.arange(rows, dtype=F32), GRID_W)
    col = jnp.tile(jnp.arange(GRID_W, dtype=F32), rows)
    n_freq = MLA_ROPE // 4
    freqs = ROPE_BASE ** (-jnp.arange(n_freq, dtype=F32) / n_freq)
    ang = jnp.stack([row[:, None] * freqs, col[:, None] * freqs], axis=1)
    return jnp.cos(ang), jnp.sin(ang)


def _rope(x, cos, sin):
    xs = x.reshape(x.shape[:-1] + (2, 2, MLA_ROPE // 4))
    x1, x2 = xs[..., 0, :], xs[..., 1, :]
    return jnp.stack([x1 * cos - x2 * sin, x1 * sin + x2 * cos], axis=-2).reshape(x.shape)


def mla_mixer(p, m, q_norm, w_q_up, kv_norm, w_kv_up):
    l = p.shape[0]
    n = l - m
    cq, ckv, k_rope = _pcol(p, "cq"), _pcol(p, "ckv"), _pcol(p, "k_rope")
    q = pmatmul(_rms(cq, q_norm), w_q_up, tm=1408, tn=768, name="mla_q_up").reshape(l, MLA_HEADS, MLA_NOPE + MLA_ROPE)
    kv = pmatmul(_rms(ckv, kv_norm), w_kv_up, tm=1408, tn=1024, name="mla_kv_up").reshape(l, MLA_HEADS, MLA_NOPE + MLA_V)
    k_nope, v = kv[..., :MLA_NOPE], kv[..., MLA_NOPE:]
    cos, sin = _rope_tables(n)
    q_nope, q_rope = q[..., :MLA_NOPE], q[..., MLA_NOPE:]
    q_rope = jnp.concatenate([q_rope[:m], _rope(q_rope[m:], cos[:, None], sin[:, None])], axis=0)
    k_rope = jnp.concatenate([k_rope[:m], _rope(k_rope[m:], cos, sin)], axis=0)
    scale = (MLA_NOPE + MLA_ROPE) ** -0.5
    qh = (jnp.concatenate([q_nope, q_rope], axis=-1) * scale).astype(BF16).transpose(1, 0, 2)
    kh = jnp.concatenate([k_nope, jnp.broadcast_to(k_rope[:, None, :], (l, MLA_HEADS, MLA_ROPE))], axis=-1)
    kh = kh.astype(BF16).transpose(1, 0, 2)
    v_ext = jnp.concatenate([v, jnp.ones((l, MLA_HEADS, 1), F32), jnp.zeros((l, MLA_HEADS, MLA_V - 1), F32)], axis=-1)
    vh = v_ext.astype(BF16).transpose(1, 0, 2)
    o_ctx = flash_attention(qh[:, :m], kh[:, :m], vh[:, :m], dv=MLA_V, tq=m, name="mla_attn_ctx")
    o_lat = flash_attention(qh[:, m:], kh, vh, dv=MLA_V, tq=512, name="mla_attn_lat")
    return jnp.concatenate([o_ctx, o_lat], axis=0)


def _shift_mix(z, m, mu_prev, mu_next):
    def one(zp):
        zprev = jnp.pad(zp[:-1], ((1, 0), (0, 0)))
        znext = jnp.pad(zp[1:], ((0, 1), (0, 0)))
        return zp + mu_prev * (zprev - zp) + mu_next * (znext - zp)
    return jnp.concatenate([one(z[:m]), one(z[m:])], axis=0)


def rwkv7_mixer(p, m, mu, w0, w_up, a0, a_up, k_k, k_a, u, g_up, ln_g, ln_b):
    l = p.shape[0]
    p = jnp.concatenate([_pcol(p, nm) for nm in ("rwkv_r", "rwkv_k", "rwkv_v", "w_dn", "a_dn", "g_dn")], axis=1)
    p = _shift_mix(p, m, mu[0], mu[1])
    r, k, v, w_dn, a_dn, g_dn = jnp.split(p, RWKV_SPLITS, axis=-1)
    hd = lambda t: t.reshape(l, RWKV_HEADS, RWKV_HEAD)
    r_h, v_h = hd(r), hd(v)
    lws, kmods, kks, bs, bonus = [], [], [], [], 0.0
    for d in range(2):
        w_lora = pmatmul(jnp.tanh(w_dn[:, d * RWKV_W_LORA:(d + 1) * RWKV_W_LORA]).astype(BF16), w_up, (d,),
                         tm=1408, tn=GROUP_W, name="rwkv_w_up")
        w = w0[d] + w_lora
        lws.append(-jnp.exp(-jax.nn.softplus(-w) - 0.5))
        a_lora = pmatmul(a_dn[:, d * RWKV_A_LORA:(d + 1) * RWKV_A_LORA].astype(BF16), a_up, (d,),
                         tm=1408, tn=GROUP_W, name="rwkv_a_up")
        a = jax.nn.sigmoid(a0[d] + a_lora)
        kk = hd(k * k_k[d])
        kk = kk * lax.rsqrt(jnp.maximum(jnp.sum(kk * kk, axis=-1, keepdims=True), 1e-24))
        k_mod = k * (1 + (a - 1) * k_a[d])
        kmods.append(k_mod)
        kks.append(kk.reshape(l, GROUP_W))
        bs.append(kk.reshape(l, GROUP_W) * a)
        bonus = bonus + jnp.sum(r_h * hd(k_mod) * u[d].reshape(RWKV_HEADS, RWKV_HEAD), axis=-1, keepdims=True) * v_h
    o_f, o_b = rwkv_scan(r, v, jnp.stack(lws), jnp.stack(kmods), jnp.stack(kks), jnp.stack(bs), n_ctx=m)
    o = hd(o_f + o_b)
    mean = jnp.mean(o, axis=-1, keepdims=True)
    var = jnp.mean(jnp.square(o - mean), axis=-1, keepdims=True)
    o = ((o - mean) * lax.rsqrt(var + RWKV_LN_EPS)).reshape(l, GROUP_W) * ln_g + ln_b
    o = o + bonus.reshape(l, GROUP_W)
    g = pmatmul(jax.nn.sigmoid(g_dn).astype(BF16), g_up, tm=1408, tn=GROUP_W, name="rwkv_g_up")
    return (o * g).astype(BF16)


PACKED = {}


def _build_packed():
    orig = dict(cq=(0, MLA_Q_LORA), ckv=(MLA_Q_LORA, MLA_KV_LORA), k_rope=(MLA_Q_LORA + MLA_KV_LORA, MLA_ROPE))
    b = MLA_COLS
    for i, nm in enumerate(("rwkv_r", "rwkv_k", "rwkv_v")):
        orig[nm] = (b + i * GROUP_W, GROUP_W)
    b += 3 * GROUP_W
    orig.update(w_dn=(b, 2 * RWKV_W_LORA), a_dn=(b + 2 * RWKV_W_LORA, 2 * RWKV_A_LORA),
                g_dn=(b + 2 * RWKV_W_LORA + 2 * RWKV_A_LORA, RWKV_G_LORA))
    b = IN_SPLITS[1]
    orig.update(gla_q=(b, GLA_KD), gla_k=(b + GLA_KD, GLA_KD), gla_v=(b + 2 * GLA_KD, GROUP_W),
                gla_a=(b + GLA_SPLITS[2], 2 * GLA_GATE_RANK), gla_r=(b + GLA_SPLITS[3], GROUP_W))
    b = IN_SPLITS[2]
    for i, nm in enumerate(("hgrn_q", "hgrn_f0", "hgrn_f1", "hgrn_i", "hgrn_g")):
        orig[nm] = (b + i * GROUP_W, GROUP_W)
    order = [(nm, 512) for nm in ("rwkv_r", "rwkv_k", "rwkv_v", "gla_v", "gla_r", "hgrn_q", "hgrn_f0", "hgrn_f1",
                                  "hgrn_i", "hgrn_g", "cq")]
    order += [(nm, 256) for nm in ("gla_q", "gla_k", "ckv")]
    order += [(nm, 128) for nm in ("w_dn", "a_dn", "g_dn", "gla_a", "k_rope")]
    off = 0
    for nm, wp in order:
        PACKED[nm] = (orig[nm][0], orig[nm][1], off, wp)
        off += wp
    return off


N_PACKED = _build_packed()


def pack_w_in(w):
    pieces = []
    for o, wd, _, wp in PACKED.values():
        pieces.append(w[:, o:o + wd])
        if wp > wd:
            pieces.append(jnp.zeros((w.shape[0], wp - wd), w.dtype))
    return jnp.concatenate(pieces, axis=1)


def _pcol(p, nm):
    _, wd, off, _ = PACKED[nm]
    return p[:, off:off + wd]


def _blk(nm):
    return (PACKED[nm][2], PACKED[nm][3])


def gla_mixer(p, m, a_up, a_bias, g_norm):
    par = jnp.zeros((2, 136, GLA_KD), F32)
    for d in range(2):
        par = par.at[d, d * GLA_GATE_RANK:(d + 1) * GLA_GATE_RANK].set(a_up[d]).at[d, 128].set(a_bias[d])
    cols = [[_blk("gla_q"), _blk("gla_k"), _blk("gla_v"), _blk("gla_a")]] * 2
    o_f, o_b = gla_scan(p, cols, par, mode="gla", heads=GLA_HEADS, dk=GLA_DK, dv=GLA_DV, pack=4, n_ctx=m,
                        name="gla_scan")
    return mix_post(o_f, o_b, p, PACKED["gla_r"][2], g_norm, heads=GLA_HEADS, dv=GLA_DV, name="gla_post")


def hgrn2_mixer(p, m, lb, g_norm):
    par = jnp.zeros((2, 8, GROUP_W), F32).at[:, 0].set(jnp.log(lb)).at[:, 1].set(jnp.log1p(-lb)).at[:, 2].set(1.0 - lb)
    cols = [[_blk("hgrn_q"), _blk("hgrn_f%d" % d), _blk("hgrn_i")] for d in range(2)]
    o_f, o_b = gla_scan(p, cols, par, mode="hgrn", heads=HGRN_HEADS, dk=HGRN_EXPAND, dv=HGRN_DV, pack=2, n_ctx=m,
                        name="hgrn_scan")
    return mix_post(o_f, o_b, p, PACKED["hgrn_g"][2], g_norm, heads=HGRN_HEADS, dv=HGRN_DV, name="hgrn_post")


def dense_ffn(xs, h2, w1, w3, w2, j, gates2, m):
    act = pmatmul(h2, w1, (j,), w3=w3, tm=1408, tn=512, out_dtype=BF16, name="ffn_up")
    return pmatmul(act, w2, (j,), tm=768, tn=256, res=xs, gates=gates2, m_ctx=m, name="ffn_down")


def _route(top_idx, weights):
    n_pairs = top_idx.size
    r = n_pairs + N_EXPERTS * MOE_TILE
    n_tiles = r // MOE_TILE
    e_flat = top_idx.reshape(-1)
    onehot = (e_flat[:, None] == jnp.arange(N_EXPERTS)[None, :]).astype(jnp.int32)
    rank = jnp.sum((jnp.cumsum(onehot, axis=0) - onehot) * onehot, axis=1)
    counts = jnp.sum(onehot, axis=0)
    padded = (counts + MOE_TILE - 1) // MOE_TILE * MOE_TILE
    ends = jnp.cumsum(padded)
    dest = (ends - padded)[e_flat] + rank
    src_token = jnp.zeros((r,), jnp.int32).at[dest].set(jnp.arange(n_pairs, dtype=jnp.int32) // TOP_K)
    row_gate = jnp.zeros((r,), F32).at[dest].set(weights.reshape(-1))
    tile_start = jnp.arange(n_tiles, dtype=jnp.int32) * MOE_TILE
    tile_expert = jnp.minimum(jnp.sum(tile_start[:, None] >= ends[None, :], axis=1), N_EXPERTS - 1)
    meta = jnp.concatenate([tile_expert.astype(jnp.int32), (ends[-1:] // MOE_TILE).astype(jnp.int32)])
    return src_token, row_gate, dest.reshape(top_idx.shape), meta


def moe_ffn(xs, h2f, router, w1, w3, w2, j, gates2, m):
    l, d = h2f.shape
    router_p = jnp.pad(router[j], ((0, 0), (0, 128 - N_EXPERTS)))
    logits = pmatmul(h2f, router_p, tm=768, tn=128, precision=lax.Precision.HIGHEST, name="moe_router")[:, :N_EXPERTS]
    top_vals, top_idx = lax.top_k(logits, TOP_K)
    weights = jax.nn.softmax(top_vals, axis=-1)
    src_token, row_gate, dest, meta = _route(top_idx, weights)
    x_sorted = jnp.take(h2f.astype(BF16), src_token, axis=0)
    act = gmatmul(meta, x_sorted, w1, j, w3=w3, tn=512, out_dtype=BF16, rowscale=row_gate[:, None], name="moe_up")
    y = gmatmul(meta, act, w2, j, tn=512, out_dtype=F32, name="moe_down")
    f = jnp.sum(jnp.take(y, dest.reshape(-1), axis=0).reshape(l, TOP_K, d), axis=1)
    gate_rows = jnp.where(jnp.arange(l)[:, None] < m, gates2[0:1], gates2[1:2])
    return xs + gate_rows * f


def kernel(x, c, ctx, c_ctx, norm1_g, norm2_g, w_mod, b_mod, w_in, w_out, mla_q_norm, mla_w_q_up, mla_kv_norm, mla_w_kv_up, rwkv_mu, rwkv_w0, rwkv_w_up, rwkv_a0, rwkv_a_up, rwkv_k_k, rwkv_k_a, rwkv_u, rwkv_g_up, rwkv_ln_g, rwkv_ln_b, gla_a_up, gla_a_bias, gla_norm, hgrn_lb, hgrn_norm, ffn_w1, ffn_w3, ffn_w2, moe_router, moe_w1, moe_w3, moe_w2, final_norm_g):
    m, n, d = ctx.shape[1], x.shape[1], x.shape[2]
    lb_all = jnp.cumsum(jax.nn.softmax(hgrn_lb.astype(F32), axis=1), axis=1)
    lb_all = lb_all - lb_all[:, :1]
    xs = jnp.concatenate([ctx[0], x[0]], axis=0)
    cvec = jnp.zeros((16, d), F32).at[0].set(jax.nn.silu(c[0])).at[1].set(jax.nn.silu(c_ctx))
    for l in range(DEPTH):
        mod = pmatmul(cvec, w_mod, (l,), tm=16, tn=1024, name="mod") + b_mod[l]
        mods = jnp.stack([mod[1].reshape(6, d), mod[0].reshape(6, d)], axis=1)
        sh1, sc1, gt1, sh2, sc2, gt2 = (mods[i] for i in range(6))

        h = norm_mod(xs, norm1_g[l], sh1, sc1, m_ctx=m, out_dtype=BF16)
        p = pmatmul(h, pack_w_in(w_in[l]), tm=1408, tn=640, name="w_in")
        o = [
            mla_mixer(p, m, mla_q_norm[l], mla_w_q_up[l], mla_kv_norm[l], mla_w_kv_up[l]),
            rwkv7_mixer(p, m, rwkv_mu[l], rwkv_w0[l], rwkv_w_up[l], rwkv_a0[l], rwkv_a_up[l],
                        rwkv_k_k[l], rwkv_k_a[l], rwkv_u[l], rwkv_g_up[l], rwkv_ln_g[l], rwkv_ln_b[l]),
            gla_mixer(p, m, gla_a_up[l], gla_a_bias[l], gla_norm[l]),
            hgrn2_mixer(p, m, lb_all[:, l], hgrn_norm[l]),
        ]
        xs = pmatmul(o, w_out, (l,), tm=1408, tn=512, res=xs, gates=gt1, m_ctx=m, name="w_out")

        j = l // 2
        if l % 2 == 0:
            h2 = norm_mod(xs, norm2_g[l], sh2, sc2, m_ctx=m, out_dtype=BF16)
            xs = dense_ffn(xs, h2, ffn_w1, ffn_w3, ffn_w2, j, gt2, m)
        else:
            h2f = norm_mod(xs, norm2_g[l], sh2, sc2, m_ctx=m, out_dtype=F32)
            xs = moe_ffn(xs, h2f, moe_router, moe_w1, moe_w3, moe_w2, j, gt2, m)
    zeros2 = jnp.zeros((2, d), F32)
    out = norm_mod(xs, final_norm_g, zeros2, zeros2, m_ctx=m, out_dtype=F32, name="final_norm")
    return out[m:][None]
```

```python
import functools
import math

import numpy as np
import jax
import jax.numpy as jnp
from jax import lax
from jax.experimental import pallas as pl
from jax.experimental.pallas import tpu as pltpu

F32 = jnp.float32
BF16 = jnp.bfloat16

DEPTH = 4
GRID_W = 64
EPS = 1e-6
GROUP_W = 512
MLA_HEADS, MLA_NOPE, MLA_ROPE, MLA_V = 4, 128, 64, 128
MLA_Q_LORA, MLA_KV_LORA = 384, 256
ROPE_BASE = 10000.0
RWKV_HEADS, RWKV_HEAD = 8, 64
RWKV_W_LORA, RWKV_A_LORA, RWKV_G_LORA = 64, 64, 128
RWKV_LN_EPS = 64e-5
GLA_HEADS, GLA_DK, GLA_DV = 4, 64, 128
GLA_GATE_RANK, GLA_GATE_NORM = 16, 16.0
HGRN_HEADS, HGRN_EXPAND, HGRN_DV = 4, 128, 128
CHUNK = 64
N_EXPERTS, TOP_K = 8, 2
MLA_COLS = MLA_Q_LORA + MLA_KV_LORA + MLA_ROPE
RWKV_COLS = 3 * GROUP_W + 2 * RWKV_W_LORA + 2 * RWKV_A_LORA + RWKV_G_LORA
GLA_KD = GLA_HEADS * GLA_DK
GLA_COLS = 2 * GLA_KD + GROUP_W + 2 * GLA_GATE_RANK + GROUP_W
IN_SPLITS = (MLA_COLS, MLA_COLS + RWKV_COLS, MLA_COLS + RWKV_COLS + GLA_COLS)
RWKV_SPLITS = (GROUP_W, 2 * GROUP_W, 3 * GROUP_W, 3 * GROUP_W + 2 * RWKV_W_LORA,
               3 * GROUP_W + 2 * RWKV_W_LORA + 2 * RWKV_A_LORA)
GLA_SPLITS = (GLA_KD, 2 * GLA_KD, 2 * GLA_KD + GROUP_W, 2 * GLA_KD + GROUP_W + 2 * GLA_GATE_RANK)

V7X_VMEM_LIMIT = 56 * 1024 * 1024
V7X_MXU = 256
N_LEVELS = 6
RWKV_PACK = V7X_MXU // RWKV_HEAD
MOE_TILE = 512


def _cparams(sem, vmem=V7X_VMEM_LIMIT):
    return pltpu.CompilerParams(dimension_semantics=sem, vmem_limit_bytes=vmem)


def _dot(a, b):
    return jnp.dot(a.astype(BF16), b.astype(BF16), preferred_element_type=F32)


def _dot_nt(a, b):
    return lax.dot_general(a.astype(BF16), b.astype(BF16), (((1,), (1,)), ((), ())), preferred_element_type=F32)


def _split2(x):
    hi = x.astype(BF16)
    lo = (x - hi.astype(F32)).astype(BF16)
    return hi, lo


def _dot_exact_lhs(m_bf16, x):
    hi, lo = _split2(x)
    return (jnp.dot(m_bf16, hi, preferred_element_type=F32) + jnp.dot(m_bf16, lo, preferred_element_type=F32))


def _mm_kernel(*refs, n_x, n_w, has_res, m_ctx, tm, precision):
    it = iter(refs)
    x_refs = [next(it) for _ in range(n_x)]
    w_refs = [next(it) for _ in range(n_w)]
    res_ref = next(it) if has_res else None
    gate_ref = next(it) if has_res else None
    o_ref = next(it)
    wb_refs = [next(it) for _ in range(n_w)] if precision is None else w_refs
    i = pl.program_id(1)

    if precision is None:
        @pl.when(i == 0)
        def _():
            for w_ref, wb_ref in zip(w_refs, wb_refs):
                wb_ref[...] = w_ref[...].astype(BF16)
        x = jnp.concatenate([x_ref[...].astype(BF16) for x_ref in x_refs], axis=1)
    else:
        x = x_refs[0][...]

    acc = jnp.dot(x, wb_refs[0][...], preferred_element_type=F32, precision=precision)
    if n_w == 2:
        acc3 = jnp.dot(x, wb_refs[1][...], preferred_element_type=F32)
        acc = acc * jax.nn.sigmoid(acc) * acc3
    if has_res:
        rows = i * tm + lax.broadcasted_iota(jnp.int32, acc.shape, 0)
        g = jnp.where(rows < m_ctx, gate_ref[0:1, :], gate_ref[1:2, :])
        acc = res_ref[...] + g * acc
    o_ref[...] = acc.astype(o_ref.dtype)


def pmatmul(x, w, widx=(), *, w3=None, tm, tn, out_dtype=F32, res=None, gates=None, m_ctx=0, precision=None,
            name="mm"):
    xs = list(x) if isinstance(x, (list, tuple)) else [x]
    m = xs[0].shape[0]
    k = sum(xi.shape[1] for xi in xs)
    n = w.shape[-1]
    tm = math.gcd(m, tm)
    assert w.shape[-2] == k and tm % 16 == 0
    nj, ni = pl.cdiv(n, tn), m // tm
    lead = (None,) * len(widx)
    w_spec = pl.BlockSpec(lead + (k, tn), lambda j, i: tuple(widx) + (0, j))
    ws = [w] if w3 is None else [w, w3]
    in_specs = [pl.BlockSpec((tm, xi.shape[1]), lambda j, i: (i, 0)) for xi in xs] + [w_spec] * len(ws)
    args = xs + ws
    if res is not None:
        in_specs += [pl.BlockSpec((tm, tn), lambda j, i: (i, j)), pl.BlockSpec((2, tn), lambda j, i: (0, j))]
        args += [res, gates]
    kern = functools.partial(_mm_kernel, n_x=len(xs), n_w=len(ws), has_res=res is not None, m_ctx=m_ctx, tm=tm,
                             precision=precision)
    return pl.pallas_call(
        kern,
        out_shape=jax.ShapeDtypeStruct((m, n), out_dtype),
        grid=(nj, ni),
        in_specs=in_specs,
        out_specs=pl.BlockSpec((tm, tn), lambda j, i: (i, j)),
        scratch_shapes=[pltpu.VMEM((k, tn), BF16) for _ in ws] if precision is None else [],
        compiler_params=_cparams(("arbitrary", "arbitrary")),
        name=name,
    )(*args)


def _gmm_kernel(meta_ref, x_ref, *refs, n_w, has_rowscale, n_tiles):
    it = iter(refs)
    w_refs = [next(it) for _ in range(n_w)]
    rs_ref = next(it) if has_rowscale else None
    o_ref = next(it)
    wb_refs = [next(it) for _ in range(n_w)]
    t = pl.program_id(1)
    e = meta_ref[t]
    e_prev = meta_ref[jnp.maximum(t - 1, 0)]

    @pl.when((t == 0) | (e != e_prev))
    def _():
        for w_ref, wb_ref in zip(w_refs, wb_refs):
            wb_ref[...] = w_ref[...].astype(BF16)

    @pl.when(t < meta_ref[n_tiles])
    def _():
        x = x_ref[...].astype(BF16)
        acc = jnp.dot(x, wb_refs[0][...], preferred_element_type=F32)
        if n_w == 2:
            acc3 = jnp.dot(x, wb_refs[1][...], preferred_element_type=F32)
            acc = acc * jax.nn.sigmoid(acc) * acc3
        if has_rowscale:
            acc = acc * rs_ref[...]
        o_ref[...] = acc.astype(o_ref.dtype)

    @pl.when(t >= meta_ref[n_tiles])
    def _():
        o_ref[...] = jnp.zeros(o_ref.shape, o_ref.dtype)


def gmatmul(meta, x, w, jl, *, w3=None, tn, out_dtype, rowscale=None, name):
    r, k = x.shape
    n = w.shape[-1]
    n_tiles = r // MOE_TILE
    ws = [w] if w3 is None else [w, w3]
    w_spec = pl.BlockSpec((None, None, k, tn), lambda j, t, mr: (jl, mr[t], 0, j))
    in_specs = [pl.BlockSpec((MOE_TILE, k), lambda j, t, mr: (t, 0))] + [w_spec] * len(ws)
    args = [x] + ws
    if rowscale is not None:
        in_specs.append(pl.BlockSpec((MOE_TILE, 1), lambda j, t, mr: (t, 0)))
        args.append(rowscale)
    return pl.pallas_call(
        functools.partial(_gmm_kernel, n_w=len(ws), has_rowscale=rowscale is not None, n_tiles=n_tiles),
        out_shape=jax.ShapeDtypeStruct((r, n), out_dtype),
        grid_spec=pltpu.PrefetchScalarGridSpec(
            num_scalar_prefetch=1,
            grid=(n // tn, n_tiles),
            in_specs=in_specs,
            out_specs=pl.BlockSpec((MOE_TILE, tn), lambda j, t, mr: (t, j)),
            scratch_shapes=[pltpu.VMEM((k, tn), BF16) for _ in ws]),
        compiler_params=_cparams(("arbitrary", "arbitrary")),
        name=name,
    )(meta, *args)


def _norm_kernel(x_ref, g_ref, sh_ref, sc_ref, o_ref, *, m_ctx, tm):
    i = pl.program_id(0)
    x = x_ref[...]
    y = x * lax.rsqrt(jnp.mean(x * x, axis=-1, keepdims=True) + EPS) * g_ref[...]
    rows = i * tm + lax.broadcasted_iota(jnp.int32, x.shape, 0)
    is_ctx = rows < m_ctx
    sc = jnp.where(is_ctx, sc_ref[0:1, :], sc_ref[1:2, :])
    sh = jnp.where(is_ctx, sh_ref[0:1, :], sh_ref[1:2, :])
    o_ref[...] = (y * (1.0 + sc) + sh).astype(o_ref.dtype)


def norm_mod(x, g, shift2, scale2, *, m_ctx, out_dtype, tm=256, name="norm_mod"):
    m, d = x.shape
    return pl.pallas_call(
        functools.partial(_norm_kernel, m_ctx=m_ctx, tm=tm),
        out_shape=jax.ShapeDtypeStruct((m, d), out_dtype),
        grid=(m // tm,),
        in_specs=[pl.BlockSpec((tm, d), lambda i: (i, 0)), pl.BlockSpec((1, d), lambda i: (0, 0)),
                  pl.BlockSpec((2, d), lambda i: (0, 0)), pl.BlockSpec((2, d), lambda i: (0, 0))],
        out_specs=pl.BlockSpec((tm, d), lambda i: (i, 0)),
        compiler_params=_cparams(("parallel",)),
        name=name,
    )(x, g.reshape(1, d), shift2, scale2)


def _attn_kernel(q_ref, k_ref, v_ref, o_ref, *, tk, n_kv, dv):
    q = q_ref[...]
    m = acc = None
    for j in range(n_kv):
        kb = k_ref[j * tk:(j + 1) * tk, :]
        vb = v_ref[j * tk:(j + 1) * tk, :]
        s = lax.dot_general(q, kb, (((1,), (1,)), ((), ())), preferred_element_type=F32)
        m_blk = jnp.max(s, axis=-1, keepdims=True)
        m_new = m_blk if j == 0 else jnp.maximum(m, m_blk)
        pv = jnp.dot(jnp.exp(s - m_new).astype(BF16), vb, preferred_element_type=F32)
        acc = pv if j == 0 else jnp.exp(m - m_new) * acc + pv
        m = m_new
    o_ref[...] = (acc[:, :dv] / acc[:, dv:dv + 1]).astype(o_ref.dtype)


def _attn_kv_tile(lk, cap=1408):
    return max(t for t in range(128, cap + 1, 128) if lk % t == 0)


def flash_attention(q, k, v_ext, *, dv, tq, name="mla_attn"):
    h, lq, dqk = q.shape
    lk, dve = k.shape[1], v_ext.shape[2]
    tk = _attn_kv_tile(lk)
    assert lq % tq == 0
    return pl.pallas_call(
        functools.partial(_attn_kernel, tk=tk, n_kv=lk // tk, dv=dv),
        out_shape=jax.ShapeDtypeStruct((lq, h * dv), BF16),
        grid=(h, lq // tq),
        in_specs=[pl.BlockSpec((None, tq, dqk), lambda hh, i: (hh, i, 0)),
                  pl.BlockSpec((None, lk, dqk), lambda hh, i: (hh, 0, 0)),
                  pl.BlockSpec((None, lk, dve), lambda hh, i: (hh, 0, 0))],
        out_specs=pl.BlockSpec((tq, dv), lambda hh, i: (i, hh)),
        compiler_params=_cparams(("parallel", "parallel")),
        name=name,
    )(q, k, v_ext)


def _chunk_constants():
    c = CHUNK
    t = np.arange(c)
    tri = (t[None, :] <= t[:, None]).astype(np.float32)
    strict = (t[None, :] < t[:, None]).astype(np.float32)
    eye = np.eye(c, dtype=np.float32)
    seg, off = [], []
    for lv in range(N_LEVELS):
        s = c >> (lv + 1)
        blk = t // s
        same = blk[:, None] == blk[None, :]
        odd = (blk % 2 == 1)[:, None]
        seg.append(np.where(odd, same & (t[None, :] <= t[:, None]), same & (t[None, :] > t[:, None])).astype(np.float32))
        off.append((odd & (blk[None, :] == blk[:, None] - 1)).astype(np.float32))
    seg, off = np.stack(seg), np.stack(off)

    def both(a):
        return np.stack([a, a[..., ::-1, ::-1]])

    return {k: both(v) for k, v in dict(tri=tri, strict=strict, eye=eye, seg=seg, off=off).items()}


_CC = _chunk_constants()


def _chunk_pos(d, c, n_ctx_chunks, n_chunks):
    back = jnp.where(c < n_ctx_chunks, n_ctx_chunks - 1 - c, n_chunks + n_ctx_chunks - 1 - c)
    return jnp.where(d == 0, c, back)


def _log_sigmoid(x):
    return jnp.minimum(x, 0.0) - jnp.log1p(jnp.exp(-jnp.abs(x)))


def _gla_kernel(*refs, mode, dk, dv, pack, groups, n_in):
    ins = [refs[:n_in], refs[n_in:2 * n_in]]
    par_ref, mall_ref, off_ref, eye_ref, hmk_ref, hmv_ref, o0_ref, o1_ref, st_ref = refs[2 * n_in:]
    c = pl.program_id(0)

    @pl.when(c == 0)
    def _():
        st_ref[...] = jnp.zeros(st_ref.shape, F32)

    eye = eye_ref[...]
    hmk = hmk_ref[...]
    hmv = hmv_ref[...]
    wk, wv = pack * dk, pack * dv
    for d, o_ref in enumerate((o0_ref, o1_ref)):
        if mode == "hgrn":
            q_raw, f_raw, v = (r[...] for r in ins[d])
            par = par_ref[d]
            q = q_raw * jax.nn.sigmoid(q_raw)
            g = jnp.logaddexp(par[0:1], par[1:2] + _log_sigmoid(f_raw))
            k = par[2:3] * jax.nn.sigmoid(-f_raw)
        else:
            q_raw, k, v, a_dn = (r[...] for r in ins[d])
            par = par_ref[d]
            q = q_raw * dk ** -0.5
            g = _log_sigmoid(_dot(a_dn, par[0:128]) + par[128:129]) / GLA_GATE_NORM
        e_all = _dot_exact_lhs(mall_ref[d], g)
        bc = e_all[0:CHUNK]
        btot = jnp.sum(g, axis=0, keepdims=True)
        qhat = q * jnp.exp(bc)
        kt = k * jnp.exp(btot - bc)
        ebt = jnp.exp(btot)
        qw, kw = [q], [k]
        for lv in range(N_LEVELS):
            w = jnp.exp(e_all[(lv + 1) * CHUNK:(lv + 2) * CHUNK])
            qw.append(q * w)
            kw.append(k * w)
        for gi in range(groups):
            slk = slice(gi * wk, (gi + 1) * wk)
            slv = slice(gi * wv, (gi + 1) * wv)

            def stk(x):
                return (jnp.concatenate([x[:, slk]] * pack, axis=0) * hmk).astype(BF16)

            att = eye * _dot_nt(stk(qw[0]), stk(kw[0]))
            for lv in range(N_LEVELS):
                att = att + off_ref[d, lv] * _dot_nt(stk(qw[lv + 1]), stk(kw[lv + 1]))
            vs = jnp.concatenate([v[:, slv]] * pack, axis=0) * hmv
            st = st_ref[d, gi]
            o = _dot_nt(stk(qhat), st) + _dot(att, vs)
            o_ref[:, slv] = sum(o[h * CHUNK:(h + 1) * CHUNK] for h in range(pack))
            st_ref[d, gi] = st * ebt[:, slk] + _dot(vs.T, stk(kt))


def gla_scan(p, cols, par, *, mode, heads, dk, dv, pack, n_ctx, name):
    l = p.shape[0]
    groups = heads // pack
    nch, ncc = l // CHUNK, n_ctx // CHUNK
    nb = pack * CHUNK

    def col_spec(d, off, width):
        assert off % width == 0
        return pl.BlockSpec((CHUNK, width), lambda c: (_chunk_pos(d, c, ncc, nch), off // width))

    def const(a):
        return pl.BlockSpec(a.shape, lambda c: (0,) * a.ndim)

    bd = lambda a: np.kron(np.eye(pack, dtype=np.float32), a)
    mall = jnp.asarray(np.concatenate([_CC["tri"][:, None], _CC["seg"]], axis=1).reshape(2, -1, CHUNK), BF16)
    off = jnp.asarray(np.stack([np.stack([bd(_CC["off"][d, lv]) for lv in range(N_LEVELS)]) for d in range(2)]), F32)
    eye = jnp.asarray(np.eye(nb, dtype=np.float32))
    hmk = jnp.asarray(np.kron(np.eye(pack, dtype=np.float32), np.ones((CHUNK, dk), np.float32)))
    hmv = jnp.asarray(np.kron(np.eye(pack, dtype=np.float32), np.ones((CHUNK, dv), np.float32)))
    consts = [par, mall, off, eye, hmk, hmv]
    in_specs = [col_spec(d, o, w) for d in range(2) for (o, w) in cols[d]]
    out_spec = lambda d: pl.BlockSpec((CHUNK, heads * dv), lambda c: (_chunk_pos(d, c, ncc, nch), 0))
    return pl.pallas_call(
        functools.partial(_gla_kernel, mode=mode, dk=dk, dv=dv, pack=pack, groups=groups, n_in=len(cols[0])),
        out_shape=[jax.ShapeDtypeStruct((l, heads * dv), F32)] * 2,
        grid=(nch,),
        in_specs=in_specs + [const(a) for a in consts],
        out_specs=[out_spec(0), out_spec(1)],
        scratch_shapes=[pltpu.VMEM((2, groups, pack * dv, pack * dk), F32)],
        compiler_params=_cparams(("arbitrary",)),
        name=name,
    )(*([p] * len(in_specs)), *consts)


def _post_kernel(of_ref, ob_ref, gate_ref, gn_ref, o_ref, *, heads, dv):
    o = of_ref[...] + ob_ref[...]
    gate = gate_ref[...]
    gn = gn_ref[...]
    for h in range(heads):
        sl = slice(h * dv, (h + 1) * dv)
        oh = o[:, sl]
        y = oh * lax.rsqrt(jnp.mean(oh * oh, axis=-1, keepdims=True) + EPS) * gn
        gh = gate[:, sl]
        o_ref[:, sl] = (y * gh * jax.nn.sigmoid(gh)).astype(o_ref.dtype)


def mix_post(o_f, o_b, p, gate_off, g_norm, *, heads, dv, tm=768, name):
    l, w = o_f.shape
    tm = math.gcd(l, tm)
    assert gate_off % w == 0
    row = pl.BlockSpec((tm, w), lambda i: (i, 0))
    return pl.pallas_call(
        functools.partial(_post_kernel, heads=heads, dv=dv),
        out_shape=jax.ShapeDtypeStruct((l, w), BF16),
        grid=(l // tm,),
        in_specs=[row, row, pl.BlockSpec((tm, w), lambda i: (i, gate_off // w)), pl.BlockSpec((1, dv), lambda i: (0, 0))],
        out_specs=row,
        compiler_params=_cparams(("parallel",)),
        name=name,
    )(o_f, o_b, p, g_norm.reshape(1, dv))


def _seg_sum(x, seg_bf16):
    hi, lo = _split2(x)
    return jnp.dot(hi, seg_bf16, preferred_element_type=F32) + jnp.dot(lo, seg_bf16, preferred_element_type=F32)


def _shift_mix_block(x, halo_prev, halo_next, mu, seg_start, seg_end):
    row = lax.broadcasted_iota(jnp.int32, x.shape, 0)
    first = jnp.where(seg_start, 0.0, halo_prev[7:8, :])
    last = jnp.where(seg_end, 0.0, halo_next[0:1, :])
    xp = jnp.where(row == 0, first, pltpu.roll(x, 1, 0))
    xn = jnp.where(row == CHUNK - 1, last, pltpu.roll(x, CHUNK - 1, 0))
    return x + mu[0:1] * (xp - x) + mu[1:2] * (xn - x)


def _rwkv_kernel(*refs, groups, ncc, nch):
    ins = [refs[0:6], refs[6:12]]
    (mu_rkv_ref, mu_lo_ref, vec_ref, wup_ref, aup_ref, gup_ref, seg_ref,
     tri_ref, strict_ref, incl_ref, off_ref, eye_ref, hm_ref,
     o0_ref, o1_ref, bo0_ref, bo1_ref, g_ref, st_ref) = refs[12:]
    c = pl.program_id(0)

    @pl.when(c == 0)
    def _():
        st_ref[...] = jnp.zeros(st_ref.shape, F32)

    hm = hm_ref[...]
    eye = eye_ref[...]
    seg = seg_ref[...]
    w = RWKV_PACK * RWKV_HEAD
    nb = RWKV_PACK * CHUNK
    gw = GROUP_W

    def stack(x):
        return jnp.concatenate([x] * RWKV_PACK, axis=0) * hm

    streams = []
    for d, (o_ref, bo_ref) in enumerate(((o0_ref, bo0_ref), (o1_ref, bo1_ref))):
        rkv_ref, rkv_p, rkv_n, lo_ref, lo_p, lo_n = ins[d]
        pos = _chunk_pos(d, c, ncc, nch)
        seg_start = (pos == 0) | (pos == ncc)
        seg_end = (pos == ncc - 1) | (pos == nch - 1)
        rkv = _shift_mix_block(rkv_ref[...], rkv_p[...], rkv_n[...], mu_rkv_ref[...], seg_start, seg_end)
        lora = _shift_mix_block(lo_ref[...], lo_p[...], lo_n[...], mu_lo_ref[...], seg_start, seg_end)
        r, k, v = rkv[:, 0:gw], rkv[:, gw:2 * gw], rkv[:, 2 * gw:3 * gw]
        w_dn, a_dn, g_dn = lora[:, 0:128], lora[:, 128:256], lora[:, 256:384]
        vec = vec_ref[d]
        wl = vec[0:1] + _dot(jnp.tanh(w_dn), wup_ref[d])
        lw = -jnp.exp(-(jnp.maximum(-wl, 0.0) + jnp.log1p(jnp.exp(-jnp.abs(wl)))) - 0.5)
        a = jax.nn.sigmoid(vec[1:2] + _dot(a_dn, aup_ref[d]))
        kk = k * vec[2:3]
        kk = kk * lax.rsqrt(jnp.maximum(_seg_sum(kk * kk, seg), 1e-24))
        k = k * (1.0 + (a - 1.0) * vec[3:4])
        b = kk * a
        bo_ref[...] = _seg_sum(r * k * vec[4:5], seg) * v
        if d == 0:
            g_ref[...] = _dot(jax.nn.sigmoid(g_dn), gup_ref[...])
        bc = _dot_exact_lhs(tri_ref[d], lw)
        ebt = jnp.exp(jnp.sum(lw, axis=0, keepdims=True))
        einv = jnp.exp(-bc)
        khat = kk * jnp.exp(bc - lw)
        rhat = r * jnp.exp(bc)
        ks = k * einv
        bs = b * einv
        for g in range(groups):
            sl = slice(g * w, (g + 1) * w)
            streams.append(dict(
                d=d, g=g, sl=sl, o_ref=o_ref, ebt=ebt[:, sl],
                kr=jnp.concatenate([stack(khat[:, sl]), stack(rhat[:, sl])], axis=0).astype(BF16),
                bk=jnp.concatenate([stack(bs[:, sl]), stack(ks[:, sl])], axis=0).astype(BF16),
                vs=stack(v[:, sl])))

    for s in streams:
        aa = _dot_nt(s["kr"], s["bk"])
        strict, incl = strict_ref[s["d"]], incl_ref[s["d"]]
        s["akb"] = strict * aa[:nb, :nb]
        s["arb"] = (incl * aa[nb:, :nb]).astype(BF16)
        s["ak_v"] = jnp.concatenate([strict * aa[:nb, nb:], incl * aa[nb:, nb:]], axis=0).astype(BF16)
        s["minv"] = eye - off_ref[s["d"], N_LEVELS - 1] * s["akb"]
    for lv in range(N_LEVELS - 2, -1, -1):
        for s in streams:
            s["t1"] = _dot(off_ref[s["d"], lv] * s["akb"], s["minv"])
        for s in streams:
            s["minv"] = s["minv"] - _dot(s["minv"], s["t1"])
    for s in streams:
        st = st_ref[s["d"], s["g"]]
        from_state = _dot_nt(s["kr"], st)
        from_v = _dot(s["ak_v"], s["vs"])
        u = _dot(s["minv"], from_state[:nb] + from_v[:nb])
        o = from_state[nb:] + from_v[nb:] - _dot(s["arb"], u)
        s["o_ref"][:, s["sl"]] = sum(o[h * CHUNK:(h + 1) * CHUNK] for h in range(RWKV_PACK))
        upd = _dot(s["vs"].T, s["bk"][nb:]) - _dot(u.T, s["bk"][:nb])
        st_ref[s["d"], s["g"]] = (st + upd) * s["ebt"]


def rwkv_scan(p, rkv_off, lora_off, params, *, n_ctx, name="rwkv_scan"):
    l = p.shape[0]
    hw = GROUP_W
    w = RWKV_PACK * RWKV_HEAD
    groups = hw // w
    nch, ncc = l // CHUNK, n_ctx // CHUNK
    nb = RWKV_PACK * CHUNK
    halo = 8
    per_chunk = CHUNK // halo

    def shared(d):
        return pl.BlockSpec((CHUNK, hw), lambda c: (_chunk_pos(d, c, ncc, nch), 0))

    def piece(d, off, width):
        assert off % width == 0
        cb = off // width
        pos = lambda c: _chunk_pos(d, c, ncc, nch)
        return [pl.BlockSpec((CHUNK, width), lambda c: (pos(c), cb)),
                pl.BlockSpec((halo, width), lambda c: (jnp.maximum(pos(c) * per_chunk - 1, 0), cb)),
                pl.BlockSpec((halo, width), lambda c: (jnp.minimum((pos(c) + 1) * per_chunk, l // halo - 1), cb))]

    def const(a):
        return pl.BlockSpec(a.shape, lambda c: (0,) * a.ndim)

    bd = lambda a: np.kron(np.eye(RWKV_PACK, dtype=np.float32), a)
    strict = jnp.asarray(np.stack([bd(_CC["strict"][d]) for d in range(2)]), F32)
    incl = jnp.asarray(np.stack([bd(_CC["strict"][d] + _CC["eye"][d]) for d in range(2)]), F32)
    off = jnp.asarray(np.stack([np.stack([bd(_CC["off"][d, lv]) for lv in range(N_LEVELS)]) for d in range(2)]), F32)
    eye = jnp.asarray(np.eye(nb, dtype=np.float32))
    hm = jnp.asarray(np.kron(np.eye(RWKV_PACK, dtype=np.float32), np.ones((CHUNK, RWKV_HEAD), np.float32)))
    tri = jnp.asarray(_CC["tri"], BF16)
    seg = jnp.asarray(np.kron(np.eye(RWKV_HEADS, dtype=np.float32), np.ones((RWKV_HEAD, RWKV_HEAD), np.float32)), BF16)
    consts = list(params) + [seg, tri, strict, incl, off, eye, hm]
    in_specs = []
    for d in range(2):
        in_specs += piece(d, rkv_off, 3 * hw) + piece(d, lora_off, 3 * 128)
    return pl.pallas_call(
        functools.partial(_rwkv_kernel, groups=groups, ncc=ncc, nch=nch),
        out_shape=[jax.ShapeDtypeStruct((l, hw), F32)] * 5,
        grid=(nch,),
        in_specs=in_specs + [const(a) for a in consts],
        out_specs=[shared(0), shared(1), shared(0), shared(1), shared(0)],
        scratch_shapes=[pltpu.VMEM((2, groups, nb, nb), F32)],
        compiler_params=_cparams(("arbitrary",)),
        name=name,
    )(*([p] * len(in_specs)), *consts)


def _rwkv_post_kernel(of_ref, ob_ref, bf_ref, bb_ref, g_ref, ln_ref, seg_ref, o_ref):
    seg = seg_ref[...]
    o = of_ref[...] + ob_ref[...]
    mean = _seg_sum(o, seg) * (1.0 / RWKV_HEAD)
    oc = o - mean
    var = _seg_sum(oc * oc, seg) * (1.0 / RWKV_HEAD)
    y = oc * lax.rsqrt(var + RWKV_LN_EPS) * ln_ref[0:1] + ln_ref[1:2] + bf_ref[...] + bb_ref[...]
    o_ref[...] = (y * g_ref[...]).astype(o_ref.dtype)


def rwkv_post(o_f, o_b, bo_f, bo_b, g, ln_g, ln_b, *, tm=768, name="rwkv_post"):
    l, w = o_f.shape
    tm = math.gcd(l, tm)
    row = pl.BlockSpec((tm, w), lambda i: (i, 0))
    ln = jnp.zeros((8, w), F32).at[0].set(ln_g).at[1].set(ln_b)
    seg = jnp.asarray(np.kron(np.eye(RWKV_HEADS, dtype=np.float32), np.ones((RWKV_HEAD, RWKV_HEAD), np.float32)), BF16)
    return pl.pallas_call(
        _rwkv_post_kernel,
        out_shape=jax.ShapeDtypeStruct((l, w), BF16),
        grid=(l // tm,),
        in_specs=[row] * 5 + [pl.BlockSpec((8, w), lambda i: (0, 0)), pl.BlockSpec((w, w), lambda i: (0, 0))],
        out_specs=row,
        compiler_params=_cparams(("parallel",)),
        name=name,
    )(o_f, o_b, bo_f, bo_b, g, ln, seg)


def _rms(x, g, eps=EPS):
    return x * lax.rsqrt(jnp.mean(x * x, axis=-1, keepdims=True) + eps) * g


def _heads_major(t, h):
    l = t.shape[0]
    return t.reshape(l, h, -1).transpose(1, 0, 2)


def _rope_tables(n):
    rows = n // GRID_W
    row = jnp.repeat(jnp.arange(rows, dtype=F32), GRID_W)
    col = jnp.tile(jnp.arange(GRID_W, dtype=F32), rows)
    n_freq = MLA_ROPE // 4
    freqs = ROPE_BASE ** (-jnp.arange(n_freq, dtype=F32) / n_freq)
    ang = jnp.stack([row[:, None] * freqs, col[:, None] * freqs], axis=1)
    return jnp.cos(ang), jnp.sin(ang)


def _rope(x, cos, sin):
    xs = x.reshape(x.shape[:-1] + (2, 2, MLA_ROPE // 4))
    x1, x2 = xs[..., 0, :], xs[..., 1, :]
    return jnp.stack([x1 * cos - x2 * sin, x1 * sin + x2 * cos], axis=-2).reshape(x.shape)


def mla_mixer(p, m, q_norm, w_q_up, kv_norm, w_kv_up):
    l = p.shape[0]
    n = l - m
    cq, ckv, k_rope = _pcol(p, "cq"), _pcol(p, "ckv"), _pcol(p, "k_rope")
    q = pmatmul(_rms(cq, q_norm), w_q_up, tm=1408, tn=768, name="mla_q_up").reshape(l, MLA_HEADS, MLA_NOPE + MLA_ROPE)
    kv = pmatmul(_rms(ckv, kv_norm), w_kv_up, tm=1408, tn=1024, name="mla_kv_up").reshape(l, MLA_HEADS, MLA_NOPE + MLA_V)
    k_nope, v = kv[..., :MLA_NOPE], kv[..., MLA_NOPE:]
    cos, sin = _rope_tables(n)
    q_nope, q_rope = q[..., :MLA_NOPE], q[..., MLA_NOPE:]
    q_rope = jnp.concatenate([q_rope[:m], _rope(q_rope[m:], cos[:, None], sin[:, None])], axis=0)
    k_rope = jnp.concatenate([k_rope[:m], _rope(k_rope[m:], cos, sin)], axis=0)
    scale = (MLA_NOPE + MLA_ROPE) ** -0.5
    qh = (jnp.concatenate([q_nope, q_rope], axis=-1) * scale).astype(BF16).transpose(1, 0, 2)
    kh = jnp.concatenate([k_nope, jnp.broadcast_to(k_rope[:, None, :], (l, MLA_HEADS, MLA_ROPE))], axis=-1)
    kh = kh.astype(BF16).transpose(1, 0, 2)
    v_ext = jnp.concatenate([v, jnp.ones((l, MLA_HEADS, 1), F32), jnp.zeros((l, MLA_HEADS, MLA_V - 1), F32)], axis=-1)
    vh = v_ext.astype(BF16).transpose(1, 0, 2)
    o_ctx = flash_attention(qh[:, :m], kh[:, :m], vh[:, :m], dv=MLA_V, tq=m, name="mla_attn_ctx")
    o_lat = flash_attention(qh[:, m:], kh, vh, dv=MLA_V, tq=512, name="mla_attn_lat")
    return jnp.concatenate([o_ctx, o_lat], axis=0)


def _shift_mix(z, m, mu_prev, mu_next):
    def one(zp):
        zprev = jnp.pad(zp[:-1], ((1, 0), (0, 0)))
        znext = jnp.pad(zp[1:], ((0, 1), (0, 0)))
        return zp + mu_prev * (zprev - zp) + mu_next * (znext - zp)
    return jnp.concatenate([one(z[:m]), one(z[m:])], axis=0)


def rwkv7_mixer(p, m, mu, w0, w_up, a0, a_up, k_k, k_a, u, g_up, ln_g, ln_b):
    mu_rkv, mu_lo = mu[:, :3 * GROUP_W], mu[:, 3 * GROUP_W:]
    vec = jnp.zeros((2, 8, GROUP_W), F32)
    for i, t in enumerate((w0, a0, k_k, k_a, u)):
        vec = vec.at[:, i].set(t)
    wup = jnp.zeros((2, 128, GROUP_W), F32)
    aup = jnp.zeros((2, 128, GROUP_W), F32)
    for d in range(2):
        wup = wup.at[d, d * RWKV_W_LORA:(d + 1) * RWKV_W_LORA].set(w_up[d])
        aup = aup.at[d, d * RWKV_A_LORA:(d + 1) * RWKV_A_LORA].set(a_up[d])
    outs = rwkv_scan(p, PACKED["rwkv_r"][2], PACKED["w_dn"][2], (mu_rkv, mu_lo, vec, wup, aup, g_up), n_ctx=m)
    return rwkv_post(*outs, ln_g, ln_b)


PACKED = {}


def _build_packed():
    orig = dict(cq=(0, MLA_Q_LORA), ckv=(MLA_Q_LORA, MLA_KV_LORA), k_rope=(MLA_Q_LORA + MLA_KV_LORA, MLA_ROPE))
    b = MLA_COLS
    for i, nm in enumerate(("rwkv_r", "rwkv_k", "rwkv_v")):
        orig[nm] = (b + i * GROUP_W, GROUP_W)
    b += 3 * GROUP_W
    orig.update(w_dn=(b, 2 * RWKV_W_LORA), a_dn=(b + 2 * RWKV_W_LORA, 2 * RWKV_A_LORA),
                g_dn=(b + 2 * RWKV_W_LORA + 2 * RWKV_A_LORA, RWKV_G_LORA))
    b = IN_SPLITS[1]
    orig.update(gla_q=(b, GLA_KD), gla_k=(b + GLA_KD, GLA_KD), gla_v=(b + 2 * GLA_KD, GROUP_W),
                gla_a=(b + GLA_SPLITS[2], 2 * GLA_GATE_RANK), gla_r=(b + GLA_SPLITS[3], GROUP_W))
    b = IN_SPLITS[2]
    for i, nm in enumerate(("hgrn_q", "hgrn_f0", "hgrn_f1", "hgrn_i", "hgrn_g")):
        orig[nm] = (b + i * GROUP_W, GROUP_W)
    order = [(nm, 512) for nm in ("rwkv_r", "rwkv_k", "rwkv_v", "gla_v", "gla_r", "hgrn_q", "hgrn_f0", "hgrn_f1",
                                  "hgrn_i", "hgrn_g", "cq")]
    orig["pad"] = (0, 0)
    order += [(nm, 256) for nm in ("gla_q", "gla_k")]
    order += [(nm, 128) for nm in ("w_dn", "a_dn", "g_dn", "gla_a", "k_rope", "pad")]
    order += [("ckv", 256)]
    off = 0
    for nm, wp in order:
        PACKED[nm] = (orig[nm][0], orig[nm][1], off, wp)
        off += wp
    return off


N_PACKED = _build_packed()


def pack_w_in(w):
    pieces = []
    for o, wd, _, wp in PACKED.values():
        pieces.append(w[:, o:o + wd])
        if wp > wd:
            pieces.append(jnp.zeros((w.shape[0], wp - wd), w.dtype))
    return jnp.concatenate(pieces, axis=1)


def _pcol(p, nm):
    _, wd, off, _ = PACKED[nm]
    return p[:, off:off + wd]


def _blk(nm):
    return (PACKED[nm][2], PACKED[nm][3])


def gla_mixer(p, m, a_up, a_bias, g_norm):
    par = jnp.zeros((2, 136, GLA_KD), F32)
    for d in range(2):
        par = par.at[d, d * GLA_GATE_RANK:(d + 1) * GLA_GATE_RANK].set(a_up[d]).at[d, 128].set(a_bias[d])
    cols = [[_blk("gla_q"), _blk("gla_k"), _blk("gla_v"), _blk("gla_a")]] * 2
    o_f, o_b = gla_scan(p, cols, par, mode="gla", heads=GLA_HEADS, dk=GLA_DK, dv=GLA_DV, pack=4, n_ctx=m,
                        name="gla_scan")
    return mix_post(o_f, o_b, p, PACKED["gla_r"][2], g_norm, heads=GLA_HEADS, dv=GLA_DV, name="gla_post")


def hgrn2_mixer(p, m, lb, g_norm):
    par = jnp.zeros((2, 8, GROUP_W), F32).at[:, 0].set(jnp.log(lb)).at[:, 1].set(jnp.log1p(-lb)).at[:, 2].set(1.0 - lb)
    cols = [[_blk("hgrn_q"), _blk("hgrn_f%d" % d), _blk("hgrn_i")] for d in range(2)]
    o_f, o_b = gla_scan(p, cols, par, mode="hgrn", heads=HGRN_HEADS, dk=HGRN_EXPAND, dv=HGRN_DV, pack=2, n_ctx=m,
                        name="hgrn_scan")
    return mix_post(o_f, o_b, p, PACKED["hgrn_g"][2], g_norm, heads=HGRN_HEADS, dv=HGRN_DV, name="hgrn_post")


def dense_ffn(xs, h2, w1, w3, w2, j, gates2, m):
    act = pmatmul(h2, w1, (j,), w3=w3, tm=1408, tn=512, out_dtype=BF16, name="ffn_up")
    return pmatmul(act, w2, (j,), tm=768, tn=256, res=xs, gates=gates2, m_ctx=m, name="ffn_down")


def _route(top_idx, weights):
    n_pairs = top_idx.size
    r = n_pairs + N_EXPERTS * MOE_TILE
    n_tiles = r // MOE_TILE
    e_flat = top_idx.reshape(-1)
    onehot = (e_flat[:, None] == jnp.arange(N_EXPERTS)[None, :]).astype(jnp.int32)
    rank = jnp.sum((jnp.cumsum(onehot, axis=0) - onehot) * onehot, axis=1)
    counts = jnp.sum(onehot, axis=0)
    padded = (counts + MOE_TILE - 1) // MOE_TILE * MOE_TILE
    ends = jnp.cumsum(padded)
    dest = (ends - padded)[e_flat] + rank
    src_token = jnp.zeros((r,), jnp.int32).at[dest].set(jnp.arange(n_pairs, dtype=jnp.int32) // TOP_K)
    row_gate = jnp.zeros((r,), F32).at[dest].set(weights.reshape(-1))
    tile_start = jnp.arange(n_tiles, dtype=jnp.int32) * MOE_TILE
    tile_expert = jnp.minimum(jnp.sum(tile_start[:, None] >= ends[None, :], axis=1), N_EXPERTS - 1)
    meta = jnp.concatenate([tile_expert.astype(jnp.int32), (ends[-1:] // MOE_TILE).astype(jnp.int32)])
    return src_token, row_gate, dest.reshape(top_idx.shape), meta


def _combine_kernel(x_ref, y_ref, gate_ref, o_ref, *, m_ctx, tm, d):
    i = pl.program_id(0)
    rows = i * tm + lax.broadcasted_iota(jnp.int32, (tm, d), 0)
    g = jnp.where(rows < m_ctx, gate_ref[0:1, :], gate_ref[1:2, :])
    o_ref[...] = x_ref[...] + g * (y_ref[:, 0:d] + y_ref[:, d:2 * d])


def moe_combine(xs, y_tok, gates2, *, m_ctx, tm=256, name="moe_combine"):
    l, d = xs.shape
    return pl.pallas_call(
        functools.partial(_combine_kernel, m_ctx=m_ctx, tm=tm, d=d),
        out_shape=jax.ShapeDtypeStruct((l, d), F32),
        grid=(l // tm,),
        in_specs=[pl.BlockSpec((tm, d), lambda i: (i, 0)), pl.BlockSpec((tm, TOP_K * d), lambda i: (i, 0)),
                  pl.BlockSpec((2, d), lambda i: (0, 0))],
        out_specs=pl.BlockSpec((tm, d), lambda i: (i, 0)),
        compiler_params=_cparams(("parallel",)),
        name=name,
    )(xs, y_tok, gates2)


def moe_ffn(xs, h2f, router, w1, w3, w2, j, gates2, m):
    l, d = h2f.shape
    router_p = jnp.pad(router[j], ((0, 0), (0, 128 - N_EXPERTS)))
    logits = pmatmul(h2f, router_p, tm=768, tn=128, precision=lax.Precision.HIGHEST, name="moe_router")[:, :N_EXPERTS]
    top_vals, top_idx = lax.top_k(logits, TOP_K)
    weights = jax.nn.softmax(top_vals, axis=-1)
    src_token, row_gate, dest, meta = _route(top_idx, weights)
    x_sorted = jnp.take(h2f.astype(BF16), src_token, axis=0)
    act = gmatmul(meta, x_sorted, w1, j, w3=w3, tn=512, out_dtype=BF16, rowscale=row_gate[:, None], name="moe_up")
    y = gmatmul(meta, act, w2, j, tn=512, out_dtype=F32, name="moe_down")
    y_tok = jnp.take(y, dest.reshape(-1), axis=0).reshape(l, TOP_K * d)
    return moe_combine(xs, y_tok, gates2, m_ctx=m)


def kernel(x, c, ctx, c_ctx, norm1_g, norm2_g, w_mod, b_mod, w_in, w_out, mla_q_norm, mla_w_q_up, mla_kv_norm, mla_w_kv_up, rwkv_mu, rwkv_w0, rwkv_w_up, rwkv_a0, rwkv_a_up, rwkv_k_k, rwkv_k_a, rwkv_u, rwkv_g_up, rwkv_ln_g, rwkv_ln_b, gla_a_up, gla_a_bias, gla_norm, hgrn_lb, hgrn_norm, ffn_w1, ffn_w3, ffn_w2, moe_router, moe_w1, moe_w3, moe_w2, final_norm_g):
    m, n, d = ctx.shape[1], x.shape[1], x.shape[2]
    lb_all = jnp.cumsum(jax.nn.softmax(hgrn_lb.astype(F32), axis=1), axis=1)
    lb_all = lb_all - lb_all[:, :1]
    xs = jnp.concatenate([ctx[0], x[0]], axis=0)
    cvec = jnp.zeros((16, d), F32).at[0].set(jax.nn.silu(c[0])).at[1].set(jax.nn.silu(c_ctx))
    for l in range(DEPTH):
        mod = pmatmul(cvec, w_mod, (l,), tm=16, tn=1024, name="mod") + b_mod[l]
        mods = jnp.stack([mod[1].reshape(6, d), mod[0].reshape(6, d)], axis=1)
        sh1, sc1, gt1, sh2, sc2, gt2 = (mods[i] for i in range(6))

        h = norm_mod(xs, norm1_g[l], sh1, sc1, m_ctx=m, out_dtype=BF16)
        p = pmatmul(h, pack_w_in(w_in[l]), tm=1408, tn=1024, name="w_in")
        o = [
            mla_mixer(p, m, mla_q_norm[l], mla_w_q_up[l], mla_kv_norm[l], mla_w_kv_up[l]),
            rwkv7_mixer(p, m, rwkv_mu[l], rwkv_w0[l], rwkv_w_up[l], rwkv_a0[l], rwkv_a_up[l],
                        rwkv_k_k[l], rwkv_k_a[l], rwkv_u[l], rwkv_g_up[l], rwkv_ln_g[l], rwkv_ln_b[l]),
            gla_mixer(p, m, gla_a_up[l], gla_a_bias[l], gla_norm[l]),
            hgrn2_mixer(p, m, lb_all[:, l], hgrn_norm[l]),
        ]
        xs = pmatmul(o, w_out, (l,), tm=1408, tn=512, res=xs, gates=gt1, m_ctx=m, name="w_out")

        j = l // 2
        if l % 2 == 0:
            h2 = norm_mod(xs, norm2_g[l], sh2, sc2, m_ctx=m, out_dtype=BF16)
            xs = dense_ffn(xs, h2, ffn_w1, ffn_w3, ffn_w2, j, gt2, m)
        else:
            h2f = norm_mod(xs, norm2_g[l], sh2, sc2, m_ctx=m, out_dtype=F32)
            xs = moe_ffn(xs, h2f, moe_router, moe_w1, moe_w3, moe_w2, j, gt2, m)
    zeros2 = jnp.zeros((2, d), F32)
    out = norm_mod(xs, final_norm_g, zeros2, zeros2, m_ctx=m, out_dtype=F32, name="final_norm")
    return out[m:][None]
```

```python
import functools
import math

import numpy as np
import jax
import jax.numpy as jnp
from jax import lax
from jax.experimental import pallas as pl
from jax.experimental.pallas import tpu as pltpu

F32 = jnp.float32
BF16 = jnp.bfloat16

DEPTH = 4
GRID_W = 64
EPS = 1e-6
GROUP_W = 512
MLA_HEADS, MLA_NOPE, MLA_ROPE, MLA_V = 4, 128, 64, 128
MLA_Q_LORA, MLA_KV_LORA = 384, 256
ROPE_BASE = 10000.0
RWKV_HEADS, RWKV_HEAD = 8, 64
RWKV_W_LORA, RWKV_A_LORA, RWKV_G_LORA = 64, 64, 128
RWKV_LN_EPS = 64e-5
GLA_HEADS, GLA_DK, GLA_DV = 4, 64, 128
GLA_GATE_RANK, GLA_GATE_NORM = 16, 16.0
HGRN_HEADS, HGRN_EXPAND, HGRN_DV = 4, 128, 128
CHUNK = 64
N_EXPERTS, TOP_K = 8, 2
MLA_COLS = MLA_Q_LORA + MLA_KV_LORA + MLA_ROPE
RWKV_COLS = 3 * GROUP_W + 2 * RWKV_W_LORA + 2 * RWKV_A_LORA + RWKV_G_LORA
GLA_KD = GLA_HEADS * GLA_DK
GLA_COLS = 2 * GLA_KD + GROUP_W + 2 * GLA_GATE_RANK + GROUP_W
IN_SPLITS = (MLA_COLS, MLA_COLS + RWKV_COLS, MLA_COLS + RWKV_COLS + GLA_COLS)
RWKV_SPLITS = (GROUP_W, 2 * GROUP_W, 3 * GROUP_W, 3 * GROUP_W + 2 * RWKV_W_LORA,
               3 * GROUP_W + 2 * RWKV_W_LORA + 2 * RWKV_A_LORA)
GLA_SPLITS = (GLA_KD, 2 * GLA_KD, 2 * GLA_KD + GROUP_W, 2 * GLA_KD + GROUP_W + 2 * GLA_GATE_RANK)

V7X_VMEM_LIMIT = 56 * 1024 * 1024
V7X_MXU = 256
N_LEVELS = 6
RWKV_PACK = V7X_MXU // RWKV_HEAD
MOE_TILE = 512


def _cparams(sem, vmem=V7X_VMEM_LIMIT):
    return pltpu.CompilerParams(dimension_semantics=sem, vmem_limit_bytes=vmem)


def _dot(a, b):
    return jnp.dot(a.astype(BF16), b.astype(BF16), preferred_element_type=F32)


def _dot_nt(a, b):
    return lax.dot_general(a.astype(BF16), b.astype(BF16), (((1,), (1,)), ((), ())), preferred_element_type=F32)


def _split2(x):
    hi = x.astype(BF16)
    lo = (x - hi.astype(F32)).astype(BF16)
    return hi, lo


def _dot_exact_lhs(m_bf16, x):
    hi, lo = _split2(x)
    return (jnp.dot(m_bf16, hi, preferred_element_type=F32) + jnp.dot(m_bf16, lo, preferred_element_type=F32))


def _mm_kernel(*refs, n_x, n_w, has_res, m_ctx, tm, precision):
    it = iter(refs)
    x_refs = [next(it) for _ in range(n_x)]
    w_refs = [next(it) for _ in range(n_w)]
    res_ref = next(it) if has_res else None
    gate_ref = next(it) if has_res else None
    o_ref = next(it)
    wb_refs = [next(it) for _ in range(n_w)] if precision is None else w_refs
    i = pl.program_id(1)

    if precision is None:
        @pl.when(i == 0)
        def _():
            for w_ref, wb_ref in zip(w_refs, wb_refs):
                wb_ref[...] = w_ref[...].astype(BF16)
        x = jnp.concatenate([x_ref[...].astype(BF16) for x_ref in x_refs], axis=1)
    else:
        x = x_refs[0][...]

    acc = jnp.dot(x, wb_refs[0][...], preferred_element_type=F32, precision=precision)
    if n_w == 2:
        acc3 = jnp.dot(x, wb_refs[1][...], preferred_element_type=F32)
        acc = acc * jax.nn.sigmoid(acc) * acc3
    if has_res:
        rows = i * tm + lax.broadcasted_iota(jnp.int32, acc.shape, 0)
        g = jnp.where(rows < m_ctx, gate_ref[0:1, :], gate_ref[1:2, :])
        acc = res_ref[...] + g * acc
    o_ref[...] = acc.astype(o_ref.dtype)


def pmatmul(x, w, widx=(), *, w3=None, tm, tn, out_dtype=F32, res=None, gates=None, m_ctx=0, precision=None,
            name="mm"):
    xs = list(x) if isinstance(x, (list, tuple)) else [x]
    m = xs[0].shape[0]
    k = sum(xi.shape[1] for xi in xs)
    n = w.shape[-1]
    tm = math.gcd(m, tm)
    assert w.shape[-2] == k and tm % 16 == 0
    nj, ni = pl.cdiv(n, tn), m // tm
    lead = (None,) * len(widx)
    w_spec = pl.BlockSpec(lead + (k, tn), lambda j, i: tuple(widx) + (0, j))
    ws = [w] if w3 is None else [w, w3]
    in_specs = [pl.BlockSpec((tm, xi.shape[1]), lambda j, i: (i, 0)) for xi in xs] + [w_spec] * len(ws)
    args = xs + ws
    if res is not None:
        in_specs += [pl.BlockSpec((tm, tn), lambda j, i: (i, j)), pl.BlockSpec((2, tn), lambda j, i: (0, j))]
        args += [res, gates]
    kern = functools.partial(_mm_kernel, n_x=len(xs), n_w=len(ws), has_res=res is not None, m_ctx=m_ctx, tm=tm,
                             precision=precision)
    return pl.pallas_call(
        kern,
        out_shape=jax.ShapeDtypeStruct((m, n), out_dtype),
        grid=(nj, ni),
        in_specs=in_specs,
        out_specs=pl.BlockSpec((tm, tn), lambda j, i: (i, j)),
        scratch_shapes=[pltpu.VMEM((k, tn), BF16) for _ in ws] if precision is None else [],
        compiler_params=_cparams(("arbitrary", "arbitrary")),
        name=name,
    )(*args)


def _gmm_kernel(meta_ref, x_ref, *refs, n_w, has_rowscale, n_tiles):
    it = iter(refs)
    w_refs = [next(it) for _ in range(n_w)]
    rs_ref = next(it) if has_rowscale else None
    o_ref = next(it)
    wb_refs = [next(it) for _ in range(n_w)]
    t = pl.program_id(1)
    e = meta_ref[t]
    e_prev = meta_ref[jnp.maximum(t - 1, 0)]

    @pl.when((t == 0) | (e != e_prev))
    def _():
        for w_ref, wb_ref in zip(w_refs, wb_refs):
            wb_ref[...] = w_ref[...].astype(BF16)

    @pl.when(t < meta_ref[n_tiles])
    def _():
        x = x_ref[...].astype(BF16)
        acc = jnp.dot(x, wb_refs[0][...], preferred_element_type=F32)
        if n_w == 2:
            acc3 = jnp.dot(x, wb_refs[1][...], preferred_element_type=F32)
            acc = acc * jax.nn.sigmoid(acc) * acc3
        if has_rowscale:
            acc = acc * rs_ref[...]
        o_ref[...] = acc.astype(o_ref.dtype)

    @pl.when(t >= meta_ref[n_tiles])
    def _():
        o_ref[...] = jnp.zeros(o_ref.shape, o_ref.dtype)


def gmatmul(meta, x, w, jl, *, w3=None, tn, out_dtype, rowscale=None, name):
    r, k = x.shape
    n = w.shape[-1]
    n_tiles = r // MOE_TILE
    ws = [w] if w3 is None else [w, w3]
    w_spec = pl.BlockSpec((None, None, k, tn), lambda j, t, mr: (jl, mr[t], 0, j))
    in_specs = [pl.BlockSpec((MOE_TILE, k), lambda j, t, mr: (t, 0))] + [w_spec] * len(ws)
    args = [x] + ws
    if rowscale is not None:
        in_specs.append(pl.BlockSpec((MOE_TILE, 1), lambda j, t, mr: (t, 0)))
        args.append(rowscale)
    return pl.pallas_call(
        functools.partial(_gmm_kernel, n_w=len(ws), has_rowscale=rowscale is not None, n_tiles=n_tiles),
        out_shape=jax.ShapeDtypeStruct((r, n), out_dtype),
        grid_spec=pltpu.PrefetchScalarGridSpec(
            num_scalar_prefetch=1,
            grid=(n // tn, n_tiles),
            in_specs=in_specs,
            out_specs=pl.BlockSpec((MOE_TILE, tn), lambda j, t, mr: (t, j)),
            scratch_shapes=[pltpu.VMEM((k, tn), BF16) for _ in ws]),
        compiler_params=_cparams(("arbitrary", "arbitrary")),
        name=name,
    )(meta, *args)


def _norm_kernel(x_ref, g_ref, sh_ref, sc_ref, o_ref, *, m_ctx, tm):
    i = pl.program_id(0)
    x = x_ref[...]
    y = x * lax.rsqrt(jnp.mean(x * x, axis=-1, keepdims=True) + EPS) * g_ref[...]
    rows = i * tm + lax.broadcasted_iota(jnp.int32, x.shape, 0)
    is_ctx = rows < m_ctx
    sc = jnp.where(is_ctx, sc_ref[0:1, :], sc_ref[1:2, :])
    sh = jnp.where(is_ctx, sh_ref[0:1, :], sh_ref[1:2, :])
    o_ref[...] = (y * (1.0 + sc) + sh).astype(o_ref.dtype)


def norm_mod(x, g, shift2, scale2, *, m_ctx, out_dtype, tm=768, name="norm_mod"):
    m, d = x.shape
    tm = math.gcd(m, tm)
    return pl.pallas_call(
        functools.partial(_norm_kernel, m_ctx=m_ctx, tm=tm),
        out_shape=jax.ShapeDtypeStruct((m, d), out_dtype),
        grid=(m // tm,),
        in_specs=[pl.BlockSpec((tm, d), lambda i: (i, 0)), pl.BlockSpec((1, d), lambda i: (0, 0)),
                  pl.BlockSpec((2, d), lambda i: (0, 0)), pl.BlockSpec((2, d), lambda i: (0, 0))],
        out_specs=pl.BlockSpec((tm, d), lambda i: (i, 0)),
        compiler_params=_cparams(("parallel",)),
        name=name,
    )(x, g.reshape(1, d), shift2, scale2)


def _attn_kernel(q_ref, k_ref, v_ref, o_ref, *, tk, n_kv, dv):
    q = q_ref[...]
    m = acc = None
    for j in range(n_kv):
        kb = k_ref[j * tk:(j + 1) * tk, :]
        vb = v_ref[j * tk:(j + 1) * tk, :]
        s = lax.dot_general(q, kb, (((1,), (1,)), ((), ())), preferred_element_type=F32)
        m_blk = jnp.max(s, axis=-1, keepdims=True)
        m_new = m_blk if j == 0 else jnp.maximum(m, m_blk)
        pv = jnp.dot(jnp.exp(s - m_new).astype(BF16), vb, preferred_element_type=F32)
        acc = pv if j == 0 else jnp.exp(m - m_new) * acc + pv
        m = m_new
    o_ref[...] = (acc[:, :dv] / acc[:, dv:dv + 1]).astype(o_ref.dtype)


def _attn_kv_tile(lk, cap=1408):
    return max(t for t in range(128, cap + 1, 128) if lk % t == 0)


def flash_attention(q, k, v_ext, *, dv, tq, name="mla_attn"):
    h, lq, dqk = q.shape
    lk, dve = k.shape[1], v_ext.shape[2]
    tk = _attn_kv_tile(lk)
    assert lq % tq == 0
    return pl.pallas_call(
        functools.partial(_attn_kernel, tk=tk, n_kv=lk // tk, dv=dv),
        out_shape=jax.ShapeDtypeStruct((lq, h * dv), BF16),
        grid=(h, lq // tq),
        in_specs=[pl.BlockSpec((None, tq, dqk), lambda hh, i: (hh, i, 0)),
                  pl.BlockSpec((None, lk, dqk), lambda hh, i: (hh, 0, 0)),
                  pl.BlockSpec((None, lk, dve), lambda hh, i: (hh, 0, 0))],
        out_specs=pl.BlockSpec((tq, dv), lambda hh, i: (i, hh)),
        compiler_params=_cparams(("parallel", "parallel")),
        name=name,
    )(q, k, v_ext)


def _chunk_constants():
    c = CHUNK
    t = np.arange(c)
    tri = (t[None, :] <= t[:, None]).astype(np.float32)
    strict = (t[None, :] < t[:, None]).astype(np.float32)
    eye = np.eye(c, dtype=np.float32)
    seg, off = [], []
    for lv in range(N_LEVELS):
        s = c >> (lv + 1)
        blk = t // s
        same = blk[:, None] == blk[None, :]
        odd = (blk % 2 == 1)[:, None]
        seg.append(np.where(odd, same & (t[None, :] <= t[:, None]), same & (t[None, :] > t[:, None])).astype(np.float32))
        off.append((odd & (blk[None, :] == blk[:, None] - 1)).astype(np.float32))
    seg, off = np.stack(seg), np.stack(off)

    def both(a):
        return np.stack([a, a[..., ::-1, ::-1]])

    return {k: both(v) for k, v in dict(tri=tri, strict=strict, eye=eye, seg=seg, off=off).items()}


_CC = _chunk_constants()


def _chunk_pos(d, c, n_ctx_chunks, n_chunks):
    back = jnp.where(c < n_ctx_chunks, n_ctx_chunks - 1 - c, n_chunks + n_ctx_chunks - 1 - c)
    return jnp.where(d == 0, c, back)


def _log_sigmoid(x):
    return jnp.minimum(x, 0.0) - jnp.log1p(jnp.exp(-jnp.abs(x)))


def _gla_kernel(*refs, mode, dk, dv, pack, groups, n_in):
    ins = [refs[:n_in], refs[n_in:2 * n_in]]
    par_ref, mall_ref, off_ref, eye_ref, hmk_ref, hmv_ref, o0_ref, o1_ref, st_ref = refs[2 * n_in:]
    c = pl.program_id(0)

    @pl.when(c == 0)
    def _():
        st_ref[...] = jnp.zeros(st_ref.shape, F32)

    eye = eye_ref[...]
    hmk = hmk_ref[...]
    hmv = hmv_ref[...]
    wk, wv = pack * dk, pack * dv
    for d, o_ref in enumerate((o0_ref, o1_ref)):
        if mode == "hgrn":
            q_raw, f_raw, v = (r[...] for r in ins[d])
            par = par_ref[d]
            q = q_raw * jax.nn.sigmoid(q_raw)
            g = jnp.logaddexp(par[0:1], par[1:2] + _log_sigmoid(f_raw))
            k = par[2:3] * jax.nn.sigmoid(-f_raw)
        else:
            q_raw, k, v, a_dn = (r[...] for r in ins[d])
            par = par_ref[d]
            q = q_raw * dk ** -0.5
            g = _log_sigmoid(_dot(a_dn, par[0:128]) + par[128:129]) / GLA_GATE_NORM
        e_all = _dot_exact_lhs(mall_ref[d], g)
        bc = e_all[0:CHUNK]
        btot = jnp.sum(g, axis=0, keepdims=True)
        qhat = q * jnp.exp(bc)
        kt = k * jnp.exp(btot - bc)
        ebt = jnp.exp(btot)
        qw, kw = [q], [k]
        for lv in range(N_LEVELS):
            w = jnp.exp(e_all[(lv + 1) * CHUNK:(lv + 2) * CHUNK])
            qw.append(q * w)
            kw.append(k * w)
        for gi in range(groups):
            slk = slice(gi * wk, (gi + 1) * wk)
            slv = slice(gi * wv, (gi + 1) * wv)

            def stk(x):
                return (jnp.concatenate([x[:, slk]] * pack, axis=0) * hmk).astype(BF16)

            def tile(x):
                return jnp.concatenate([x[:, slk].astype(BF16)] * pack, axis=0)

            att = eye * _dot_nt(stk(qw[0]), tile(kw[0]))
            for lv in range(N_LEVELS):
                att = att + off_ref[d, lv] * _dot_nt(stk(qw[lv + 1]), tile(kw[lv + 1]))
            vs = jnp.concatenate([v[:, slv]] * pack, axis=0) * hmv
            st = st_ref[d, gi]
            o = _dot_nt(stk(qhat), st) + _dot(att, vs)
            o_ref[:, slv] = sum(o[h * CHUNK:(h + 1) * CHUNK] for h in range(pack))
            st_ref[d, gi] = st * ebt[:, slk] + _dot(vs.T, stk(kt))


def gla_scan(p, cols, par, *, mode, heads, dk, dv, pack, n_ctx, name):
    l = p.shape[0]
    groups = heads // pack
    nch, ncc = l // CHUNK, n_ctx // CHUNK
    nb = pack * CHUNK

    def col_spec(d, off, width):
        assert off % width == 0
        return pl.BlockSpec((CHUNK, width), lambda c: (_chunk_pos(d, c, ncc, nch), off // width))

    def const(a):
        return pl.BlockSpec(a.shape, lambda c: (0,) * a.ndim)

    bd = lambda a: np.kron(np.eye(pack, dtype=np.float32), a)
    mall = jnp.asarray(np.concatenate([_CC["tri"][:, None], _CC["seg"]], axis=1).reshape(2, -1, CHUNK), BF16)
    off = jnp.asarray(np.stack([np.stack([bd(_CC["off"][d, lv]) for lv in range(N_LEVELS)]) for d in range(2)]), F32)
    eye = jnp.asarray(np.eye(nb, dtype=np.float32))
    hmk = jnp.asarray(np.kron(np.eye(pack, dtype=np.float32), np.ones((CHUNK, dk), np.float32)))
    hmv = jnp.asarray(np.kron(np.eye(pack, dtype=np.float32), np.ones((CHUNK, dv), np.float32)))
    consts = [par, mall, off, eye, hmk, hmv]
    in_specs = [col_spec(d, o, w) for d in range(2) for (o, w) in cols[d]]
    out_spec = lambda d: pl.BlockSpec((CHUNK, heads * dv), lambda c: (_chunk_pos(d, c, ncc, nch), 0))
    return pl.pallas_call(
        functools.partial(_gla_kernel, mode=mode, dk=dk, dv=dv, pack=pack, groups=groups, n_in=len(cols[0])),
        out_shape=[jax.ShapeDtypeStruct((l, heads * dv), F32)] * 2,
        grid=(nch,),
        in_specs=in_specs + [const(a) for a in consts],
        out_specs=[out_spec(0), out_spec(1)],
        scratch_shapes=[pltpu.VMEM((2, groups, pack * dv, pack * dk), F32)],
        compiler_params=_cparams(("arbitrary",)),
        name=name,
    )(*([p] * len(in_specs)), *consts)


def _post_kernel(of_ref, ob_ref, gate_ref, gn_ref, o_ref, *, heads, dv):
    o = of_ref[...] + ob_ref[...]
    gate = gate_ref[...]
    gn = gn_ref[...]
    for h in range(heads):
        sl = slice(h * dv, (h + 1) * dv)
        oh = o[:, sl]
        y = oh * lax.rsqrt(jnp.mean(oh * oh, axis=-1, keepdims=True) + EPS) * gn
        gh = gate[:, sl]
        o_ref[:, sl] = (y * gh * jax.nn.sigmoid(gh)).astype(o_ref.dtype)


def mix_post(o_f, o_b, p, gate_off, g_norm, *, heads, dv, tm=768, name):
    l, w = o_f.shape
    tm = math.gcd(l, tm)
    assert gate_off % w == 0
    row = pl.BlockSpec((tm, w), lambda i: (i, 0))
    return pl.pallas_call(
        functools.partial(_post_kernel, heads=heads, dv=dv),
        out_shape=jax.ShapeDtypeStruct((l, w), BF16),
        grid=(l // tm,),
        in_specs=[row, row, pl.BlockSpec((tm, w), lambda i: (i, gate_off // w)), pl.BlockSpec((1, dv), lambda i: (0, 0))],
        out_specs=row,
        compiler_params=_cparams(("parallel",)),
        name=name,
    )(o_f, o_b, p, g_norm.reshape(1, dv))


def _seg_sum(x, seg_bf16):
    hi, lo = _split2(x)
    return jnp.dot(hi, seg_bf16, preferred_element_type=F32) + jnp.dot(lo, seg_bf16, preferred_element_type=F32)


def _shift_mix_block(x, halo_prev, halo_next, mu, seg_start, seg_end):
    row = lax.broadcasted_iota(jnp.int32, x.shape, 0)
    first = jnp.where(seg_start, 0.0, halo_prev[7:8, :])
    last = jnp.where(seg_end, 0.0, halo_next[0:1, :])
    xp = jnp.where(row == 0, first, pltpu.roll(x, 1, 0))
    xn = jnp.where(row == CHUNK - 1, last, pltpu.roll(x, CHUNK - 1, 0))
    return x + mu[0:1] * (xp - x) + mu[1:2] * (xn - x)


def _rwkv_kernel(*refs, groups, ncc, nch):
    ins = [refs[0:6], refs[6:12]]
    (mu_rkv_ref, mu_lo_ref, vec_ref, wup_ref, aup_ref, gup_ref, seg_ref,
     tri_ref, strict_ref, incl_ref, off_ref, eye_ref, hm_ref,
     o0_ref, o1_ref, bo0_ref, bo1_ref, g_ref, st_ref) = refs[12:]
    c = pl.program_id(0)

    @pl.when(c == 0)
    def _():
        st_ref[...] = jnp.zeros(st_ref.shape, F32)

    hm = hm_ref[...]
    eye = eye_ref[...]
    seg = seg_ref[...]
    w = RWKV_PACK * RWKV_HEAD
    nb = RWKV_PACK * CHUNK
    gw = GROUP_W

    def stack(x):
        return jnp.concatenate([x] * RWKV_PACK, axis=0) * hm

    streams = []
    for d, (o_ref, bo_ref) in enumerate(((o0_ref, bo0_ref), (o1_ref, bo1_ref))):
        rkv_ref, rkv_p, rkv_n, lo_ref, lo_p, lo_n = ins[d]
        pos = _chunk_pos(d, c, ncc, nch)
        seg_start = (pos == 0) | (pos == ncc)
        seg_end = (pos == ncc - 1) | (pos == nch - 1)
        rkv = _shift_mix_block(rkv_ref[...], rkv_p[...], rkv_n[...], mu_rkv_ref[...], seg_start, seg_end)
        lora = _shift_mix_block(lo_ref[...], lo_p[...], lo_n[...], mu_lo_ref[...], seg_start, seg_end)
        r, k, v = rkv[:, 0:gw], rkv[:, gw:2 * gw], rkv[:, 2 * gw:3 * gw]
        w_dn, a_dn, g_dn = lora[:, 0:128], lora[:, 128:256], lora[:, 256:384]
        vec = vec_ref[d]
        wl = vec[0:1] + _dot(jnp.tanh(w_dn), wup_ref[d])
        lw = -jnp.exp(-(jnp.maximum(-wl, 0.0) + jnp.log1p(jnp.exp(-jnp.abs(wl)))) - 0.5)
        a = jax.nn.sigmoid(vec[1:2] + _dot(a_dn, aup_ref[d]))
        kk = k * vec[2:3]
        kk = kk * lax.rsqrt(jnp.maximum(_seg_sum(kk * kk, seg), 1e-24))
        k = k * (1.0 + (a - 1.0) * vec[3:4])
        b = kk * a
        bo_ref[...] = _seg_sum(r * k * vec[4:5], seg) * v
        if d == 0:
            g_ref[...] = _dot(jax.nn.sigmoid(g_dn), gup_ref[...])
        bc = _dot_exact_lhs(tri_ref[d], lw)
        ebt = jnp.exp(jnp.sum(lw, axis=0, keepdims=True))
        einv = jnp.exp(-bc)
        khat = kk * jnp.exp(bc - lw)
        rhat = r * jnp.exp(bc)
        ks = k * einv
        bs = b * einv
        for g in range(groups):
            sl = slice(g * w, (g + 1) * w)
            streams.append(dict(
                d=d, g=g, sl=sl, o_ref=o_ref, ebt=ebt[:, sl],
                kr=jnp.concatenate([stack(khat[:, sl]), stack(rhat[:, sl])], axis=0).astype(BF16),
                bk=jnp.concatenate([stack(bs[:, sl]), stack(ks[:, sl])], axis=0).astype(BF16),
                vs=stack(v[:, sl])))

    for s in streams:
        aa = _dot_nt(s["kr"], s["bk"])
        strict, incl = strict_ref[s["d"]], incl_ref[s["d"]]
        s["akb"] = strict * aa[:nb, :nb]
        s["arb"] = (incl * aa[nb:, :nb]).astype(BF16)
        s["ak_v"] = jnp.concatenate([strict * aa[:nb, nb:], incl * aa[nb:, nb:]], axis=0).astype(BF16)
        s["minv"] = eye - off_ref[s["d"], N_LEVELS - 1] * s["akb"]
    for lv in range(N_LEVELS - 2, -1, -1):
        for s in streams:
            s["t1"] = _dot(off_ref[s["d"], lv] * s["akb"], s["minv"])
        for s in streams:
            s["minv"] = s["minv"] - _dot(s["minv"], s["t1"])
    for s in streams:
        st = st_ref[s["d"], s["g"]]
        from_state = _dot_nt(s["kr"], st)
        from_v = _dot(s["ak_v"], s["vs"])
        u = _dot(s["minv"], from_state[:nb] + from_v[:nb])
        o = from_state[nb:] + from_v[nb:] - _dot(s["arb"], u)
        s["o_ref"][:, s["sl"]] = sum(o[h * CHUNK:(h + 1) * CHUNK] for h in range(RWKV_PACK))
        upd = _dot(s["vs"].T, s["bk"][nb:]) - _dot(u.T, s["bk"][:nb])
        st_ref[s["d"], s["g"]] = (st + upd) * s["ebt"]


def rwkv_scan(p, rkv_off, lora_off, params, *, n_ctx, name="rwkv_scan"):
    l = p.shape[0]
    hw = GROUP_W
    w = RWKV_PACK * RWKV_HEAD
    groups = hw // w
    nch, ncc = l // CHUNK, n_ctx // CHUNK
    nb = RWKV_PACK * CHUNK
    halo = 8
    per_chunk = CHUNK // halo

    def shared(d):
        return pl.BlockSpec((CHUNK, hw), lambda c: (_chunk_pos(d, c, ncc, nch), 0))

    def piece(d, off, width):
        assert off % width == 0
        cb = off // width
        pos = lambda c: _chunk_pos(d, c, ncc, nch)
        return [pl.BlockSpec((CHUNK, width), lambda c: (pos(c), cb)),
                pl.BlockSpec((halo, width), lambda c: (jnp.maximum(pos(c) * per_chunk - 1, 0), cb)),
                pl.BlockSpec((halo, width), lambda c: (jnp.minimum((pos(c) + 1) * per_chunk, l // halo - 1), cb))]

    def const(a):
        return pl.BlockSpec(a.shape, lambda c: (0,) * a.ndim)

    bd = lambda a: np.kron(np.eye(RWKV_PACK, dtype=np.float32), a)
    strict = jnp.asarray(np.stack([bd(_CC["strict"][d]) for d in range(2)]), F32)
    incl = jnp.asarray(np.stack([bd(_CC["strict"][d] + _CC["eye"][d]) for d in range(2)]), F32)
    off = jnp.asarray(np.stack([np.stack([bd(_CC["off"][d, lv]) for lv in range(N_LEVELS)]) for d in range(2)]), F32)
    eye = jnp.asarray(np.eye(nb, dtype=np.float32))
    hm = jnp.asarray(np.kron(np.eye(RWKV_PACK, dtype=np.float32), np.ones((CHUNK, RWKV_HEAD), np.float32)))
    tri = jnp.asarray(_CC["tri"], BF16)
    seg = jnp.asarray(np.kron(np.eye(RWKV_HEADS, dtype=np.float32), np.ones((RWKV_HEAD, RWKV_HEAD), np.float32)), BF16)
    consts = list(params) + [seg, tri, strict, incl, off, eye, hm]
    in_specs = []
    for d in range(2):
        in_specs += piece(d, rkv_off, 3 * hw) + piece(d, lora_off, 3 * 128)
    return pl.pallas_call(
        functools.partial(_rwkv_kernel, groups=groups, ncc=ncc, nch=nch),
        out_shape=[jax.ShapeDtypeStruct((l, hw), F32)] * 5,
        grid=(nch,),
        in_specs=in_specs + [const(a) for a in consts],
        out_specs=[shared(0), shared(1), shared(0), shared(1), shared(0)],
        scratch_shapes=[pltpu.VMEM((2, groups, nb, nb), F32)],
        compiler_params=_cparams(("arbitrary",)),
        name=name,
    )(*([p] * len(in_specs)), *consts)


def _rwkv_post_kernel(of_ref, ob_ref, bf_ref, bb_ref, g_ref, ln_ref, seg_ref, o_ref):
    seg = seg_ref[...]
    o = of_ref[...] + ob_ref[...]
    mean = _seg_sum(o, seg) * (1.0 / RWKV_HEAD)
    oc = o - mean
    var = _seg_sum(oc * oc, seg) * (1.0 / RWKV_HEAD)
    y = oc * lax.rsqrt(var + RWKV_LN_EPS) * ln_ref[0:1] + ln_ref[1:2] + bf_ref[...] + bb_ref[...]
    o_ref[...] = (y * g_ref[...]).astype(o_ref.dtype)


def rwkv_post(o_f, o_b, bo_f, bo_b, g, ln_g, ln_b, *, tm=768, name="rwkv_post"):
    l, w = o_f.shape
    tm = math.gcd(l, tm)
    row = pl.BlockSpec((tm, w), lambda i: (i, 0))
    ln = jnp.zeros((8, w), F32).at[0].set(ln_g).at[1].set(ln_b)
    seg = jnp.asarray(np.kron(np.eye(RWKV_HEADS, dtype=np.float32), np.ones((RWKV_HEAD, RWKV_HEAD), np.float32)), BF16)
    return pl.pallas_call(
        _rwkv_post_kernel,
        out_shape=jax.ShapeDtypeStruct((l, w), BF16),
        grid=(l // tm,),
        in_specs=[row] * 5 + [pl.BlockSpec((8, w), lambda i: (0, 0)), pl.BlockSpec((w, w), lambda i: (0, 0))],
        out_specs=row,
        compiler_params=_cparams(("parallel",)),
        name=name,
    )(o_f, o_b, bo_f, bo_b, g, ln, seg)


def _rms(x, g, eps=EPS):
    return x * lax.rsqrt(jnp.mean(x * x, axis=-1, keepdims=True) + eps) * g


def _heads_major(t, h):
    l = t.shape[0]
    return t.reshape(l, h, -1).transpose(1, 0, 2)


def _rope_tables(n):
    rows = n // GRID_W
    row = jnp.repeat(jnp.arange(rows, dtype=F32), GRID_W)
    col = jnp.tile(jnp.arange(GRID_W, dtype=F32), rows)
    n_freq = MLA_ROPE // 4
    freqs = ROPE_BASE ** (-jnp.arange(n_freq, dtype=F32) / n_freq)
    ang = jnp.stack([row[:, None] * freqs, col[:, None] * freqs], axis=1)
    return jnp.cos(ang), jnp.sin(ang)


def _rope(x, cos, sin):
    xs = x.reshape(x.shape[:-1] + (2, 2, MLA_ROPE // 4))
    x1, x2 = xs[..., 0, :], xs[..., 1, :]
    return jnp.stack([x1 * cos - x2 * sin, x1 * sin + x2 * cos], axis=-2).reshape(x.shape)


def _rope_lanes(x, cos_t, sin_t):
    n = x.shape[1]
    lane = lax.broadcasted_iota(jnp.int32, x.shape, 1)
    swapped = jnp.where(lane % 32 < 16, pltpu.roll(x, n - 16, 1), pltpu.roll(x, 16, 1))
    return x * cos_t + swapped * sin_t


def _mla_prep_kernel(cq_ref, ckv_ref, kr_ref, cos_ref, sin_ref, qn_ref, kvn_ref, wq_ref, wkv_ref,
                     q_ref, k_ref, v_ref):
    cq = cq_ref[...]
    ms = jnp.sum(cq * cq, axis=-1, keepdims=True) * (1.0 / MLA_Q_LORA)
    q = _dot(cq * lax.rsqrt(ms + EPS) * qn_ref[...], wq_ref[...])
    ckv = ckv_ref[...]
    kvn = ckv * lax.rsqrt(jnp.mean(ckv * ckv, axis=-1, keepdims=True) + EPS) * kvn_ref[...]
    kv = _dot(kvn, wkv_ref[...])
    cos_t, sin_t = cos_ref[...], sin_ref[...]
    hn = MLA_HEADS * MLA_NOPE
    scale = (MLA_NOPE + MLA_ROPE) ** -0.5
    q_rope = _rope_lanes(q[:, hn:], cos_t, sin_t)
    k_rope = _rope_lanes(kr_ref[...], cos_t[:, :128], sin_t[:, :128])
    k_rope_hi = pltpu.roll(k_rope, MLA_ROPE, 1)
    lane = lax.broadcasted_iota(jnp.int32, k_rope.shape, 1)
    ones_col = jnp.where(lane == 0, 1.0, 0.0)
    for h in range(MLA_HEADS):
        pair = q_rope[:, (h // 2) * 128:(h // 2 + 1) * 128]
        q_ref[h] = (jnp.concatenate([q[:, h * MLA_NOPE:(h + 1) * MLA_NOPE], pair], axis=1) * scale).astype(q_ref.dtype)
        k_ref[h] = jnp.concatenate([kv[:, h * MLA_NOPE:(h + 1) * MLA_NOPE], k_rope if h % 2 == 0 else k_rope_hi],
                                   axis=1).astype(k_ref.dtype)
        v_ref[h] = jnp.concatenate([kv[:, hn + h * MLA_V:hn + (h + 1) * MLA_V], ones_col], axis=1).astype(v_ref.dtype)


def mla_prep(p, m, q_norm, w_q_up, kv_norm, w_kv_up, *, tm=256, name="mla_prep"):
    l = p.shape[0]
    n = l - m
    dq = MLA_NOPE + MLA_ROPE
    wq = w_q_up.reshape(MLA_Q_LORA, MLA_HEADS, dq)
    wq = jnp.concatenate([wq[..., :MLA_NOPE].reshape(MLA_Q_LORA, -1), wq[..., MLA_NOPE:].reshape(MLA_Q_LORA, -1)], axis=1)
    wq = jnp.pad(wq, ((0, PACKED["cq"][3] - MLA_Q_LORA), (0, 0)))
    qn = jnp.pad(q_norm, (0, PACKED["cq"][3] - MLA_Q_LORA)).reshape(1, -1)
    wkv = w_kv_up.reshape(MLA_KV_LORA, MLA_HEADS, MLA_NOPE + MLA_V)
    wkv = jnp.concatenate([wkv[..., :MLA_NOPE].reshape(MLA_KV_LORA, -1), wkv[..., MLA_NOPE:].reshape(MLA_KV_LORA, -1)],
                          axis=1)
    cos, sin = _rope_tables(n)
    cos64 = jnp.repeat(cos, 2, axis=1).reshape(n, MLA_ROPE)
    sin64 = (jnp.repeat(sin, 2, axis=1) * jnp.array([-1.0, 1.0, -1.0, 1.0])[None, :, None]).reshape(n, MLA_ROPE)
    cos_t = jnp.tile(jnp.concatenate([jnp.ones((m, MLA_ROPE), F32), cos64], axis=0), (1, MLA_HEADS))
    sin_t = jnp.tile(jnp.concatenate([jnp.zeros((m, MLA_ROPE), F32), sin64], axis=0), (1, MLA_HEADS))

    def col(nm):
        _, _, off, wp = PACKED[nm]
        return pl.BlockSpec((tm, wp), lambda i: (i, off // wp))

    const = lambda a: pl.BlockSpec(a.shape, lambda i: (0,) * a.ndim)
    row = lambda wd: pl.BlockSpec((tm, wd), lambda i: (i, 0))
    out_spec = pl.BlockSpec((MLA_HEADS, tm, 256), lambda i: (0, i, 0))
    return pl.pallas_call(
        _mla_prep_kernel,
        out_shape=[jax.ShapeDtypeStruct((MLA_HEADS, l, 256), BF16)] * 3,
        grid=(l // tm,),
        in_specs=[col("cq"), col("ckv"), col("k_rope"), row(256), row(256), const(qn), const(kv_norm.reshape(1, -1)),
                  const(wq), const(wkv)],
        out_specs=[out_spec] * 3,
        compiler_params=_cparams(("parallel",)),
        name=name,
    )(p, p, p, cos_t, sin_t, qn, kv_norm.reshape(1, -1), wq, wkv)


def mla_mixer(p, m, q_norm, w_q_up, kv_norm, w_kv_up):
    qh, kh, vh = mla_prep(p, m, q_norm, w_q_up, kv_norm, w_kv_up)
    o_ctx = flash_attention(qh[:, :m], kh[:, :m], vh[:, :m], dv=MLA_V, tq=m, name="mla_attn_ctx")
    o_lat = flash_attention(qh[:, m:], kh, vh, dv=MLA_V, tq=512, name="mla_attn_lat")
    return jnp.concatenate([o_ctx, o_lat], axis=0)


def _shift_mix(z, m, mu_prev, mu_next):
    def one(zp):
        zprev = jnp.pad(zp[:-1], ((1, 0), (0, 0)))
        znext = jnp.pad(zp[1:], ((0, 1), (0, 0)))
        return zp + mu_prev * (zprev - zp) + mu_next * (znext - zp)
    return jnp.concatenate([one(z[:m]), one(z[m:])], axis=0)


def rwkv7_mixer(p, m, mu, w0, w_up, a0, a_up, k_k, k_a, u, g_up, ln_g, ln_b):
    mu_rkv, mu_lo = mu[:, :3 * GROUP_W], mu[:, 3 * GROUP_W:]
    vec = jnp.zeros((2, 8, GROUP_W), F32)
    for i, t in enumerate((w0, a0, k_k, k_a, u)):
        vec = vec.at[:, i].set(t)
    wup = jnp.zeros((2, 128, GROUP_W), F32)
    aup = jnp.zeros((2, 128, GROUP_W), F32)
    for d in range(2):
        wup = wup.at[d, d * RWKV_W_LORA:(d + 1) * RWKV_W_LORA].set(w_up[d])
        aup = aup.at[d, d * RWKV_A_LORA:(d + 1) * RWKV_A_LORA].set(a_up[d])
    outs = rwkv_scan(p, PACKED["rwkv_r"][2], PACKED["w_dn"][2], (mu_rkv, mu_lo, vec, wup, aup, g_up), n_ctx=m)
    return rwkv_post(*outs, ln_g, ln_b)


PACKED = {}


def _build_packed():
    orig = dict(cq=(0, MLA_Q_LORA), ckv=(MLA_Q_LORA, MLA_KV_LORA), k_rope=(MLA_Q_LORA + MLA_KV_LORA, MLA_ROPE))
    b = MLA_COLS
    for i, nm in enumerate(("rwkv_r", "rwkv_k", "rwkv_v")):
        orig[nm] = (b + i * GROUP_W, GROUP_W)
    b += 3 * GROUP_W
    orig.update(w_dn=(b, 2 * RWKV_W_LORA), a_dn=(b + 2 * RWKV_W_LORA, 2 * RWKV_A_LORA),
                g_dn=(b + 2 * RWKV_W_LORA + 2 * RWKV_A_LORA, RWKV_G_LORA))
    b = IN_SPLITS[1]
    orig.update(gla_q=(b, GLA_KD), gla_k=(b + GLA_KD, GLA_KD), gla_v=(b + 2 * GLA_KD, GROUP_W),
                gla_a=(b + GLA_SPLITS[2], 2 * GLA_GATE_RANK), gla_r=(b + GLA_SPLITS[3], GROUP_W))
    b = IN_SPLITS[2]
    for i, nm in enumerate(("hgrn_q", "hgrn_f0", "hgrn_f1", "hgrn_i", "hgrn_g")):
        orig[nm] = (b + i * GROUP_W, GROUP_W)
    order = [(nm, 512) for nm in ("rwkv_r", "rwkv_k", "rwkv_v", "gla_v", "gla_r", "hgrn_q", "hgrn_f0", "hgrn_f1",
                                  "hgrn_i", "hgrn_g", "cq")]
    orig["pad"] = (0, 0)
    order += [(nm, 256) for nm in ("gla_q", "gla_k")]
    order += [(nm, 128) for nm in ("w_dn", "a_dn", "g_dn", "gla_a", "k_rope", "pad")]
    order += [("ckv", 256)]
    off = 0
    for nm, wp in order:
        PACKED[nm] = (orig[nm][0], orig[nm][1], off, wp)
        off += wp
    return off


N_PACKED = _build_packed()


def pack_w_in(w):
    pieces = []
    for o, wd, _, wp in PACKED.values():
        pieces.append(w[:, o:o + wd])
        if wp > wd:
            pieces.append(jnp.zeros((w.shape[0], wp - wd), w.dtype))
    return jnp.concatenate(pieces, axis=1)


def _pcol(p, nm):
    _, wd, off, _ = PACKED[nm]
    return p[:, off:off + wd]


def _blk(nm):
    return (PACKED[nm][2], PACKED[nm][3])


def gla_mixer(p, m, a_up, a_bias, g_norm):
    par = jnp.zeros((2, 136, GLA_KD), F32)
    for d in range(2):
        par = par.at[d, d * GLA_GATE_RANK:(d + 1) * GLA_GATE_RANK].set(a_up[d]).at[d, 128].set(a_bias[d])
    cols = [[_blk("gla_q"), _blk("gla_k"), _blk("gla_v"), _blk("gla_a")]] * 2
    o_f, o_b = gla_scan(p, cols, par, mode="gla", heads=GLA_HEADS, dk=GLA_DK, dv=GLA_DV, pack=4, n_ctx=m,
                        name="gla_scan")
    return mix_post(o_f, o_b, p, PACKED["gla_r"][2], g_norm, heads=GLA_HEADS, dv=GLA_DV, name="gla_post")


def hgrn2_mixer(p, m, lb, g_norm):
    par = jnp.zeros((2, 8, GROUP_W), F32).at[:, 0].set(jnp.log(lb)).at[:, 1].set(jnp.log1p(-lb)).at[:, 2].set(1.0 - lb)
    cols = [[_blk("hgrn_q"), _blk("hgrn_f%d" % d), _blk("hgrn_i")] for d in range(2)]
    o_f, o_b = gla_scan(p, cols, par, mode="hgrn", heads=HGRN_HEADS, dk=HGRN_EXPAND, dv=HGRN_DV, pack=2, n_ctx=m,
                        name="hgrn_scan")
    return mix_post(o_f, o_b, p, PACKED["hgrn_g"][2], g_norm, heads=HGRN_HEADS, dv=HGRN_DV, name="hgrn_post")


def dense_ffn(xs, h2, w1, w3, w2, j, gates2, m):
    act = pmatmul(h2, w1, (j,), w3=w3, tm=1408, tn=512, out_dtype=BF16, name="ffn_up")
    return pmatmul(act, w2, (j,), tm=768, tn=256, res=xs, gates=gates2, m_ctx=m, name="ffn_down")


def _route(top_idx, weights):
    n_pairs = top_idx.size
    r = n_pairs + N_EXPERTS * MOE_TILE
    n_tiles = r // MOE_TILE
    e_flat = top_idx.reshape(-1)
    onehot = (e_flat[:, None] == jnp.arange(N_EXPERTS)[None, :]).astype(jnp.int32)
    rank = jnp.sum((jnp.cumsum(onehot, axis=0) - onehot) * onehot, axis=1)
    counts = jnp.sum(onehot, axis=0)
    padded = (counts + MOE_TILE - 1) // MOE_TILE * MOE_TILE
    ends = jnp.cumsum(padded)
    dest = (ends - padded)[e_flat] + rank
    src_token = jnp.zeros((r,), jnp.int32).at[dest].set(jnp.arange(n_pairs, dtype=jnp.int32) // TOP_K)
    row_gate = jnp.zeros((r,), F32).at[dest].set(weights.reshape(-1))
    tile_start = jnp.arange(n_tiles, dtype=jnp.int32) * MOE_TILE
    tile_expert = jnp.minimum(jnp.sum(tile_start[:, None] >= ends[None, :], axis=1), N_EXPERTS - 1)
    meta = jnp.concatenate([tile_expert.astype(jnp.int32), (ends[-1:] // MOE_TILE).astype(jnp.int32)])
    return src_token, row_gate, dest.reshape(top_idx.shape), meta


def _combine_kernel(x_ref, y_ref, gate_ref, o_ref, *, m_ctx, tm, d):
    i = pl.program_id(0)
    rows = i * tm + lax.broadcasted_iota(jnp.int32, (tm, d), 0)
    g = jnp.where(rows < m_ctx, gate_ref[0:1, :], gate_ref[1:2, :])
    o_ref[...] = x_ref[...] + g * (y_ref[:, 0:d].astype(F32) + y_ref[:, d:2 * d].astype(F32))


def moe_combine(xs, y_tok, gates2, *, m_ctx, tm=256, name="moe_combine"):
    l, d = xs.shape
    return pl.pallas_call(
        functools.partial(_combine_kernel, m_ctx=m_ctx, tm=tm, d=d),
        out_shape=jax.ShapeDtypeStruct((l, d), F32),
        grid=(l // tm,),
        in_specs=[pl.BlockSpec((tm, d), lambda i: (i, 0)), pl.BlockSpec((tm, TOP_K * d), lambda i: (i, 0)),
                  pl.BlockSpec((2, d), lambda i: (0, 0))],
        out_specs=pl.BlockSpec((tm, d), lambda i: (i, 0)),
        compiler_params=_cparams(("parallel",)),
        name=name,
    )(xs, y_tok, gates2)


def moe_ffn(xs, h2f, router, w1, w3, w2, j, gates2, m):
    l, d = h2f.shape
    router_p = jnp.pad(router[j], ((0, 0), (0, 128 - N_EXPERTS)))
    logits = pmatmul(h2f, router_p, tm=768, tn=128, precision=lax.Precision.HIGHEST, name="moe_router")[:, :N_EXPERTS]
    top_vals, top_idx = lax.top_k(logits, TOP_K)
    weights = jax.nn.softmax(top_vals, axis=-1)
    src_token, row_gate, dest, meta = _route(top_idx, weights)
    x_sorted = jnp.take(h2f.astype(BF16), src_token, axis=0)
    act = gmatmul(meta, x_sorted, w1, j, w3=w3, tn=512, out_dtype=BF16, rowscale=row_gate[:, None], name="moe_up")
    y = gmatmul(meta, act, w2, j, tn=512, out_dtype=BF16, name="moe_down")
    y_tok = jnp.take(y, dest.reshape(-1), axis=0).reshape(l, TOP_K * d)
    return moe_combine(xs, y_tok, gates2, m_ctx=m)


def kernel(x, c, ctx, c_ctx, norm1_g, norm2_g, w_mod, b_mod, w_in, w_out, mla_q_norm, mla_w_q_up, mla_kv_norm, mla_w_kv_up, rwkv_mu, rwkv_w0, rwkv_w_up, rwkv_a0, rwkv_a_up, rwkv_k_k, rwkv_k_a, rwkv_u, rwkv_g_up, rwkv_ln_g, rwkv_ln_b, gla_a_up, gla_a_bias, gla_norm, hgrn_lb, hgrn_norm, ffn_w1, ffn_w3, ffn_w2, moe_router, moe_w1, moe_w3, moe_w2, final_norm_g):
    m, n, d = ctx.shape[1], x.shape[1], x.shape[2]
    lb_all = jnp.cumsum(jax.nn.softmax(hgrn_lb.astype(F32), axis=1), axis=1)
    lb_all = lb_all - lb_all[:, :1]
    xs = jnp.concatenate([ctx[0], x[0]], axis=0)
    cvec = jnp.zeros((16, d), F32).at[0].set(jax.nn.silu(c[0])).at[1].set(jax.nn.silu(c_ctx))
    for l in range(DEPTH):
        mod = pmatmul(cvec, w_mod, (l,), tm=16, tn=1024, name="mod") + b_mod[l]
        mods = jnp.stack([mod[1].reshape(6, d), mod[0].reshape(6, d)], axis=1)
        sh1, sc1, gt1, sh2, sc2, gt2 = (mods[i] for i in range(6))

        h = norm_mod(xs, norm1_g[l], sh1, sc1, m_ctx=m, out_dtype=BF16)
        p = pmatmul(h, pack_w_in(w_in[l]), tm=1408, tn=1024, name="w_in")
        o = [
            mla_mixer(p, m, mla_q_norm[l], mla_w_q_up[l], mla_kv_norm[l], mla_w_kv_up[l]),
            rwkv7_mixer(p, m, rwkv_mu[l], rwkv_w0[l], rwkv_w_up[l], rwkv_a0[l], rwkv_a_up[l],
                        rwkv_k_k[l], rwkv_k_a[l], rwkv_u[l], rwkv_g_up[l], rwkv_ln_g[l], rwkv_ln_b[l]),
            gla_mixer(p, m, gla_a_up[l], gla_a_bias[l], gla_norm[l]),
            hgrn2_mixer(p, m, lb_all[:, l], hgrn_norm[l]),
        ]
        xs = pmatmul(o, w_out, (l,), tm=1408, tn=512, res=xs, gates=gt1, m_ctx=m, name="w_out")

        j = l // 2
        if l % 2 == 0:
            h2 = norm_mod(xs, norm2_g[l], sh2, sc2, m_ctx=m, out_dtype=BF16)
            xs = dense_ffn(xs, h2, ffn_w1, ffn_w3, ffn_w2, j, gt2, m)
        else:
            h2f = norm_mod(xs, norm2_g[l], sh2, sc2, m_ctx=m, out_dtype=F32)
            xs = moe_ffn(xs, h2f, moe_router, moe_w1, moe_w3, moe_w2, j, gt2, m)
    zeros2 = jnp.zeros((2, d), F32)
    out = norm_mod(xs, final_norm_g, zeros2, zeros2, m_ctx=m, out_dtype=F32, name="final_norm")
    return out[m:][None]
```

```python
import functools
import math

import numpy as np
import jax
import jax.numpy as jnp
from jax import lax
from jax.experimental import pallas as pl
from jax.experimental.pallas import tpu as pltpu

F32 = jnp.float32
BF16 = jnp.bfloat16

DEPTH = 4
GRID_W = 64
EPS = 1e-6
GROUP_W = 512
MLA_HEADS, MLA_NOPE, MLA_ROPE, MLA_V = 4, 128, 64, 128
MLA_Q_LORA, MLA_KV_LORA = 384, 256
ROPE_BASE = 10000.0
RWKV_HEADS, RWKV_HEAD = 8, 64
RWKV_W_LORA, RWKV_A_LORA, RWKV_G_LORA = 64, 64, 128
RWKV_LN_EPS = 64e-5
GLA_HEADS, GLA_DK, GLA_DV = 4, 64, 128
GLA_GATE_RANK, GLA_GATE_NORM = 16, 16.0
HGRN_HEADS, HGRN_EXPAND, HGRN_DV = 4, 128, 128
CHUNK = 64
N_EXPERTS, TOP_K = 8, 2
MLA_COLS = MLA_Q_LORA + MLA_KV_LORA + MLA_ROPE
RWKV_COLS = 3 * GROUP_W + 2 * RWKV_W_LORA + 2 * RWKV_A_LORA + RWKV_G_LORA
GLA_KD = GLA_HEADS * GLA_DK
GLA_COLS = 2 * GLA_KD + GROUP_W + 2 * GLA_GATE_RANK + GROUP_W
IN_SPLITS = (MLA_COLS, MLA_COLS + RWKV_COLS, MLA_COLS + RWKV_COLS + GLA_COLS)
RWKV_SPLITS = (GROUP_W, 2 * GROUP_W, 3 * GROUP_W, 3 * GROUP_W + 2 * RWKV_W_LORA,
               3 * GROUP_W + 2 * RWKV_W_LORA + 2 * RWKV_A_LORA)
GLA_SPLITS = (GLA_KD, 2 * GLA_KD, 2 * GLA_KD + GROUP_W, 2 * GLA_KD + GROUP_W + 2 * GLA_GATE_RANK)

V7X_VMEM_LIMIT = 56 * 1024 * 1024
V7X_MXU = 256
N_LEVELS = 6
RWKV_PACK = V7X_MXU // RWKV_HEAD
MOE_TILE = 512


def _cparams(sem, vmem=V7X_VMEM_LIMIT):
    return pltpu.CompilerParams(dimension_semantics=sem, vmem_limit_bytes=vmem)


def _dot(a, b):
    return jnp.dot(a.astype(BF16), b.astype(BF16), preferred_element_type=F32)


def _dot_nt(a, b):
    return lax.dot_general(a.astype(BF16), b.astype(BF16), (((1,), (1,)), ((), ())), preferred_element_type=F32)


def _split2(x):
    hi = x.astype(BF16)
    lo = (x - hi.astype(F32)).astype(BF16)
    return hi, lo


def _dot_exact_lhs(m_bf16, x):
    hi, lo = _split2(x)
    return (jnp.dot(m_bf16, hi, preferred_element_type=F32) + jnp.dot(m_bf16, lo, preferred_element_type=F32))


def _mm_kernel(*refs, n_x, n_w, has_res, m_ctx, tm, precision):
    it = iter(refs)
    x_refs = [next(it) for _ in range(n_x)]
    w_refs = [next(it) for _ in range(n_w)]
    res_ref = next(it) if has_res else None
    gate_ref = next(it) if has_res else None
    o_ref = next(it)
    wb_refs = [next(it) for _ in range(n_w)] if precision is None else w_refs
    i = pl.program_id(1)

    if precision is None:
        @pl.when(i == 0)
        def _():
            for w_ref, wb_ref in zip(w_refs, wb_refs):
                wb_ref[...] = w_ref[...].astype(BF16)
        x = jnp.concatenate([x_ref[...].astype(BF16) for x_ref in x_refs], axis=1)
    else:
        x = x_refs[0][...]

    acc = jnp.dot(x, wb_refs[0][...], preferred_element_type=F32, precision=precision)
    if n_w == 2:
        acc3 = jnp.dot(x, wb_refs[1][...], preferred_element_type=F32)
        acc = acc * jax.nn.sigmoid(acc) * acc3
    if has_res:
        rows = i * tm + lax.broadcasted_iota(jnp.int32, acc.shape, 0)
        g = jnp.where(rows < m_ctx, gate_ref[0:1, :], gate_ref[1:2, :])
        acc = res_ref[...] + g * acc
    o_ref[...] = acc.astype(o_ref.dtype)


def pmatmul(x, w, widx=(), *, w3=None, tm, tn, out_dtype=F32, res=None, gates=None, m_ctx=0, precision=None,
            name="mm"):
    xs = list(x) if isinstance(x, (list, tuple)) else [x]
    m = xs[0].shape[0]
    k = sum(xi.shape[1] for xi in xs)
    n = w.shape[-1]
    tm = math.gcd(m, tm)
    assert w.shape[-2] == k and tm % 16 == 0
    nj, ni = pl.cdiv(n, tn), m // tm
    lead = (None,) * len(widx)
    w_spec = pl.BlockSpec(lead + (k, tn), lambda j, i: tuple(widx) + (0, j))
    ws = [w] if w3 is None else [w, w3]
    in_specs = [pl.BlockSpec((tm, xi.shape[1]), lambda j, i: (i, 0)) for xi in xs] + [w_spec] * len(ws)
    args = xs + ws
    if res is not None:
        in_specs += [pl.BlockSpec((tm, tn), lambda j, i: (i, j)), pl.BlockSpec((2, tn), lambda j, i: (0, j))]
        args += [res, gates]
    kern = functools.partial(_mm_kernel, n_x=len(xs), n_w=len(ws), has_res=res is not None, m_ctx=m_ctx, tm=tm,
                             precision=precision)
    return pl.pallas_call(
        kern,
        out_shape=jax.ShapeDtypeStruct((m, n), out_dtype),
        grid=(nj, ni),
        in_specs=in_specs,
        out_specs=pl.BlockSpec((tm, tn), lambda j, i: (i, j)),
        scratch_shapes=[pltpu.VMEM((k, tn), BF16) for _ in ws] if precision is None else [],
        compiler_params=_cparams(("arbitrary", "arbitrary")),
        name=name,
    )(*args)


def _gmm_kernel(meta_ref, x_ref, *refs, n_w, has_rowscale, n_tiles):
    it = iter(refs)
    w_refs = [next(it) for _ in range(n_w)]
    rs_ref = next(it) if has_rowscale else None
    o_ref = next(it)
    wb_refs = [next(it) for _ in range(n_w)]
    t = pl.program_id(1)
    e = meta_ref[t]
    e_prev = meta_ref[jnp.maximum(t - 1, 0)]

    @pl.when((t == 0) | (e != e_prev))
    def _():
        for w_ref, wb_ref in zip(w_refs, wb_refs):
            wb_ref[...] = w_ref[...].astype(BF16)

    @pl.when(t < meta_ref[n_tiles])
    def _():
        x = x_ref[...].astype(BF16)
        acc = jnp.dot(x, wb_refs[0][...], preferred_element_type=F32)
        if n_w == 2:
            acc3 = jnp.dot(x, wb_refs[1][...], preferred_element_type=F32)
            acc = acc * jax.nn.sigmoid(acc) * acc3
        if has_rowscale:
            acc = acc * rs_ref[...]
        o_ref[...] = acc.astype(o_ref.dtype)

    @pl.when(t >= meta_ref[n_tiles])
    def _():
        o_ref[...] = jnp.zeros(o_ref.shape, o_ref.dtype)


def gmatmul(meta, x, w, jl, *, w3=None, tn, out_dtype, rowscale=None, name):
    r, k = x.shape
    n = w.shape[-1]
    n_tiles = r // MOE_TILE
    ws = [w] if w3 is None else [w, w3]
    w_spec = pl.BlockSpec((None, None, k, tn), lambda j, t, mr: (jl, mr[t], 0, j))
    in_specs = [pl.BlockSpec((MOE_TILE, k), lambda j, t, mr: (t, 0))] + [w_spec] * len(ws)
    args = [x] + ws
    if rowscale is not None:
        in_specs.append(pl.BlockSpec((MOE_TILE, 1), lambda j, t, mr: (t, 0)))
        args.append(rowscale)
    return pl.pallas_call(
        functools.partial(_gmm_kernel, n_w=len(ws), has_rowscale=rowscale is not None, n_tiles=n_tiles),
        out_shape=jax.ShapeDtypeStruct((r, n), out_dtype),
        grid_spec=pltpu.PrefetchScalarGridSpec(
            num_scalar_prefetch=1,
            grid=(n // tn, n_tiles),
            in_specs=in_specs,
            out_specs=pl.BlockSpec((MOE_TILE, tn), lambda j, t, mr: (t, j)),
            scratch_shapes=[pltpu.VMEM((k, tn), BF16) for _ in ws]),
        compiler_params=_cparams(("arbitrary", "arbitrary")),
        name=name,
    )(meta, *args)


def _norm_kernel(x_ref, g_ref, sh_ref, sc_ref, *rest, m_ctx, tm, with_router):
    i = pl.program_id(0)
    x = x_ref[...]
    y = x * lax.rsqrt(jnp.mean(x * x, axis=-1, keepdims=True) + EPS) * g_ref[...]
    rows = i * tm + lax.broadcasted_iota(jnp.int32, x.shape, 0)
    is_ctx = rows < m_ctx
    sc = jnp.where(is_ctx, sc_ref[0:1, :], sc_ref[1:2, :])
    sh = jnp.where(is_ctx, sh_ref[0:1, :], sh_ref[1:2, :])
    h = y * (1.0 + sc) + sh
    if with_router:
        r_ref, o_ref, logit_ref = rest
        logit_ref[...] = jnp.dot(h, r_ref[...], preferred_element_type=F32, precision=lax.Precision.HIGHEST)
    else:
        (o_ref,) = rest
    o_ref[...] = h.astype(o_ref.dtype)


def norm_mod(x, g, shift2, scale2, *, m_ctx, out_dtype, router=None, tm=768, name="norm_mod"):
    m, d = x.shape
    tm = math.gcd(m, tm)
    row = pl.BlockSpec((tm, d), lambda i: (i, 0))
    in_specs = [row, pl.BlockSpec((1, d), lambda i: (0, 0)),
                pl.BlockSpec((2, d), lambda i: (0, 0)), pl.BlockSpec((2, d), lambda i: (0, 0))]
    args = [x, g.reshape(1, d), shift2, scale2]
    out_shape = jax.ShapeDtypeStruct((m, d), out_dtype)
    out_specs = row
    if router is not None:
        in_specs.append(pl.BlockSpec(router.shape, lambda i: (0, 0)))
        args.append(router)
        out_shape = [out_shape, jax.ShapeDtypeStruct((m, router.shape[1]), F32)]
        out_specs = [row, pl.BlockSpec((tm, router.shape[1]), lambda i: (i, 0))]
    return pl.pallas_call(
        functools.partial(_norm_kernel, m_ctx=m_ctx, tm=tm, with_router=router is not None),
        out_shape=out_shape,
        grid=(m // tm,),
        in_specs=in_specs,
        out_specs=out_specs,
        compiler_params=_cparams(("parallel",)),
        name=name,
    )(*args)


def _attn_kernel(q_ref, k_ref, v_ref, o_ref, *, tk, n_kv, dv):
    q = q_ref[...]
    m = acc = None
    for j in range(n_kv):
        kb = k_ref[j * tk:(j + 1) * tk, :]
        vb = v_ref[j * tk:(j + 1) * tk, :]
        s = lax.dot_general(q, kb, (((1,), (1,)), ((), ())), preferred_element_type=F32)
        m_blk = jnp.max(s, axis=-1, keepdims=True)
        m_new = m_blk if j == 0 else jnp.maximum(m, m_blk)
        pv = jnp.dot(jnp.exp(s - m_new).astype(BF16), vb, preferred_element_type=F32)
        acc = pv if j == 0 else jnp.exp(m - m_new) * acc + pv
        m = m_new
    o_ref[...] = (acc[:, :dv] / acc[:, dv:dv + 1]).astype(o_ref.dtype)


def _attn_kv_tile(lk, cap=1408):
    return max(t for t in range(128, cap + 1, 128) if lk % t == 0)


def flash_attention(q, k, v_ext, *, dv, tq, name="mla_attn"):
    h, lq, dqk = q.shape
    lk, dve = k.shape[1], v_ext.shape[2]
    tk = _attn_kv_tile(lk)
    assert lq % tq == 0
    return pl.pallas_call(
        functools.partial(_attn_kernel, tk=tk, n_kv=lk // tk, dv=dv),
        out_shape=jax.ShapeDtypeStruct((lq, h * dv), BF16),
        grid=(h, lq // tq),
        in_specs=[pl.BlockSpec((None, tq, dqk), lambda hh, i: (hh, i, 0)),
                  pl.BlockSpec((None, lk, dqk), lambda hh, i: (hh, 0, 0)),
                  pl.BlockSpec((None, lk, dve), lambda hh, i: (hh, 0, 0))],
        out_specs=pl.BlockSpec((tq, dv), lambda hh, i: (i, hh)),
        compiler_params=_cparams(("parallel", "parallel")),
        name=name,
    )(q, k, v_ext)


def _chunk_constants():
    c = CHUNK
    t = np.arange(c)
    tri = (t[None, :] <= t[:, None]).astype(np.float32)
    strict = (t[None, :] < t[:, None]).astype(np.float32)
    eye = np.eye(c, dtype=np.float32)
    seg, off = [], []
    for lv in range(N_LEVELS):
        s = c >> (lv + 1)
        blk = t // s
        same = blk[:, None] == blk[None, :]
        odd = (blk % 2 == 1)[:, None]
        seg.append(np.where(odd, same & (t[None, :] <= t[:, None]), same & (t[None, :] > t[:, None])).astype(np.float32))
        off.append((odd & (blk[None, :] == blk[:, None] - 1)).astype(np.float32))
    seg, off = np.stack(seg), np.stack(off)

    def both(a):
        return np.stack([a, a[..., ::-1, ::-1]])

    return {k: both(v) for k, v in dict(tri=tri, strict=strict, eye=eye, seg=seg, off=off).items()}


_CC = _chunk_constants()


def _chunk_pos(d, c, n_ctx_chunks, n_chunks):
    back = jnp.where(c < n_ctx_chunks, n_ctx_chunks - 1 - c, n_chunks + n_ctx_chunks - 1 - c)
    return jnp.where(d == 0, c, back)


def _log_sigmoid(x):
    return jnp.minimum(x, 0.0) - jnp.log1p(jnp.exp(-jnp.abs(x)))


def _gla_kernel(*refs, mode, dk, dv, pack, groups, n_in):
    ins = [refs[:n_in], refs[n_in:2 * n_in]]
    par_ref, mall_ref, off_ref, eye_ref, hmk_ref, hmv_ref, o0_ref, o1_ref, st_ref = refs[2 * n_in:]
    c = pl.program_id(0)

    @pl.when(c == 0)
    def _():
        st_ref[...] = jnp.zeros(st_ref.shape, F32)

    eye = eye_ref[...]
    hmk = hmk_ref[...]
    hmv = hmv_ref[...]
    wk, wv = pack * dk, pack * dv
    for d, o_ref in enumerate((o0_ref, o1_ref)):
        if mode == "hgrn":
            q_raw, f_raw, v = (r[...] for r in ins[d])
            par = par_ref[d]
            q = q_raw * jax.nn.sigmoid(q_raw)
            g = jnp.logaddexp(par[0:1], par[1:2] + _log_sigmoid(f_raw))
            k = par[2:3] * jax.nn.sigmoid(-f_raw)
        else:
            q_raw, k, v, a_dn = (r[...] for r in ins[d])
            par = par_ref[d]
            q = q_raw * dk ** -0.5
            g = _log_sigmoid(_dot(a_dn, par[0:128]) + par[128:129]) / GLA_GATE_NORM
        e_all = _dot_exact_lhs(mall_ref[d], g)
        bc = e_all[0:CHUNK]
        btot = jnp.sum(g, axis=0, keepdims=True)
        qhat = q * jnp.exp(bc)
        kt = k * jnp.exp(btot - bc)
        ebt = jnp.exp(btot)
        qw, kw = [q], [k]
        for lv in range(N_LEVELS):
            w = jnp.exp(e_all[(lv + 1) * CHUNK:(lv + 2) * CHUNK])
            qw.append(q * w)
            kw.append(k * w)
        for gi in range(groups):
            slk = slice(gi * wk, (gi + 1) * wk)
            slv = slice(gi * wv, (gi + 1) * wv)

            def stk(x):
                return jnp.concatenate([x[:, slk].astype(BF16)] * pack, axis=0) * hmk

            def tile(x):
                return jnp.concatenate([x[:, slk].astype(BF16)] * pack, axis=0)

            att = eye * _dot_nt(stk(qw[0]), tile(kw[0]))
            for lv in range(N_LEVELS):
                att = att + off_ref[d, lv] * _dot_nt(stk(qw[lv + 1]), tile(kw[lv + 1]))
            vs = jnp.concatenate([v[:, slv]] * pack, axis=0) * hmv
            st = st_ref[d, gi]
            o = _dot_nt(stk(qhat), st) + _dot(att, vs)
            o_ref[:, slv] = sum(o[h * CHUNK:(h + 1) * CHUNK] for h in range(pack))
            st_ref[d, gi] = st * ebt[:, slk] + _dot(vs.T, stk(kt))


def gla_scan(p, cols, par, *, mode, heads, dk, dv, pack, n_ctx, name):
    l = p.shape[0]
    groups = heads // pack
    nch, ncc = l // CHUNK, n_ctx // CHUNK
    nb = pack * CHUNK

    def col_spec(d, off, width):
        assert off % width == 0
        return pl.BlockSpec((CHUNK, width), lambda c: (_chunk_pos(d, c, ncc, nch), off // width))

    def const(a):
        return pl.BlockSpec(a.shape, lambda c: (0,) * a.ndim)

    bd = lambda a: np.kron(np.eye(pack, dtype=np.float32), a)
    mall = jnp.asarray(np.concatenate([_CC["tri"][:, None], _CC["seg"]], axis=1).reshape(2, -1, CHUNK), BF16)
    off = jnp.asarray(np.stack([np.stack([bd(_CC["off"][d, lv]) for lv in range(N_LEVELS)]) for d in range(2)]), F32)
    eye = jnp.asarray(np.eye(nb, dtype=np.float32))
    hmk = jnp.asarray(np.kron(np.eye(pack, dtype=np.float32), np.ones((CHUNK, dk), np.float32)), BF16)
    hmv = jnp.asarray(np.kron(np.eye(pack, dtype=np.float32), np.ones((CHUNK, dv), np.float32)))
    consts = [par, mall, off, eye, hmk, hmv]
    in_specs = [col_spec(d, o, w) for d in range(2) for (o, w) in cols[d]]
    out_spec = lambda d: pl.BlockSpec((CHUNK, heads * dv), lambda c: (_chunk_pos(d, c, ncc, nch), 0))
    return pl.pallas_call(
        functools.partial(_gla_kernel, mode=mode, dk=dk, dv=dv, pack=pack, groups=groups, n_in=len(cols[0])),
        out_shape=[jax.ShapeDtypeStruct((l, heads * dv), F32)] * 2,
        grid=(nch,),
        in_specs=in_specs + [const(a) for a in consts],
        out_specs=[out_spec(0), out_spec(1)],
        scratch_shapes=[pltpu.VMEM((2, groups, pack * dv, pack * dk), F32)],
        compiler_params=_cparams(("arbitrary",)),
        name=name,
    )(*([p] * len(in_specs)), *consts)


def _post_kernel(of_ref, ob_ref, gate_ref, gn_ref, o_ref, *, heads, dv):
    o = of_ref[...] + ob_ref[...]
    gate = gate_ref[...]
    gn = gn_ref[...]
    for h in range(heads):
        sl = slice(h * dv, (h + 1) * dv)
        oh = o[:, sl]
        y = oh * lax.rsqrt(jnp.mean(oh * oh, axis=-1, keepdims=True) + EPS) * gn
        gh = gate[:, sl]
        o_ref[:, sl] = (y * gh * jax.nn.sigmoid(gh)).astype(o_ref.dtype)


def mix_post(o_f, o_b, p, gate_off, g_norm, *, heads, dv, tm=768, name):
    l, w = o_f.shape
    tm = math.gcd(l, tm)
    assert gate_off % w == 0
    row = pl.BlockSpec((tm, w), lambda i: (i, 0))
    return pl.pallas_call(
        functools.partial(_post_kernel, heads=heads, dv=dv),
        out_shape=jax.ShapeDtypeStruct((l, w), BF16),
        grid=(l // tm,),
        in_specs=[row, row, pl.BlockSpec((tm, w), lambda i: (i, gate_off // w)), pl.BlockSpec((1, dv), lambda i: (0, 0))],
        out_specs=row,
        compiler_params=_cparams(("parallel",)),
        name=name,
    )(o_f, o_b, p, g_norm.reshape(1, dv))


def _seg_sum(x, seg_bf16):
    hi, lo = _split2(x)
    return jnp.dot(hi, seg_bf16, preferred_element_type=F32) + jnp.dot(lo, seg_bf16, preferred_element_type=F32)


def _shift_mix_block(x, halo_prev, halo_next, mu, seg_start, seg_end):
    row = lax.broadcasted_iota(jnp.int32, x.shape, 0)
    first = jnp.where(seg_start, 0.0, halo_prev[7:8, :])
    last = jnp.where(seg_end, 0.0, halo_next[0:1, :])
    xp = jnp.where(row == 0, first, pltpu.roll(x, 1, 0))
    xn = jnp.where(row == CHUNK - 1, last, pltpu.roll(x, CHUNK - 1, 0))
    return x + mu[0:1] * (xp - x) + mu[1:2] * (xn - x)


def _rwkv_kernel(*refs, groups, ncc, nch):
    ins = [refs[0:6], refs[6:12]]
    (mu_rkv_ref, mu_lo_ref, vec_ref, wup_ref, aup_ref, gup_ref, seg_ref,
     tri_ref, strict_ref, incl_ref, off_ref, eye_ref, hm_ref,
     o0_ref, o1_ref, bo0_ref, bo1_ref, g_ref, st_ref) = refs[12:]
    c = pl.program_id(0)

    @pl.when(c == 0)
    def _():
        st_ref[...] = jnp.zeros(st_ref.shape, F32)

    hm = hm_ref[...]
    eye = eye_ref[...]
    seg = seg_ref[...]
    w = RWKV_PACK * RWKV_HEAD
    nb = RWKV_PACK * CHUNK
    gw = GROUP_W

    def stack(x):
        return jnp.concatenate([x.astype(BF16)] * RWKV_PACK, axis=0) * hm

    streams = []
    for d, (o_ref, bo_ref) in enumerate(((o0_ref, bo0_ref), (o1_ref, bo1_ref))):
        rkv_ref, rkv_p, rkv_n, lo_ref, lo_p, lo_n = ins[d]
        pos = _chunk_pos(d, c, ncc, nch)
        seg_start = (pos == 0) | (pos == ncc)
        seg_end = (pos == ncc - 1) | (pos == nch - 1)
        rkv = _shift_mix_block(rkv_ref[...], rkv_p[...], rkv_n[...], mu_rkv_ref[...], seg_start, seg_end)
        lora = _shift_mix_block(lo_ref[...], lo_p[...], lo_n[...], mu_lo_ref[...], seg_start, seg_end)
        r, k, v = rkv[:, 0:gw], rkv[:, gw:2 * gw], rkv[:, 2 * gw:3 * gw]
        w_dn, a_dn, g_dn = lora[:, 0:128], lora[:, 128:256], lora[:, 256:384]
        vec = vec_ref[d]
        wl = vec[0:1] + _dot(jnp.tanh(w_dn), wup_ref[d])
        lw = -jnp.exp(-(jnp.maximum(-wl, 0.0) + jnp.log1p(jnp.exp(-jnp.abs(wl)))) - 0.5)
        a = jax.nn.sigmoid(vec[1:2] + _dot(a_dn, aup_ref[d]))
        kk = k * vec[2:3]
        kk = kk * lax.rsqrt(jnp.maximum(_seg_sum(kk * kk, seg), 1e-24))
        k = k * (1.0 + (a - 1.0) * vec[3:4])
        b = kk * a
        bo_ref[...] = _seg_sum(r * k * vec[4:5], seg) * v
        if d == 0:
            g_ref[...] = _dot(jax.nn.sigmoid(g_dn), gup_ref[...])
        bc = _dot_exact_lhs(tri_ref[d], lw)
        ebt = jnp.exp(jnp.sum(lw, axis=0, keepdims=True))
        einv = jnp.exp(-bc)
        khat = kk * jnp.exp(bc - lw)
        rhat = r * jnp.exp(bc)
        ks = k * einv
        bs = b * einv
        for g in range(groups):
            sl = slice(g * w, (g + 1) * w)
            streams.append(dict(
                d=d, g=g, sl=sl, o_ref=o_ref, ebt=ebt[:, sl],
                kr=jnp.concatenate([stack(khat[:, sl]), stack(rhat[:, sl])], axis=0),
                bk=jnp.concatenate([stack(bs[:, sl]), stack(ks[:, sl])], axis=0),
                vs=stack(v[:, sl])))

    for s in streams:
        aa = _dot_nt(s["kr"], s["bk"])
        strict, incl = strict_ref[s["d"]], incl_ref[s["d"]]
        s["akb"] = strict.astype(F32) * aa[:nb, :nb]
        s["akb_b"] = s["akb"].astype(BF16)
        s["arb"] = incl * aa[nb:, :nb].astype(BF16)
        s["ak_v"] = jnp.concatenate([strict * aa[:nb, nb:].astype(BF16), incl * aa[nb:, nb:].astype(BF16)], axis=0)
        s["minv"] = eye - off_ref[s["d"], N_LEVELS - 1].astype(F32) * s["akb"]
    for lv in range(N_LEVELS - 2, -1, -1):
        for s in streams:
            s["minv_b"] = s["minv"].astype(BF16)
            s["t1"] = _dot(off_ref[s["d"], lv] * s["akb_b"], s["minv_b"])
        for s in streams:
            s["minv"] = s["minv"] - _dot(s["minv_b"], s["t1"])
    for s in streams:
        st = st_ref[s["d"], s["g"]]
        from_state = _dot_nt(s["kr"], st)
        from_v = _dot(s["ak_v"], s["vs"])
        u = _dot(s["minv"], from_state[:nb] + from_v[:nb])
        o = from_state[nb:] + from_v[nb:] - _dot(s["arb"], u)
        s["o_ref"][:, s["sl"]] = sum(o[h * CHUNK:(h + 1) * CHUNK] for h in range(RWKV_PACK))
        upd = _dot(s["vs"].T, s["bk"][nb:]) - _dot(u.T, s["bk"][:nb])
        st_ref[s["d"], s["g"]] = (st + upd) * s["ebt"]


def rwkv_scan(p, rkv_off, lora_off, params, *, n_ctx, name="rwkv_scan"):
    l = p.shape[0]
    hw = GROUP_W
    w = RWKV_PACK * RWKV_HEAD
    groups = hw // w
    nch, ncc = l // CHUNK, n_ctx // CHUNK
    nb = RWKV_PACK * CHUNK
    halo = 8
    per_chunk = CHUNK // halo

    def shared(d):
        return pl.BlockSpec((CHUNK, hw), lambda c: (_chunk_pos(d, c, ncc, nch), 0))

    def piece(d, off, width):
        assert off % width == 0
        cb = off // width
        pos = lambda c: _chunk_pos(d, c, ncc, nch)
        return [pl.BlockSpec((CHUNK, width), lambda c: (pos(c), cb)),
                pl.BlockSpec((halo, width), lambda c: (jnp.maximum(pos(c) * per_chunk - 1, 0), cb)),
                pl.BlockSpec((halo, width), lambda c: (jnp.minimum((pos(c) + 1) * per_chunk, l // halo - 1), cb))]

    def const(a):
        return pl.BlockSpec(a.shape, lambda c: (0,) * a.ndim)

    bd = lambda a: np.kron(np.eye(RWKV_PACK, dtype=np.float32), a)
    strict = jnp.asarray(np.stack([bd(_CC["strict"][d]) for d in range(2)]), BF16)
    incl = jnp.asarray(np.stack([bd(_CC["strict"][d] + _CC["eye"][d]) for d in range(2)]), BF16)
    off = jnp.asarray(np.stack([np.stack([bd(_CC["off"][d, lv]) for lv in range(N_LEVELS)]) for d in range(2)]), BF16)
    eye = jnp.asarray(np.eye(nb, dtype=np.float32))
    hm = jnp.asarray(np.kron(np.eye(RWKV_PACK, dtype=np.float32), np.ones((CHUNK, RWKV_HEAD), np.float32)), BF16)
    tri = jnp.asarray(_CC["tri"], BF16)
    seg = jnp.asarray(np.kron(np.eye(RWKV_HEADS, dtype=np.float32), np.ones((RWKV_HEAD, RWKV_HEAD), np.float32)), BF16)
    consts = list(params) + [seg, tri, strict, incl, off, eye, hm]
    in_specs = []
    for d in range(2):
        in_specs += piece(d, rkv_off, 3 * hw) + piece(d, lora_off, 3 * 128)
    return pl.pallas_call(
        functools.partial(_rwkv_kernel, groups=groups, ncc=ncc, nch=nch),
        out_shape=[jax.ShapeDtypeStruct((l, hw), F32)] * 5,
        grid=(nch,),
        in_specs=in_specs + [const(a) for a in consts],
        out_specs=[shared(0), shared(1), shared(0), shared(1), shared(0)],
        scratch_shapes=[pltpu.VMEM((2, groups, nb, nb), F32)],
        compiler_params=_cparams(("arbitrary",)),
        name=name,
    )(*([p] * len(in_specs)), *consts)


def _rwkv_post_kernel(of_ref, ob_ref, bf_ref, bb_ref, g_ref, ln_ref, seg_ref, o_ref):
    seg = seg_ref[...]
    o = of_ref[...] + ob_ref[...]
    mean = _seg_sum(o, seg) * (1.0 / RWKV_HEAD)
    oc = o - mean
    var = _seg_sum(oc * oc, seg) * (1.0 / RWKV_HEAD)
    y = oc * lax.rsqrt(var + RWKV_LN_EPS) * ln_ref[0:1] + ln_ref[1:2] + bf_ref[...] + bb_ref[...]
    o_ref[...] = (y * g_ref[...]).astype(o_ref.dtype)


def rwkv_post(o_f, o_b, bo_f, bo_b, g, ln_g, ln_b, *, tm=768, name="rwkv_post"):
    l, w = o_f.shape
    tm = math.gcd(l, tm)
    row = pl.BlockSpec((tm, w), lambda i: (i, 0))
    ln = jnp.zeros((8, w), F32).at[0].set(ln_g).at[1].set(ln_b)
    seg = jnp.asarray(np.kron(np.eye(RWKV_HEADS, dtype=np.float32), np.ones((RWKV_HEAD, RWKV_HEAD), np.float32)), BF16)
    return pl.pallas_call(
        _rwkv_post_kernel,
        out_shape=jax.ShapeDtypeStruct((l, w), BF16),
        grid=(l // tm,),
        in_specs=[row] * 5 + [pl.BlockSpec((8, w), lambda i: (0, 0)), pl.BlockSpec((w, w), lambda i: (0, 0))],
        out_specs=row,
        compiler_params=_cparams(("parallel",)),
        name=name,
    )(o_f, o_b, bo_f, bo_b, g, ln, seg)


def _rms(x, g, eps=EPS):
    return x * lax.rsqrt(jnp.mean(x * x, axis=-1, keepdims=True) + eps) * g


def _heads_major(t, h):
    l = t.shape[0]
    return t.reshape(l, h, -1).transpose(1, 0, 2)


def _rope_tables(n):
    rows = n // GRID_W
    row = jnp.repeat(jnp.arange(rows, dtype=F32), GRID_W)
    col = jnp.tile(jnp.arange(GRID_W, dtype=F32), rows)
    n_freq = MLA_ROPE // 4
    freqs = ROPE_BASE ** (-jnp.arange(n_freq, dtype=F32) / n_freq)
    ang = jnp.stack([row[:, None] * freqs, col[:, None] * freqs], axis=1)
    return jnp.cos(ang), jnp.sin(ang)


def _rope(x, cos, sin):
    xs = x.reshape(x.shape[:-1] + (2, 2, MLA_ROPE // 4))
    x1, x2 = xs[..., 0, :], xs[..., 1, :]
    return jnp.stack([x1 * cos - x2 * sin, x1 * sin + x2 * cos], axis=-2).reshape(x.shape)


def _rope_lanes(x, cos_t, sin_t):
    n = x.shape[1]
    lane = lax.broadcasted_iota(jnp.int32, x.shape, 1)
    swapped = jnp.where(lane % 32 < 16, pltpu.roll(x, n - 16, 1), pltpu.roll(x, 16, 1))
    return x * cos_t + swapped * sin_t


def _mla_prep_kernel(cq_ref, ckv_ref, kr_ref, cos_ref, sin_ref, qn_ref, kvn_ref, wq_ref, wkv_ref,
                     q_ref, k_ref, v_ref):
    cq = cq_ref[...]
    ms = jnp.sum(cq * cq, axis=-1, keepdims=True) * (1.0 / MLA_Q_LORA)
    q = _dot(cq * lax.rsqrt(ms + EPS) * qn_ref[...], wq_ref[...])
    ckv = ckv_ref[...]
    kvn = ckv * lax.rsqrt(jnp.mean(ckv * ckv, axis=-1, keepdims=True) + EPS) * kvn_ref[...]
    kv = _dot(kvn, wkv_ref[...])
    cos_t, sin_t = cos_ref[...], sin_ref[...]
    hn = MLA_HEADS * MLA_NOPE
    scale = (MLA_NOPE + MLA_ROPE) ** -0.5
    q_rope = _rope_lanes(q[:, hn:], cos_t, sin_t)
    k_rope = _rope_lanes(kr_ref[...], cos_t[:, :128], sin_t[:, :128])
    k_rope_hi = pltpu.roll(k_rope, MLA_ROPE, 1)
    lane = lax.broadcasted_iota(jnp.int32, k_rope.shape, 1)
    ones_col = jnp.where(lane == 0, 1.0, 0.0)
    for h in range(MLA_HEADS):
        pair = q_rope[:, (h // 2) * 128:(h // 2 + 1) * 128]
        q_ref[h] = (jnp.concatenate([q[:, h * MLA_NOPE:(h + 1) * MLA_NOPE], pair], axis=1) * scale).astype(q_ref.dtype)
        k_ref[h] = jnp.concatenate([kv[:, h * MLA_NOPE:(h + 1) * MLA_NOPE], k_rope if h % 2 == 0 else k_rope_hi],
                                   axis=1).astype(k_ref.dtype)
        v_ref[h] = jnp.concatenate([kv[:, hn + h * MLA_V:hn + (h + 1) * MLA_V], ones_col], axis=1).astype(v_ref.dtype)


def mla_prep(p, m, q_norm, w_q_up, kv_norm, w_kv_up, *, tm=256, name="mla_prep"):
    l = p.shape[0]
    n = l - m
    dq = MLA_NOPE + MLA_ROPE
    wq = w_q_up.reshape(MLA_Q_LORA, MLA_HEADS, dq)
    wq = jnp.concatenate([wq[..., :MLA_NOPE].reshape(MLA_Q_LORA, -1), wq[..., MLA_NOPE:].reshape(MLA_Q_LORA, -1)], axis=1)
    wq = jnp.pad(wq, ((0, PACKED["cq"][3] - MLA_Q_LORA), (0, 0)))
    qn = jnp.pad(q_norm, (0, PACKED["cq"][3] - MLA_Q_LORA)).reshape(1, -1)
    wkv = w_kv_up.reshape(MLA_KV_LORA, MLA_HEADS, MLA_NOPE + MLA_V)
    wkv = jnp.concatenate([wkv[..., :MLA_NOPE].reshape(MLA_KV_LORA, -1), wkv[..., MLA_NOPE:].reshape(MLA_KV_LORA, -1)],
                          axis=1)
    cos, sin = _rope_tables(n)
    cos64 = jnp.repeat(cos, 2, axis=1).reshape(n, MLA_ROPE)
    sin64 = (jnp.repeat(sin, 2, axis=1) * jnp.array([-1.0, 1.0, -1.0, 1.0])[None, :, None]).reshape(n, MLA_ROPE)
    cos_t = jnp.tile(jnp.concatenate([jnp.ones((m, MLA_ROPE), F32), cos64], axis=0), (1, MLA_HEADS))
    sin_t = jnp.tile(jnp.concatenate([jnp.zeros((m, MLA_ROPE), F32), sin64], axis=0), (1, MLA_HEADS))

    def col(nm):
        _, _, off, wp = PACKED[nm]
        return pl.BlockSpec((tm, wp), lambda i: (i, off // wp))

    const = lambda a: pl.BlockSpec(a.shape, lambda i: (0,) * a.ndim)
    row = lambda wd: pl.BlockSpec((tm, wd), lambda i: (i, 0))
    out_spec = pl.BlockSpec((MLA_HEADS, tm, 256), lambda i: (0, i, 0))
    return pl.pallas_call(
        _mla_prep_kernel,
        out_shape=[jax.ShapeDtypeStruct((MLA_HEADS, l, 256), BF16)] * 3,
        grid=(l // tm,),
        in_specs=[col("cq"), col("ckv"), col("k_rope"), row(256), row(256), const(qn), const(kv_norm.reshape(1, -1)),
                  const(wq), const(wkv)],
        out_specs=[out_spec] * 3,
        compiler_params=_cparams(("parallel",)),
        name=name,
    )(p, p, p, cos_t, sin_t, qn, kv_norm.reshape(1, -1), wq, wkv)


def mla_mixer(p, m, q_norm, w_q_up, kv_norm, w_kv_up):
    qh, kh, vh = mla_prep(p, m, q_norm, w_q_up, kv_norm, w_kv_up)
    o_ctx = flash_attention(qh[:, :m], kh[:, :m], vh[:, :m], dv=MLA_V, tq=m, name="mla_attn_ctx")
    o_lat = flash_attention(qh[:, m:], kh, vh, dv=MLA_V, tq=512, name="mla_attn_lat")
    return jnp.concatenate([o_ctx, o_lat], axis=0)


def _shift_mix(z, m, mu_prev, mu_next):
    def one(zp):
        zprev = jnp.pad(zp[:-1], ((1, 0), (0, 0)))
        znext = jnp.pad(zp[1:], ((0, 1), (0, 0)))
        return zp + mu_prev * (zprev - zp) + mu_next * (znext - zp)
    return jnp.concatenate([one(z[:m]), one(z[m:])], axis=0)


def rwkv7_mixer(p, m, mu, w0, w_up, a0, a_up, k_k, k_a, u, g_up, ln_g, ln_b):
    mu_rkv, mu_lo = mu[:, :3 * GROUP_W], mu[:, 3 * GROUP_W:]
    vec = jnp.zeros((2, 8, GROUP_W), F32)
    for i, t in enumerate((w0, a0, k_k, k_a, u)):
        vec = vec.at[:, i].set(t)
    wup = jnp.zeros((2, 128, GROUP_W), F32)
    aup = jnp.zeros((2, 128, GROUP_W), F32)
    for d in range(2):
        wup = wup.at[d, d * RWKV_W_LORA:(d + 1) * RWKV_W_LORA].set(w_up[d])
        aup = aup.at[d, d * RWKV_A_LORA:(d + 1) * RWKV_A_LORA].set(a_up[d])
    outs = rwkv_scan(p, PACKED["rwkv_r"][2], PACKED["w_dn"][2], (mu_rkv, mu_lo, vec, wup, aup, g_up), n_ctx=m)
    return rwkv_post(*outs, ln_g, ln_b)


PACKED = {}


def _build_packed():
    orig = dict(cq=(0, MLA_Q_LORA), ckv=(MLA_Q_LORA, MLA_KV_LORA), k_rope=(MLA_Q_LORA + MLA_KV_LORA, MLA_ROPE))
    b = MLA_COLS
    for i, nm in enumerate(("rwkv_r", "rwkv_k", "rwkv_v")):
        orig[nm] = (b + i * GROUP_W, GROUP_W)
    b += 3 * GROUP_W
    orig.update(w_dn=(b, 2 * RWKV_W_LORA), a_dn=(b + 2 * RWKV_W_LORA, 2 * RWKV_A_LORA),
                g_dn=(b + 2 * RWKV_W_LORA + 2 * RWKV_A_LORA, RWKV_G_LORA))
    b = IN_SPLITS[1]
    orig.update(gla_q=(b, GLA_KD), gla_k=(b + GLA_KD, GLA_KD), gla_v=(b + 2 * GLA_KD, GROUP_W),
                gla_a=(b + GLA_SPLITS[2], 2 * GLA_GATE_RANK), gla_r=(b + GLA_SPLITS[3], GROUP_W))
    b = IN_SPLITS[2]
    for i, nm in enumerate(("hgrn_q", "hgrn_f0", "hgrn_f1", "hgrn_i", "hgrn_g")):
        orig[nm] = (b + i * GROUP_W, GROUP_W)
    order = [(nm, 512) for nm in ("rwkv_r", "rwkv_k", "rwkv_v", "gla_v", "gla_r", "hgrn_q", "hgrn_f0", "hgrn_f1",
                                  "hgrn_i", "hgrn_g", "cq")]
    orig["pad"] = (0, 0)
    order += [(nm, 256) for nm in ("gla_q", "gla_k")]
    order += [(nm, 128) for nm in ("w_dn", "a_dn", "g_dn", "gla_a", "k_rope", "pad")]
    order += [("ckv", 256)]
    off = 0
    for nm, wp in order:
        PACKED[nm] = (orig[nm][0], orig[nm][1], off, wp)
        off += wp
    return off


N_PACKED = _build_packed()


def pack_w_in(w):
    pieces = []
    for o, wd, _, wp in PACKED.values():
        pieces.append(w[:, o:o + wd])
        if wp > wd:
            pieces.append(jnp.zeros((w.shape[0], wp - wd), w.dtype))
    return jnp.concatenate(pieces, axis=1)


def _pcol(p, nm):
    _, wd, off, _ = PACKED[nm]
    return p[:, off:off + wd]


def _blk(nm):
    return (PACKED[nm][2], PACKED[nm][3])


def gla_mixer(p, m, a_up, a_bias, g_norm):
    par = jnp.zeros((2, 136, GLA_KD), F32)
    for d in range(2):
        par = par.at[d, d * GLA_GATE_RANK:(d + 1) * GLA_GATE_RANK].set(a_up[d]).at[d, 128].set(a_bias[d])
    cols = [[_blk("gla_q"), _blk("gla_k"), _blk("gla_v"), _blk("gla_a")]] * 2
    o_f, o_b = gla_scan(p, cols, par, mode="gla", heads=GLA_HEADS, dk=GLA_DK, dv=GLA_DV, pack=4, n_ctx=m,
                        name="gla_scan")
    return mix_post(o_f, o_b, p, PACKED["gla_r"][2], g_norm, heads=GLA_HEADS, dv=GLA_DV, name="gla_post")


def hgrn2_mixer(p, m, lb, g_norm):
    par = jnp.zeros((2, 8, GROUP_W), F32).at[:, 0].set(jnp.log(lb)).at[:, 1].set(jnp.log1p(-lb)).at[:, 2].set(1.0 - lb)
    cols = [[_blk("hgrn_q"), _blk("hgrn_f%d" % d), _blk("hgrn_i")] for d in range(2)]
    o_f, o_b = gla_scan(p, cols, par, mode="hgrn", heads=HGRN_HEADS, dk=HGRN_EXPAND, dv=HGRN_DV, pack=2, n_ctx=m,
                        name="hgrn_scan")
    return mix_post(o_f, o_b, p, PACKED["hgrn_g"][2], g_norm, heads=HGRN_HEADS, dv=HGRN_DV, name="hgrn_post")


def dense_ffn(xs, h2, w1, w3, w2, j, gates2, m):
    act = pmatmul(h2, w1, (j,), w3=w3, tm=1408, tn=512, out_dtype=BF16, name="ffn_up")
    return pmatmul(act, w2, (j,), tm=768, tn=256, res=xs, gates=gates2, m_ctx=m, name="ffn_down")


def _route(top_idx, weights):
    n_pairs = top_idx.size
    r = n_pairs + N_EXPERTS * MOE_TILE
    n_tiles = r // MOE_TILE
    e_flat = top_idx.reshape(-1)
    onehot = (e_flat[:, None] == jnp.arange(N_EXPERTS)[None, :]).astype(jnp.int32)
    rank = jnp.sum((jnp.cumsum(onehot, axis=0) - onehot) * onehot, axis=1)
    counts = jnp.sum(onehot, axis=0)
    padded = (counts + MOE_TILE - 1) // MOE_TILE * MOE_TILE
    ends = jnp.cumsum(padded)
    dest = (ends - padded)[e_flat] + rank
    src_token = jnp.zeros((r,), jnp.int32).at[dest].set(jnp.arange(n_pairs, dtype=jnp.int32) // TOP_K)
    row_gate = jnp.zeros((r,), F32).at[dest].set(weights.reshape(-1))
    tile_start = jnp.arange(n_tiles, dtype=jnp.int32) * MOE_TILE
    tile_expert = jnp.minimum(jnp.sum(tile_start[:, None] >= ends[None, :], axis=1), N_EXPERTS - 1)
    meta = jnp.concatenate([tile_expert.astype(jnp.int32), (ends[-1:] // MOE_TILE).astype(jnp.int32)])
    return src_token, row_gate, dest.reshape(top_idx.shape), meta


def _combine_kernel(x_ref, y_ref, gate_ref, o_ref, *, m_ctx, tm, d):
    i = pl.program_id(0)
    rows = i * tm + lax.broadcasted_iota(jnp.int32, (tm, d), 0)
    g = jnp.where(rows < m_ctx, gate_ref[0:1, :], gate_ref[1:2, :])
    o_ref[...] = x_ref[...] + g * (y_ref[0].astype(F32) + y_ref[1].astype(F32))


def moe_combine(xs, y_tok, gates2, *, m_ctx, tm=256, name="moe_combine"):
    l, d = xs.shape
    return pl.pallas_call(
        functools.partial(_combine_kernel, m_ctx=m_ctx, tm=tm, d=d),
        out_shape=jax.ShapeDtypeStruct((l, d), F32),
        grid=(l // tm,),
        in_specs=[pl.BlockSpec((tm, d), lambda i: (i, 0)), pl.BlockSpec((TOP_K, tm, d), lambda i: (0, i, 0)),
                  pl.BlockSpec((2, d), lambda i: (0, 0))],
        out_specs=pl.BlockSpec((tm, d), lambda i: (i, 0)),
        compiler_params=_cparams(("parallel",)),
        name=name,
    )(xs, y_tok, gates2)


def moe_ffn(xs, h2, logits, w1, w3, w2, j, gates2, m):
    l, d = h2.shape
    top_vals, top_idx = lax.top_k(logits[:, :N_EXPERTS], TOP_K)
    weights = jax.nn.softmax(top_vals, axis=-1)
    src_token, row_gate, dest, meta = _route(top_idx, weights)
    x_sorted = jnp.take(h2, src_token, axis=0)
    act = gmatmul(meta, x_sorted, w1, j, w3=w3, tn=512, out_dtype=BF16, rowscale=row_gate[:, None], name="moe_up")
    y = gmatmul(meta, act, w2, j, tn=512, out_dtype=BF16, name="moe_down")
    y_tok = jnp.take(y, dest.T.reshape(-1), axis=0).reshape(TOP_K, l, d)
    return moe_combine(xs, y_tok, gates2, m_ctx=m)


def kernel(x, c, ctx, c_ctx, norm1_g, norm2_g, w_mod, b_mod, w_in, w_out, mla_q_norm, mla_w_q_up, mla_kv_norm, mla_w_kv_up, rwkv_mu, rwkv_w0, rwkv_w_up, rwkv_a0, rwkv_a_up, rwkv_k_k, rwkv_k_a, rwkv_u, rwkv_g_up, rwkv_ln_g, rwkv_ln_b, gla_a_up, gla_a_bias, gla_norm, hgrn_lb, hgrn_norm, ffn_w1, ffn_w3, ffn_w2, moe_router, moe_w1, moe_w3, moe_w2, final_norm_g):
    m, n, d = ctx.shape[1], x.shape[1], x.shape[2]
    lb_all = jnp.cumsum(jax.nn.softmax(hgrn_lb.astype(F32), axis=1), axis=1)
    lb_all = lb_all - lb_all[:, :1]
    xs = jnp.concatenate([ctx[0], x[0]], axis=0)
    cvec = jnp.zeros((16, d), F32).at[0].set(jax.nn.silu(c[0])).at[1].set(jax.nn.silu(c_ctx))
    for l in range(DEPTH):
        mod = pmatmul(cvec, w_mod, (l,), tm=16, tn=1024, name="mod") + b_mod[l]
        mods = jnp.stack([mod[1].reshape(6, d), mod[0].reshape(6, d)], axis=1)
        sh1, sc1, gt1, sh2, sc2, gt2 = (mods[i] for i in range(6))

        h = norm_mod(xs, norm1_g[l], sh1, sc1, m_ctx=m, out_dtype=BF16)
        p = pmatmul(h, pack_w_in(w_in[l]), tm=1408, tn=1024, name="w_in")
        o = [
            mla_mixer(p, m, mla_q_norm[l], mla_w_q_up[l], mla_kv_norm[l], mla_w_kv_up[l]),
            rwkv7_mixer(p, m, rwkv_mu[l], rwkv_w0[l], rwkv_w_up[l], rwkv_a0[l], rwkv_a_up[l],
                        rwkv_k_k[l], rwkv_k_a[l], rwkv_u[l], rwkv_g_up[l], rwkv_ln_g[l], rwkv_ln_b[l]),
            gla_mixer(p, m, gla_a_up[l], gla_a_bias[l], gla_norm[l]),
            hgrn2_mixer(p, m, lb_all[:, l], hgrn_norm[l]),
        ]
        xs = pmatmul(o, w_out, (l,), tm=1408, tn=512, res=xs, gates=gt1, m_ctx=m, name="w_out")

        j = l // 2
        if l % 2 == 0:
            h2 = norm_mod(xs, norm2_g[l], sh2, sc2, m_ctx=m, out_dtype=BF16)
            xs = dense_ffn(xs, h2, ffn_w1, ffn_w3, ffn_w2, j, gt2, m)
        else:
            router_p = jnp.pad(moe_router[j], ((0, 0), (0, 128 - N_EXPERTS)))
            h2, logits = norm_mod(xs, norm2_g[l], sh2, sc2, m_ctx=m, out_dtype=BF16, router=router_p,
                                  name="norm_mod_router")
            xs = moe_ffn(xs, h2, logits, moe_w1, moe_w3, moe_w2, j, gt2, m)
    zeros2 = jnp.zeros((2, d), F32)
    out = norm_mod(xs, final_norm_g, zeros2, zeros2, m_ctx=m, out_dtype=F32, name="final_norm")
    return out[m:][None]
```

```python
import functools
import math

import numpy as np
import jax
import jax.numpy as jnp
from jax import lax
from jax.experimental import pallas as pl
from jax.experimental.pallas import tpu as pltpu

F32 = jnp.float32
BF16 = jnp.bfloat16

DEPTH = 4
GRID_W = 64
EPS = 1e-6
GROUP_W = 512
MLA_HEADS, MLA_NOPE, MLA_ROPE, MLA_V = 4, 128, 64, 128
MLA_Q_LORA, MLA_KV_LORA = 384, 256
ROPE_BASE = 10000.0
RWKV_HEADS, RWKV_HEAD = 8, 64
RWKV_W_LORA, RWKV_A_LORA, RWKV_G_LORA = 64, 64, 128
RWKV_LN_EPS = 64e-5
GLA_HEADS, GLA_DK, GLA_DV = 4, 64, 128
GLA_GATE_RANK, GLA_GATE_NORM = 16, 16.0
HGRN_HEADS, HGRN_EXPAND, HGRN_DV = 4, 128, 128
CHUNK = 64
N_EXPERTS, TOP_K = 8, 2
MLA_COLS = MLA_Q_LORA + MLA_KV_LORA + MLA_ROPE
RWKV_COLS = 3 * GROUP_W + 2 * RWKV_W_LORA + 2 * RWKV_A_LORA + RWKV_G_LORA
GLA_KD = GLA_HEADS * GLA_DK
GLA_COLS = 2 * GLA_KD + GROUP_W + 2 * GLA_GATE_RANK + GROUP_W
IN_SPLITS = (MLA_COLS, MLA_COLS + RWKV_COLS, MLA_COLS + RWKV_COLS + GLA_COLS)
RWKV_SPLITS = (GROUP_W, 2 * GROUP_W, 3 * GROUP_W, 3 * GROUP_W + 2 * RWKV_W_LORA,
               3 * GROUP_W + 2 * RWKV_W_LORA + 2 * RWKV_A_LORA)
GLA_SPLITS = (GLA_KD, 2 * GLA_KD, 2 * GLA_KD + GROUP_W, 2 * GLA_KD + GROUP_W + 2 * GLA_GATE_RANK)

V7X_VMEM_LIMIT = 56 * 1024 * 1024
V7X_MXU = 256
N_LEVELS = 6
RWKV_PACK = V7X_MXU // RWKV_HEAD
MOE_TILE = 512


def _cparams(sem, vmem=V7X_VMEM_LIMIT):
    return pltpu.CompilerParams(dimension_semantics=sem, vmem_limit_bytes=vmem)


class _Tab:
    def __init__(self, arr, idx=()):
        self.arr, self.idx = arr, tuple(idx)

    @property
    def shape(self):
        return self.arr.shape[len(self.idx):]

    @property
    def spec(self):
        idx, rest = self.idx, self.shape
        return pl.BlockSpec((None,) * len(idx) + rest, lambda *_: idx + (0,) * len(rest))


def _tab(a):
    return a if isinstance(a, _Tab) else _Tab(a)


def _dot(a, b):
    return jnp.dot(a.astype(BF16), b.astype(BF16), preferred_element_type=F32)


def _dot_nt(a, b):
    return lax.dot_general(a.astype(BF16), b.astype(BF16), (((1,), (1,)), ((), ())), preferred_element_type=F32)


def _split2(x):
    hi = x.astype(BF16)
    lo = (x - hi.astype(F32)).astype(BF16)
    return hi, lo


def _dot_exact_lhs(m_bf16, x):
    hi, lo = _split2(x)
    return (jnp.dot(m_bf16, hi, preferred_element_type=F32) + jnp.dot(m_bf16, lo, preferred_element_type=F32))


def _mm_kernel(*refs, n_x, n_w, has_res, m_ctx, tm, precision):
    it = iter(refs)
    x_refs = [next(it) for _ in range(n_x)]
    w_refs = [next(it) for _ in range(n_w)]
    res_ref = next(it) if has_res else None
    gate_ref = next(it) if has_res else None
    o_ref = next(it)
    wb_refs = [next(it) for _ in range(n_w)] if precision is None else w_refs
    i = pl.program_id(1)

    if precision is None:
        @pl.when(i == 0)
        def _():
            for w_ref, wb_ref in zip(w_refs, wb_refs):
                wb_ref[...] = w_ref[...].astype(BF16)
        x = jnp.concatenate([x_ref[...].astype(BF16) for x_ref in x_refs], axis=1)
    else:
        x = x_refs[0][...]

    acc = jnp.dot(x, wb_refs[0][...], preferred_element_type=F32, precision=precision)
    if n_w == 2:
        acc3 = jnp.dot(x, wb_refs[1][...], preferred_element_type=F32)
        acc = acc * jax.nn.sigmoid(acc) * acc3
    if has_res:
        rows = i * tm + lax.broadcasted_iota(jnp.int32, acc.shape, 0)
        g = jnp.where(rows < m_ctx, gate_ref[0:1, :], gate_ref[1:2, :])
        acc = res_ref[...] + g * acc
    o_ref[...] = acc.astype(o_ref.dtype)


def pmatmul(x, w, widx=(), *, w3=None, tm, tn, out_dtype=F32, res=None, gates=None, m_ctx=0, precision=None,
            name="mm"):
    xs = list(x) if isinstance(x, (list, tuple)) else [x]
    m = xs[0].shape[0]
    k = sum(xi.shape[1] for xi in xs)
    n = w.shape[-1]
    tm = math.gcd(m, tm)
    assert w.shape[-2] == k and tm % 16 == 0
    nj, ni = pl.cdiv(n, tn), m // tm
    lead = (None,) * len(widx)
    w_spec = pl.BlockSpec(lead + (k, tn), lambda j, i: tuple(widx) + (0, j))
    ws = [w] if w3 is None else [w, w3]
    in_specs = [pl.BlockSpec((tm, xi.shape[1]), lambda j, i: (i, 0)) for xi in xs] + [w_spec] * len(ws)
    args = xs + ws
    if res is not None:
        gates = _tab(gates)
        gi = gates.idx
        in_specs += [pl.BlockSpec((tm, tn), lambda j, i: (i, j)),
                     pl.BlockSpec((None,) * len(gi) + (2, tn), lambda j, i: gi + (0, j))]
        args += [res, gates.arr]
    kern = functools.partial(_mm_kernel, n_x=len(xs), n_w=len(ws), has_res=res is not None, m_ctx=m_ctx, tm=tm,
                             precision=precision)
    return pl.pallas_call(
        kern,
        out_shape=jax.ShapeDtypeStruct((m, n), out_dtype),
        grid=(nj, ni),
        in_specs=in_specs,
        out_specs=pl.BlockSpec((tm, tn), lambda j, i: (i, j)),
        scratch_shapes=[pltpu.VMEM((k, tn), BF16) for _ in ws] if precision is None else [],
        compiler_params=_cparams(("arbitrary", "arbitrary")),
        name=name,
    )(*args)


def _gmm_kernel(meta_ref, x_ref, *refs, n_w, has_rowscale, n_tiles):
    it = iter(refs)
    w_refs = [next(it) for _ in range(n_w)]
    rs_ref = next(it) if has_rowscale else None
    o_ref = next(it)
    wb_refs = [next(it) for _ in range(n_w)]
    t = pl.program_id(1)
    e = meta_ref[t]
    e_prev = meta_ref[jnp.maximum(t - 1, 0)]

    @pl.when((t == 0) | (e != e_prev))
    def _():
        for w_ref, wb_ref in zip(w_refs, wb_refs):
            wb_ref[...] = w_ref[...].astype(BF16)

    @pl.when(t < meta_ref[n_tiles])
    def _():
        x = x_ref[...].astype(BF16)
        acc = jnp.dot(x, wb_refs[0][...], preferred_element_type=F32)
        if n_w == 2:
            acc3 = jnp.dot(x, wb_refs[1][...], preferred_element_type=F32)
            acc = acc * jax.nn.sigmoid(acc) * acc3
        if has_rowscale:
            acc = acc * rs_ref[...]
        o_ref[...] = acc.astype(o_ref.dtype)

    @pl.when(t >= meta_ref[n_tiles])
    def _():
        o_ref[...] = jnp.zeros(o_ref.shape, o_ref.dtype)


def gmatmul(meta, x, w, jl, *, w3=None, tn, out_dtype, rowscale=None, name):
    r, k = x.shape
    n = w.shape[-1]
    n_tiles = r // MOE_TILE
    ws = [w] if w3 is None else [w, w3]
    w_spec = pl.BlockSpec((None, None, k, tn), lambda j, t, mr: (jl, mr[t], 0, j))
    in_specs = [pl.BlockSpec((MOE_TILE, k), lambda j, t, mr: (t, 0))] + [w_spec] * len(ws)
    args = [x] + ws
    if rowscale is not None:
        in_specs.append(pl.BlockSpec((MOE_TILE, 1), lambda j, t, mr: (t, 0)))
        args.append(rowscale)
    return pl.pallas_call(
        functools.partial(_gmm_kernel, n_w=len(ws), has_rowscale=rowscale is not None, n_tiles=n_tiles),
        out_shape=jax.ShapeDtypeStruct((r, n), out_dtype),
        grid_spec=pltpu.PrefetchScalarGridSpec(
            num_scalar_prefetch=1,
            grid=(n // tn, n_tiles),
            in_specs=in_specs,
            out_specs=pl.BlockSpec((MOE_TILE, tn), lambda j, t, mr: (t, j)),
            scratch_shapes=[pltpu.VMEM((k, tn), BF16) for _ in ws]),
        compiler_params=_cparams(("arbitrary", "arbitrary")),
        name=name,
    )(meta, *args)


def _norm_kernel(x_ref, g_ref, sh_ref, sc_ref, *rest, m_ctx, tm, with_router):
    i = pl.program_id(0)
    x = x_ref[...]
    y = x * lax.rsqrt(jnp.mean(x * x, axis=-1, keepdims=True) + EPS) * g_ref[...]
    rows = i * tm + lax.broadcasted_iota(jnp.int32, x.shape, 0)
    is_ctx = rows < m_ctx
    sc = jnp.where(is_ctx, sc_ref[0:1, :], sc_ref[1:2, :])
    sh = jnp.where(is_ctx, sh_ref[0:1, :], sh_ref[1:2, :])
    h = y * (1.0 + sc) + sh
    if with_router:
        r_ref, o_ref, logit_ref = rest
        logit_ref[...] = jnp.dot(h, r_ref[...], preferred_element_type=F32, precision=lax.Precision.HIGHEST)
    else:
        (o_ref,) = rest
    o_ref[...] = h.astype(o_ref.dtype)


def norm_mod(x, g, shift2, scale2, *, m_ctx, out_dtype, router=None, tm=768, name="norm_mod"):
    m, d = x.shape
    tm = math.gcd(m, tm)
    row = pl.BlockSpec((tm, d), lambda i: (i, 0))
    tabs = [_tab(g), _tab(shift2), _tab(scale2)]
    in_specs = [row] + [t.spec for t in tabs]
    args = [x] + [t.arr for t in tabs]
    out_shape = jax.ShapeDtypeStruct((m, d), out_dtype)
    out_specs = row
    if router is not None:
        router = _tab(router)
        in_specs.append(router.spec)
        args.append(router.arr)
        out_shape = [out_shape, jax.ShapeDtypeStruct((m, router.shape[1]), F32)]
        out_specs = [row, pl.BlockSpec((tm, router.shape[1]), lambda i: (i, 0))]
    return pl.pallas_call(
        functools.partial(_norm_kernel, m_ctx=m_ctx, tm=tm, with_router=router is not None),
        out_shape=out_shape,
        grid=(m // tm,),
        in_specs=in_specs,
        out_specs=out_specs,
        compiler_params=_cparams(("parallel",)),
        name=name,
    )(*args)


def _attn_kernel(q_ref, k_ref, v_ref, o_ref, *, tk, n_kv, dv):
    q = q_ref[...]
    m = acc = None
    for j in range(n_kv):
        kb = k_ref[j * tk:(j + 1) * tk, :]
        vb = v_ref[j * tk:(j + 1) * tk, :]
        s = lax.dot_general(q, kb, (((1,), (1,)), ((), ())), preferred_element_type=F32)
        m_blk = jnp.max(s, axis=-1, keepdims=True)
        m_new = m_blk if j == 0 else jnp.maximum(m, m_blk)
        pv = jnp.dot(jnp.exp(s - m_new).astype(BF16), vb, preferred_element_type=F32)
        acc = pv if j == 0 else jnp.exp(m - m_new) * acc + pv
        m = m_new
    o_ref[...] = (acc[:, :dv] / acc[:, dv:dv + 1]).astype(o_ref.dtype)


def _attn_kv_tile(lk, cap=1408):
    return max(t for t in range(128, cap + 1, 128) if lk % t == 0)


def flash_attention(q, k, v_ext, *, dv, tq, name="mla_attn"):
    h, lq, dqk = q.shape
    lk, dve = k.shape[1], v_ext.shape[2]
    tk = _attn_kv_tile(lk)
    assert lq % tq == 0
    return pl.pallas_call(
        functools.partial(_attn_kernel, tk=tk, n_kv=lk // tk, dv=dv),
        out_shape=jax.ShapeDtypeStruct((lq, h * dv), BF16),
        grid=(h, lq // tq),
        in_specs=[pl.BlockSpec((None, tq, dqk), lambda hh, i: (hh, i, 0)),
                  pl.BlockSpec((None, lk, dqk), lambda hh, i: (hh, 0, 0)),
                  pl.BlockSpec((None, lk, dve), lambda hh, i: (hh, 0, 0))],
        out_specs=pl.BlockSpec((tq, dv), lambda hh, i: (i, hh)),
        compiler_params=_cparams(("parallel", "parallel")),
        name=name,
    )(q, k, v_ext)


def _chunk_constants():
    c = CHUNK
    t = np.arange(c)
    tri = (t[None, :] <= t[:, None]).astype(np.float32)
    strict = (t[None, :] < t[:, None]).astype(np.float32)
    eye = np.eye(c, dtype=np.float32)
    seg, off = [], []
    for lv in range(N_LEVELS):
        s = c >> (lv + 1)
        blk = t // s
        same = blk[:, None] == blk[None, :]
        odd = (blk % 2 == 1)[:, None]
        seg.append(np.where(odd, same & (t[None, :] <= t[:, None]), same & (t[None, :] > t[:, None])).astype(np.float32))
        off.append((odd & (blk[None, :] == blk[:, None] - 1)).astype(np.float32))
    seg, off = np.stack(seg), np.stack(off)

    def both(a):
        return np.stack([a, a[..., ::-1, ::-1]])

    return {k: both(v) for k, v in dict(tri=tri, strict=strict, eye=eye, seg=seg, off=off).items()}


_CC = _chunk_constants()


def _chunk_pos(d, c, n_ctx_chunks, n_chunks):
    back = jnp.where(c < n_ctx_chunks, n_ctx_chunks - 1 - c, n_chunks + n_ctx_chunks - 1 - c)
    return jnp.where(d == 0, c, back)


def _log_sigmoid(x):
    return jnp.minimum(x, 0.0) - jnp.log1p(jnp.exp(-jnp.abs(x)))


def _gla_kernel(*refs, mode, dk, dv, pack, groups, n_in):
    ins = [refs[:n_in], refs[n_in:2 * n_in]]
    par_ref, mall_ref, off_ref, eye_ref, hmk_ref, hmv_ref, o0_ref, o1_ref, st_ref = refs[2 * n_in:]
    c = pl.program_id(0)

    @pl.when(c == 0)
    def _():
        st_ref[...] = jnp.zeros(st_ref.shape, F32)

    eye = eye_ref[...]
    hmk = hmk_ref[...]
    hmv = hmv_ref[...]
    wk, wv = pack * dk, pack * dv
    for d, o_ref in enumerate((o0_ref, o1_ref)):
        if mode == "hgrn":
            q_raw, f_raw, v = (r[...] for r in ins[d])
            par = par_ref[d]
            q = q_raw * jax.nn.sigmoid(q_raw)
            g = jnp.logaddexp(par[0:1], par[1:2] + _log_sigmoid(f_raw))
            k = par[2:3] * jax.nn.sigmoid(-f_raw)
        else:
            q_raw, k, v, a_dn = (r[...] for r in ins[d])
            par = par_ref[d]
            q = q_raw * dk ** -0.5
            g = _log_sigmoid(_dot(a_dn, par[0:128]) + par[128:129]) / GLA_GATE_NORM
        e_all = _dot_exact_lhs(mall_ref[d], g)
        bc = e_all[0:CHUNK]
        btot = jnp.sum(g, axis=0, keepdims=True)
        qhat = q * jnp.exp(bc)
        kt = k * jnp.exp(btot - bc)
        ebt = jnp.exp(btot)
        qw, kw = [q], [k]
        for lv in range(N_LEVELS):
            w = jnp.exp(e_all[(lv + 1) * CHUNK:(lv + 2) * CHUNK])
            qw.append(q * w)
            kw.append(k * w)
        for gi in range(groups):
            slk = slice(gi * wk, (gi + 1) * wk)
            slv = slice(gi * wv, (gi + 1) * wv)

            def stk(x):
                return jnp.concatenate([x[:, slk].astype(BF16)] * pack, axis=0) * hmk

            def tile(x):
                return jnp.concatenate([x[:, slk].astype(BF16)] * pack, axis=0)

            att = eye * _dot_nt(stk(qw[0]), tile(kw[0]))
            for lv in range(N_LEVELS):
                att = att + off_ref[d, lv] * _dot_nt(stk(qw[lv + 1]), tile(kw[lv + 1]))
            vs = jnp.concatenate([v[:, slv]] * pack, axis=0) * hmv
            st = st_ref[d, gi]
            o = _dot_nt(stk(qhat), st) + _dot(att, vs)
            o_ref[:, slv] = sum(o[h * CHUNK:(h + 1) * CHUNK] for h in range(pack))
            st_ref[d, gi] = st * ebt[:, slk] + _dot(vs.T, stk(kt))


def gla_scan(p, cols, par, *, mode, heads, dk, dv, pack, n_ctx, name):
    l = p.shape[0]
    groups = heads // pack
    nch, ncc = l // CHUNK, n_ctx // CHUNK
    nb = pack * CHUNK

    def col_spec(d, off, width):
        assert off % width == 0
        return pl.BlockSpec((CHUNK, width), lambda c: (_chunk_pos(d, c, ncc, nch), off // width))

    def const(a):
        return pl.BlockSpec(a.shape, lambda c: (0,) * a.ndim)

    bd = lambda a: np.kron(np.eye(pack, dtype=np.float32), a)
    mall = jnp.asarray(np.concatenate([_CC["tri"][:, None], _CC["seg"]], axis=1).reshape(2, -1, CHUNK), BF16)
    off = jnp.asarray(np.stack([np.stack([bd(_CC["off"][d, lv]) for lv in range(N_LEVELS)]) for d in range(2)]), F32)
    eye = jnp.asarray(np.eye(nb, dtype=np.float32))
    hmk = jnp.asarray(np.kron(np.eye(pack, dtype=np.float32), np.ones((CHUNK, dk), np.float32)), BF16)
    hmv = jnp.asarray(np.kron(np.eye(pack, dtype=np.float32), np.ones((CHUNK, dv), np.float32)))
    consts = [_tab(a) for a in (par, mall, off, eye, hmk, hmv)]
    in_specs = [col_spec(d, o, w) for d in range(2) for (o, w) in cols[d]]
    out_spec = lambda d: pl.BlockSpec((CHUNK, heads * dv), lambda c: (_chunk_pos(d, c, ncc, nch), 0))
    return pl.pallas_call(
        functools.partial(_gla_kernel, mode=mode, dk=dk, dv=dv, pack=pack, groups=groups, n_in=len(cols[0])),
        out_shape=[jax.ShapeDtypeStruct((l, heads * dv), F32)] * 2,
        grid=(nch,),
        in_specs=in_specs + [t.spec for t in consts],
        out_specs=[out_spec(0), out_spec(1)],
        scratch_shapes=[pltpu.VMEM((2, groups, pack * dv, pack * dk), F32)],
        compiler_params=_cparams(("arbitrary",)),
        name=name,
    )(*([p] * len(in_specs)), *[t.arr for t in consts])


def _post_kernel(of_ref, ob_ref, gate_ref, gn_ref, o_ref, *, heads, dv):
    o = of_ref[...] + ob_ref[...]
    gate = gate_ref[...]
    gn = gn_ref[...]
    for h in range(heads):
        sl = slice(h * dv, (h + 1) * dv)
        oh = o[:, sl]
        y = oh * lax.rsqrt(jnp.mean(oh * oh, axis=-1, keepdims=True) + EPS) * gn
        gh = gate[:, sl]
        o_ref[:, sl] = (y * gh * jax.nn.sigmoid(gh)).astype(o_ref.dtype)


def mix_post(o_f, o_b, p, gate_off, g_norm, *, heads, dv, tm=768, name):
    l, w = o_f.shape
    tm = math.gcd(l, tm)
    assert gate_off % w == 0
    row = pl.BlockSpec((tm, w), lambda i: (i, 0))
    g_norm = _tab(g_norm)
    return pl.pallas_call(
        functools.partial(_post_kernel, heads=heads, dv=dv),
        out_shape=jax.ShapeDtypeStruct((l, w), BF16),
        grid=(l // tm,),
        in_specs=[row, row, pl.BlockSpec((tm, w), lambda i: (i, gate_off // w)), g_norm.spec],
        out_specs=row,
        compiler_params=_cparams(("parallel",)),
        name=name,
    )(o_f, o_b, p, g_norm.arr)


def _seg_sum(x, seg_bf16):
    hi, lo = _split2(x)
    return jnp.dot(hi, seg_bf16, preferred_element_type=F32) + jnp.dot(lo, seg_bf16, preferred_element_type=F32)


def _shift_mix_block(x, halo_prev, halo_next, mu, seg_start, seg_end):
    row = lax.broadcasted_iota(jnp.int32, x.shape, 0)
    first = jnp.where(seg_start, 0.0, halo_prev[7:8, :])
    last = jnp.where(seg_end, 0.0, halo_next[0:1, :])
    xp = jnp.where(row == 0, first, pltpu.roll(x, 1, 0))
    xn = jnp.where(row == CHUNK - 1, last, pltpu.roll(x, CHUNK - 1, 0))
    return x + mu[0:1] * (xp - x) + mu[1:2] * (xn - x)


def _rwkv_kernel(*refs, groups, ncc, nch):
    ins = [refs[0:6], refs[6:12]]
    (mu_rkv_ref, mu_lo_ref, vec_ref, wup_ref, aup_ref, gup_ref, seg_ref,
     tri_ref, strict_ref, incl_ref, off_ref, eye_ref, hm_ref,
     o0_ref, o1_ref, bo0_ref, bo1_ref, g_ref, st_ref) = refs[12:]
    c = pl.program_id(0)

    @pl.when(c == 0)
    def _():
        st_ref[...] = jnp.zeros(st_ref.shape, F32)

    hm = hm_ref[...]
    eye = eye_ref[...]
    seg = seg_ref[...]
    w = RWKV_PACK * RWKV_HEAD
    nb = RWKV_PACK * CHUNK
    gw = GROUP_W

    def stack(x):
        return jnp.concatenate([x.astype(BF16)] * RWKV_PACK, axis=0) * hm

    streams = []
    for d, (o_ref, bo_ref) in enumerate(((o0_ref, bo0_ref), (o1_ref, bo1_ref))):
        rkv_ref, rkv_p, rkv_n, lo_ref, lo_p, lo_n = ins[d]
        pos = _chunk_pos(d, c, ncc, nch)
        seg_start = (pos == 0) | (pos == ncc)
        seg_end = (pos == ncc - 1) | (pos == nch - 1)
        rkv = _shift_mix_block(rkv_ref[...], rkv_p[...], rkv_n[...], mu_rkv_ref[...], seg_start, seg_end)
        lora = _shift_mix_block(lo_ref[...], lo_p[...], lo_n[...], mu_lo_ref[...], seg_start, seg_end)
        r, k, v = rkv[:, 0:gw], rkv[:, gw:2 * gw], rkv[:, 2 * gw:3 * gw]
        w_dn, a_dn, g_dn = lora[:, 0:128], lora[:, 128:256], lora[:, 256:384]
        vec = vec_ref[d]
        wl = vec[0:1] + _dot(jnp.tanh(w_dn), wup_ref[d])
        lw = -jnp.exp(-(jnp.maximum(-wl, 0.0) + jnp.log1p(jnp.exp(-jnp.abs(wl)))) - 0.5)
        a = jax.nn.sigmoid(vec[1:2] + _dot(a_dn, aup_ref[d]))
        kk = k * vec[2:3]
        kk = kk * lax.rsqrt(jnp.maximum(_seg_sum(kk * kk, seg), 1e-24))
        k = k * (1.0 + (a - 1.0) * vec[3:4])
        b = kk * a
        bo_ref[...] = _seg_sum(r * k * vec[4:5], seg) * v
        if d == 0:
            g_ref[...] = _dot(jax.nn.sigmoid(g_dn), gup_ref[...])
        bc = _dot_exact_lhs(tri_ref[d], lw)
        ebt = jnp.exp(jnp.sum(lw, axis=0, keepdims=True))
        einv = jnp.exp(-bc)
        khat = kk * jnp.exp(bc - lw)
        rhat = r * jnp.exp(bc)
        ks = k * einv
        bs = b * einv
        for g in range(groups):
            sl = slice(g * w, (g + 1) * w)
            streams.append(dict(
                d=d, g=g, sl=sl, o_ref=o_ref, ebt=ebt[:, sl],
                kr=jnp.concatenate([stack(khat[:, sl]), stack(rhat[:, sl])], axis=0),
                bk=jnp.concatenate([stack(bs[:, sl]), stack(ks[:, sl])], axis=0),
                vs=stack(v[:, sl])))

    for s in streams:
        aa = _dot_nt(s["kr"], s["bk"])
        strict, incl = strict_ref[s["d"]], incl_ref[s["d"]]
        s["akb"] = strict.astype(F32) * aa[:nb, :nb]
        s["akb_b"] = s["akb"].astype(BF16)
        s["arb"] = incl * aa[nb:, :nb].astype(BF16)
        s["ak_v"] = jnp.concatenate([strict * aa[:nb, nb:].astype(BF16), incl * aa[nb:, nb:].astype(BF16)], axis=0)
        s["minv"] = eye - off_ref[s["d"], N_LEVELS - 1].astype(F32) * s["akb"]
    for lv in range(N_LEVELS - 2, -1, -1):
        for s in streams:
            s["minv_b"] = s["minv"].astype(BF16)
            s["t1"] = _dot(off_ref[s["d"], lv] * s["akb_b"], s["minv_b"])
        for s in streams:
            s["minv"] = s["minv"] - _dot(s["minv_b"], s["t1"])
    for s in streams:
        st = st_ref[s["d"], s["g"]]
        from_state = _dot_nt(s["kr"], st)
        from_v = _dot(s["ak_v"], s["vs"])
        u = _dot(s["minv"], from_state[:nb] + from_v[:nb])
        o = from_state[nb:] + from_v[nb:] - _dot(s["arb"], u)
        s["o_ref"][:, s["sl"]] = sum(o[h * CHUNK:(h + 1) * CHUNK] for h in range(RWKV_PACK))
        upd = _dot(s["vs"].T, s["bk"][nb:]) - _dot(u.T, s["bk"][:nb])
        st_ref[s["d"], s["g"]] = (st + upd) * s["ebt"]


def rwkv_scan(p, rkv_off, lora_off, params, *, n_ctx, name="rwkv_scan"):
    l = p.shape[0]
    hw = GROUP_W
    w = RWKV_PACK * RWKV_HEAD
    groups = hw // w
    nch, ncc = l // CHUNK, n_ctx // CHUNK
    nb = RWKV_PACK * CHUNK
    halo = 8
    per_chunk = CHUNK // halo

    def shared(d):
        return pl.BlockSpec((CHUNK, hw), lambda c: (_chunk_pos(d, c, ncc, nch), 0))

    def piece(d, off, width):
        assert off % width == 0
        cb = off // width
        pos = lambda c: _chunk_pos(d, c, ncc, nch)
        return [pl.BlockSpec((CHUNK, width), lambda c: (pos(c), cb)),
                pl.BlockSpec((halo, width), lambda c: (jnp.maximum(pos(c) * per_chunk - 1, 0), cb)),
                pl.BlockSpec((halo, width), lambda c: (jnp.minimum((pos(c) + 1) * per_chunk, l // halo - 1), cb))]

    def const(a):
        return pl.BlockSpec(a.shape, lambda c: (0,) * a.ndim)

    bd = lambda a: np.kron(np.eye(RWKV_PACK, dtype=np.float32), a)
    strict = jnp.asarray(np.stack([bd(_CC["strict"][d]) for d in range(2)]), BF16)
    incl = jnp.asarray(np.stack([bd(_CC["strict"][d] + _CC["eye"][d]) for d in range(2)]), BF16)
    off = jnp.asarray(np.stack([np.stack([bd(_CC["off"][d, lv]) for lv in range(N_LEVELS)]) for d in range(2)]), BF16)
    eye = jnp.asarray(np.eye(nb, dtype=np.float32))
    hm = jnp.asarray(np.kron(np.eye(RWKV_PACK, dtype=np.float32), np.ones((CHUNK, RWKV_HEAD), np.float32)), BF16)
    tri = jnp.asarray(_CC["tri"], BF16)
    seg = jnp.asarray(np.kron(np.eye(RWKV_HEADS, dtype=np.float32), np.ones((RWKV_HEAD, RWKV_HEAD), np.float32)), BF16)
    consts = [_tab(a) for a in list(params) + [seg, tri, strict, incl, off, eye, hm]]
    in_specs = []
    for d in range(2):
        in_specs += piece(d, rkv_off, 3 * hw) + piece(d, lora_off, 3 * 128)
    return pl.pallas_call(
        functools.partial(_rwkv_kernel, groups=groups, ncc=ncc, nch=nch),
        out_shape=[jax.ShapeDtypeStruct((l, hw), F32)] * 5,
        grid=(nch,),
        in_specs=in_specs + [t.spec for t in consts],
        out_specs=[shared(0), shared(1), shared(0), shared(1), shared(0)],
        scratch_shapes=[pltpu.VMEM((2, groups, nb, nb), F32)],
        compiler_params=_cparams(("arbitrary",)),
        name=name,
    )(*([p] * len(in_specs)), *[t.arr for t in consts])


def _rwkv_post_kernel(of_ref, ob_ref, bf_ref, bb_ref, g_ref, ln_ref, seg_ref, o_ref):
    seg = seg_ref[...]
    o = of_ref[...] + ob_ref[...]
    mean = _seg_sum(o, seg) * (1.0 / RWKV_HEAD)
    oc = o - mean
    var = _seg_sum(oc * oc, seg) * (1.0 / RWKV_HEAD)
    y = oc * lax.rsqrt(var + RWKV_LN_EPS) * ln_ref[0:1] + ln_ref[1:2] + bf_ref[...] + bb_ref[...]
    o_ref[...] = (y * g_ref[...]).astype(o_ref.dtype)


def rwkv_post(o_f, o_b, bo_f, bo_b, g, ln, *, tm=768, name="rwkv_post"):
    l, w = o_f.shape
    tm = math.gcd(l, tm)
    row = pl.BlockSpec((tm, w), lambda i: (i, 0))
    ln = _tab(ln)
    seg = jnp.asarray(np.kron(np.eye(RWKV_HEADS, dtype=np.float32), np.ones((RWKV_HEAD, RWKV_HEAD), np.float32)), BF16)
    return pl.pallas_call(
        _rwkv_post_kernel,
        out_shape=jax.ShapeDtypeStruct((l, w), BF16),
        grid=(l // tm,),
        in_specs=[row] * 5 + [ln.spec, pl.BlockSpec((w, w), lambda i: (0, 0))],
        out_specs=row,
        compiler_params=_cparams(("parallel",)),
        name=name,
    )(o_f, o_b, bo_f, bo_b, g, ln.arr, seg)


@functools.lru_cache(maxsize=None)
def _rope_lane_tables(m, n):
    rows = n // GRID_W
    row = np.repeat(np.arange(rows, dtype=np.float32), GRID_W)
    col = np.tile(np.arange(GRID_W, dtype=np.float32), rows)
    n_freq = MLA_ROPE // 4
    freqs = (np.float32(ROPE_BASE) ** (-np.arange(n_freq, dtype=np.float32) / np.float32(n_freq))).astype(np.float32)
    ang = np.stack([row[:, None] * freqs, col[:, None] * freqs], axis=1).astype(np.float32)
    cos64 = np.repeat(np.cos(ang), 2, axis=1).reshape(n, MLA_ROPE)
    sin64 = (np.repeat(np.sin(ang), 2, axis=1) * np.array([-1.0, 1.0, -1.0, 1.0])[None, :, None]).reshape(n, MLA_ROPE)
    cos_t = np.tile(np.concatenate([np.ones((m, MLA_ROPE)), cos64], axis=0), (1, MLA_HEADS)).astype(np.float32)
    sin_t = np.tile(np.concatenate([np.zeros((m, MLA_ROPE)), sin64], axis=0), (1, MLA_HEADS)).astype(np.float32)
    return cos_t, sin_t


def _rope_lanes(x, cos_t, sin_t):
    n = x.shape[1]
    lane = lax.broadcasted_iota(jnp.int32, x.shape, 1)
    swapped = jnp.where(lane % 32 < 16, pltpu.roll(x, n - 16, 1), pltpu.roll(x, 16, 1))
    return x * cos_t + swapped * sin_t


def _mla_prep_kernel(cq_ref, ckv_ref, kr_ref, cos_ref, sin_ref, qn_ref, kvn_ref, wq_ref, wkv_ref,
                     q_ref, k_ref, v_ref):
    cq = cq_ref[...]
    ms = jnp.sum(cq * cq, axis=-1, keepdims=True) * (1.0 / MLA_Q_LORA)
    q = _dot(cq * lax.rsqrt(ms + EPS) * qn_ref[...], wq_ref[...])
    ckv = ckv_ref[...]
    kvn = ckv * lax.rsqrt(jnp.mean(ckv * ckv, axis=-1, keepdims=True) + EPS) * kvn_ref[...]
    kv = _dot(kvn, wkv_ref[...])
    cos_t, sin_t = cos_ref[...], sin_ref[...]
    hn = MLA_HEADS * MLA_NOPE
    scale = (MLA_NOPE + MLA_ROPE) ** -0.5
    q_rope = _rope_lanes(q[:, hn:], cos_t, sin_t)
    k_rope = _rope_lanes(kr_ref[...], cos_t[:, :128], sin_t[:, :128])
    k_rope_hi = pltpu.roll(k_rope, MLA_ROPE, 1)
    lane = lax.broadcasted_iota(jnp.int32, k_rope.shape, 1)
    ones_col = jnp.where(lane == 0, 1.0, 0.0)
    for h in range(MLA_HEADS):
        pair = q_rope[:, (h // 2) * 128:(h // 2 + 1) * 128]
        q_ref[h] = (jnp.concatenate([q[:, h * MLA_NOPE:(h + 1) * MLA_NOPE], pair], axis=1) * scale).astype(q_ref.dtype)
        k_ref[h] = jnp.concatenate([kv[:, h * MLA_NOPE:(h + 1) * MLA_NOPE], k_rope if h % 2 == 0 else k_rope_hi],
                                   axis=1).astype(k_ref.dtype)
        v_ref[h] = jnp.concatenate([kv[:, hn + h * MLA_V:hn + (h + 1) * MLA_V], ones_col], axis=1).astype(v_ref.dtype)


def mla_tables(q_norm, w_q_up, kv_norm, w_kv_up):
    nl = w_q_up.shape[0]
    dq = MLA_NOPE + MLA_ROPE
    pad = PACKED["cq"][3] - MLA_Q_LORA
    wq = w_q_up.reshape(nl, MLA_Q_LORA, MLA_HEADS, dq)
    wq = jnp.concatenate([wq[..., :MLA_NOPE].reshape(nl, MLA_Q_LORA, -1), wq[..., MLA_NOPE:].reshape(nl, MLA_Q_LORA, -1)],
                         axis=2)
    wq = jnp.pad(wq, ((0, 0), (0, pad), (0, 0)))
    qn = jnp.pad(q_norm, ((0, 0), (0, pad)))[:, None, :]
    wkv = w_kv_up.reshape(nl, MLA_KV_LORA, MLA_HEADS, MLA_NOPE + MLA_V)
    wkv = jnp.concatenate([wkv[..., :MLA_NOPE].reshape(nl, MLA_KV_LORA, -1), wkv[..., MLA_NOPE:].reshape(nl, MLA_KV_LORA, -1)],
                          axis=2)
    return qn, kv_norm[:, None, :], wq, wkv


def mla_prep(p, m, tables, *, tm=256, name="mla_prep"):
    l = p.shape[0]
    cos_t, sin_t = (jnp.asarray(t) for t in _rope_lane_tables(m, l - m))
    tabs = [_tab(t) for t in tables]

    def col(nm):
        _, _, off, wp = PACKED[nm]
        return pl.BlockSpec((tm, wp), lambda i: (i, off // wp))

    row = lambda wd: pl.BlockSpec((tm, wd), lambda i: (i, 0))
    out_spec = pl.BlockSpec((MLA_HEADS, tm, 256), lambda i: (0, i, 0))
    return pl.pallas_call(
        _mla_prep_kernel,
        out_shape=[jax.ShapeDtypeStruct((MLA_HEADS, l, 256), BF16)] * 3,
        grid=(l // tm,),
        in_specs=[col("cq"), col("ckv"), col("k_rope"), row(256), row(256)] + [t.spec for t in tabs],
        out_specs=[out_spec] * 3,
        compiler_params=_cparams(("parallel",)),
        name=name,
    )(p, p, p, cos_t, sin_t, *[t.arr for t in tabs])


def mla_mixer(p, m, tables):
    qh, kh, vh = mla_prep(p, m, tables)
    o_ctx = flash_attention(qh[:, :m], kh[:, :m], vh[:, :m], dv=MLA_V, tq=m, name="mla_attn_ctx")
    o_lat = flash_attention(qh[:, m:], kh, vh, dv=MLA_V, tq=512, name="mla_attn_lat")
    return jnp.concatenate([o_ctx, o_lat], axis=0)


def _pad_rows(t, before, total):
    cfg = [(0, 0)] * t.ndim
    cfg[-2] = (before, total - before - t.shape[-2])
    return jnp.pad(t, cfg)


def rwkv_tables(mu, w0, w_up, a0, a_up, k_k, k_a, u, g_up, ln_g, ln_b):
    zeros = jnp.zeros_like(w0)
    vec = jnp.stack([w0, a0, k_k, k_a, u, zeros, zeros, zeros], axis=2)
    wup = jnp.stack([_pad_rows(w_up[:, d], d * RWKV_W_LORA, 128) for d in range(2)], axis=1)
    aup = jnp.stack([_pad_rows(a_up[:, d], d * RWKV_A_LORA, 128) for d in range(2)], axis=1)
    ln = jnp.stack([ln_g, ln_b] + [jnp.zeros_like(ln_g)] * 6, axis=1)
    return (mu[:, :, :3 * GROUP_W], mu[:, :, 3 * GROUP_W:], vec, wup, aup, g_up), ln


def rwkv7_mixer(p, m, params, ln):
    outs = rwkv_scan(p, PACKED["rwkv_r"][2], PACKED["w_dn"][2], params, n_ctx=m)
    return rwkv_post(*outs, ln)


PACKED = {}


def _build_packed():
    orig = dict(cq=(0, MLA_Q_LORA), ckv=(MLA_Q_LORA, MLA_KV_LORA), k_rope=(MLA_Q_LORA + MLA_KV_LORA, MLA_ROPE))
    b = MLA_COLS
    for i, nm in enumerate(("rwkv_r", "rwkv_k", "rwkv_v")):
        orig[nm] = (b + i * GROUP_W, GROUP_W)
    b += 3 * GROUP_W
    orig.update(w_dn=(b, 2 * RWKV_W_LORA), a_dn=(b + 2 * RWKV_W_LORA, 2 * RWKV_A_LORA),
                g_dn=(b + 2 * RWKV_W_LORA + 2 * RWKV_A_LORA, RWKV_G_LORA))
    b = IN_SPLITS[1]
    orig.update(gla_q=(b, GLA_KD), gla_k=(b + GLA_KD, GLA_KD), gla_v=(b + 2 * GLA_KD, GROUP_W),
                gla_a=(b + GLA_SPLITS[2], 2 * GLA_GATE_RANK), gla_r=(b + GLA_SPLITS[3], GROUP_W))
    b = IN_SPLITS[2]
    for i, nm in enumerate(("hgrn_q", "hgrn_f0", "hgrn_f1", "hgrn_i", "hgrn_g")):
        orig[nm] = (b + i * GROUP_W, GROUP_W)
    order = [(nm, 512) for nm in ("rwkv_r", "rwkv_k", "rwkv_v", "gla_v", "gla_r", "hgrn_q", "hgrn_f0", "hgrn_f1",
                                  "hgrn_i", "hgrn_g", "cq")]
    orig["pad"] = (0, 0)
    order += [(nm, 256) for nm in ("gla_q", "gla_k")]
    order += [(nm, 128) for nm in ("w_dn", "a_dn", "g_dn", "gla_a", "k_rope", "pad")]
    order += [("ckv", 256)]
    off = 0
    for nm, wp in order:
        PACKED[nm] = (orig[nm][0], orig[nm][1], off, wp)
        off += wp
    return off


N_PACKED = _build_packed()


def pack_w_in(w):
    pieces = []
    for o, wd, _, wp in PACKED.values():
        pieces.append(w[..., o:o + wd])
        if wp > wd:
            pieces.append(jnp.zeros(w.shape[:-1] + (wp - wd,), w.dtype))
    return jnp.concatenate(pieces, axis=-1)


def _pcol(p, nm):
    _, wd, off, _ = PACKED[nm]
    return p[:, off:off + wd]


def _blk(nm):
    return (PACKED[nm][2], PACKED[nm][3])


def gla_tables(a_up, a_bias):
    return jnp.stack([jnp.concatenate([_pad_rows(a_up[:, d], d * GLA_GATE_RANK, 128),
                                       _pad_rows(a_bias[:, d][:, None, :], 0, 8)], axis=1) for d in range(2)], axis=1)


def hgrn_tables(lb_all):
    lb = jnp.swapaxes(lb_all, 0, 1)
    zeros = jnp.zeros_like(lb)
    return jnp.stack([jnp.log(lb), jnp.log1p(-lb), 1.0 - lb] + [zeros] * 5, axis=2)


def gla_mixer(p, m, par, g_norm):
    cols = [[_blk("gla_q"), _blk("gla_k"), _blk("gla_v"), _blk("gla_a")]] * 2
    o_f, o_b = gla_scan(p, cols, par, mode="gla", heads=GLA_HEADS, dk=GLA_DK, dv=GLA_DV, pack=4, n_ctx=m,
                        name="gla_scan")
    return mix_post(o_f, o_b, p, PACKED["gla_r"][2], g_norm, heads=GLA_HEADS, dv=GLA_DV, name="gla_post")


def hgrn2_mixer(p, m, par, g_norm):
    cols = [[_blk("hgrn_q"), _blk("hgrn_f%d" % d), _blk("hgrn_i")] for d in range(2)]
    o_f, o_b = gla_scan(p, cols, par, mode="hgrn", heads=HGRN_HEADS, dk=HGRN_EXPAND, dv=HGRN_DV, pack=2, n_ctx=m,
                        name="hgrn_scan")
    return mix_post(o_f, o_b, p, PACKED["hgrn_g"][2], g_norm, heads=HGRN_HEADS, dv=HGRN_DV, name="hgrn_post")


def dense_ffn(xs, h2, w1, w3, w2, j, gates2, m):
    act = pmatmul(h2, w1, (j,), w3=w3, tm=1408, tn=512, out_dtype=BF16, name="ffn_up")
    return pmatmul(act, w2, (j,), tm=768, tn=256, res=xs, gates=gates2, m_ctx=m, name="ffn_down")


def _route(top_idx, weights):
    n_pairs = top_idx.size
    r = n_pairs + N_EXPERTS * MOE_TILE
    n_tiles = r // MOE_TILE
    e_flat = top_idx.reshape(-1)
    onehot = (e_flat[:, None] == jnp.arange(N_EXPERTS)[None, :]).astype(jnp.int32)
    rank = jnp.sum((jnp.cumsum(onehot, axis=0) - onehot) * onehot, axis=1)
    counts = jnp.sum(onehot, axis=0)
    padded = (counts + MOE_TILE - 1) // MOE_TILE * MOE_TILE
    ends = jnp.cumsum(padded)
    dest = (ends - padded)[e_flat] + rank
    src_token = jnp.zeros((r,), jnp.int32).at[dest].set(jnp.arange(n_pairs, dtype=jnp.int32) // TOP_K)
    row_gate = jnp.zeros((r,), F32).at[dest].set(weights.reshape(-1))
    tile_start = jnp.arange(n_tiles, dtype=jnp.int32) * MOE_TILE
    tile_expert = jnp.minimum(jnp.sum(tile_start[:, None] >= ends[None, :], axis=1), N_EXPERTS - 1)
    meta = jnp.concatenate([tile_expert.astype(jnp.int32), (ends[-1:] // MOE_TILE).astype(jnp.int32)])
    return src_token, row_gate, dest.reshape(top_idx.shape), meta


def _combine_kernel(x_ref, y_ref, gate_ref, o_ref, *, m_ctx, tm, d):
    i = pl.program_id(0)
    rows = i * tm + lax.broadcasted_iota(jnp.int32, (tm, d), 0)
    g = jnp.where(rows < m_ctx, gate_ref[0:1, :], gate_ref[1:2, :])
    o_ref[...] = x_ref[...] + g * (y_ref[0].astype(F32) + y_ref[1].astype(F32))


def moe_combine(xs, y_tok, gates2, *, m_ctx, tm=256, name="moe_combine"):
    l, d = xs.shape
    gates2 = _tab(gates2)
    return pl.pallas_call(
        functools.partial(_combine_kernel, m_ctx=m_ctx, tm=tm, d=d),
        out_shape=jax.ShapeDtypeStruct((l, d), F32),
        grid=(l // tm,),
        in_specs=[pl.BlockSpec((tm, d), lambda i: (i, 0)), pl.BlockSpec((TOP_K, tm, d), lambda i: (0, i, 0)),
                  gates2.spec],
        out_specs=pl.BlockSpec((tm, d), lambda i: (i, 0)),
        compiler_params=_cparams(("parallel",)),
        name=name,
    )(xs, y_tok, gates2.arr)


def moe_ffn(xs, h2, logits, w1, w3, w2, j, gates2, m):
    l, d = h2.shape
    top_vals, top_idx = lax.top_k(logits[:, :N_EXPERTS], TOP_K)
    weights = jax.nn.softmax(top_vals, axis=-1)
    src_token, row_gate, dest, meta = _route(top_idx, weights)
    src_token = lax.optimization_barrier(src_token)
    x_sorted = jnp.take(h2, src_token, axis=0)
    act = gmatmul(meta, x_sorted, w1, j, w3=w3, tn=512, out_dtype=BF16, rowscale=row_gate[:, None], name="moe_up")
    y = gmatmul(meta, act, w2, j, tn=512, out_dtype=BF16, name="moe_down")
    dest_t = lax.optimization_barrier(dest.T.reshape(-1))
    y_tok = jnp.take(y, dest_t, axis=0).reshape(TOP_K, l, d)
    return moe_combine(xs, y_tok, gates2, m_ctx=m)


def kernel(x, c, ctx, c_ctx, norm1_g, norm2_g, w_mod, b_mod, w_in, w_out, mla_q_norm, mla_w_q_up, mla_kv_norm, mla_w_kv_up, rwkv_mu, rwkv_w0, rwkv_w_up, rwkv_a0, rwkv_a_up, rwkv_k_k, rwkv_k_a, rwkv_u, rwkv_g_up, rwkv_ln_g, rwkv_ln_b, gla_a_up, gla_a_bias, gla_norm, hgrn_lb, hgrn_norm, ffn_w1, ffn_w3, ffn_w2, moe_router, moe_w1, moe_w3, moe_w2, final_norm_g):
    m, n, d = ctx.shape[1], x.shape[1], x.shape[2]
    lb_all = jnp.cumsum(jax.nn.softmax(hgrn_lb.astype(F32), axis=1), axis=1)
    lb_all = lb_all - lb_all[:, :1]
    xs = jnp.concatenate([ctx[0], x[0]], axis=0)

    cvec = jnp.zeros((16, d), F32).at[0].set(jax.nn.silu(c[0])).at[1].set(jax.nn.silu(c_ctx))
    mod = jnp.stack([pmatmul(cvec, w_mod, (l,), tm=16, tn=2048, name="mod")[:2] for l in range(DEPTH)]) + b_mod[:, None, :]
    mods = jnp.swapaxes(mod[:, ::-1].reshape(DEPTH, 2, 6, d), 1, 2)
    zmods = jnp.zeros((2, d), F32)
    g1, g2 = norm1_g[:, None, :], norm2_g[:, None, :]
    w_in_p = pack_w_in(w_in)
    mla_tabs = mla_tables(mla_q_norm, mla_w_q_up, mla_kv_norm, mla_w_kv_up)
    rwkv_tabs, rwkv_ln = rwkv_tables(rwkv_mu, rwkv_w0, rwkv_w_up, rwkv_a0, rwkv_a_up, rwkv_k_k, rwkv_k_a, rwkv_u,
                                      rwkv_g_up, rwkv_ln_g, rwkv_ln_b)
    gla_par, hgrn_par = gla_tables(gla_a_up, gla_a_bias), hgrn_tables(lb_all)
    gla_g, hgrn_g = gla_norm[:, None, :], hgrn_norm[:, None, :]
    router_p = jnp.pad(moe_router, ((0, 0), (0, 0), (0, 128 - N_EXPERTS)))

    for l in range(DEPTH):
        sh1, sc1, gt1, sh2, sc2, gt2 = (_Tab(mods, (l, i)) for i in range(6))
        h = norm_mod(xs, _Tab(g1, (l,)), sh1, sc1, m_ctx=m, out_dtype=BF16)
        p = pmatmul(h, w_in_p, (l,), tm=1408, tn=1024, name="w_in")
        o = [
            mla_mixer(p, m, [_Tab(t, (l,)) for t in mla_tabs]),
            rwkv7_mixer(p, m, [_Tab(t, (l,)) for t in rwkv_tabs], _Tab(rwkv_ln, (l,))),
            gla_mixer(p, m, _Tab(gla_par, (l,)), _Tab(gla_g, (l,))),
            hgrn2_mixer(p, m, _Tab(hgrn_par, (l,)), _Tab(hgrn_g, (l,))),
        ]
        xs = pmatmul(o, w_out, (l,), tm=1408, tn=512, res=xs, gates=gt1, m_ctx=m, name="w_out")

        j = l // 2
        if l % 2 == 0:
            h2 = norm_mod(xs, _Tab(g2, (l,)), sh2, sc2, m_ctx=m, out_dtype=BF16)
            xs = dense_ffn(xs, h2, ffn_w1, ffn_w3, ffn_w2, j, gt2, m)
        else:
            h2, logits = norm_mod(xs, _Tab(g2, (l,)), sh2, sc2, m_ctx=m, out_dtype=BF16, router=_Tab(router_p, (j,)),
                                  name="norm_mod_router")
            xs = moe_ffn(xs, h2, logits, moe_w1, moe_w3, moe_w2, j, gt2, m)
    out = norm_mod(xs, final_norm_g[None, :], zmods, zmods, m_ctx=m, out_dtype=F32, name="final_norm")
    return out[m:][None]
```

```python
import functools
import math

import numpy as np
import jax
import jax.numpy as jnp
from jax import lax
from jax.experimental import pallas as pl
from jax.experimental.pallas import tpu as pltpu

F32 = jnp.float32
BF16 = jnp.bfloat16

DEPTH = 4
GRID_W = 64
EPS = 1e-6
GROUP_W = 512
MLA_HEADS, MLA_NOPE, MLA_ROPE, MLA_V = 4, 128, 64, 128
MLA_Q_LORA, MLA_KV_LORA = 384, 256
ROPE_BASE = 10000.0
RWKV_HEADS, RWKV_HEAD = 8, 64
RWKV_W_LORA, RWKV_A_LORA, RWKV_G_LORA = 64, 64, 128
RWKV_LN_EPS = 64e-5
GLA_HEADS, GLA_DK, GLA_DV = 4, 64, 128
GLA_GATE_RANK, GLA_GATE_NORM = 16, 16.0
HGRN_HEADS, HGRN_EXPAND, HGRN_DV = 4, 128, 128
CHUNK = 64
N_EXPERTS, TOP_K = 8, 2
MLA_COLS = MLA_Q_LORA + MLA_KV_LORA + MLA_ROPE
RWKV_COLS = 3 * GROUP_W + 2 * RWKV_W_LORA + 2 * RWKV_A_LORA + RWKV_G_LORA
GLA_KD = GLA_HEADS * GLA_DK
GLA_COLS = 2 * GLA_KD + GROUP_W + 2 * GLA_GATE_RANK + GROUP_W
IN_SPLITS = (MLA_COLS, MLA_COLS + RWKV_COLS, MLA_COLS + RWKV_COLS + GLA_COLS)
RWKV_SPLITS = (GROUP_W, 2 * GROUP_W, 3 * GROUP_W, 3 * GROUP_W + 2 * RWKV_W_LORA,
               3 * GROUP_W + 2 * RWKV_W_LORA + 2 * RWKV_A_LORA)
GLA_SPLITS = (GLA_KD, 2 * GLA_KD, 2 * GLA_KD + GROUP_W, 2 * GLA_KD + GROUP_W + 2 * GLA_GATE_RANK)

V7X_VMEM_LIMIT = 56 * 1024 * 1024
V7X_MXU = 256
N_LEVELS = 6
RWKV_PACK = V7X_MXU // RWKV_HEAD
MOE_TILE = 512


def _cparams(sem, vmem=V7X_VMEM_LIMIT):
    return pltpu.CompilerParams(dimension_semantics=sem, vmem_limit_bytes=vmem)


class _Tab:
    def __init__(self, arr, idx=()):
        self.arr, self.idx = arr, tuple(idx)

    @property
    def shape(self):
        return self.arr.shape[len(self.idx):]

    @property
    def spec(self):
        idx, rest = self.idx, self.shape
        return pl.BlockSpec((None,) * len(idx) + rest, lambda *_: idx + (0,) * len(rest))


def _tab(a):
    return a if isinstance(a, _Tab) else _Tab(a)


def _dot(a, b):
    return jnp.dot(a.astype(BF16), b.astype(BF16), preferred_element_type=F32)


def _dot_nt(a, b):
    return lax.dot_general(a.astype(BF16), b.astype(BF16), (((1,), (1,)), ((), ())), preferred_element_type=F32)


def _split2(x):
    hi = x.astype(BF16)
    lo = (x - hi.astype(F32)).astype(BF16)
    return hi, lo


def _dot_exact_lhs(m_bf16, x):
    hi, lo = _split2(x)
    return (jnp.dot(m_bf16, hi, preferred_element_type=F32) + jnp.dot(m_bf16, lo, preferred_element_type=F32))


def _mm_kernel(*refs, n_x, n_w, has_res, m_ctx, tm, precision):
    it = iter(refs)
    x_refs = [next(it) for _ in range(n_x)]
    w_refs = [next(it) for _ in range(n_w)]
    res_ref = next(it) if has_res else None
    gate_ref = next(it) if has_res else None
    o_ref = next(it)
    wb_refs = [next(it) for _ in range(n_w)] if precision is None else w_refs
    i = pl.program_id(1)

    if precision is None:
        @pl.when(i == 0)
        def _():
            for w_ref, wb_ref in zip(w_refs, wb_refs):
                wb_ref[...] = w_ref[...].astype(BF16)
        x = jnp.concatenate([x_ref[...].astype(BF16) for x_ref in x_refs], axis=1)
    else:
        x = x_refs[0][...]

    acc = jnp.dot(x, wb_refs[0][...], preferred_element_type=F32, precision=precision)
    if n_w == 2:
        acc3 = jnp.dot(x, wb_refs[1][...], preferred_element_type=F32)
        acc = acc * jax.nn.sigmoid(acc) * acc3
    if has_res:
        rows = i * tm + lax.broadcasted_iota(jnp.int32, acc.shape, 0)
        g = jnp.where(rows < m_ctx, gate_ref[0:1, :], gate_ref[1:2, :])
        acc = res_ref[...] + g * acc
    o_ref[...] = acc.astype(o_ref.dtype)


def pmatmul(x, w, widx=(), *, w3=None, tm, tn, out_dtype=F32, res=None, gates=None, m_ctx=0, precision=None,
            name="mm"):
    xs = list(x) if isinstance(x, (list, tuple)) else [x]
    m = xs[0].shape[0]
    k = sum(xi.shape[1] for xi in xs)
    n = w.shape[-1]
    tm = math.gcd(m, tm)
    assert w.shape[-2] == k and tm % 16 == 0
    nj, ni = pl.cdiv(n, tn), m // tm
    lead = (None,) * len(widx)
    w_spec = pl.BlockSpec(lead + (k, tn), lambda j, i: tuple(widx) + (0, j))
    ws = [w] if w3 is None else [w, w3]
    in_specs = [pl.BlockSpec((tm, xi.shape[1]), lambda j, i: (i, 0)) for xi in xs] + [w_spec] * len(ws)
    args = xs + ws
    if res is not None:
        gates = _tab(gates)
        gi = gates.idx
        in_specs += [pl.BlockSpec((tm, tn), lambda j, i: (i, j)),
                     pl.BlockSpec((None,) * len(gi) + (2, tn), lambda j, i: gi + (0, j))]
        args += [res, gates.arr]
    kern = functools.partial(_mm_kernel, n_x=len(xs), n_w=len(ws), has_res=res is not None, m_ctx=m_ctx, tm=tm,
                             precision=precision)
    return pl.pallas_call(
        kern,
        out_shape=jax.ShapeDtypeStruct((m, n), out_dtype),
        grid=(nj, ni),
        in_specs=in_specs,
        out_specs=pl.BlockSpec((tm, tn), lambda j, i: (i, j)),
        scratch_shapes=[pltpu.VMEM((k, tn), BF16) for _ in ws] if precision is None else [],
        compiler_params=_cparams(("arbitrary", "arbitrary")),
        name=name,
    )(*args)


def _gmm_kernel(meta_ref, x_ref, *refs, n_w, has_rowscale, n_tiles):
    it = iter(refs)
    w_refs = [next(it) for _ in range(n_w)]
    rs_ref = next(it) if has_rowscale else None
    o_ref = next(it)
    wb_refs = [next(it) for _ in range(n_w)]
    t = pl.program_id(1)
    e = meta_ref[t]
    e_prev = meta_ref[jnp.maximum(t - 1, 0)]

    @pl.when((t == 0) | (e != e_prev))
    def _():
        for w_ref, wb_ref in zip(w_refs, wb_refs):
            wb_ref[...] = w_ref[...].astype(BF16)

    @pl.when(t < meta_ref[n_tiles])
    def _():
        x = x_ref[...].astype(BF16)
        acc = jnp.dot(x, wb_refs[0][...], preferred_element_type=F32)
        if n_w == 2:
            acc3 = jnp.dot(x, wb_refs[1][...], preferred_element_type=F32)
            acc = acc * jax.nn.sigmoid(acc) * acc3
        if has_rowscale:
            acc = acc * rs_ref[...]
        o_ref[...] = acc.astype(o_ref.dtype)

    @pl.when(t >= meta_ref[n_tiles])
    def _():
        o_ref[...] = jnp.zeros(o_ref.shape, o_ref.dtype)


def gmatmul(meta, x, w, jl, *, w3=None, tn, out_dtype, rowscale=None, name):
    r, k = x.shape
    n = w.shape[-1]
    n_tiles = r // MOE_TILE
    ws = [w] if w3 is None else [w, w3]
    w_spec = pl.BlockSpec((None, None, k, tn), lambda j, t, mr: (jl, mr[t], 0, j))
    in_specs = [pl.BlockSpec((MOE_TILE, k), lambda j, t, mr: (t, 0))] + [w_spec] * len(ws)
    args = [x] + ws
    if rowscale is not None:
        in_specs.append(pl.BlockSpec((MOE_TILE, 1), lambda j, t, mr: (t, 0)))
        args.append(rowscale)
    return pl.pallas_call(
        functools.partial(_gmm_kernel, n_w=len(ws), has_rowscale=rowscale is not None, n_tiles=n_tiles),
        out_shape=jax.ShapeDtypeStruct((r, n), out_dtype),
        grid_spec=pltpu.PrefetchScalarGridSpec(
            num_scalar_prefetch=1,
            grid=(n // tn, n_tiles),
            in_specs=in_specs,
            out_specs=pl.BlockSpec((MOE_TILE, tn), lambda j, t, mr: (t, j)),
            scratch_shapes=[pltpu.VMEM((k, tn), BF16) for _ in ws]),
        compiler_params=_cparams(("arbitrary", "arbitrary")),
        name=name,
    )(meta, *args)


def _norm_kernel(x_ref, g_ref, sh_ref, sc_ref, *rest, m_ctx, tm, with_router):
    i = pl.program_id(0)
    x = x_ref[...]
    y = x * lax.rsqrt(jnp.mean(x * x, axis=-1, keepdims=True) + EPS) * g_ref[...]
    rows = i * tm + lax.broadcasted_iota(jnp.int32, x.shape, 0)
    is_ctx = rows < m_ctx
    sc = jnp.where(is_ctx, sc_ref[0:1, :], sc_ref[1:2, :])
    sh = jnp.where(is_ctx, sh_ref[0:1, :], sh_ref[1:2, :])
    h = y * (1.0 + sc) + sh
    if with_router:
        r_ref, o_ref, logit_ref = rest
        logit_ref[...] = jnp.dot(h, r_ref[...], preferred_element_type=F32, precision=lax.Precision.HIGHEST)
    else:
        (o_ref,) = rest
    o_ref[...] = h.astype(o_ref.dtype)


def norm_mod(x, g, shift2, scale2, *, m_ctx, out_dtype, router=None, tm=768, name="norm_mod"):
    m, d = x.shape
    tm = math.gcd(m, tm)
    row = pl.BlockSpec((tm, d), lambda i: (i, 0))
    tabs = [_tab(g), _tab(shift2), _tab(scale2)]
    in_specs = [row] + [t.spec for t in tabs]
    args = [x] + [t.arr for t in tabs]
    out_shape = jax.ShapeDtypeStruct((m, d), out_dtype)
    out_specs = row
    if router is not None:
        router = _tab(router)
        in_specs.append(router.spec)
        args.append(router.arr)
        out_shape = [out_shape, jax.ShapeDtypeStruct((m, router.shape[1]), F32)]
        out_specs = [row, pl.BlockSpec((tm, router.shape[1]), lambda i: (i, 0))]
    return pl.pallas_call(
        functools.partial(_norm_kernel, m_ctx=m_ctx, tm=tm, with_router=router is not None),
        out_shape=out_shape,
        grid=(m // tm,),
        in_specs=in_specs,
        out_specs=out_specs,
        compiler_params=_cparams(("parallel",)),
        name=name,
    )(*args)


def _attn_kernel(q_ref, k_ref, v_ref, o_ref, *, tk, n_kv, dv):
    q = q_ref[...]
    m = acc = None
    for j in range(n_kv):
        kb = k_ref[j * tk:(j + 1) * tk, :]
        vb = v_ref[j * tk:(j + 1) * tk, :]
        s = lax.dot_general(q, kb, (((1,), (1,)), ((), ())), preferred_element_type=F32)
        m_blk = jnp.max(s, axis=-1, keepdims=True)
        m_new = m_blk if j == 0 else jnp.maximum(m, m_blk)
        pv = jnp.dot(jnp.exp2(s - m_new).astype(BF16), vb, preferred_element_type=F32)
        acc = pv if j == 0 else jnp.exp2(m - m_new) * acc + pv
        m = m_new
    o_ref[...] = (acc[:, :dv] / acc[:, dv:dv + 1]).astype(o_ref.dtype)


def _attn_kv_tile(lk, cap=1408):
    return max(t for t in range(128, cap + 1, 128) if lk % t == 0)


def flash_attention(q, k, v_ext, *, dv, tq, lq=None, lk=None, name="mla_attn"):
    h, dqk = q.shape[0], q.shape[2]
    lq = q.shape[1] if lq is None else lq
    lk = k.shape[1] if lk is None else lk
    dve = v_ext.shape[2]
    tk = _attn_kv_tile(lk)
    tq = math.gcd(lq, tq)
    return pl.pallas_call(
        functools.partial(_attn_kernel, tk=tk, n_kv=lk // tk, dv=dv),
        out_shape=jax.ShapeDtypeStruct((lq, h * dv), BF16),
        grid=(h, lq // tq),
        in_specs=[pl.BlockSpec((None, tq, dqk), lambda hh, i: (hh, i, 0)),
                  pl.BlockSpec((None, lk, dqk), lambda hh, i: (hh, 0, 0)),
                  pl.BlockSpec((None, lk, dve), lambda hh, i: (hh, 0, 0))],
        out_specs=pl.BlockSpec((tq, dv), lambda hh, i: (i, hh)),
        compiler_params=_cparams(("parallel", "parallel")),
        name=name,
    )(q, k, v_ext)


def _chunk_constants():
    c = CHUNK
    t = np.arange(c)
    tri = (t[None, :] <= t[:, None]).astype(np.float32)
    strict = (t[None, :] < t[:, None]).astype(np.float32)
    eye = np.eye(c, dtype=np.float32)
    seg, off = [], []
    for lv in range(N_LEVELS):
        s = c >> (lv + 1)
        blk = t // s
        same = blk[:, None] == blk[None, :]
        odd = (blk % 2 == 1)[:, None]
        seg.append(np.where(odd, same & (t[None, :] <= t[:, None]), same & (t[None, :] > t[:, None])).astype(np.float32))
        off.append((odd & (blk[None, :] == blk[:, None] - 1)).astype(np.float32))
    seg, off = np.stack(seg), np.stack(off)

    def both(a):
        return np.stack([a, a[..., ::-1, ::-1]])

    return {k: both(v) for k, v in dict(tri=tri, strict=strict, eye=eye, seg=seg, off=off).items()}


_CC = _chunk_constants()


def _chunk_pos(d, c, n_ctx_chunks, n_chunks):
    back = jnp.where(c < n_ctx_chunks, n_ctx_chunks - 1 - c, n_chunks + n_ctx_chunks - 1 - c)
    return jnp.where(d == 0, c, back)


def _log_sigmoid(x):
    return jnp.minimum(x, 0.0) - jnp.log1p(jnp.exp(-jnp.abs(x)))


def _gla_kernel(*refs, mode, dk, dv, pack, groups, n_in):
    ins = [refs[:n_in], refs[n_in:2 * n_in]]
    par_ref, mall_ref, off_ref, eye_ref, hmk_ref, hmv_ref, o0_ref, o1_ref, st_ref = refs[2 * n_in:]
    c = pl.program_id(0)

    @pl.when(c == 0)
    def _():
        st_ref[...] = jnp.zeros(st_ref.shape, F32)

    eye = eye_ref[...]
    hmk = hmk_ref[...]
    hmv = hmv_ref[...]
    wk, wv = pack * dk, pack * dv
    for d, o_ref in enumerate((o0_ref, o1_ref)):
        if mode == "hgrn":
            q_raw, f_raw, v = (r[...] for r in ins[d])
            par = par_ref[d]
            q = q_raw * jax.nn.sigmoid(q_raw)
            g = jnp.logaddexp(par[0:1], par[1:2] + _log_sigmoid(f_raw))
            k = par[2:3] * jax.nn.sigmoid(-f_raw)
        else:
            q_raw, k, v, a_dn = (r[...] for r in ins[d])
            par = par_ref[d]
            q = q_raw * dk ** -0.5
            g = _log_sigmoid(_dot(a_dn, par[0:128]) + par[128:129]) / GLA_GATE_NORM
        e_all = _dot_exact_lhs(mall_ref[d], g)
        bc = e_all[0:CHUNK]
        btot = jnp.sum(g, axis=0, keepdims=True)
        qhat = q * jnp.exp(bc)
        kt = k * jnp.exp(btot - bc)
        ebt = jnp.exp(btot)
        qw, kw = [q], [k]
        for lv in range(N_LEVELS):
            w = jnp.exp(e_all[(lv + 1) * CHUNK:(lv + 2) * CHUNK])
            qw.append(q * w)
            kw.append(k * w)
        for gi in range(groups):
            slk = slice(gi * wk, (gi + 1) * wk)
            slv = slice(gi * wv, (gi + 1) * wv)

            def stk(x):
                return jnp.concatenate([x[:, slk].astype(BF16)] * pack, axis=0) * hmk

            def tile(x):
                return jnp.concatenate([x[:, slk].astype(BF16)] * pack, axis=0)

            att = eye * _dot_nt(stk(qw[0]), tile(kw[0]))
            for lv in range(N_LEVELS):
                att = att + off_ref[d, lv] * _dot_nt(stk(qw[lv + 1]), tile(kw[lv + 1]))
            vs = jnp.concatenate([v[:, slv]] * pack, axis=0) * hmv
            st = st_ref[d, gi]
            o = _dot_nt(stk(qhat), st) + _dot(att, vs)
            o_ref[:, slv] = sum(o[h * CHUNK:(h + 1) * CHUNK] for h in range(pack))
            st_ref[d, gi] = st * ebt[:, slk] + _dot(vs.T, stk(kt))


def gla_scan(p, cols, par, *, mode, heads, dk, dv, pack, n_ctx, name):
    l = p.shape[0]
    groups = heads // pack
    nch, ncc = l // CHUNK, n_ctx // CHUNK
    nb = pack * CHUNK

    def col_spec(d, off, width):
        assert off % width == 0
        return pl.BlockSpec((CHUNK, width), lambda c: (_chunk_pos(d, c, ncc, nch), off // width))

    def const(a):
        return pl.BlockSpec(a.shape, lambda c: (0,) * a.ndim)

    bd = lambda a: np.kron(np.eye(pack, dtype=np.float32), a)
    mall = jnp.asarray(np.concatenate([_CC["tri"][:, None], _CC["seg"]], axis=1).reshape(2, -1, CHUNK), BF16)
    off = jnp.asarray(np.stack([np.stack([bd(_CC["off"][d, lv]) for lv in range(N_LEVELS)]) for d in range(2)]), F32)
    eye = jnp.asarray(np.eye(nb, dtype=np.float32))
    hmk = jnp.asarray(np.kron(np.eye(pack, dtype=np.float32), np.ones((CHUNK, dk), np.float32)), BF16)
    hmv = jnp.asarray(np.kron(np.eye(pack, dtype=np.float32), np.ones((CHUNK, dv), np.float32)))
    consts = [_tab(a) for a in (par, mall, off, eye, hmk, hmv)]
    in_specs = [col_spec(d, o, w) for d in range(2) for (o, w) in cols[d]]
    out_spec = lambda d: pl.BlockSpec((CHUNK, heads * dv), lambda c: (_chunk_pos(d, c, ncc, nch), 0))
    return pl.pallas_call(
        functools.partial(_gla_kernel, mode=mode, dk=dk, dv=dv, pack=pack, groups=groups, n_in=len(cols[0])),
        out_shape=[jax.ShapeDtypeStruct((l, heads * dv), F32)] * 2,
        grid=(nch,),
        in_specs=in_specs + [t.spec for t in consts],
        out_specs=[out_spec(0), out_spec(1)],
        scratch_shapes=[pltpu.VMEM((2, groups, pack * dv, pack * dk), F32)],
        compiler_params=_cparams(("arbitrary",)),
        name=name,
    )(*([p] * len(in_specs)), *[t.arr for t in consts])


def _post_kernel(of_ref, ob_ref, gate_ref, gn_ref, o_ref, *, heads, dv):
    o = of_ref[...] + ob_ref[...]
    gate = gate_ref[...]
    gn = gn_ref[...]
    for h in range(heads):
        sl = slice(h * dv, (h + 1) * dv)
        oh = o[:, sl]
        y = oh * lax.rsqrt(jnp.mean(oh * oh, axis=-1, keepdims=True) + EPS) * gn
        gh = gate[:, sl]
        o_ref[:, sl] = (y * gh * jax.nn.sigmoid(gh)).astype(o_ref.dtype)


def mix_post(o_f, o_b, p, gate_off, g_norm, *, heads, dv, tm=768, name):
    l, w = o_f.shape
    tm = math.gcd(l, tm)
    assert gate_off % w == 0
    row = pl.BlockSpec((tm, w), lambda i: (i, 0))
    g_norm = _tab(g_norm)
    return pl.pallas_call(
        functools.partial(_post_kernel, heads=heads, dv=dv),
        out_shape=jax.ShapeDtypeStruct((l, w), BF16),
        grid=(l // tm,),
        in_specs=[row, row, pl.BlockSpec((tm, w), lambda i: (i, gate_off // w)), g_norm.spec],
        out_specs=row,
        compiler_params=_cparams(("parallel",)),
        name=name,
    )(o_f, o_b, p, g_norm.arr)


def _seg_sum(x, seg_bf16):
    hi, lo = _split2(x)
    return jnp.dot(hi, seg_bf16, preferred_element_type=F32) + jnp.dot(lo, seg_bf16, preferred_element_type=F32)


def _shift_mix_block(x, halo_prev, halo_next, mu, seg_start, seg_end):
    row = lax.broadcasted_iota(jnp.int32, x.shape, 0)
    first = jnp.where(seg_start, 0.0, halo_prev[7:8, :])
    last = jnp.where(seg_end, 0.0, halo_next[0:1, :])
    xp = jnp.where(row == 0, first, pltpu.roll(x, 1, 0))
    xn = jnp.where(row == CHUNK - 1, last, pltpu.roll(x, CHUNK - 1, 0))
    return x + mu[0:1] * (xp - x) + mu[1:2] * (xn - x)


def _rwkv_kernel(*refs, groups, ncc, nch):
    ins = [refs[0:6], refs[6:12]]
    (mu_rkv_ref, mu_lo_ref, vec_ref, wup_ref, aup_ref, gup_ref, seg_ref,
     tri_ref, strict_ref, incl_ref, off_ref, eye_ref, hm_ref,
     o0_ref, o1_ref, bo0_ref, bo1_ref, g_ref, st_ref) = refs[12:]
    c = pl.program_id(0)

    @pl.when(c == 0)
    def _():
        st_ref[...] = jnp.zeros(st_ref.shape, F32)

    hm = hm_ref[...]
    eye = eye_ref[...]
    seg = seg_ref[...]
    w = RWKV_PACK * RWKV_HEAD
    nb = RWKV_PACK * CHUNK
    gw = GROUP_W

    def stack(x):
        return jnp.concatenate([x.astype(BF16)] * RWKV_PACK, axis=0) * hm

    streams = []
    for d, (o_ref, bo_ref) in enumerate(((o0_ref, bo0_ref), (o1_ref, bo1_ref))):
        rkv_ref, rkv_p, rkv_n, lo_ref, lo_p, lo_n = ins[d]
        pos = _chunk_pos(d, c, ncc, nch)
        seg_start = (pos == 0) | (pos == ncc)
        seg_end = (pos == ncc - 1) | (pos == nch - 1)
        rkv = _shift_mix_block(rkv_ref[...], rkv_p[...], rkv_n[...], mu_rkv_ref[...], seg_start, seg_end)
        lora = _shift_mix_block(lo_ref[...], lo_p[...], lo_n[...], mu_lo_ref[...], seg_start, seg_end)
        r, k, v = rkv[:, 0:gw], rkv[:, gw:2 * gw], rkv[:, 2 * gw:3 * gw]
        w_dn, a_dn, g_dn = lora[:, 0:128], lora[:, 128:256], lora[:, 256:384]
        vec = vec_ref[d]
        wl = vec[0:1] + _dot(jnp.tanh(w_dn), wup_ref[d])
        lw = -jnp.exp(-(jnp.maximum(-wl, 0.0) + jnp.log1p(jnp.exp(-jnp.abs(wl)))) - 0.5)
        a = jax.nn.sigmoid(vec[1:2] + _dot(a_dn, aup_ref[d]))
        kk = k * vec[2:3]
        kk = kk * lax.rsqrt(jnp.maximum(_seg_sum(kk * kk, seg), 1e-24))
        k = k * (1.0 + (a - 1.0) * vec[3:4])
        b = kk * a
        bo_ref[...] = _seg_sum(r * k * vec[4:5], seg) * v
        if d == 0:
            g_ref[...] = _dot(jax.nn.sigmoid(g_dn), gup_ref[...])
        bc = _dot_exact_lhs(tri_ref[d], lw)
        ebt = jnp.exp(jnp.sum(lw, axis=0, keepdims=True))
        einv = jnp.exp(-bc)
        khat = kk * jnp.exp(bc - lw)
        rhat = r * jnp.exp(bc)
        ks = k * einv
        bs = b * einv
        for g in range(groups):
            sl = slice(g * w, (g + 1) * w)
            streams.append(dict(
                d=d, g=g, sl=sl, o_ref=o_ref, ebt=ebt[:, sl],
                kr=jnp.concatenate([stack(khat[:, sl]), stack(rhat[:, sl])], axis=0),
                bk=jnp.concatenate([stack(bs[:, sl]), stack(ks[:, sl])], axis=0),
                vs=stack(v[:, sl])))

    for s in streams:
        aa = _dot_nt(s["kr"], s["bk"])
        strict, incl = strict_ref[s["d"]], incl_ref[s["d"]]
        s["akb"] = strict.astype(F32) * aa[:nb, :nb]
        s["akb_b"] = s["akb"].astype(BF16)
        s["arb"] = incl * aa[nb:, :nb].astype(BF16)
        s["ak_v"] = jnp.concatenate([strict * aa[:nb, nb:].astype(BF16), incl * aa[nb:, nb:].astype(BF16)], axis=0)
        s["minv"] = eye - off_ref[s["d"], N_LEVELS - 1].astype(F32) * s["akb"]
    for lv in range(N_LEVELS - 2, -1, -1):
        for s in streams:
            s["minv_b"] = s["minv"].astype(BF16)
            s["t1"] = _dot(off_ref[s["d"], lv] * s["akb_b"], s["minv_b"])
        for s in streams:
            s["minv"] = s["minv"] - _dot(s["minv_b"], s["t1"])
    for s in streams:
        st = st_ref[s["d"], s["g"]]
        from_state = _dot_nt(s["kr"], st)
        from_v = _dot(s["ak_v"], s["vs"])
        u = _dot(s["minv"], from_state[:nb] + from_v[:nb])
        o = from_state[nb:] + from_v[nb:] - _dot(s["arb"], u)
        s["o_ref"][:, s["sl"]] = sum(o[h * CHUNK:(h + 1) * CHUNK] for h in range(RWKV_PACK))
        upd = _dot(s["vs"].T, s["bk"][nb:]) - _dot(u.T, s["bk"][:nb])
        st_ref[s["d"], s["g"]] = (st + upd) * s["ebt"]


def rwkv_scan(p, rkv_off, lora_off, params, *, n_ctx, name="rwkv_scan"):
    l = p.shape[0]
    hw = GROUP_W
    w = RWKV_PACK * RWKV_HEAD
    groups = hw // w
    nch, ncc = l // CHUNK, n_ctx // CHUNK
    nb = RWKV_PACK * CHUNK
    halo = 8
    per_chunk = CHUNK // halo

    def shared(d):
        return pl.BlockSpec((CHUNK, hw), lambda c: (_chunk_pos(d, c, ncc, nch), 0))

    def piece(d, off, width):
        assert off % width == 0
        cb = off // width
        pos = lambda c: _chunk_pos(d, c, ncc, nch)
        return [pl.BlockSpec((CHUNK, width), lambda c: (pos(c), cb)),
                pl.BlockSpec((halo, width), lambda c: (jnp.maximum(pos(c) * per_chunk - 1, 0), cb)),
                pl.BlockSpec((halo, width), lambda c: (jnp.minimum((pos(c) + 1) * per_chunk, l // halo - 1), cb))]

    def const(a):
        return pl.BlockSpec(a.shape, lambda c: (0,) * a.ndim)

    bd = lambda a: np.kron(np.eye(RWKV_PACK, dtype=np.float32), a)
    strict = jnp.asarray(np.stack([bd(_CC["strict"][d]) for d in range(2)]), BF16)
    incl = jnp.asarray(np.stack([bd(_CC["strict"][d] + _CC["eye"][d]) for d in range(2)]), BF16)
    off = jnp.asarray(np.stack([np.stack([bd(_CC["off"][d, lv]) for lv in range(N_LEVELS)]) for d in range(2)]), BF16)
    eye = jnp.asarray(np.eye(nb, dtype=np.float32))
    hm = jnp.asarray(np.kron(np.eye(RWKV_PACK, dtype=np.float32), np.ones((CHUNK, RWKV_HEAD), np.float32)), BF16)
    tri = jnp.asarray(_CC["tri"], BF16)
    seg = jnp.asarray(np.kron(np.eye(RWKV_HEADS, dtype=np.float32), np.ones((RWKV_HEAD, RWKV_HEAD), np.float32)), BF16)
    consts = [_tab(a) for a in list(params) + [seg, tri, strict, incl, off, eye, hm]]
    in_specs = []
    for d in range(2):
        in_specs += piece(d, rkv_off, 3 * hw) + piece(d, lora_off, 3 * 128)
    return pl.pallas_call(
        functools.partial(_rwkv_kernel, groups=groups, ncc=ncc, nch=nch),
        out_shape=[jax.ShapeDtypeStruct((l, hw), F32)] * 5,
        grid=(nch,),
        in_specs=in_specs + [t.spec for t in consts],
        out_specs=[shared(0), shared(1), shared(0), shared(1), shared(0)],
        scratch_shapes=[pltpu.VMEM((2, groups, nb, nb), F32)],
        compiler_params=_cparams(("arbitrary",)),
        name=name,
    )(*([p] * len(in_specs)), *[t.arr for t in consts])


def _rwkv_post_kernel(of_ref, ob_ref, bf_ref, bb_ref, g_ref, ln_ref, seg_ref, o_ref):
    seg = seg_ref[...]
    o = of_ref[...] + ob_ref[...]
    mean = _seg_sum(o, seg) * (1.0 / RWKV_HEAD)
    oc = o - mean
    var = _seg_sum(oc * oc, seg) * (1.0 / RWKV_HEAD)
    y = oc * lax.rsqrt(var + RWKV_LN_EPS) * ln_ref[0:1] + ln_ref[1:2] + bf_ref[...] + bb_ref[...]
    o_ref[...] = (y * g_ref[...]).astype(o_ref.dtype)


def rwkv_post(o_f, o_b, bo_f, bo_b, g, ln, *, tm=768, name="rwkv_post"):
    l, w = o_f.shape
    tm = math.gcd(l, tm)
    row = pl.BlockSpec((tm, w), lambda i: (i, 0))
    ln = _tab(ln)
    seg = jnp.asarray(np.kron(np.eye(RWKV_HEADS, dtype=np.float32), np.ones((RWKV_HEAD, RWKV_HEAD), np.float32)), BF16)
    return pl.pallas_call(
        _rwkv_post_kernel,
        out_shape=jax.ShapeDtypeStruct((l, w), BF16),
        grid=(l // tm,),
        in_specs=[row] * 5 + [ln.spec, pl.BlockSpec((w, w), lambda i: (0, 0))],
        out_specs=row,
        compiler_params=_cparams(("parallel",)),
        name=name,
    )(o_f, o_b, bo_f, bo_b, g, ln.arr, seg)


@functools.lru_cache(maxsize=None)
def _rope_lane_tables(m, n):
    rows = n // GRID_W
    row = np.repeat(np.arange(rows, dtype=np.float32), GRID_W)
    col = np.tile(np.arange(GRID_W, dtype=np.float32), rows)
    n_freq = MLA_ROPE // 4
    freqs = (np.float32(ROPE_BASE) ** (-np.arange(n_freq, dtype=np.float32) / np.float32(n_freq))).astype(np.float32)
    ang = np.stack([row[:, None] * freqs, col[:, None] * freqs], axis=1).astype(np.float32)
    cos64 = np.repeat(np.cos(ang), 2, axis=1).reshape(n, MLA_ROPE)
    sin64 = (np.repeat(np.sin(ang), 2, axis=1) * np.array([-1.0, 1.0, -1.0, 1.0])[None, :, None]).reshape(n, MLA_ROPE)
    cos_t = np.tile(np.concatenate([np.ones((m, MLA_ROPE)), cos64], axis=0), (1, MLA_HEADS)).astype(np.float32)
    sin_t = np.tile(np.concatenate([np.zeros((m, MLA_ROPE)), sin64], axis=0), (1, MLA_HEADS)).astype(np.float32)
    return cos_t, sin_t


def _rope_lanes(x, cos_t, sin_t):
    n = x.shape[1]
    lane = lax.broadcasted_iota(jnp.int32, x.shape, 1)
    swapped = jnp.where(lane % 32 < 16, pltpu.roll(x, n - 16, 1), pltpu.roll(x, 16, 1))
    return x * cos_t + swapped * sin_t


def _mla_prep_kernel(cq_ref, ckv_ref, kr_ref, cos_ref, sin_ref, qn_ref, kvn_ref, wq_ref, wkv_ref,
                     q_ref, k_ref, v_ref):
    cq = cq_ref[...]
    ms = jnp.sum(cq * cq, axis=-1, keepdims=True) * (1.0 / MLA_Q_LORA)
    q = _dot(cq * lax.rsqrt(ms + EPS) * qn_ref[...], wq_ref[...])
    ckv = ckv_ref[...]
    kvn = ckv * lax.rsqrt(jnp.mean(ckv * ckv, axis=-1, keepdims=True) + EPS) * kvn_ref[...]
    kv = _dot(kvn, wkv_ref[...])
    cos_t, sin_t = cos_ref[...], sin_ref[...]
    hn = MLA_HEADS * MLA_NOPE
    scale = (MLA_NOPE + MLA_ROPE) ** -0.5 * math.log2(math.e)
    q_rope = _rope_lanes(q[:, hn:], cos_t, sin_t)
    k_rope = _rope_lanes(kr_ref[...], cos_t[:, :128], sin_t[:, :128])
    k_rope_hi = pltpu.roll(k_rope, MLA_ROPE, 1)
    lane = lax.broadcasted_iota(jnp.int32, k_rope.shape, 1)
    ones_col = jnp.where(lane == 0, 1.0, 0.0)
    for h in range(MLA_HEADS):
        pair = q_rope[:, (h // 2) * 128:(h // 2 + 1) * 128]
        q_ref[h] = (jnp.concatenate([q[:, h * MLA_NOPE:(h + 1) * MLA_NOPE], pair], axis=1) * scale).astype(q_ref.dtype)
        k_ref[h] = jnp.concatenate([kv[:, h * MLA_NOPE:(h + 1) * MLA_NOPE], k_rope if h % 2 == 0 else k_rope_hi],
                                   axis=1).astype(k_ref.dtype)
        v_ref[h] = jnp.concatenate([kv[:, hn + h * MLA_V:hn + (h + 1) * MLA_V], ones_col], axis=1).astype(v_ref.dtype)


def mla_tables(q_norm, w_q_up, kv_norm, w_kv_up):
    nl = w_q_up.shape[0]
    dq = MLA_NOPE + MLA_ROPE
    pad = PACKED["cq"][3] - MLA_Q_LORA
    wq = w_q_up.reshape(nl, MLA_Q_LORA, MLA_HEADS, dq)
    wq = jnp.concatenate([wq[..., :MLA_NOPE].reshape(nl, MLA_Q_LORA, -1), wq[..., MLA_NOPE:].reshape(nl, MLA_Q_LORA, -1)],
                         axis=2)
    wq = jnp.pad(wq, ((0, 0), (0, pad), (0, 0)))
    qn = jnp.pad(q_norm, ((0, 0), (0, pad)))[:, None, :]
    wkv = w_kv_up.reshape(nl, MLA_KV_LORA, MLA_HEADS, MLA_NOPE + MLA_V)
    wkv = jnp.concatenate([wkv[..., :MLA_NOPE].reshape(nl, MLA_KV_LORA, -1), wkv[..., MLA_NOPE:].reshape(nl, MLA_KV_LORA, -1)],
                          axis=2)
    return qn, kv_norm[:, None, :], wq, wkv


def mla_prep(p, m, tables, *, tm=256, name="mla_prep"):
    l = p.shape[0]
    cos_t, sin_t = (jnp.asarray(t) for t in _rope_lane_tables(m, l - m))
    tabs = [_tab(t) for t in tables]

    def col(nm):
        _, _, off, wp = PACKED[nm]
        return pl.BlockSpec((tm, wp), lambda i: (i, off // wp))

    row = lambda wd: pl.BlockSpec((tm, wd), lambda i: (i, 0))
    out_spec = pl.BlockSpec((MLA_HEADS, tm, 256), lambda i: (0, i, 0))
    return pl.pallas_call(
        _mla_prep_kernel,
        out_shape=[jax.ShapeDtypeStruct((MLA_HEADS, l, 256), BF16)] * 3,
        grid=(l // tm,),
        in_specs=[col("cq"), col("ckv"), col("k_rope"), row(256), row(256)] + [t.spec for t in tabs],
        out_specs=[out_spec] * 3,
        compiler_params=_cparams(("parallel",)),
        name=name,
    )(p, p, p, cos_t, sin_t, *[t.arr for t in tabs])


def mla_mixer(p, m, tables):
    qh, kh, vh = mla_prep(p, m, tables)
    o_ctx = flash_attention(qh, kh, vh, dv=MLA_V, tq=m, lq=m, lk=m, name="mla_attn_ctx")
    o_lat = flash_attention(qh[:, m:], kh, vh, dv=MLA_V, tq=1024, name="mla_attn_lat")
    return jnp.concatenate([o_ctx, o_lat], axis=0)


def _pad_rows(t, before, total):
    cfg = [(0, 0)] * t.ndim
    cfg[-2] = (before, total - before - t.shape[-2])
    return jnp.pad(t, cfg)


def rwkv_tables(mu, w0, w_up, a0, a_up, k_k, k_a, u, g_up, ln_g, ln_b):
    zeros = jnp.zeros_like(w0)
    vec = jnp.stack([w0, a0, k_k, k_a, u, zeros, zeros, zeros], axis=2)
    wup = jnp.stack([_pad_rows(w_up[:, d], d * RWKV_W_LORA, 128) for d in range(2)], axis=1)
    aup = jnp.stack([_pad_rows(a_up[:, d], d * RWKV_A_LORA, 128) for d in range(2)], axis=1)
    ln = jnp.stack([ln_g, ln_b] + [jnp.zeros_like(ln_g)] * 6, axis=1)
    return (mu[:, :, :3 * GROUP_W], mu[:, :, 3 * GROUP_W:], vec, wup, aup, g_up), ln


def rwkv7_mixer(p, m, params, ln):
    outs = rwkv_scan(p, PACKED["rwkv_r"][2], PACKED["w_dn"][2], params, n_ctx=m)
    return rwkv_post(*outs, ln)


PACKED = {}


def _build_packed():
    orig = dict(cq=(0, MLA_Q_LORA), ckv=(MLA_Q_LORA, MLA_KV_LORA), k_rope=(MLA_Q_LORA + MLA_KV_LORA, MLA_ROPE))
    b = MLA_COLS
    for i, nm in enumerate(("rwkv_r", "rwkv_k", "rwkv_v")):
        orig[nm] = (b + i * GROUP_W, GROUP_W)
    b += 3 * GROUP_W
    orig.update(w_dn=(b, 2 * RWKV_W_LORA), a_dn=(b + 2 * RWKV_W_LORA, 2 * RWKV_A_LORA),
                g_dn=(b + 2 * RWKV_W_LORA + 2 * RWKV_A_LORA, RWKV_G_LORA))
    b = IN_SPLITS[1]
    orig.update(gla_q=(b, GLA_KD), gla_k=(b + GLA_KD, GLA_KD), gla_v=(b + 2 * GLA_KD, GROUP_W),
                gla_a=(b + GLA_SPLITS[2], 2 * GLA_GATE_RANK), gla_r=(b + GLA_SPLITS[3], GROUP_W))
    b = IN_SPLITS[2]
    for i, nm in enumerate(("hgrn_q", "hgrn_f0", "hgrn_f1", "hgrn_i", "hgrn_g")):
        orig[nm] = (b + i * GROUP_W, GROUP_W)
    order = [(nm, 512) for nm in ("rwkv_r", "rwkv_k", "rwkv_v", "gla_v", "gla_r", "hgrn_q", "hgrn_f0", "hgrn_f1",
                                  "hgrn_i", "hgrn_g", "cq")]
    orig["pad"] = (0, 0)
    order += [(nm, 256) for nm in ("gla_q", "gla_k")]
    order += [(nm, 128) for nm in ("w_dn", "a_dn", "g_dn", "gla_a", "k_rope", "pad")]
    order += [("ckv", 256)]
    off = 0
    for nm, wp in order:
        PACKED[nm] = (orig[nm][0], orig[nm][1], off, wp)
        off += wp
    return off


N_PACKED = _build_packed()


def pack_w_in(w):
    pieces = []
    for o, wd, _, wp in PACKED.values():
        pieces.append(w[..., o:o + wd])
        if wp > wd:
            pieces.append(jnp.zeros(w.shape[:-1] + (wp - wd,), w.dtype))
    return jnp.concatenate(pieces, axis=-1)


def _pcol(p, nm):
    _, wd, off, _ = PACKED[nm]
    return p[:, off:off + wd]


def _blk(nm):
    return (PACKED[nm][2], PACKED[nm][3])


def gla_tables(a_up, a_bias):
    return jnp.stack([jnp.concatenate([_pad_rows(a_up[:, d], d * GLA_GATE_RANK, 128),
                                       _pad_rows(a_bias[:, d][:, None, :], 0, 8)], axis=1) for d in range(2)], axis=1)


def hgrn_tables(lb_all):
    lb = jnp.swapaxes(lb_all, 0, 1)
    zeros = jnp.zeros_like(lb)
    return jnp.stack([jnp.log(lb), jnp.log1p(-lb), 1.0 - lb] + [zeros] * 5, axis=2)


def gla_mixer(p, m, par, g_norm):
    cols = [[_blk("gla_q"), _blk("gla_k"), _blk("gla_v"), _blk("gla_a")]] * 2
    o_f, o_b = gla_scan(p, cols, par, mode="gla", heads=GLA_HEADS, dk=GLA_DK, dv=GLA_DV, pack=4, n_ctx=m,
                        name="gla_scan")
    return mix_post(o_f, o_b, p, PACKED["gla_r"][2], g_norm, heads=GLA_HEADS, dv=GLA_DV, name="gla_post")


def hgrn2_mixer(p, m, par, g_norm):
    cols = [[_blk("hgrn_q"), _blk("hgrn_f%d" % d), _blk("hgrn_i")] for d in range(2)]
    o_f, o_b = gla_scan(p, cols, par, mode="hgrn", heads=HGRN_HEADS, dk=HGRN_EXPAND, dv=HGRN_DV, pack=2, n_ctx=m,
                        name="hgrn_scan")
    return mix_post(o_f, o_b, p, PACKED["hgrn_g"][2], g_norm, heads=HGRN_HEADS, dv=HGRN_DV, name="hgrn_post")


def dense_ffn(xs, h2, w1, w3, w2, j, gates2, m):
    act = pmatmul(h2, w1, (j,), w3=w3, tm=1408, tn=512, out_dtype=BF16, name="ffn_up")
    return pmatmul(act, w2, (j,), tm=768, tn=256, res=xs, gates=gates2, m_ctx=m, name="ffn_down")


def _route(top_idx, weights):
    n_pairs = top_idx.size
    r = n_pairs + N_EXPERTS * MOE_TILE
    n_tiles = r // MOE_TILE
    e_flat = top_idx.reshape(-1)
    onehot = (e_flat[:, None] == jnp.arange(N_EXPERTS)[None, :]).astype(jnp.int32)
    csum = jnp.cumsum(onehot, axis=0)
    counts = csum[-1]
    padded = (counts + MOE_TILE - 1) // MOE_TILE * MOE_TILE
    ends = jnp.cumsum(padded)
    dest = jnp.sum(onehot * (csum - 1 + (ends - padded)[None, :]), axis=1)
    src_token = jnp.zeros((r,), jnp.int32).at[dest].set(jnp.arange(n_pairs, dtype=jnp.int32) // TOP_K)
    row_gate = jnp.zeros((r,), F32).at[dest].set(weights.reshape(-1))
    tile_start = jnp.arange(n_tiles, dtype=jnp.int32) * MOE_TILE
    tile_expert = jnp.minimum(jnp.sum(tile_start[:, None] >= ends[None, :], axis=1), N_EXPERTS - 1)
    meta = jnp.concatenate([tile_expert.astype(jnp.int32), (ends[-1:] // MOE_TILE).astype(jnp.int32)])
    return src_token, row_gate, dest.reshape(top_idx.shape), meta


def _combine_kernel(x_ref, y_ref, gate_ref, o_ref, *, m_ctx, tm, d):
    i = pl.program_id(0)
    rows = i * tm + lax.broadcasted_iota(jnp.int32, (tm, d), 0)
    g = jnp.where(rows < m_ctx, gate_ref[0:1, :], gate_ref[1:2, :])
    o_ref[...] = x_ref[...] + g * (y_ref[0].astype(F32) + y_ref[1].astype(F32))


def moe_combine(xs, y_tok, gates2, *, m_ctx, tm=256, name="moe_combine"):
    l, d = xs.shape
    gates2 = _tab(gates2)
    return pl.pallas_call(
        functools.partial(_combine_kernel, m_ctx=m_ctx, tm=tm, d=d),
        out_shape=jax.ShapeDtypeStruct((l, d), F32),
        grid=(l // tm,),
        in_specs=[pl.BlockSpec((tm, d), lambda i: (i, 0)), pl.BlockSpec((TOP_K, tm, d), lambda i: (0, i, 0)),
                  gates2.spec],
        out_specs=pl.BlockSpec((tm, d), lambda i: (i, 0)),
        compiler_params=_cparams(("parallel",)),
        name=name,
    )(xs, y_tok, gates2.arr)


def moe_ffn(xs, h2, logits, w1, w3, w2, j, gates2, m):
    l, d = h2.shape
    top_vals, top_idx = lax.top_k(logits[:, :N_EXPERTS], TOP_K)
    weights = jax.nn.softmax(top_vals, axis=-1)
    src_token, row_gate, dest, meta = _route(top_idx, weights)
    src_token = lax.optimization_barrier(src_token)
    x_sorted = jnp.take(h2, src_token, axis=0)
    act = gmatmul(meta, x_sorted, w1, j, w3=w3, tn=512, out_dtype=BF16, rowscale=row_gate[:, None], name="moe_up")
    y = gmatmul(meta, act, w2, j, tn=512, out_dtype=BF16, name="moe_down")
    dest_t = lax.optimization_barrier(dest.T.reshape(-1))
    y_tok = jnp.take(y, dest_t, axis=0).reshape(TOP_K, l, d)
    return moe_combine(xs, y_tok, gates2, m_ctx=m)


def kernel(x, c, ctx, c_ctx, norm1_g, norm2_g, w_mod, b_mod, w_in, w_out, mla_q_norm, mla_w_q_up, mla_kv_norm, mla_w_kv_up, rwkv_mu, rwkv_w0, rwkv_w_up, rwkv_a0, rwkv_a_up, rwkv_k_k, rwkv_k_a, rwkv_u, rwkv_g_up, rwkv_ln_g, rwkv_ln_b, gla_a_up, gla_a_bias, gla_norm, hgrn_lb, hgrn_norm, ffn_w1, ffn_w3, ffn_w2, moe_router, moe_w1, moe_w3, moe_w2, final_norm_g):
    m, n, d = ctx.shape[1], x.shape[1], x.shape[2]
    lb_all = jnp.cumsum(jax.nn.softmax(hgrn_lb.astype(F32), axis=1), axis=1)
    lb_all = lb_all - lb_all[:, :1]
    xs = jnp.concatenate([ctx[0], x[0]], axis=0)

    cvec = jnp.zeros((16, d), F32).at[0].set(jax.nn.silu(c[0])).at[1].set(jax.nn.silu(c_ctx))
    mod = jnp.stack([pmatmul(cvec, w_mod, (l,), tm=16, tn=2048, name="mod")[:2] for l in range(DEPTH)]) + b_mod[:, None, :]
    mods = jnp.swapaxes(mod[:, ::-1].reshape(DEPTH, 2, 6, d), 1, 2)
    zmods = jnp.zeros((2, d), F32)
    g1, g2 = norm1_g[:, None, :], norm2_g[:, None, :]
    w_in_p = pack_w_in(w_in)
    mla_tabs = mla_tables(mla_q_norm, mla_w_q_up, mla_kv_norm, mla_w_kv_up)
    rwkv_tabs, rwkv_ln = rwkv_tables(rwkv_mu, rwkv_w0, rwkv_w_up, rwkv_a0, rwkv_a_up, rwkv_k_k, rwkv_k_a, rwkv_u,
                                      rwkv_g_up, rwkv_ln_g, rwkv_ln_b)
    gla_par, hgrn_par = gla_tables(gla_a_up, gla_a_bias), hgrn_tables(lb_all)
    gla_g, hgrn_g = gla_norm[:, None, :], hgrn_norm[:, None, :]
    router_p = jnp.pad(moe_router, ((0, 0), (0, 0), (0, 128 - N_EXPERTS)))

    for l in range(DEPTH):
        sh1, sc1, gt1, sh2, sc2, gt2 = (_Tab(mods, (l, i)) for i in range(6))
        h = norm_mod(xs, _Tab(g1, (l,)), sh1, sc1, m_ctx=m, out_dtype=BF16)
        p = pmatmul(h, w_in_p, (l,), tm=1408, tn=1024, name="w_in")
        o = [
            mla_mixer(p, m, [_Tab(t, (l,)) for t in mla_tabs]),
            rwkv7_mixer(p, m, [_Tab(t, (l,)) for t in rwkv_tabs], _Tab(rwkv_ln, (l,))),
            gla_mixer(p, m, _Tab(gla_par, (l,)), _Tab(gla_g, (l,))),
            hgrn2_mixer(p, m, _Tab(hgrn_par, (l,)), _Tab(hgrn_g, (l,))),
        ]
        xs = pmatmul(o, w_out, (l,), tm=1408, tn=512, res=xs, gates=gt1, m_ctx=m, name="w_out")

        j = l // 2
        if l % 2 == 0:
            h2 = norm_mod(xs, _Tab(g2, (l,)), sh2, sc2, m_ctx=m, out_dtype=BF16)
            xs = dense_ffn(xs, h2, ffn_w1, ffn_w3, ffn_w2, j, gt2, m)
        else:
            h2, logits = norm_mod(xs, _Tab(g2, (l,)), sh2, sc2, m_ctx=m, out_dtype=BF16, router=_Tab(router_p, (j,)),
                                  name="norm_mod_router")
            xs = moe_ffn(xs, h2, logits, moe_w1, moe_w3, moe_w2, j, gt2, m)
    out = norm_mod(xs, final_norm_g[None, :], zmods, zmods, m_ctx=m, out_dtype=F32, name="final_norm")
    return out[m:][None]
```

```python
import functools
import math

import numpy as np
import jax
import jax.numpy as jnp
from jax import lax
from jax.experimental import pallas as pl
from jax.experimental.pallas import tpu as pltpu

F32 = jnp.float32
BF16 = jnp.bfloat16

DEPTH = 4
GRID_W = 64
EPS = 1e-6
GROUP_W = 512
MLA_HEADS, MLA_NOPE, MLA_ROPE, MLA_V = 4, 128, 64, 128
MLA_Q_LORA, MLA_KV_LORA = 384, 256
ROPE_BASE = 10000.0
RWKV_HEADS, RWKV_HEAD = 8, 64
RWKV_W_LORA, RWKV_A_LORA, RWKV_G_LORA = 64, 64, 128
RWKV_LN_EPS = 64e-5
GLA_HEADS, GLA_DK, GLA_DV = 4, 64, 128
GLA_GATE_RANK, GLA_GATE_NORM = 16, 16.0
HGRN_HEADS, HGRN_EXPAND, HGRN_DV = 4, 128, 128
CHUNK = 64
N_EXPERTS, TOP_K = 8, 2
MLA_COLS = MLA_Q_LORA + MLA_KV_LORA + MLA_ROPE
RWKV_COLS = 3 * GROUP_W + 2 * RWKV_W_LORA + 2 * RWKV_A_LORA + RWKV_G_LORA
GLA_KD = GLA_HEADS * GLA_DK
GLA_COLS = 2 * GLA_KD + GROUP_W + 2 * GLA_GATE_RANK + GROUP_W
IN_SPLITS = (MLA_COLS, MLA_COLS + RWKV_COLS, MLA_COLS + RWKV_COLS + GLA_COLS)
RWKV_SPLITS = (GROUP_W, 2 * GROUP_W, 3 * GROUP_W, 3 * GROUP_W + 2 * RWKV_W_LORA,
               3 * GROUP_W + 2 * RWKV_W_LORA + 2 * RWKV_A_LORA)
GLA_SPLITS = (GLA_KD, 2 * GLA_KD, 2 * GLA_KD + GROUP_W, 2 * GLA_KD + GROUP_W + 2 * GLA_GATE_RANK)

V7X_VMEM_LIMIT = 56 * 1024 * 1024
V7X_MXU = 256
N_LEVELS = 6
RWKV_PACK = V7X_MXU // RWKV_HEAD
MOE_TILE = 512


def _cparams(sem, vmem=V7X_VMEM_LIMIT):
    return pltpu.CompilerParams(dimension_semantics=sem, vmem_limit_bytes=vmem)


class _Tab:
    def __init__(self, arr, idx=()):
        self.arr, self.idx = arr, tuple(idx)

    @property
    def shape(self):
        return self.arr.shape[len(self.idx):]

    @property
    def spec(self):
        idx, rest = self.idx, self.shape
        return pl.BlockSpec((None,) * len(idx) + rest, lambda *_: idx + (0,) * len(rest))


def _tab(a):
    return a if isinstance(a, _Tab) else _Tab(a)


def _dot(a, b):
    return jnp.dot(a.astype(BF16), b.astype(BF16), preferred_element_type=F32)


def _dot_nt(a, b):
    return lax.dot_general(a.astype(BF16), b.astype(BF16), (((1,), (1,)), ((), ())), preferred_element_type=F32)


def _split2(x):
    hi = x.astype(BF16)
    lo = (x - hi.astype(F32)).astype(BF16)
    return hi, lo


def _dot_exact_lhs(m_bf16, x):
    hi, lo = _split2(x)
    return (jnp.dot(m_bf16, hi, preferred_element_type=F32) + jnp.dot(m_bf16, lo, preferred_element_type=F32))


def _mm_kernel(*refs, n_x, n_w, has_res, m_ctx, tm, cast_w):
    it = iter(refs)
    x_refs = [next(it) for _ in range(n_x)]
    w_refs = [next(it) for _ in range(n_w)]
    res_ref = next(it) if has_res else None
    gate_ref = next(it) if has_res else None
    o_ref = next(it)
    wb_refs = [next(it) for _ in range(n_w)] if cast_w else w_refs
    i = pl.program_id(1)

    if cast_w:
        @pl.when(i == 0)
        def _():
            for w_ref, wb_ref in zip(w_refs, wb_refs):
                wb_ref[...] = w_ref[...].astype(BF16)
    x = jnp.concatenate([x_ref[...].astype(BF16) for x_ref in x_refs], axis=1)

    acc = jnp.dot(x, wb_refs[0][...], preferred_element_type=F32)
    if n_w == 2:
        acc3 = jnp.dot(x, wb_refs[1][...], preferred_element_type=F32)
        acc = acc * jax.nn.sigmoid(acc) * acc3
    if has_res:
        rows = i * tm + lax.broadcasted_iota(jnp.int32, acc.shape, 0)
        g = jnp.where(rows < m_ctx, gate_ref[0:1, :], gate_ref[1:2, :])
        acc = res_ref[...] + g * acc
    o_ref[...] = acc.astype(o_ref.dtype)


def pmatmul(x, w, widx=(), *, w3=None, tm, tn, out_dtype=F32, res=None, gates=None, m_ctx=0, name="mm"):
    xs = list(x) if isinstance(x, (list, tuple)) else [x]
    m = xs[0].shape[0]
    k = sum(xi.shape[1] for xi in xs)
    n = w.shape[-1]
    tm = math.gcd(m, tm)
    assert w.shape[-2] == k and tm % 16 == 0
    nj, ni = pl.cdiv(n, tn), m // tm
    lead = (None,) * len(widx)
    w_spec = pl.BlockSpec(lead + (k, tn), lambda j, i: tuple(widx) + (0, j))
    ws = [w] if w3 is None else [w, w3]
    in_specs = [pl.BlockSpec((tm, xi.shape[1]), lambda j, i: (i, 0)) for xi in xs] + [w_spec] * len(ws)
    args = xs + ws
    if res is not None:
        gates = _tab(gates)
        gi = gates.idx
        in_specs += [pl.BlockSpec((tm, tn), lambda j, i: (i, j)),
                     pl.BlockSpec((None,) * len(gi) + (2, tn), lambda j, i: gi + (0, j))]
        args += [res, gates.arr]
    cast_w = w.dtype != BF16
    kern = functools.partial(_mm_kernel, n_x=len(xs), n_w=len(ws), has_res=res is not None, m_ctx=m_ctx, tm=tm,
                             cast_w=cast_w)
    return pl.pallas_call(
        kern,
        out_shape=jax.ShapeDtypeStruct((m, n), out_dtype),
        grid=(nj, ni),
        in_specs=in_specs,
        out_specs=pl.BlockSpec((tm, tn), lambda j, i: (i, j)),
        scratch_shapes=[pltpu.VMEM((k, tn), BF16) for _ in ws] if cast_w else [],
        compiler_params=_cparams(("arbitrary", "arbitrary")),
        name=name,
    )(*args)


def _gmm_kernel(meta_ref, x_ref, *refs, n_w, has_rowscale, n_tiles):
    it = iter(refs)
    w_refs = [next(it) for _ in range(n_w)]
    rs_ref = next(it) if has_rowscale else None
    o_ref = next(it)
    wb_refs = [next(it) for _ in range(n_w)]
    t = pl.program_id(1)
    e = meta_ref[t]
    e_prev = meta_ref[jnp.maximum(t - 1, 0)]

    @pl.when((t == 0) | (e != e_prev))
    def _():
        for w_ref, wb_ref in zip(w_refs, wb_refs):
            wb_ref[...] = w_ref[...].astype(BF16)

    @pl.when(t < meta_ref[n_tiles])
    def _():
        x = x_ref[...].astype(BF16)
        acc = jnp.dot(x, wb_refs[0][...], preferred_element_type=F32)
        if n_w == 2:
            acc3 = jnp.dot(x, wb_refs[1][...], preferred_element_type=F32)
            acc = acc * jax.nn.sigmoid(acc) * acc3
        if has_rowscale:
            acc = acc * rs_ref[...]
        o_ref[...] = acc.astype(o_ref.dtype)

    @pl.when(t >= meta_ref[n_tiles])
    def _():
        o_ref[...] = jnp.zeros(o_ref.shape, o_ref.dtype)


def gmatmul(meta, x, w, jl, *, w3=None, tn, out_dtype, rowscale=None, name):
    r, k = x.shape
    n = w.shape[-1]
    n_tiles = r // MOE_TILE
    ws = [w] if w3 is None else [w, w3]
    w_spec = pl.BlockSpec((None, None, k, tn), lambda j, t, mr: (jl, mr[t], 0, j))
    in_specs = [pl.BlockSpec((MOE_TILE, k), lambda j, t, mr: (t, 0))] + [w_spec] * len(ws)
    args = [x] + ws
    if rowscale is not None:
        in_specs.append(pl.BlockSpec((MOE_TILE, 1), lambda j, t, mr: (t, 0)))
        args.append(rowscale)
    return pl.pallas_call(
        functools.partial(_gmm_kernel, n_w=len(ws), has_rowscale=rowscale is not None, n_tiles=n_tiles),
        out_shape=jax.ShapeDtypeStruct((r, n), out_dtype),
        grid_spec=pltpu.PrefetchScalarGridSpec(
            num_scalar_prefetch=1,
            grid=(n // tn, n_tiles),
            in_specs=in_specs,
            out_specs=pl.BlockSpec((MOE_TILE, tn), lambda j, t, mr: (t, j)),
            scratch_shapes=[pltpu.VMEM((k, tn), BF16) for _ in ws]),
        compiler_params=_cparams(("arbitrary", "arbitrary")),
        name=name,
    )(meta, *args)


def _norm_kernel(x_ref, g_ref, sh_ref, sc_ref, *rest, m_ctx, tm, with_router):
    i = pl.program_id(0)
    x = x_ref[...]
    y = x * lax.rsqrt(jnp.mean(x * x, axis=-1, keepdims=True) + EPS) * g_ref[...]
    rows = i * tm + lax.broadcasted_iota(jnp.int32, x.shape, 0)
    is_ctx = rows < m_ctx
    sc = jnp.where(is_ctx, sc_ref[0:1, :], sc_ref[1:2, :])
    sh = jnp.where(is_ctx, sh_ref[0:1, :], sh_ref[1:2, :])
    h = y * (1.0 + sc) + sh
    if with_router:
        r_ref, o_ref, logit_ref = rest
        logit_ref[...] = jnp.dot(h, r_ref[...], preferred_element_type=F32, precision=lax.Precision.HIGHEST)
    else:
        (o_ref,) = rest
    o_ref[...] = h.astype(o_ref.dtype)


def norm_mod(x, g, shift2, scale2, *, m_ctx, out_dtype, router=None, tm=768, name="norm_mod"):
    m, d = x.shape
    tm = math.gcd(m, tm)
    row = pl.BlockSpec((tm, d), lambda i: (i, 0))
    tabs = [_tab(g), _tab(shift2), _tab(scale2)]
    in_specs = [row] + [t.spec for t in tabs]
    args = [x] + [t.arr for t in tabs]
    out_shape = jax.ShapeDtypeStruct((m, d), out_dtype)
    out_specs = row
    if router is not None:
        router = _tab(router)
        in_specs.append(router.spec)
        args.append(router.arr)
        out_shape = [out_shape, jax.ShapeDtypeStruct((m, router.shape[1]), F32)]
        out_specs = [row, pl.BlockSpec((tm, router.shape[1]), lambda i: (i, 0))]
    return pl.pallas_call(
        functools.partial(_norm_kernel, m_ctx=m_ctx, tm=tm, with_router=router is not None),
        out_shape=out_shape,
        grid=(m // tm,),
        in_specs=in_specs,
        out_specs=out_specs,
        compiler_params=_cparams(("parallel",)),
        name=name,
    )(*args)


def _attn_kernel(q_ref, k_ref, v_ref, o_ref, *, tk, n_kv, dv):
    q = q_ref[...]
    m = acc = None
    for j in range(n_kv):
        kb = k_ref[j * tk:(j + 1) * tk, :]
        vb = v_ref[j * tk:(j + 1) * tk, :]
        s = lax.dot_general(q, kb, (((1,), (1,)), ((), ())), preferred_element_type=F32)
        m_blk = jnp.max(s, axis=-1, keepdims=True)
        m_new = m_blk if j == 0 else jnp.maximum(m, m_blk)
        pv = jnp.dot(jnp.exp2(s - m_new).astype(BF16), vb, preferred_element_type=F32)
        acc = pv if j == 0 else jnp.exp2(m - m_new) * acc + pv
        m = m_new
    o_ref[...] = (acc[:, :dv] / acc[:, dv:dv + 1]).astype(o_ref.dtype)


def _attn_kv_tile(lk, cap=1408):
    return max(t for t in range(128, cap + 1, 128) if lk % t == 0)


def _attn_into_kernel(q_ref, k_ref, v_ref, prev_ref, o_ref, **kw):
    del prev_ref
    _attn_kernel(q_ref, k_ref, v_ref, o_ref, **kw)


def flash_attention(q, k, v_ext, *, dv, tq, lq=None, lk=None, into=None, name="mla_attn"):
    h, dqk = q.shape[0], q.shape[2]
    lq = q.shape[1] if lq is None else lq
    lk = k.shape[1] if lk is None else lk
    dve = v_ext.shape[2]
    tk = _attn_kv_tile(lk)
    tq = math.gcd(lq, tq)
    kw = dict(tk=tk, n_kv=lk // tk, dv=dv)
    in_specs = [pl.BlockSpec((None, tq, dqk), lambda hh, i: (hh, i, 0)),
                pl.BlockSpec((None, lk, dqk), lambda hh, i: (hh, 0, 0)),
                pl.BlockSpec((None, lk, dve), lambda hh, i: (hh, 0, 0))]
    args = [q, k, v_ext]
    if into is not None:
        in_specs.append(pl.BlockSpec(memory_space=pl.ANY))
        args.append(into)
    return pl.pallas_call(
        functools.partial(_attn_kernel if into is None else _attn_into_kernel, **kw),
        out_shape=jax.ShapeDtypeStruct((lq, h * dv) if into is None else into.shape, BF16),
        grid=(h, lq // tq),
        in_specs=in_specs,
        out_specs=pl.BlockSpec((tq, dv), lambda hh, i: (i, hh)),
        input_output_aliases={} if into is None else {3: 0},
        compiler_params=_cparams(("parallel", "parallel")),
        name=name,
    )(*args)


def _chunk_constants():
    c = CHUNK
    t = np.arange(c)
    tri = (t[None, :] <= t[:, None]).astype(np.float32)
    strict = (t[None, :] < t[:, None]).astype(np.float32)
    eye = np.eye(c, dtype=np.float32)
    seg, off = [], []
    for lv in range(N_LEVELS):
        s = c >> (lv + 1)
        blk = t // s
        same = blk[:, None] == blk[None, :]
        odd = (blk % 2 == 1)[:, None]
        seg.append(np.where(odd, same & (t[None, :] <= t[:, None]), same & (t[None, :] > t[:, None])).astype(np.float32))
        off.append((odd & (blk[None, :] == blk[:, None] - 1)).astype(np.float32))
    seg, off = np.stack(seg), np.stack(off)

    def both(a):
        return np.stack([a, a[..., ::-1, ::-1]])

    return {k: both(v) for k, v in dict(tri=tri, strict=strict, eye=eye, seg=seg, off=off).items()}


_CC = _chunk_constants()


def _chunk_pos(d, c, n_ctx_chunks, n_chunks):
    back = jnp.where(c < n_ctx_chunks, n_ctx_chunks - 1 - c, n_chunks + n_ctx_chunks - 1 - c)
    return jnp.where(d == 0, c, back)


def _log_sigmoid(x):
    return jnp.minimum(x, 0.0) - jnp.log1p(jnp.exp(-jnp.abs(x)))


def _gla_kernel(*refs, mode, dk, dv, pack, groups, n_in):
    ins = [refs[:n_in], refs[n_in:2 * n_in]]
    par_ref, mall_ref, off_ref, eye_ref, hmk_ref, hmv_ref, o0_ref, o1_ref, st_ref = refs[2 * n_in:]
    c = pl.program_id(0)

    @pl.when(c == 0)
    def _():
        st_ref[...] = jnp.zeros(st_ref.shape, F32)

    eye = eye_ref[...]
    hmk = hmk_ref[...]
    hmv = hmv_ref[...]
    wk, wv = pack * dk, pack * dv
    for d, o_ref in enumerate((o0_ref, o1_ref)):
        if mode == "hgrn":
            q_raw, f_raw, v = (r[...] for r in ins[d])
            par = par_ref[d]
            q = q_raw * jax.nn.sigmoid(q_raw)
            g = jnp.logaddexp(par[0:1], par[1:2] + _log_sigmoid(f_raw))
            k = par[2:3] * jax.nn.sigmoid(-f_raw)
        else:
            q_raw, k, v, a_dn = (r[...] for r in ins[d])
            par = par_ref[d]
            q = q_raw * dk ** -0.5
            g = _log_sigmoid(_dot(a_dn, par[0:128]) + par[128:129]) / GLA_GATE_NORM
        e_all = _dot_exact_lhs(mall_ref[d], g)
        bc = e_all[0:CHUNK]
        btot = jnp.sum(g, axis=0, keepdims=True)
        qhat = q * jnp.exp(bc)
        kt = k * jnp.exp(btot - bc)
        ebt = jnp.exp(btot)
        qw, kw = [q], [k]
        for lv in range(N_LEVELS):
            w = jnp.exp(e_all[(lv + 1) * CHUNK:(lv + 2) * CHUNK])
            qw.append(q * w)
            kw.append(k * w)
        for gi in range(groups):
            slk = slice(gi * wk, (gi + 1) * wk)
            slv = slice(gi * wv, (gi + 1) * wv)

            def stk(x):
                return jnp.concatenate([x[:, slk].astype(BF16)] * pack, axis=0) * hmk

            def tile(x):
                return jnp.concatenate([x[:, slk].astype(BF16)] * pack, axis=0)

            att = eye * _dot_nt(stk(qw[0]), tile(kw[0]))
            for lv in range(N_LEVELS):
                att = att + off_ref[d, lv] * _dot_nt(stk(qw[lv + 1]), tile(kw[lv + 1]))
            vs = jnp.concatenate([v[:, slv]] * pack, axis=0) * hmv
            st = st_ref[d, gi]
            o = _dot_nt(stk(qhat), st) + _dot(att, vs)
            o_ref[:, slv] = sum(o[h * CHUNK:(h + 1) * CHUNK] for h in range(pack))
            st_ref[d, gi] = st * ebt[:, slk] + _dot(vs.T, stk(kt))


def gla_scan(p, cols, par, *, mode, heads, dk, dv, pack, n_ctx, name):
    l = p.shape[0]
    groups = heads // pack
    nch, ncc = l // CHUNK, n_ctx // CHUNK
    nb = pack * CHUNK

    def col_spec(d, off, width):
        assert off % width == 0
        return pl.BlockSpec((CHUNK, width), lambda c: (_chunk_pos(d, c, ncc, nch), off // width))

    def const(a):
        return pl.BlockSpec(a.shape, lambda c: (0,) * a.ndim)

    bd = lambda a: np.kron(np.eye(pack, dtype=np.float32), a)
    mall = jnp.asarray(np.concatenate([_CC["tri"][:, None], _CC["seg"]], axis=1).reshape(2, -1, CHUNK), BF16)
    off = jnp.asarray(np.stack([np.stack([bd(_CC["off"][d, lv]) for lv in range(N_LEVELS)]) for d in range(2)]), F32)
    eye = jnp.asarray(np.eye(nb, dtype=np.float32))
    hmk = jnp.asarray(np.kron(np.eye(pack, dtype=np.float32), np.ones((CHUNK, dk), np.float32)), BF16)
    hmv = jnp.asarray(np.kron(np.eye(pack, dtype=np.float32), np.ones((CHUNK, dv), np.float32)))
    consts = [_tab(a) for a in (par, mall, off, eye, hmk, hmv)]
    in_specs = [col_spec(d, o, w) for d in range(2) for (o, w) in cols[d]]
    out_spec = lambda d: pl.BlockSpec((CHUNK, heads * dv), lambda c: (_chunk_pos(d, c, ncc, nch), 0))
    return pl.pallas_call(
        functools.partial(_gla_kernel, mode=mode, dk=dk, dv=dv, pack=pack, groups=groups, n_in=len(cols[0])),
        out_shape=[jax.ShapeDtypeStruct((l, heads * dv), F32)] * 2,
        grid=(nch,),
        in_specs=in_specs + [t.spec for t in consts],
        out_specs=[out_spec(0), out_spec(1)],
        scratch_shapes=[pltpu.VMEM((2, groups, pack * dv, pack * dk), F32)],
        compiler_params=_cparams(("arbitrary",)),
        name=name,
    )(*([p] * len(in_specs)), *[t.arr for t in consts])


def _post_kernel(of_ref, ob_ref, gate_ref, gn_ref, o_ref, *, heads, dv):
    o = of_ref[...] + ob_ref[...]
    gate = gate_ref[...]
    gn = gn_ref[...]
    for h in range(heads):
        sl = slice(h * dv, (h + 1) * dv)
        oh = o[:, sl]
        y = oh * lax.rsqrt(jnp.mean(oh * oh, axis=-1, keepdims=True) + EPS) * gn
        gh = gate[:, sl]
        o_ref[:, sl] = (y * gh * jax.nn.sigmoid(gh)).astype(o_ref.dtype)


def mix_post(o_f, o_b, p, gate_off, g_norm, *, heads, dv, tm=768, name):
    l, w = o_f.shape
    tm = math.gcd(l, tm)
    assert gate_off % w == 0
    row = pl.BlockSpec((tm, w), lambda i: (i, 0))
    g_norm = _tab(g_norm)
    return pl.pallas_call(
        functools.partial(_post_kernel, heads=heads, dv=dv),
        out_shape=jax.ShapeDtypeStruct((l, w), BF16),
        grid=(l // tm,),
        in_specs=[row, row, pl.BlockSpec((tm, w), lambda i: (i, gate_off // w)), g_norm.spec],
        out_specs=row,
        compiler_params=_cparams(("parallel",)),
        name=name,
    )(o_f, o_b, p, g_norm.arr)


def _seg_sum(x, seg_bf16):
    hi, lo = _split2(x)
    return jnp.dot(hi, seg_bf16, preferred_element_type=F32) + jnp.dot(lo, seg_bf16, preferred_element_type=F32)


def _shift_mix_block(x, halo_prev, halo_next, mu, seg_start, seg_end):
    row = lax.broadcasted_iota(jnp.int32, x.shape, 0)
    first = jnp.where(seg_start, 0.0, halo_prev[7:8, :])
    last = jnp.where(seg_end, 0.0, halo_next[0:1, :])
    xp = jnp.where(row == 0, first, pltpu.roll(x, 1, 0))
    xn = jnp.where(row == CHUNK - 1, last, pltpu.roll(x, CHUNK - 1, 0))
    return x + mu[0:1] * (xp - x) + mu[1:2] * (xn - x)


def _rwkv_kernel(*refs, groups, ncc, nch):
    ins = [refs[0:6], refs[6:12]]
    (mu_rkv_ref, mu_lo_ref, vec_ref, wup_ref, aup_ref, gup_ref, seg_ref,
     tri_ref, strict_ref, incl_ref, off_ref, eye_ref, hm_ref,
     o0_ref, o1_ref, bo0_ref, bo1_ref, g_ref, st_ref) = refs[12:]
    c = pl.program_id(0)

    @pl.when(c == 0)
    def _():
        st_ref[...] = jnp.zeros(st_ref.shape, F32)

    hm = hm_ref[...]
    eye = eye_ref[...]
    seg = seg_ref[...]
    w = RWKV_PACK * RWKV_HEAD
    nb = RWKV_PACK * CHUNK
    gw = GROUP_W

    def stack(x):
        return jnp.concatenate([x.astype(BF16)] * RWKV_PACK, axis=0) * hm

    streams = []
    for d, (o_ref, bo_ref) in enumerate(((o0_ref, bo0_ref), (o1_ref, bo1_ref))):
        rkv_ref, rkv_p, rkv_n, lo_ref, lo_p, lo_n = ins[d]
        pos = _chunk_pos(d, c, ncc, nch)
        seg_start = (pos == 0) | (pos == ncc)
        seg_end = (pos == ncc - 1) | (pos == nch - 1)
        rkv = _shift_mix_block(rkv_ref[...], rkv_p[...], rkv_n[...], mu_rkv_ref[...], seg_start, seg_end)
        lora = _shift_mix_block(lo_ref[...], lo_p[...], lo_n[...], mu_lo_ref[...], seg_start, seg_end)
        r, k, v = rkv[:, 0:gw], rkv[:, gw:2 * gw], rkv[:, 2 * gw:3 * gw]
        w_dn, a_dn, g_dn = lora[:, 0:128], lora[:, 128:256], lora[:, 256:384]
        vec = vec_ref[d]
        wl = vec[0:1] + _dot(jnp.tanh(w_dn), wup_ref[d])
        lw = -jnp.exp(-(jnp.maximum(-wl, 0.0) + jnp.log1p(jnp.exp(-jnp.abs(wl)))) - 0.5)
        a = jax.nn.sigmoid(vec[1:2] + _dot(a_dn, aup_ref[d]))
        kk = k * vec[2:3]
        kk = kk * lax.rsqrt(jnp.maximum(_seg_sum(kk * kk, seg), 1e-24))
        k = k * (1.0 + (a - 1.0) * vec[3:4])
        b = kk * a
        bo_ref[...] = _seg_sum(r * k * vec[4:5], seg) * v
        if d == 0:
            g_ref[...] = _dot(jax.nn.sigmoid(g_dn), gup_ref[...])
        bc = _dot_exact_lhs(tri_ref[d], lw)
        ebt = jnp.exp(jnp.sum(lw, axis=0, keepdims=True))
        einv = jnp.exp(-bc)
        khat = kk * jnp.exp(bc - lw)
        rhat = r * jnp.exp(bc)
        ks = k * einv
        bs = b * einv
        for g in range(groups):
            sl = slice(g * w, (g + 1) * w)
            streams.append(dict(
                d=d, g=g, sl=sl, o_ref=o_ref, ebt=ebt[:, sl],
                kr=jnp.concatenate([stack(khat[:, sl]), stack(rhat[:, sl])], axis=0),
                bk=jnp.concatenate([stack(bs[:, sl]), stack(ks[:, sl])], axis=0),
                vs=stack(v[:, sl])))

    for s in streams:
        aa = _dot_nt(s["kr"], s["bk"])
        strict, incl = strict_ref[s["d"]], incl_ref[s["d"]]
        s["akb"] = strict.astype(F32) * aa[:nb, :nb]
        s["akb_b"] = s["akb"].astype(BF16)
        s["arb"] = incl * aa[nb:, :nb].astype(BF16)
        s["ak_v"] = jnp.concatenate([strict * aa[:nb, nb:].astype(BF16), incl * aa[nb:, nb:].astype(BF16)], axis=0)
        s["minv"] = eye - off_ref[s["d"], N_LEVELS - 1].astype(F32) * s["akb"]
    for lv in range(N_LEVELS - 2, -1, -1):
        for s in streams:
            s["minv_b"] = s["minv"].astype(BF16)
            s["t1"] = _dot(off_ref[s["d"], lv] * s["akb_b"], s["minv_b"])
        for s in streams:
            s["minv"] = s["minv"] - _dot(s["minv_b"], s["t1"])
    for s in streams:
        st = st_ref[s["d"], s["g"]]
        from_state = _dot_nt(s["kr"], st)
        from_v = _dot(s["ak_v"], s["vs"])
        u = _dot(s["minv"], from_state[:nb] + from_v[:nb])
        o = from_state[nb:] + from_v[nb:] - _dot(s["arb"], u)
        s["o_ref"][:, s["sl"]] = sum(o[h * CHUNK:(h + 1) * CHUNK] for h in range(RWKV_PACK))
        upd = _dot(s["vs"].T, s["bk"][nb:]) - _dot(u.T, s["bk"][:nb])
        st_ref[s["d"], s["g"]] = (st + upd) * s["ebt"]


def rwkv_scan(p, rkv_off, lora_off, params, *, n_ctx, name="rwkv_scan"):
    l = p.shape[0]
    hw = GROUP_W
    w = RWKV_PACK * RWKV_HEAD
    groups = hw // w
    nch, ncc = l // CHUNK, n_ctx // CHUNK
    nb = RWKV_PACK * CHUNK
    halo = 8
    per_chunk = CHUNK // halo

    def shared(d):
        return pl.BlockSpec((CHUNK, hw), lambda c: (_chunk_pos(d, c, ncc, nch), 0))

    def piece(d, off, width):
        assert off % width == 0
        cb = off // width
        pos = lambda c: _chunk_pos(d, c, ncc, nch)
        return [pl.BlockSpec((CHUNK, width), lambda c: (pos(c), cb)),
                pl.BlockSpec((halo, width), lambda c: (jnp.maximum(pos(c) * per_chunk - 1, 0), cb)),
                pl.BlockSpec((halo, width), lambda c: (jnp.minimum((pos(c) + 1) * per_chunk, l // halo - 1), cb))]

    def const(a):
        return pl.BlockSpec(a.shape, lambda c: (0,) * a.ndim)

    bd = lambda a: np.kron(np.eye(RWKV_PACK, dtype=np.float32), a)
    strict = jnp.asarray(np.stack([bd(_CC["strict"][d]) for d in range(2)]), BF16)
    incl = jnp.asarray(np.stack([bd(_CC["strict"][d] + _CC["eye"][d]) for d in range(2)]), BF16)
    off = jnp.asarray(np.stack([np.stack([bd(_CC["off"][d, lv]) for lv in range(N_LEVELS)]) for d in range(2)]), BF16)
    eye = jnp.asarray(np.eye(nb, dtype=np.float32))
    hm = jnp.asarray(np.kron(np.eye(RWKV_PACK, dtype=np.float32), np.ones((CHUNK, RWKV_HEAD), np.float32)), BF16)
    tri = jnp.asarray(_CC["tri"], BF16)
    seg = jnp.asarray(np.kron(np.eye(RWKV_HEADS, dtype=np.float32), np.ones((RWKV_HEAD, RWKV_HEAD), np.float32)), BF16)
    consts = [_tab(a) for a in list(params) + [seg, tri, strict, incl, off, eye, hm]]
    in_specs = []
    for d in range(2):
        in_specs += piece(d, rkv_off, 3 * hw) + piece(d, lora_off, 3 * 128)
    return pl.pallas_call(
        functools.partial(_rwkv_kernel, groups=groups, ncc=ncc, nch=nch),
        out_shape=[jax.ShapeDtypeStruct((l, hw), F32)] * 5,
        grid=(nch,),
        in_specs=in_specs + [t.spec for t in consts],
        out_specs=[shared(0), shared(1), shared(0), shared(1), shared(0)],
        scratch_shapes=[pltpu.VMEM((2, groups, nb, nb), F32)],
        compiler_params=_cparams(("arbitrary",)),
        name=name,
    )(*([p] * len(in_specs)), *[t.arr for t in consts])


def _rwkv_post_kernel(of_ref, ob_ref, bf_ref, bb_ref, g_ref, ln_ref, seg_ref, o_ref):
    seg = seg_ref[...]
    o = of_ref[...] + ob_ref[...]
    mean = _seg_sum(o, seg) * (1.0 / RWKV_HEAD)
    oc = o - mean
    var = _seg_sum(oc * oc, seg) * (1.0 / RWKV_HEAD)
    y = oc * lax.rsqrt(var + RWKV_LN_EPS) * ln_ref[0:1] + ln_ref[1:2] + bf_ref[...] + bb_ref[...]
    o_ref[...] = (y * g_ref[...]).astype(o_ref.dtype)


def rwkv_post(o_f, o_b, bo_f, bo_b, g, ln, *, tm=768, name="rwkv_post"):
    l, w = o_f.shape
    tm = math.gcd(l, tm)
    row = pl.BlockSpec((tm, w), lambda i: (i, 0))
    ln = _tab(ln)
    seg = jnp.asarray(np.kron(np.eye(RWKV_HEADS, dtype=np.float32), np.ones((RWKV_HEAD, RWKV_HEAD), np.float32)), BF16)
    return pl.pallas_call(
        _rwkv_post_kernel,
        out_shape=jax.ShapeDtypeStruct((l, w), BF16),
        grid=(l // tm,),
        in_specs=[row] * 5 + [ln.spec, pl.BlockSpec((w, w), lambda i: (0, 0))],
        out_specs=row,
        compiler_params=_cparams(("parallel",)),
        name=name,
    )(o_f, o_b, bo_f, bo_b, g, ln.arr, seg)


@functools.lru_cache(maxsize=None)
def _rope_lane_tables(m, n):
    rows = n // GRID_W
    row = np.repeat(np.arange(rows, dtype=np.float32), GRID_W)
    col = np.tile(np.arange(GRID_W, dtype=np.float32), rows)
    n_freq = MLA_ROPE // 4
    freqs = (np.float32(ROPE_BASE) ** (-np.arange(n_freq, dtype=np.float32) / np.float32(n_freq))).astype(np.float32)
    ang = np.stack([row[:, None] * freqs, col[:, None] * freqs], axis=1).astype(np.float32)
    cos64 = np.repeat(np.cos(ang), 2, axis=1).reshape(n, MLA_ROPE)
    sin64 = (np.repeat(np.sin(ang), 2, axis=1) * np.array([-1.0, 1.0, -1.0, 1.0])[None, :, None]).reshape(n, MLA_ROPE)
    cos_t = np.tile(np.concatenate([np.ones((m, MLA_ROPE)), cos64], axis=0), (1, MLA_HEADS)).astype(np.float32)
    sin_t = np.tile(np.concatenate([np.zeros((m, MLA_ROPE)), sin64], axis=0), (1, MLA_HEADS)).astype(np.float32)
    return cos_t, sin_t


def _rope_lanes(x, cos_t, sin_t):
    n = x.shape[1]
    lane = lax.broadcasted_iota(jnp.int32, x.shape, 1)
    swapped = jnp.where(lane % 32 < 16, pltpu.roll(x, n - 16, 1), pltpu.roll(x, 16, 1))
    return x * cos_t + swapped * sin_t


def _mla_prep_kernel(cq_ref, ckv_ref, kr_ref, cos_ref, sin_ref, qn_ref, kvn_ref, wq_ref, wkv_ref,
                     q_ref, k_ref, v_ref):
    cq = cq_ref[...]
    ms = jnp.sum(cq * cq, axis=-1, keepdims=True) * (1.0 / MLA_Q_LORA)
    q = _dot(cq * lax.rsqrt(ms + EPS) * qn_ref[...], wq_ref[...])
    ckv = ckv_ref[...]
    kvn = ckv * lax.rsqrt(jnp.mean(ckv * ckv, axis=-1, keepdims=True) + EPS) * kvn_ref[...]
    kv = _dot(kvn, wkv_ref[...])
    cos_t, sin_t = cos_ref[...], sin_ref[...]
    hn = MLA_HEADS * MLA_NOPE
    scale = (MLA_NOPE + MLA_ROPE) ** -0.5 * math.log2(math.e)
    q_rope = _rope_lanes(q[:, hn:], cos_t, sin_t)
    k_rope = _rope_lanes(kr_ref[...], cos_t[:, :128], sin_t[:, :128])
    k_rope_hi = pltpu.roll(k_rope, MLA_ROPE, 1)
    lane = lax.broadcasted_iota(jnp.int32, k_rope.shape, 1)
    ones_col = jnp.where(lane == 0, 1.0, 0.0)
    for h in range(MLA_HEADS):
        pair = q_rope[:, (h // 2) * 128:(h // 2 + 1) * 128]
        q_ref[h] = (jnp.concatenate([q[:, h * MLA_NOPE:(h + 1) * MLA_NOPE], pair], axis=1) * scale).astype(q_ref.dtype)
        k_ref[h] = jnp.concatenate([kv[:, h * MLA_NOPE:(h + 1) * MLA_NOPE], k_rope if h % 2 == 0 else k_rope_hi],
                                   axis=1).astype(k_ref.dtype)
        v_ref[h] = jnp.concatenate([kv[:, hn + h * MLA_V:hn + (h + 1) * MLA_V], ones_col], axis=1).astype(v_ref.dtype)


def mla_tables(q_norm, w_q_up, kv_norm, w_kv_up):
    nl = w_q_up.shape[0]
    dq = MLA_NOPE + MLA_ROPE
    pad = PACKED["cq"][3] - MLA_Q_LORA
    wq = w_q_up.reshape(nl, MLA_Q_LORA, MLA_HEADS, dq)
    wq = jnp.concatenate([wq[..., :MLA_NOPE].reshape(nl, MLA_Q_LORA, -1), wq[..., MLA_NOPE:].reshape(nl, MLA_Q_LORA, -1)],
                         axis=2)
    wq = jnp.pad(wq, ((0, 0), (0, pad), (0, 0)))
    qn = jnp.pad(q_norm, ((0, 0), (0, pad)))[:, None, :]
    wkv = w_kv_up.reshape(nl, MLA_KV_LORA, MLA_HEADS, MLA_NOPE + MLA_V)
    wkv = jnp.concatenate([wkv[..., :MLA_NOPE].reshape(nl, MLA_KV_LORA, -1), wkv[..., MLA_NOPE:].reshape(nl, MLA_KV_LORA, -1)],
                          axis=2)
    return qn, kv_norm[:, None, :], wq, wkv


def mla_prep(p, m, tables, *, tm=256, name="mla_prep"):
    l = p.shape[0]
    cos_t, sin_t = (jnp.asarray(t) for t in _rope_lane_tables(m, l - m))
    tabs = [_tab(t) for t in tables]

    def col(nm):
        _, _, off, wp = PACKED[nm]
        return pl.BlockSpec((tm, wp), lambda i: (i, off // wp))

    row = lambda wd: pl.BlockSpec((tm, wd), lambda i: (i, 0))
    out_spec = pl.BlockSpec((MLA_HEADS, tm, 256), lambda i: (0, i, 0))
    return pl.pallas_call(
        _mla_prep_kernel,
        out_shape=[jax.ShapeDtypeStruct((MLA_HEADS, l, 256), BF16)] * 3,
        grid=(l // tm,),
        in_specs=[col("cq"), col("ckv"), col("k_rope"), row(256), row(256)] + [t.spec for t in tabs],
        out_specs=[out_spec] * 3,
        compiler_params=_cparams(("parallel",)),
        name=name,
    )(p, p, p, cos_t, sin_t, *[t.arr for t in tabs])


def mla_mixer(p, m, tables):
    qh, kh, vh = mla_prep(p, m, tables)
    o_all = flash_attention(qh, kh, vh, dv=MLA_V, tq=768, name="mla_attn_lat")
    return flash_attention(qh, kh, vh, dv=MLA_V, tq=m, lq=m, lk=m, into=o_all, name="mla_attn_ctx")


def _pad_rows(t, before, total):
    cfg = [(0, 0)] * t.ndim
    cfg[-2] = (before, total - before - t.shape[-2])
    return jnp.pad(t, cfg)


def rwkv_tables(mu, w0, w_up, a0, a_up, k_k, k_a, u, g_up, ln_g, ln_b):
    zeros = jnp.zeros_like(w0)
    vec = jnp.stack([w0, a0, k_k, k_a, u, zeros, zeros, zeros], axis=2)
    wup = jnp.stack([_pad_rows(w_up[:, d], d * RWKV_W_LORA, 128) for d in range(2)], axis=1)
    aup = jnp.stack([_pad_rows(a_up[:, d], d * RWKV_A_LORA, 128) for d in range(2)], axis=1)
    ln = jnp.stack([ln_g, ln_b] + [jnp.zeros_like(ln_g)] * 6, axis=1)
    return (mu[:, :, :3 * GROUP_W], mu[:, :, 3 * GROUP_W:], vec, wup, aup, g_up), ln


def rwkv7_mixer(p, m, params, ln):
    outs = rwkv_scan(p, PACKED["rwkv_r"][2], PACKED["w_dn"][2], params, n_ctx=m)
    return rwkv_post(*outs, ln)


PACKED = {}


def _build_packed():
    orig = dict(cq=(0, MLA_Q_LORA), ckv=(MLA_Q_LORA, MLA_KV_LORA), k_rope=(MLA_Q_LORA + MLA_KV_LORA, MLA_ROPE))
    b = MLA_COLS
    for i, nm in enumerate(("rwkv_r", "rwkv_k", "rwkv_v")):
        orig[nm] = (b + i * GROUP_W, GROUP_W)
    b += 3 * GROUP_W
    orig.update(w_dn=(b, 2 * RWKV_W_LORA), a_dn=(b + 2 * RWKV_W_LORA, 2 * RWKV_A_LORA),
                g_dn=(b + 2 * RWKV_W_LORA + 2 * RWKV_A_LORA, RWKV_G_LORA))
    b = IN_SPLITS[1]
    orig.update(gla_q=(b, GLA_KD), gla_k=(b + GLA_KD, GLA_KD), gla_v=(b + 2 * GLA_KD, GROUP_W),
                gla_a=(b + GLA_SPLITS[2], 2 * GLA_GATE_RANK), gla_r=(b + GLA_SPLITS[3], GROUP_W))
    b = IN_SPLITS[2]
    for i, nm in enumerate(("hgrn_q", "hgrn_f0", "hgrn_f1", "hgrn_i", "hgrn_g")):
        orig[nm] = (b + i * GROUP_W, GROUP_W)
    order = [(nm, 512) for nm in ("rwkv_r", "rwkv_k", "rwkv_v", "gla_v", "gla_r", "hgrn_q", "hgrn_f0", "hgrn_f1",
                                  "hgrn_i", "hgrn_g", "cq")]
    orig["pad"] = (0, 0)
    order += [(nm, 256) for nm in ("gla_q", "gla_k")]
    order += [(nm, 128) for nm in ("w_dn", "a_dn", "g_dn", "gla_a", "k_rope", "pad")]
    order += [("ckv", 256)]
    off = 0
    for nm, wp in order:
        PACKED[nm] = (orig[nm][0], orig[nm][1], off, wp)
        off += wp
    return off


N_PACKED = _build_packed()


def pack_w_in(w):
    pieces = []
    for o, wd, _, wp in PACKED.values():
        pieces.append(w[..., o:o + wd])
        if wp > wd:
            pieces.append(jnp.zeros(w.shape[:-1] + (wp - wd,), w.dtype))
    return jnp.concatenate(pieces, axis=-1)


def _pcol(p, nm):
    _, wd, off, _ = PACKED[nm]
    return p[:, off:off + wd]


def _blk(nm):
    return (PACKED[nm][2], PACKED[nm][3])


def gla_tables(a_up, a_bias):
    return jnp.stack([jnp.concatenate([_pad_rows(a_up[:, d], d * GLA_GATE_RANK, 128),
                                       _pad_rows(a_bias[:, d][:, None, :], 0, 8)], axis=1) for d in range(2)], axis=1)


def hgrn_tables(lb_all):
    lb = jnp.swapaxes(lb_all, 0, 1)
    zeros = jnp.zeros_like(lb)
    return jnp.stack([jnp.log(lb), jnp.log1p(-lb), 1.0 - lb] + [zeros] * 5, axis=2)


def gla_mixer(p, m, par, g_norm):
    cols = [[_blk("gla_q"), _blk("gla_k"), _blk("gla_v"), _blk("gla_a")]] * 2
    o_f, o_b = gla_scan(p, cols, par, mode="gla", heads=GLA_HEADS, dk=GLA_DK, dv=GLA_DV, pack=4, n_ctx=m,
                        name="gla_scan")
    return mix_post(o_f, o_b, p, PACKED["gla_r"][2], g_norm, heads=GLA_HEADS, dv=GLA_DV, name="gla_post")


def hgrn2_mixer(p, m, par, g_norm):
    cols = [[_blk("hgrn_q"), _blk("hgrn_f%d" % d), _blk("hgrn_i")] for d in range(2)]
    o_f, o_b = gla_scan(p, cols, par, mode="hgrn", heads=HGRN_HEADS, dk=HGRN_EXPAND, dv=HGRN_DV, pack=2, n_ctx=m,
                        name="hgrn_scan")
    return mix_post(o_f, o_b, p, PACKED["hgrn_g"][2], g_norm, heads=HGRN_HEADS, dv=HGRN_DV, name="hgrn_post")


def dense_ffn(xs, h2, w1, w3, w2, j, gates2, m):
    act = pmatmul(h2, w1, (j,), w3=w3, tm=1408, tn=512, out_dtype=BF16, name="ffn_up")
    return pmatmul(act, w2, (j,), tm=768, tn=256, res=xs, gates=gates2, m_ctx=m, name="ffn_down")


def _route(top_idx, weights):
    n_pairs = top_idx.size
    r = n_pairs + N_EXPERTS * MOE_TILE
    n_tiles = r // MOE_TILE
    e_flat = top_idx.reshape(-1)
    onehot = (e_flat[:, None] == jnp.arange(N_EXPERTS)[None, :]).astype(jnp.int32)
    csum = jnp.cumsum(onehot, axis=0)
    counts = csum[-1]
    padded = (counts + MOE_TILE - 1) // MOE_TILE * MOE_TILE
    ends = jnp.cumsum(padded)
    dest = jnp.sum(onehot * (csum - 1 + (ends - padded)[None, :]), axis=1)
    put = dict(mode="promise_in_bounds", unique_indices=True)
    src_token = jnp.zeros((r,), jnp.int32).at[dest].set(jnp.arange(n_pairs, dtype=jnp.int32) // TOP_K, **put)
    row_gate = jnp.zeros((r,), F32).at[dest].set(weights.reshape(-1), **put)
    tile_start = jnp.arange(n_tiles, dtype=jnp.int32) * MOE_TILE
    tile_expert = jnp.minimum(jnp.sum(tile_start[:, None] >= ends[None, :], axis=1), N_EXPERTS - 1)
    meta = jnp.concatenate([tile_expert.astype(jnp.int32), (ends[-1:] // MOE_TILE).astype(jnp.int32)])
    return src_token, row_gate, dest.reshape(top_idx.shape), meta


def _combine_kernel(x_ref, y_ref, gate_ref, o_ref, *, m_ctx, tm, d):
    i = pl.program_id(0)
    rows = i * tm + lax.broadcasted_iota(jnp.int32, (tm, d), 0)
    g = jnp.where(rows < m_ctx, gate_ref[0:1, :], gate_ref[1:2, :])
    o_ref[...] = x_ref[...] + g * (y_ref[0].astype(F32) + y_ref[1].astype(F32))


def moe_combine(xs, y_tok, gates2, *, m_ctx, tm=256, name="moe_combine"):
    l, d = xs.shape
    gates2 = _tab(gates2)
    return pl.pallas_call(
        functools.partial(_combine_kernel, m_ctx=m_ctx, tm=tm, d=d),
        out_shape=jax.ShapeDtypeStruct((l, d), F32),
        grid=(l // tm,),
        in_specs=[pl.BlockSpec((tm, d), lambda i: (i, 0)), pl.BlockSpec((TOP_K, tm, d), lambda i: (0, i, 0)),
                  gates2.spec],
        out_specs=pl.BlockSpec((tm, d), lambda i: (i, 0)),
        compiler_params=_cparams(("parallel",)),
        name=name,
    )(xs, y_tok, gates2.arr)


def moe_ffn(xs, h2, logits, w1, w3, w2, j, gates2, m):
    l, d = h2.shape
    top_vals, top_idx = lax.top_k(logits[:, :N_EXPERTS], TOP_K)
    weights = jax.nn.softmax(top_vals, axis=-1)
    src_token, row_gate, dest, meta = _route(top_idx, weights)
    x_sorted = h2.at[src_token].get(mode="promise_in_bounds")
    act = gmatmul(meta, x_sorted, w1, j, w3=w3, tn=512, out_dtype=BF16, rowscale=row_gate[:, None], name="moe_up")
    y = gmatmul(meta, act, w2, j, tn=512, out_dtype=BF16, name="moe_down")
    y_tok = y.at[dest.T.reshape(-1)].get(mode="promise_in_bounds").reshape(TOP_K, l, d)
    return moe_combine(xs, y_tok, gates2, m_ctx=m)


def kernel(x, c, ctx, c_ctx, norm1_g, norm2_g, w_mod, b_mod, w_in, w_out, mla_q_norm, mla_w_q_up, mla_kv_norm, mla_w_kv_up, rwkv_mu, rwkv_w0, rwkv_w_up, rwkv_a0, rwkv_a_up, rwkv_k_k, rwkv_k_a, rwkv_u, rwkv_g_up, rwkv_ln_g, rwkv_ln_b, gla_a_up, gla_a_bias, gla_norm, hgrn_lb, hgrn_norm, ffn_w1, ffn_w3, ffn_w2, moe_router, moe_w1, moe_w3, moe_w2, final_norm_g):
    m, n, d = ctx.shape[1], x.shape[1], x.shape[2]
    lb_all = jnp.cumsum(jax.nn.softmax(hgrn_lb.astype(F32), axis=1), axis=1)
    lb_all = lb_all - lb_all[:, :1]
    xs = jnp.concatenate([ctx[0], x[0]], axis=0)

    cvec = jnp.zeros((16, d), F32).at[0].set(jax.nn.silu(c[0])).at[1].set(jax.nn.silu(c_ctx))
    mod = jnp.stack([pmatmul(cvec, w_mod, (l,), tm=16, tn=2048, name="mod")[:2] for l in range(DEPTH)]) + b_mod[:, None, :]
    mods = jnp.swapaxes(mod[:, ::-1].reshape(DEPTH, 2, 6, d), 1, 2)
    zmods = jnp.zeros((2, d), F32)
    g1, g2 = norm1_g[:, None, :], norm2_g[:, None, :]
    w_in_p = pack_w_in(w_in.astype(BF16))
    mla_tabs = mla_tables(mla_q_norm, mla_w_q_up, mla_kv_norm, mla_w_kv_up)
    rwkv_tabs, rwkv_ln = rwkv_tables(rwkv_mu, rwkv_w0, rwkv_w_up, rwkv_a0, rwkv_a_up, rwkv_k_k, rwkv_k_a, rwkv_u,
                                      rwkv_g_up, rwkv_ln_g, rwkv_ln_b)
    gla_par, hgrn_par = gla_tables(gla_a_up, gla_a_bias), hgrn_tables(lb_all)
    gla_g, hgrn_g = gla_norm[:, None, :], hgrn_norm[:, None, :]
    router_p = jnp.pad(moe_router, ((0, 0), (0, 0), (0, 128 - N_EXPERTS)))

    for l in range(DEPTH):
        sh1, sc1, gt1, sh2, sc2, gt2 = (_Tab(mods, (l, i)) for i in range(6))
        h = norm_mod(xs, _Tab(g1, (l,)), sh1, sc1, m_ctx=m, out_dtype=BF16)
        p = pmatmul(h, w_in_p, (l,), tm=1408, tn=1024, name="w_in")
        o = [
            mla_mixer(p, m, [_Tab(t, (l,)) for t in mla_tabs]),
            rwkv7_mixer(p, m, [_Tab(t, (l,)) for t in rwkv_tabs], _Tab(rwkv_ln, (l,))),
            gla_mixer(p, m, _Tab(gla_par, (l,)), _Tab(gla_g, (l,))),
            hgrn2_mixer(p, m, _Tab(hgrn_par, (l,)), _Tab(hgrn_g, (l,))),
        ]
        xs = pmatmul(o, w_out, (l,), tm=1408, tn=512, res=xs, gates=gt1, m_ctx=m, name="w_out")

        j = l // 2
        if l % 2 == 0:
            h2 = norm_mod(xs, _Tab(g2, (l,)), sh2, sc2, m_ctx=m, out_dtype=BF16)
            xs = dense_ffn(xs, h2, ffn_w1, ffn_w3, ffn_w2, j, gt2, m)
        else:
            h2, logits = norm_mod(xs, _Tab(g2, (l,)), sh2, sc2, m_ctx=m, out_dtype=BF16, router=_Tab(router_p, (j,)),
                                  name="norm_mod_router")
            xs = moe_ffn(xs, h2, logits, moe_w1, moe_w3, moe_w2, j, gt2, m)
    out = norm_mod(xs, final_norm_g[None, :], zmods, zmods, m_ctx=m, out_dtype=F32, name="final_norm")
    return out[m:][None]
```

```python
import functools
import math

import numpy as np
import jax
import jax.numpy as jnp
from jax import lax
from jax.experimental import pallas as pl
from jax.experimental.pallas import tpu as pltpu

F32 = jnp.float32
BF16 = jnp.bfloat16

DEPTH = 4
GRID_W = 64
EPS = 1e-6
GROUP_W = 512
MLA_HEADS, MLA_NOPE, MLA_ROPE, MLA_V = 4, 128, 64, 128
MLA_Q_LORA, MLA_KV_LORA = 384, 256
ROPE_BASE = 10000.0
RWKV_HEADS, RWKV_HEAD = 8, 64
RWKV_W_LORA, RWKV_A_LORA, RWKV_G_LORA = 64, 64, 128
RWKV_LN_EPS = 64e-5
GLA_HEADS, GLA_DK, GLA_DV = 4, 64, 128
GLA_GATE_RANK, GLA_GATE_NORM = 16, 16.0
HGRN_HEADS, HGRN_EXPAND, HGRN_DV = 4, 128, 128
CHUNK = 64
N_EXPERTS, TOP_K = 8, 2
MLA_COLS = MLA_Q_LORA + MLA_KV_LORA + MLA_ROPE
RWKV_COLS = 3 * GROUP_W + 2 * RWKV_W_LORA + 2 * RWKV_A_LORA + RWKV_G_LORA
GLA_KD = GLA_HEADS * GLA_DK
GLA_COLS = 2 * GLA_KD + GROUP_W + 2 * GLA_GATE_RANK + GROUP_W
IN_SPLITS = (MLA_COLS, MLA_COLS + RWKV_COLS, MLA_COLS + RWKV_COLS + GLA_COLS)
RWKV_SPLITS = (GROUP_W, 2 * GROUP_W, 3 * GROUP_W, 3 * GROUP_W + 2 * RWKV_W_LORA,
               3 * GROUP_W + 2 * RWKV_W_LORA + 2 * RWKV_A_LORA)
GLA_SPLITS = (GLA_KD, 2 * GLA_KD, 2 * GLA_KD + GROUP_W, 2 * GLA_KD + GROUP_W + 2 * GLA_GATE_RANK)

V7X_VMEM_LIMIT = 56 * 1024 * 1024
V7X_MXU = 256
N_LEVELS = 6
RWKV_PACK = V7X_MXU // RWKV_HEAD
MOE_TILE = 512


def _cparams(sem, vmem=V7X_VMEM_LIMIT):
    return pltpu.CompilerParams(dimension_semantics=sem, vmem_limit_bytes=vmem)


class _Tab:
    def __init__(self, arr, idx=()):
        self.arr, self.idx = arr, tuple(idx)

    @property
    def shape(self):
        return self.arr.shape[len(self.idx):]

    @property
    def spec(self):
        idx, rest = self.idx, self.shape
        return pl.BlockSpec((None,) * len(idx) + rest, lambda *_: idx + (0,) * len(rest))


def _tab(a):
    return a if isinstance(a, _Tab) else _Tab(a)


def _dot(a, b):
    return jnp.dot(a.astype(BF16), b.astype(BF16), preferred_element_type=F32)


def _dot_nt(a, b):
    return lax.dot_general(a.astype(BF16), b.astype(BF16), (((1,), (1,)), ((), ())), preferred_element_type=F32)


def _split2(x):
    hi = x.astype(BF16)
    lo = (x - hi.astype(F32)).astype(BF16)
    return hi, lo


def _dot_exact_lhs(m_bf16, x):
    hi, lo = _split2(x)
    return (jnp.dot(m_bf16, hi, preferred_element_type=F32) + jnp.dot(m_bf16, lo, preferred_element_type=F32))


def _mm_kernel(*refs, n_x, n_w, has_res, m_ctx, tm, cast_w):
    it = iter(refs)
    x_refs = [next(it) for _ in range(n_x)]
    w_refs = [next(it) for _ in range(n_w)]
    res_ref = next(it) if has_res else None
    gate_ref = next(it) if has_res else None
    o_ref = next(it)
    wb_refs = [next(it) for _ in range(n_w)] if cast_w else w_refs
    i = pl.program_id(1)

    if cast_w:
        @pl.when(i == 0)
        def _():
            for w_ref, wb_ref in zip(w_refs, wb_refs):
                wb_ref[...] = w_ref[...].astype(BF16)
    x = jnp.concatenate([x_ref[...].astype(BF16) for x_ref in x_refs], axis=1)

    acc = jnp.dot(x, wb_refs[0][...], preferred_element_type=F32)
    if n_w == 2:
        acc3 = jnp.dot(x, wb_refs[1][...], preferred_element_type=F32)
        acc = acc * jax.nn.sigmoid(acc) * acc3
    if has_res:
        rows = i * tm + lax.broadcasted_iota(jnp.int32, acc.shape, 0)
        g = jnp.where(rows < m_ctx, gate_ref[0:1, :], gate_ref[1:2, :])
        acc = res_ref[...] + g * acc
    o_ref[...] = acc.astype(o_ref.dtype)


def pmatmul(x, w, widx=(), *, w3=None, tm, tn, out_dtype=F32, res=None, gates=None, m_ctx=0, name="mm"):
    xs = list(x) if isinstance(x, (list, tuple)) else [x]
    m = xs[0].shape[0]
    k = sum(xi.shape[1] for xi in xs)
    n = w.shape[-1]
    tm = math.gcd(m, tm)
    assert w.shape[-2] == k and tm % 16 == 0
    nj, ni = pl.cdiv(n, tn), m // tm
    lead = (None,) * len(widx)
    w_spec = pl.BlockSpec(lead + (k, tn), lambda j, i: tuple(widx) + (0, j))
    ws = [w] if w3 is None else [w, w3]
    in_specs = [pl.BlockSpec((tm, xi.shape[1]), lambda j, i: (i, 0)) for xi in xs] + [w_spec] * len(ws)
    args = xs + ws
    if res is not None:
        gates = _tab(gates)
        gi = gates.idx
        in_specs += [pl.BlockSpec((tm, tn), lambda j, i: (i, j)),
                     pl.BlockSpec((None,) * len(gi) + (2, tn), lambda j, i: gi + (0, j))]
        args += [res, gates.arr]
    cast_w = w.dtype != BF16
    kern = functools.partial(_mm_kernel, n_x=len(xs), n_w=len(ws), has_res=res is not None, m_ctx=m_ctx, tm=tm,
                             cast_w=cast_w)
    return pl.pallas_call(
        kern,
        out_shape=jax.ShapeDtypeStruct((m, n), out_dtype),
        grid=(nj, ni),
        in_specs=in_specs,
        out_specs=pl.BlockSpec((tm, tn), lambda j, i: (i, j)),
        scratch_shapes=[pltpu.VMEM((k, tn), BF16) for _ in ws] if cast_w else [],
        compiler_params=_cparams(("arbitrary", "arbitrary")),
        name=name,
    )(*args)


def _gmm_kernel(meta_ref, x_ref, *refs, n_w, has_rowscale, n_tiles):
    it = iter(refs)
    w_refs = [next(it) for _ in range(n_w)]
    rs_ref = next(it) if has_rowscale else None
    o_ref = next(it)
    wb_refs = [next(it) for _ in range(n_w)]
    t = pl.program_id(1)
    e = meta_ref[t]
    e_prev = meta_ref[jnp.maximum(t - 1, 0)]

    @pl.when((t == 0) | (e != e_prev))
    def _():
        for w_ref, wb_ref in zip(w_refs, wb_refs):
            wb_ref[...] = w_ref[...].astype(BF16)

    @pl.when(t < meta_ref[n_tiles])
    def _():
        x = x_ref[...].astype(BF16)
        acc = jnp.dot(x, wb_refs[0][...], preferred_element_type=F32)
        if n_w == 2:
            acc3 = jnp.dot(x, wb_refs[1][...], preferred_element_type=F32)
            acc = acc * jax.nn.sigmoid(acc) * acc3
        if has_rowscale:
            acc = acc * rs_ref[...]
        o_ref[...] = acc.astype(o_ref.dtype)

    @pl.when(t >= meta_ref[n_tiles])
    def _():
        o_ref[...] = jnp.zeros(o_ref.shape, o_ref.dtype)


def gmatmul(meta, x, w, jl, *, w3=None, tn, out_dtype, rowscale=None, name):
    r, k = x.shape
    n = w.shape[-1]
    n_tiles = r // MOE_TILE
    ws = [w] if w3 is None else [w, w3]
    w_spec = pl.BlockSpec((None, None, k, tn), lambda j, t, mr: (jl, mr[t], 0, j))
    in_specs = [pl.BlockSpec((MOE_TILE, k), lambda j, t, mr: (t, 0))] + [w_spec] * len(ws)
    args = [x] + ws
    if rowscale is not None:
        in_specs.append(pl.BlockSpec((MOE_TILE, 1), lambda j, t, mr: (t, 0)))
        args.append(rowscale)
    return pl.pallas_call(
        functools.partial(_gmm_kernel, n_w=len(ws), has_rowscale=rowscale is not None, n_tiles=n_tiles),
        out_shape=jax.ShapeDtypeStruct((r, n), out_dtype),
        grid_spec=pltpu.PrefetchScalarGridSpec(
            num_scalar_prefetch=1,
            grid=(n // tn, n_tiles),
            in_specs=in_specs,
            out_specs=pl.BlockSpec((MOE_TILE, tn), lambda j, t, mr: (t, j)),
            scratch_shapes=[pltpu.VMEM((k, tn), BF16) for _ in ws]),
        compiler_params=_cparams(("arbitrary", "arbitrary")),
        name=name,
    )(meta, *args)


def _norm_kernel(x_ref, g_ref, sh_ref, sc_ref, *rest, m_ctx, tm, with_router, skip_blocks):
    i = pl.program_id(0) + skip_blocks
    x = x_ref[...]
    y = x * lax.rsqrt(jnp.mean(x * x, axis=-1, keepdims=True) + EPS) * g_ref[...]
    rows = i * tm + lax.broadcasted_iota(jnp.int32, x.shape, 0)
    is_ctx = rows < m_ctx
    sc = jnp.where(is_ctx, sc_ref[0:1, :], sc_ref[1:2, :])
    sh = jnp.where(is_ctx, sh_ref[0:1, :], sh_ref[1:2, :])
    h = y * (1.0 + sc) + sh
    if with_router:
        r_ref, o_ref, logit_ref = rest
        logit_ref[...] = jnp.dot(h, r_ref[...], preferred_element_type=F32, precision=lax.Precision.HIGHEST)
    else:
        (o_ref,) = rest
    o_ref[...] = h.astype(o_ref.dtype)


def norm_mod(x, g, shift2, scale2, *, m_ctx, out_dtype, router=None, skip_rows=0, tm=768, name="norm_mod"):
    m_in, d = x.shape
    m = m_in - skip_rows
    tm = math.gcd(math.gcd(m, tm), skip_rows) if skip_rows else math.gcd(m, tm)
    skip_blocks = skip_rows // tm
    row = pl.BlockSpec((tm, d), lambda i: (i, 0))
    tabs = [_tab(g), _tab(shift2), _tab(scale2)]
    in_specs = [pl.BlockSpec((tm, d), lambda i: (i + skip_blocks, 0))] + [t.spec for t in tabs]
    args = [x] + [t.arr for t in tabs]
    out_shape = jax.ShapeDtypeStruct((m, d), out_dtype)
    out_specs = row
    if router is not None:
        router = _tab(router)
        in_specs.append(router.spec)
        args.append(router.arr)
        out_shape = [out_shape, jax.ShapeDtypeStruct((m, router.shape[1]), F32)]
        out_specs = [row, pl.BlockSpec((tm, router.shape[1]), lambda i: (i, 0))]
    return pl.pallas_call(
        functools.partial(_norm_kernel, m_ctx=m_ctx, tm=tm, with_router=router is not None, skip_blocks=skip_blocks),
        out_shape=out_shape,
        grid=(m // tm,),
        in_specs=in_specs,
        out_specs=out_specs,
        compiler_params=_cparams(("parallel",)),
        name=name,
    )(*args)


def _attn_kernel(q_ref, k_ref, v_ref, o_ref, *, tk, n_kv, dv):
    q = q_ref[...]
    m = acc = None
    for j in range(n_kv):
        kb = k_ref[j * tk:(j + 1) * tk, :]
        vb = v_ref[j * tk:(j + 1) * tk, :]
        s = lax.dot_general(q, kb, (((1,), (1,)), ((), ())), preferred_element_type=F32)
        m_blk = jnp.max(s, axis=-1, keepdims=True)
        m_new = m_blk if j == 0 else jnp.maximum(m, m_blk)
        pv = jnp.dot(jnp.exp2(s - m_new).astype(BF16), vb, preferred_element_type=F32)
        acc = pv if j == 0 else jnp.exp2(m - m_new) * acc + pv
        m = m_new
    o_ref[...] = (acc[:, :dv] / acc[:, dv:dv + 1]).astype(o_ref.dtype)


def _attn_kv_tile(lk, cap=1408):
    return max(t for t in range(128, cap + 1, 128) if lk % t == 0)


def _attn_into_kernel(q_ref, k_ref, v_ref, prev_ref, o_ref, **kw):
    del prev_ref
    _attn_kernel(q_ref, k_ref, v_ref, o_ref, **kw)


def flash_attention(q, k, v_ext, *, dv, tq, lq=None, lk=None, into=None, name="mla_attn"):
    h, dqk = q.shape[0], q.shape[2]
    lq = q.shape[1] if lq is None else lq
    lk = k.shape[1] if lk is None else lk
    dve = v_ext.shape[2]
    tk = _attn_kv_tile(lk)
    tq = math.gcd(lq, tq)
    kw = dict(tk=tk, n_kv=lk // tk, dv=dv)
    in_specs = [pl.BlockSpec((None, tq, dqk), lambda hh, i: (hh, i, 0)),
                pl.BlockSpec((None, lk, dqk), lambda hh, i: (hh, 0, 0)),
                pl.BlockSpec((None, lk, dve), lambda hh, i: (hh, 0, 0))]
    args = [q, k, v_ext]
    if into is not None:
        in_specs.append(pl.BlockSpec(memory_space=pl.ANY))
        args.append(into)
    return pl.pallas_call(
        functools.partial(_attn_kernel if into is None else _attn_into_kernel, **kw),
        out_shape=jax.ShapeDtypeStruct((lq, h * dv) if into is None else into.shape, BF16),
        grid=(h, lq // tq),
        in_specs=in_specs,
        out_specs=pl.BlockSpec((tq, dv), lambda hh, i: (i, hh)),
        input_output_aliases={} if into is None else {3: 0},
        compiler_params=_cparams(("parallel", "parallel")),
        name=name,
    )(*args)


def _chunk_constants():
    c = CHUNK
    t = np.arange(c)
    tri = (t[None, :] <= t[:, None]).astype(np.float32)
    strict = (t[None, :] < t[:, None]).astype(np.float32)
    eye = np.eye(c, dtype=np.float32)
    seg, off = [], []
    for lv in range(N_LEVELS):
        s = c >> (lv + 1)
        blk = t // s
        same = blk[:, None] == blk[None, :]
        odd = (blk % 2 == 1)[:, None]
        seg.append(np.where(odd, same & (t[None, :] <= t[:, None]), same & (t[None, :] > t[:, None])).astype(np.float32))
        off.append((odd & (blk[None, :] == blk[:, None] - 1)).astype(np.float32))
    seg, off = np.stack(seg), np.stack(off)

    def both(a):
        return np.stack([a, a[..., ::-1, ::-1]])

    return {k: both(v) for k, v in dict(tri=tri, strict=strict, eye=eye, seg=seg, off=off).items()}


_CC = _chunk_constants()


def _chunk_pos(d, c, n_ctx_chunks, n_chunks):
    back = jnp.where(c < n_ctx_chunks, n_ctx_chunks - 1 - c, n_chunks + n_ctx_chunks - 1 - c)
    return jnp.where(d == 0, c, back)


def _log_sigmoid(x):
    return jnp.minimum(x, 0.0) - jnp.log1p(jnp.exp(-jnp.abs(x)))


def _gla_kernel(*refs, mode, dk, dv, pack, groups, n_in):
    ins = [refs[:n_in], refs[n_in:2 * n_in]]
    par_ref, mall_ref, off_ref, eye_ref, hmk_ref, hmv_ref, o0_ref, o1_ref, st_ref = refs[2 * n_in:]
    c = pl.program_id(0)

    @pl.when(c == 0)
    def _():
        st_ref[...] = jnp.zeros(st_ref.shape, F32)

    eye = eye_ref[...]
    hmk = hmk_ref[...]
    hmv = hmv_ref[...]
    wk, wv = pack * dk, pack * dv
    for d, o_ref in enumerate((o0_ref, o1_ref)):
        if mode == "hgrn":
            q_raw, f_raw, v = (r[...] for r in ins[d])
            par = par_ref[d]
            q = q_raw * jax.nn.sigmoid(q_raw)
            g = jnp.logaddexp(par[0:1], par[1:2] + _log_sigmoid(f_raw))
            k = par[2:3] * jax.nn.sigmoid(-f_raw)
        else:
            q_raw, k, v, a_dn = (r[...] for r in ins[d])
            par = par_ref[d]
            q = q_raw * dk ** -0.5
            g = _log_sigmoid(_dot(a_dn, par[0:128]) + par[128:129]) / GLA_GATE_NORM
        e_all = _dot_exact_lhs(mall_ref[d], g)
        bc = e_all[0:CHUNK]
        btot = jnp.sum(g, axis=0, keepdims=True)
        qhat = q * jnp.exp(bc)
        kt = k * jnp.exp(btot - bc)
        ebt = jnp.exp(btot)
        qw, kw = [q], [k]
        for lv in range(N_LEVELS):
            w = jnp.exp(e_all[(lv + 1) * CHUNK:(lv + 2) * CHUNK])
            qw.append(q * w)
            kw.append(k * w)
        for gi in range(groups):
            slk = slice(gi * wk, (gi + 1) * wk)
            slv = slice(gi * wv, (gi + 1) * wv)

            def stk(x):
                return jnp.concatenate([x[:, slk].astype(BF16)] * pack, axis=0) * hmk

            def tile(x):
                return jnp.concatenate([x[:, slk].astype(BF16)] * pack, axis=0)

            att = eye * _dot_nt(stk(qw[0]), tile(kw[0]))
            for lv in range(N_LEVELS):
                att = att + off_ref[d, lv] * _dot_nt(stk(qw[lv + 1]), tile(kw[lv + 1]))
            vs = jnp.concatenate([v[:, slv]] * pack, axis=0) * hmv
            st = st_ref[d, gi]
            o = _dot_nt(stk(qhat), st) + _dot(att, vs)
            o_ref[:, slv] = sum(o[h * CHUNK:(h + 1) * CHUNK] for h in range(pack))
            st_ref[d, gi] = st * ebt[:, slk] + _dot(vs.T, stk(kt))


def gla_scan(p, cols, par, *, mode, heads, dk, dv, pack, n_ctx, name):
    l = p.shape[0]
    groups = heads // pack
    nch, ncc = l // CHUNK, n_ctx // CHUNK
    nb = pack * CHUNK

    def col_spec(d, off, width):
        assert off % width == 0
        return pl.BlockSpec((CHUNK, width), lambda c: (_chunk_pos(d, c, ncc, nch), off // width))

    def const(a):
        return pl.BlockSpec(a.shape, lambda c: (0,) * a.ndim)

    bd = lambda a: np.kron(np.eye(pack, dtype=np.float32), a)
    mall = jnp.asarray(np.concatenate([_CC["tri"][:, None], _CC["seg"]], axis=1).reshape(2, -1, CHUNK), BF16)
    off = jnp.asarray(np.stack([np.stack([bd(_CC["off"][d, lv]) for lv in range(N_LEVELS)]) for d in range(2)]), F32)
    eye = jnp.asarray(np.eye(nb, dtype=np.float32))
    hmk = jnp.asarray(np.kron(np.eye(pack, dtype=np.float32), np.ones((CHUNK, dk), np.float32)), BF16)
    hmv = jnp.asarray(np.kron(np.eye(pack, dtype=np.float32), np.ones((CHUNK, dv), np.float32)))
    consts = [_tab(a) for a in (par, mall, off, eye, hmk, hmv)]
    in_specs = [col_spec(d, o, w) for d in range(2) for (o, w) in cols[d]]
    out_spec = lambda d: pl.BlockSpec((CHUNK, heads * dv), lambda c: (_chunk_pos(d, c, ncc, nch), 0))
    return pl.pallas_call(
        functools.partial(_gla_kernel, mode=mode, dk=dk, dv=dv, pack=pack, groups=groups, n_in=len(cols[0])),
        out_shape=[jax.ShapeDtypeStruct((l, heads * dv), F32)] * 2,
        grid=(nch,),
        in_specs=in_specs + [t.spec for t in consts],
        out_specs=[out_spec(0), out_spec(1)],
        scratch_shapes=[pltpu.VMEM((2, groups, pack * dv, pack * dk), F32)],
        compiler_params=_cparams(("arbitrary",)),
        name=name,
    )(*([p] * len(in_specs)), *[t.arr for t in consts])


def _post_kernel(of_ref, ob_ref, gate_ref, gn_ref, o_ref, *, heads, dv):
    o = of_ref[...] + ob_ref[...]
    gate = gate_ref[...]
    gn = gn_ref[...]
    for h in range(heads):
        sl = slice(h * dv, (h + 1) * dv)
        oh = o[:, sl]
        y = oh * lax.rsqrt(jnp.mean(oh * oh, axis=-1, keepdims=True) + EPS) * gn
        gh = gate[:, sl]
        o_ref[:, sl] = (y * gh * jax.nn.sigmoid(gh)).astype(o_ref.dtype)


def mix_post(o_f, o_b, p, gate_off, g_norm, *, heads, dv, tm=768, name):
    l, w = o_f.shape
    tm = math.gcd(l, tm)
    assert gate_off % w == 0
    row = pl.BlockSpec((tm, w), lambda i: (i, 0))
    g_norm = _tab(g_norm)
    return pl.pallas_call(
        functools.partial(_post_kernel, heads=heads, dv=dv),
        out_shape=jax.ShapeDtypeStruct((l, w), BF16),
        grid=(l // tm,),
        in_specs=[row, row, pl.BlockSpec((tm, w), lambda i: (i, gate_off // w)), g_norm.spec],
        out_specs=row,
        compiler_params=_cparams(("parallel",)),
        name=name,
    )(o_f, o_b, p, g_norm.arr)


def _seg_sum(x, seg_bf16):
    hi, lo = _split2(x)
    return jnp.dot(hi, seg_bf16, preferred_element_type=F32) + jnp.dot(lo, seg_bf16, preferred_element_type=F32)


def _shift_mix_block(x, halo_prev, halo_next, mu, seg_start, seg_end):
    row = lax.broadcasted_iota(jnp.int32, x.shape, 0)
    first = jnp.where(seg_start, 0.0, halo_prev[7:8, :])
    last = jnp.where(seg_end, 0.0, halo_next[0:1, :])
    xp = jnp.where(row == 0, first, pltpu.roll(x, 1, 0))
    xn = jnp.where(row == CHUNK - 1, last, pltpu.roll(x, CHUNK - 1, 0))
    return x + mu[0:1] * (xp - x) + mu[1:2] * (xn - x)


def _rwkv_kernel(*refs, groups, ncc, nch):
    ins = [refs[0:6], refs[6:12]]
    (mu_rkv_ref, mu_lo_ref, vec_ref, wup_ref, aup_ref, gup_ref, seg_ref,
     tri_ref, strict_ref, incl_ref, off_ref, eye_ref, hm_ref,
     o0_ref, o1_ref, bo0_ref, bo1_ref, g_ref, st_ref) = refs[12:]
    c = pl.program_id(0)

    @pl.when(c == 0)
    def _():
        st_ref[...] = jnp.zeros(st_ref.shape, F32)

    hm = hm_ref[...]
    eye = eye_ref[...]
    seg = seg_ref[...]
    w = RWKV_PACK * RWKV_HEAD
    nb = RWKV_PACK * CHUNK
    gw = GROUP_W

    def stack(x):
        return jnp.concatenate([x.astype(BF16)] * RWKV_PACK, axis=0) * hm

    streams = []
    for d, (o_ref, bo_ref) in enumerate(((o0_ref, bo0_ref), (o1_ref, bo1_ref))):
        rkv_ref, rkv_p, rkv_n, lo_ref, lo_p, lo_n = ins[d]
        pos = _chunk_pos(d, c, ncc, nch)
        seg_start = (pos == 0) | (pos == ncc)
        seg_end = (pos == ncc - 1) | (pos == nch - 1)
        rkv = _shift_mix_block(rkv_ref[...], rkv_p[...], rkv_n[...], mu_rkv_ref[...], seg_start, seg_end)
        lora = _shift_mix_block(lo_ref[...], lo_p[...], lo_n[...], mu_lo_ref[...], seg_start, seg_end)
        r, k, v = rkv[:, 0:gw], rkv[:, gw:2 * gw], rkv[:, 2 * gw:3 * gw]
        w_dn, a_dn, g_dn = lora[:, 0:128], lora[:, 128:256], lora[:, 256:384]
        vec = vec_ref[d]
        wl = vec[0:1] + _dot(jnp.tanh(w_dn), wup_ref[d])
        lw = -jnp.exp(-(jnp.maximum(-wl, 0.0) + jnp.log1p(jnp.exp(-jnp.abs(wl)))) - 0.5)
        a = jax.nn.sigmoid(vec[1:2] + _dot(a_dn, aup_ref[d]))
        kk = k * vec[2:3]
        kk = kk * lax.rsqrt(jnp.maximum(_seg_sum(kk * kk, seg), 1e-24))
        k = k * (1.0 + (a - 1.0) * vec[3:4])
        b = kk * a
        bo_ref[...] = _seg_sum(r * k * vec[4:5], seg) * v
        if d == 0:
            g_ref[...] = _dot(jax.nn.sigmoid(g_dn), gup_ref[...])
        bc = _dot_exact_lhs(tri_ref[d], lw)
        ebt = jnp.exp(jnp.sum(lw, axis=0, keepdims=True))
        einv = jnp.exp(-bc)
        khat = kk * jnp.exp(bc - lw)
        rhat = r * jnp.exp(bc)
        ks = k * einv
        bs = b * einv
        for g in range(groups):
            sl = slice(g * w, (g + 1) * w)
            streams.append(dict(
                d=d, g=g, sl=sl, o_ref=o_ref, ebt=ebt[:, sl],
                kr=jnp.concatenate([stack(khat[:, sl]), stack(rhat[:, sl])], axis=0),
                bk=jnp.concatenate([stack(bs[:, sl]), stack(ks[:, sl])], axis=0),
                vs=stack(v[:, sl])))

    for s in streams:
        aa = _dot_nt(s["kr"], s["bk"])
        strict, incl = strict_ref[s["d"]], incl_ref[s["d"]]
        s["akb"] = strict.astype(F32) * aa[:nb, :nb]
        s["akb_b"] = s["akb"].astype(BF16)
        s["arb"] = incl * aa[nb:, :nb].astype(BF16)
        s["ak_v"] = jnp.concatenate([strict * aa[:nb, nb:].astype(BF16), incl * aa[nb:, nb:].astype(BF16)], axis=0)
        s["minv"] = eye - off_ref[s["d"], N_LEVELS - 1].astype(F32) * s["akb"]
    for lv in range(N_LEVELS - 2, -1, -1):
        for s in streams:
            s["minv_b"] = s["minv"].astype(BF16)
            s["t1"] = _dot(off_ref[s["d"], lv] * s["akb_b"], s["minv_b"])
        for s in streams:
            s["minv"] = s["minv"] - _dot(s["minv_b"], s["t1"])
    for s in streams:
        st = st_ref[s["d"], s["g"]]
        from_state = _dot_nt(s["kr"], st)
        from_v = _dot(s["ak_v"], s["vs"])
        u = _dot(s["minv"], from_state[:nb] + from_v[:nb])
        o = from_state[nb:] + from_v[nb:] - _dot(s["arb"], u)
        s["o_ref"][:, s["sl"]] = sum(o[h * CHUNK:(h + 1) * CHUNK] for h in range(RWKV_PACK))
        upd = _dot(s["vs"].T, s["bk"][nb:]) - _dot(u.T, s["bk"][:nb])
        st_ref[s["d"], s["g"]] = (st + upd) * s["ebt"]


def rwkv_scan(p, rkv_off, lora_off, params, *, n_ctx, name="rwkv_scan"):
    l = p.shape[0]
    hw = GROUP_W
    w = RWKV_PACK * RWKV_HEAD
    groups = hw // w
    nch, ncc = l // CHUNK, n_ctx // CHUNK
    nb = RWKV_PACK * CHUNK
    halo = 8
    per_chunk = CHUNK // halo

    def shared(d):
        return pl.BlockSpec((CHUNK, hw), lambda c: (_chunk_pos(d, c, ncc, nch), 0))

    def piece(d, off, width):
        assert off % width == 0
        cb = off // width
        pos = lambda c: _chunk_pos(d, c, ncc, nch)
        return [pl.BlockSpec((CHUNK, width), lambda c: (pos(c), cb)),
                pl.BlockSpec((halo, width), lambda c: (jnp.maximum(pos(c) * per_chunk - 1, 0), cb)),
                pl.BlockSpec((halo, width), lambda c: (jnp.minimum((pos(c) + 1) * per_chunk, l // halo - 1), cb))]

    def const(a):
        return pl.BlockSpec(a.shape, lambda c: (0,) * a.ndim)

    bd = lambda a: np.kron(np.eye(RWKV_PACK, dtype=np.float32), a)
    strict = jnp.asarray(np.stack([bd(_CC["strict"][d]) for d in range(2)]), BF16)
    incl = jnp.asarray(np.stack([bd(_CC["strict"][d] + _CC["eye"][d]) for d in range(2)]), BF16)
    off = jnp.asarray(np.stack([np.stack([bd(_CC["off"][d, lv]) for lv in range(N_LEVELS)]) for d in range(2)]), BF16)
    eye = jnp.asarray(np.eye(nb, dtype=np.float32))
    hm = jnp.asarray(np.kron(np.eye(RWKV_PACK, dtype=np.float32), np.ones((CHUNK, RWKV_HEAD), np.float32)), BF16)
    tri = jnp.asarray(_CC["tri"], BF16)
    seg = jnp.asarray(np.kron(np.eye(RWKV_HEADS, dtype=np.float32), np.ones((RWKV_HEAD, RWKV_HEAD), np.float32)), BF16)
    consts = [_tab(a) for a in list(params) + [seg, tri, strict, incl, off, eye, hm]]
    in_specs = []
    for d in range(2):
        in_specs += piece(d, rkv_off, 3 * hw) + piece(d, lora_off, 3 * 128)
    return pl.pallas_call(
        functools.partial(_rwkv_kernel, groups=groups, ncc=ncc, nch=nch),
        out_shape=[jax.ShapeDtypeStruct((l, hw), F32)] * 5,
        grid=(nch,),
        in_specs=in_specs + [t.spec for t in consts],
        out_specs=[shared(0), shared(1), shared(0), shared(1), shared(0)],
        scratch_shapes=[pltpu.VMEM((2, groups, nb, nb), F32)],
        compiler_params=_cparams(("arbitrary",)),
        name=name,
    )(*([p] * len(in_specs)), *[t.arr for t in consts])


def _rwkv_post_kernel(of_ref, ob_ref, bf_ref, bb_ref, g_ref, ln_ref, seg_ref, o_ref):
    seg = seg_ref[...]
    o = of_ref[...] + ob_ref[...]
    mean = _seg_sum(o, seg) * (1.0 / RWKV_HEAD)
    oc = o - mean
    var = _seg_sum(oc * oc, seg) * (1.0 / RWKV_HEAD)
    y = oc * lax.rsqrt(var + RWKV_LN_EPS) * ln_ref[0:1] + ln_ref[1:2] + bf_ref[...] + bb_ref[...]
    o_ref[...] = (y * g_ref[...]).astype(o_ref.dtype)


def rwkv_post(o_f, o_b, bo_f, bo_b, g, ln, *, tm=768, name="rwkv_post"):
    l, w = o_f.shape
    tm = math.gcd(l, tm)
    row = pl.BlockSpec((tm, w), lambda i: (i, 0))
    ln = _tab(ln)
    seg = jnp.asarray(np.kron(np.eye(RWKV_HEADS, dtype=np.float32), np.ones((RWKV_HEAD, RWKV_HEAD), np.float32)), BF16)
    return pl.pallas_call(
        _rwkv_post_kernel,
        out_shape=jax.ShapeDtypeStruct((l, w), BF16),
        grid=(l // tm,),
        in_specs=[row] * 5 + [ln.spec, pl.BlockSpec((w, w), lambda i: (0, 0))],
        out_specs=row,
        compiler_params=_cparams(("parallel",)),
        name=name,
    )(o_f, o_b, bo_f, bo_b, g, ln.arr, seg)


@functools.lru_cache(maxsize=None)
def _rope_lane_tables(m, n):
    rows = n // GRID_W
    row = np.repeat(np.arange(rows, dtype=np.float32), GRID_W)
    col = np.tile(np.arange(GRID_W, dtype=np.float32), rows)
    n_freq = MLA_ROPE // 4
    freqs = (np.float32(ROPE_BASE) ** (-np.arange(n_freq, dtype=np.float32) / np.float32(n_freq))).astype(np.float32)
    ang = np.stack([row[:, None] * freqs, col[:, None] * freqs], axis=1).astype(np.float32)
    cos64 = np.repeat(np.cos(ang), 2, axis=1).reshape(n, MLA_ROPE)
    sin64 = (np.repeat(np.sin(ang), 2, axis=1) * np.array([-1.0, 1.0, -1.0, 1.0])[None, :, None]).reshape(n, MLA_ROPE)
    cos_t = np.tile(np.concatenate([np.ones((m, MLA_ROPE)), cos64], axis=0), (1, MLA_HEADS)).astype(np.float32)
    sin_t = np.tile(np.concatenate([np.zeros((m, MLA_ROPE)), sin64], axis=0), (1, MLA_HEADS)).astype(np.float32)
    return cos_t, sin_t


def _rope_lanes(x, cos_t, sin_t):
    n = x.shape[1]
    lane = lax.broadcasted_iota(jnp.int32, x.shape, 1)
    swapped = jnp.where(lane % 32 < 16, pltpu.roll(x, n - 16, 1), pltpu.roll(x, 16, 1))
    return x * cos_t + swapped * sin_t


def _mla_prep_kernel(cq_ref, ckv_ref, kr_ref, cos_ref, sin_ref, qn_ref, kvn_ref, wq_ref, wkv_ref,
                     q_ref, k_ref, v_ref):
    cq = cq_ref[...]
    ms = jnp.sum(cq * cq, axis=-1, keepdims=True) * (1.0 / MLA_Q_LORA)
    q = _dot(cq * lax.rsqrt(ms + EPS) * qn_ref[...], wq_ref[...])
    ckv = ckv_ref[...]
    kvn = ckv * lax.rsqrt(jnp.mean(ckv * ckv, axis=-1, keepdims=True) + EPS) * kvn_ref[...]
    kv = _dot(kvn, wkv_ref[...])
    cos_t, sin_t = cos_ref[...], sin_ref[...]
    hn = MLA_HEADS * MLA_NOPE
    scale = (MLA_NOPE + MLA_ROPE) ** -0.5 * math.log2(math.e)
    q_rope = _rope_lanes(q[:, hn:], cos_t, sin_t)
    k_rope = _rope_lanes(kr_ref[...], cos_t[:, :128], sin_t[:, :128])
    k_rope_hi = pltpu.roll(k_rope, MLA_ROPE, 1)
    lane = lax.broadcasted_iota(jnp.int32, k_rope.shape, 1)
    ones_col = jnp.where(lane == 0, 1.0, 0.0)
    for h in range(MLA_HEADS):
        pair = q_rope[:, (h // 2) * 128:(h // 2 + 1) * 128]
        q_ref[h] = (jnp.concatenate([q[:, h * MLA_NOPE:(h + 1) * MLA_NOPE], pair], axis=1) * scale).astype(q_ref.dtype)
        k_ref[h] = jnp.concatenate([kv[:, h * MLA_NOPE:(h + 1) * MLA_NOPE], k_rope if h % 2 == 0 else k_rope_hi],
                                   axis=1).astype(k_ref.dtype)
        v_ref[h] = jnp.concatenate([kv[:, hn + h * MLA_V:hn + (h + 1) * MLA_V], ones_col], axis=1).astype(v_ref.dtype)


def mla_tables(q_norm, w_q_up, kv_norm, w_kv_up):
    nl = w_q_up.shape[0]
    dq = MLA_NOPE + MLA_ROPE
    pad = PACKED["cq"][3] - MLA_Q_LORA
    wq = w_q_up.reshape(nl, MLA_Q_LORA, MLA_HEADS, dq)
    wq = jnp.concatenate([wq[..., :MLA_NOPE].reshape(nl, MLA_Q_LORA, -1), wq[..., MLA_NOPE:].reshape(nl, MLA_Q_LORA, -1)],
                         axis=2)
    wq = jnp.pad(wq, ((0, 0), (0, pad), (0, 0)))
    qn = jnp.pad(q_norm, ((0, 0), (0, pad)))[:, None, :]
    wkv = w_kv_up.reshape(nl, MLA_KV_LORA, MLA_HEADS, MLA_NOPE + MLA_V)
    wkv = jnp.concatenate([wkv[..., :MLA_NOPE].reshape(nl, MLA_KV_LORA, -1), wkv[..., MLA_NOPE:].reshape(nl, MLA_KV_LORA, -1)],
                          axis=2)
    return qn, kv_norm[:, None, :], wq, wkv


def mla_prep(p, m, tables, *, tm=256, name="mla_prep"):
    l = p.shape[0]
    cos_t, sin_t = (jnp.asarray(t) for t in _rope_lane_tables(m, l - m))
    tabs = [_tab(t) for t in tables]

    def col(nm):
        _, _, off, wp = PACKED[nm]
        return pl.BlockSpec((tm, wp), lambda i: (i, off // wp))

    row = lambda wd: pl.BlockSpec((tm, wd), lambda i: (i, 0))
    out_spec = pl.BlockSpec((MLA_HEADS, tm, 256), lambda i: (0, i, 0))
    return pl.pallas_call(
        _mla_prep_kernel,
        out_shape=[jax.ShapeDtypeStruct((MLA_HEADS, l, 256), BF16)] * 3,
        grid=(l // tm,),
        in_specs=[col("cq"), col("ckv"), col("k_rope"), row(256), row(256)] + [t.spec for t in tabs],
        out_specs=[out_spec] * 3,
        compiler_params=_cparams(("parallel",)),
        name=name,
    )(p, p, p, cos_t, sin_t, *[t.arr for t in tabs])


def mla_mixer(p, m, tables):
    qh, kh, vh = mla_prep(p, m, tables)
    o_all = flash_attention(qh, kh, vh, dv=MLA_V, tq=768, name="mla_attn_lat")
    return flash_attention(qh, kh, vh, dv=MLA_V, tq=m, lq=m, lk=m, into=o_all, name="mla_attn_ctx")


def _pad_rows(t, before, total):
    cfg = [(0, 0)] * t.ndim
    cfg[-2] = (before, total - before - t.shape[-2])
    return jnp.pad(t, cfg)


def rwkv_tables(mu, w0, w_up, a0, a_up, k_k, k_a, u, g_up, ln_g, ln_b):
    zeros = jnp.zeros_like(w0)
    vec = jnp.stack([w0, a0, k_k, k_a, u, zeros, zeros, zeros], axis=2)
    wup = jnp.stack([_pad_rows(w_up[:, d], d * RWKV_W_LORA, 128) for d in range(2)], axis=1)
    aup = jnp.stack([_pad_rows(a_up[:, d], d * RWKV_A_LORA, 128) for d in range(2)], axis=1)
    ln = jnp.stack([ln_g, ln_b] + [jnp.zeros_like(ln_g)] * 6, axis=1)
    return (mu[:, :, :3 * GROUP_W], mu[:, :, 3 * GROUP_W:], vec, wup, aup, g_up), ln


def rwkv7_mixer(p, m, params, ln):
    outs = rwkv_scan(p, PACKED["rwkv_r"][2], PACKED["w_dn"][2], params, n_ctx=m)
    return rwkv_post(*outs, ln)


PACKED = {}


def _build_packed():
    orig = dict(cq=(0, MLA_Q_LORA), ckv=(MLA_Q_LORA, MLA_KV_LORA), k_rope=(MLA_Q_LORA + MLA_KV_LORA, MLA_ROPE))
    b = MLA_COLS
    for i, nm in enumerate(("rwkv_r", "rwkv_k", "rwkv_v")):
        orig[nm] = (b + i * GROUP_W, GROUP_W)
    b += 3 * GROUP_W
    orig.update(w_dn=(b, 2 * RWKV_W_LORA), a_dn=(b + 2 * RWKV_W_LORA, 2 * RWKV_A_LORA),
                g_dn=(b + 2 * RWKV_W_LORA + 2 * RWKV_A_LORA, RWKV_G_LORA))
    b = IN_SPLITS[1]
    orig.update(gla_q=(b, GLA_KD), gla_k=(b + GLA_KD, GLA_KD), gla_v=(b + 2 * GLA_KD, GROUP_W),
                gla_a=(b + GLA_SPLITS[2], 2 * GLA_GATE_RANK), gla_r=(b + GLA_SPLITS[3], GROUP_W))
    b = IN_SPLITS[2]
    for i, nm in enumerate(("hgrn_q", "hgrn_f0", "hgrn_f1", "hgrn_i", "hgrn_g")):
        orig[nm] = (b + i * GROUP_W, GROUP_W)
    order = [(nm, 512) for nm in ("rwkv_r", "rwkv_k", "rwkv_v", "gla_v", "gla_r", "hgrn_q", "hgrn_f0", "hgrn_f1",
                                  "hgrn_i", "hgrn_g", "cq")]
    orig["pad"] = (0, 0)
    order += [(nm, 256) for nm in ("gla_q", "gla_k")]
    order += [(nm, 128) for nm in ("w_dn", "a_dn", "g_dn", "gla_a", "k_rope", "pad")]
    order += [("ckv", 256)]
    off = 0
    for nm, wp in order:
        PACKED[nm] = (orig[nm][0], orig[nm][1], off, wp)
        off += wp
    return off


N_PACKED = _build_packed()


def pack_w_in(w):
    pieces = []
    for o, wd, _, wp in PACKED.values():
        pieces.append(w[..., o:o + wd])
        if wp > wd:
            pieces.append(jnp.zeros(w.shape[:-1] + (wp - wd,), w.dtype))
    return jnp.concatenate(pieces, axis=-1)


def _pcol(p, nm):
    _, wd, off, _ = PACKED[nm]
    return p[:, off:off + wd]


def _blk(nm):
    return (PACKED[nm][2], PACKED[nm][3])


def gla_tables(a_up, a_bias):
    return jnp.stack([jnp.concatenate([_pad_rows(a_up[:, d], d * GLA_GATE_RANK, 128),
                                       _pad_rows(a_bias[:, d][:, None, :], 0, 8)], axis=1) for d in range(2)], axis=1)


def hgrn_tables(lb_all):
    lb = jnp.swapaxes(lb_all, 0, 1)
    zeros = jnp.zeros_like(lb)
    return jnp.stack([jnp.log(lb), jnp.log1p(-lb), 1.0 - lb] + [zeros] * 5, axis=2)


def gla_mixer(p, m, par, g_norm):
    cols = [[_blk("gla_q"), _blk("gla_k"), _blk("gla_v"), _blk("gla_a")]] * 2
    o_f, o_b = gla_scan(p, cols, par, mode="gla", heads=GLA_HEADS, dk=GLA_DK, dv=GLA_DV, pack=4, n_ctx=m,
                        name="gla_scan")
    return mix_post(o_f, o_b, p, PACKED["gla_r"][2], g_norm, heads=GLA_HEADS, dv=GLA_DV, name="gla_post")


def hgrn2_mixer(p, m, par, g_norm):
    cols = [[_blk("hgrn_q"), _blk("hgrn_f%d" % d), _blk("hgrn_i")] for d in range(2)]
    o_f, o_b = gla_scan(p, cols, par, mode="hgrn", heads=HGRN_HEADS, dk=HGRN_EXPAND, dv=HGRN_DV, pack=2, n_ctx=m,
                        name="hgrn_scan")
    return mix_post(o_f, o_b, p, PACKED["hgrn_g"][2], g_norm, heads=HGRN_HEADS, dv=HGRN_DV, name="hgrn_post")


def dense_ffn(xs, h2, w1, w3, w2, j, gates2, m):
    act = pmatmul(h2, w1, (j,), w3=w3, tm=1408, tn=512, out_dtype=BF16, name="ffn_up")
    return pmatmul(act, w2, (j,), tm=704, tn=512, res=xs, gates=gates2, m_ctx=m, name="ffn_down")


def _route(top_idx, weights):
    n_pairs = top_idx.size
    r = n_pairs + N_EXPERTS * MOE_TILE
    n_tiles = r // MOE_TILE
    e_flat = top_idx.reshape(-1)
    onehot = (e_flat[:, None] == jnp.arange(N_EXPERTS)[None, :]).astype(jnp.int32)
    csum = jnp.cumsum(onehot, axis=0)
    counts = csum[-1]
    padded = (counts + MOE_TILE - 1) // MOE_TILE * MOE_TILE
    ends = jnp.cumsum(padded)
    dest = jnp.sum(onehot * (csum - 1 + (ends - padded)[None, :]), axis=1)
    put = dict(mode="promise_in_bounds", unique_indices=True)
    src_token = jnp.zeros((r,), jnp.int32).at[dest].set(jnp.arange(n_pairs, dtype=jnp.int32) // TOP_K, **put)
    row_gate = jnp.zeros((r,), F32).at[dest].set(weights.reshape(-1), **put)
    tile_start = jnp.arange(n_tiles, dtype=jnp.int32) * MOE_TILE
    tile_expert = jnp.minimum(jnp.sum(tile_start[:, None] >= ends[None, :], axis=1), N_EXPERTS - 1)
    meta = jnp.concatenate([tile_expert.astype(jnp.int32), (ends[-1:] // MOE_TILE).astype(jnp.int32)])
    return src_token, row_gate, dest.reshape(top_idx.shape), meta


def _combine_kernel(x_ref, y_ref, gate_ref, o_ref, *, m_ctx, tm, d):
    i = pl.program_id(0)
    rows = i * tm + lax.broadcasted_iota(jnp.int32, (tm, d), 0)
    g = jnp.where(rows < m_ctx, gate_ref[0:1, :], gate_ref[1:2, :])
    o_ref[...] = x_ref[...] + g * (y_ref[0].astype(F32) + y_ref[1].astype(F32))


def moe_combine(xs, y_tok, gates2, *, m_ctx, tm=256, name="moe_combine"):
    l, d = xs.shape
    gates2 = _tab(gates2)
    return pl.pallas_call(
        functools.partial(_combine_kernel, m_ctx=m_ctx, tm=tm, d=d),
        out_shape=jax.ShapeDtypeStruct((l, d), F32),
        grid=(l // tm,),
        in_specs=[pl.BlockSpec((tm, d), lambda i: (i, 0)), pl.BlockSpec((TOP_K, tm, d), lambda i: (0, i, 0)),
                  gates2.spec],
        out_specs=pl.BlockSpec((tm, d), lambda i: (i, 0)),
        compiler_params=_cparams(("parallel",)),
        name=name,
    )(xs, y_tok, gates2.arr)


def moe_ffn(xs, h2, logits, w1, w3, w2, j, gates2, m):
    l, d = h2.shape
    top_vals, top_idx = lax.top_k(logits[:, :N_EXPERTS], TOP_K)
    weights = jax.nn.softmax(top_vals, axis=-1)
    src_token, row_gate, dest, meta = _route(top_idx, weights)
    x_sorted = h2.at[src_token].get(mode="promise_in_bounds")
    act = gmatmul(meta, x_sorted, w1, j, w3=w3, tn=512, out_dtype=BF16, rowscale=row_gate[:, None], name="moe_up")
    y = gmatmul(meta, act, w2, j, tn=512, out_dtype=BF16, name="moe_down")
    y_tok = y.at[dest.T.reshape(-1)].get(mode="promise_in_bounds").reshape(TOP_K, l, d)
    return moe_combine(xs, y_tok, gates2, m_ctx=m)


def kernel(x, c, ctx, c_ctx, norm1_g, norm2_g, w_mod, b_mod, w_in, w_out, mla_q_norm, mla_w_q_up, mla_kv_norm, mla_w_kv_up, rwkv_mu, rwkv_w0, rwkv_w_up, rwkv_a0, rwkv_a_up, rwkv_k_k, rwkv_k_a, rwkv_u, rwkv_g_up, rwkv_ln_g, rwkv_ln_b, gla_a_up, gla_a_bias, gla_norm, hgrn_lb, hgrn_norm, ffn_w1, ffn_w3, ffn_w2, moe_router, moe_w1, moe_w3, moe_w2, final_norm_g):
    m, n, d = ctx.shape[1], x.shape[1], x.shape[2]
    lb_all = jnp.cumsum(jax.nn.softmax(hgrn_lb.astype(F32), axis=1), axis=1)
    lb_all = lb_all - lb_all[:, :1]
    xs = jnp.concatenate([ctx[0], x[0]], axis=0)

    cvec = jnp.zeros((16, d), F32).at[0].set(jax.nn.silu(c[0])).at[1].set(jax.nn.silu(c_ctx))
    mod = jnp.stack([pmatmul(cvec, w_mod, (l,), tm=16, tn=2048, name="mod")[:2] for l in range(DEPTH)]) + b_mod[:, None, :]
    mods = jnp.swapaxes(mod[:, ::-1].reshape(DEPTH, 2, 6, d), 1, 2)
    zmods = jnp.zeros((2, d), F32)
    g1, g2 = norm1_g[:, None, :], norm2_g[:, None, :]
    w_in_p = pack_w_in(w_in)
    mla_tabs = mla_tables(mla_q_norm, mla_w_q_up, mla_kv_norm, mla_w_kv_up)
    rwkv_tabs, rwkv_ln = rwkv_tables(rwkv_mu, rwkv_w0, rwkv_w_up, rwkv_a0, rwkv_a_up, rwkv_k_k, rwkv_k_a, rwkv_u,
                                      rwkv_g_up, rwkv_ln_g, rwkv_ln_b)
    gla_par, hgrn_par = gla_tables(gla_a_up, gla_a_bias), hgrn_tables(lb_all)
    gla_g, hgrn_g = gla_norm[:, None, :], hgrn_norm[:, None, :]
    router_p = jnp.pad(moe_router, ((0, 0), (0, 0), (0, 128 - N_EXPERTS)))

    for l in range(DEPTH):
        sh1, sc1, gt1, sh2, sc2, gt2 = (_Tab(mods, (l, i)) for i in range(6))
        h = norm_mod(xs, _Tab(g1, (l,)), sh1, sc1, m_ctx=m, out_dtype=BF16)
        p = pmatmul(h, w_in_p, (l,), tm=1408, tn=1024, name="w_in")
        o = [
            mla_mixer(p, m, [_Tab(t, (l,)) for t in mla_tabs]),
            rwkv7_mixer(p, m, [_Tab(t, (l,)) for t in rwkv_tabs], _Tab(rwkv_ln, (l,))),
            gla_mixer(p, m, _Tab(gla_par, (l,)), _Tab(gla_g, (l,))),
            hgrn2_mixer(p, m, _Tab(hgrn_par, (l,)), _Tab(hgrn_g, (l,))),
        ]
        xs = pmatmul(o, w_out, (l,), tm=1408, tn=512, res=xs, gates=gt1, m_ctx=m, name="w_out")

        j = l // 2
        if l % 2 == 0:
            h2 = norm_mod(xs, _Tab(g2, (l,)), sh2, sc2, m_ctx=m, out_dtype=BF16)
            xs = dense_ffn(xs, h2, ffn_w1, ffn_w3, ffn_w2, j, gt2, m)
        else:
            h2, logits = norm_mod(xs, _Tab(g2, (l,)), sh2, sc2, m_ctx=m, out_dtype=BF16, router=_Tab(router_p, (j,)),
                                  name="norm_mod_router")
            xs = moe_ffn(xs, h2, logits, moe_w1, moe_w3, moe_w2, j, gt2, m)
    out = norm_mod(xs, final_norm_g[None, :], zmods, zmods, m_ctx=m, out_dtype=F32, skip_rows=m, name="final_norm")
    return out[None]
```

```python
import functools
import math

import numpy as np
import jax
import jax.numpy as jnp
from jax import lax
from jax.experimental import pallas as pl
from jax.experimental.pallas import tpu as pltpu

F32 = jnp.float32
BF16 = jnp.bfloat16

DEPTH = 4
GRID_W = 64
EPS = 1e-6
GROUP_W = 512
MLA_HEADS, MLA_NOPE, MLA_ROPE, MLA_V = 4, 128, 64, 128
MLA_Q_LORA, MLA_KV_LORA = 384, 256
ROPE_BASE = 10000.0
RWKV_HEADS, RWKV_HEAD = 8, 64
RWKV_W_LORA, RWKV_A_LORA, RWKV_G_LORA = 64, 64, 128
RWKV_LN_EPS = 64e-5
GLA_HEADS, GLA_DK, GLA_DV = 4, 64, 128
GLA_GATE_RANK, GLA_GATE_NORM = 16, 16.0
HGRN_HEADS, HGRN_EXPAND, HGRN_DV = 4, 128, 128
CHUNK = 64
N_EXPERTS, TOP_K = 8, 2
MLA_COLS = MLA_Q_LORA + MLA_KV_LORA + MLA_ROPE
RWKV_COLS = 3 * GROUP_W + 2 * RWKV_W_LORA + 2 * RWKV_A_LORA + RWKV_G_LORA
GLA_KD = GLA_HEADS * GLA_DK
GLA_COLS = 2 * GLA_KD + GROUP_W + 2 * GLA_GATE_RANK + GROUP_W
IN_SPLITS = (MLA_COLS, MLA_COLS + RWKV_COLS, MLA_COLS + RWKV_COLS + GLA_COLS)
RWKV_SPLITS = (GROUP_W, 2 * GROUP_W, 3 * GROUP_W, 3 * GROUP_W + 2 * RWKV_W_LORA,
               3 * GROUP_W + 2 * RWKV_W_LORA + 2 * RWKV_A_LORA)
GLA_SPLITS = (GLA_KD, 2 * GLA_KD, 2 * GLA_KD + GROUP_W, 2 * GLA_KD + GROUP_W + 2 * GLA_GATE_RANK)

V7X_VMEM_LIMIT = 56 * 1024 * 1024
V7X_MXU = 256
N_LEVELS = 6
RWKV_PACK = V7X_MXU // RWKV_HEAD
MOE_TILE = 512


def _cparams(sem, vmem=V7X_VMEM_LIMIT):
    return pltpu.CompilerParams(dimension_semantics=sem, vmem_limit_bytes=vmem)


class _Tab:
    def __init__(self, arr, idx=()):
        self.arr, self.idx = arr, tuple(idx)

    @property
    def shape(self):
        return self.arr.shape[len(self.idx):]

    @property
    def spec(self):
        idx, rest = self.idx, self.shape
        return pl.BlockSpec((None,) * len(idx) + rest, lambda *_: idx + (0,) * len(rest))


def _tab(a):
    return a if isinstance(a, _Tab) else _Tab(a)


def _dot(a, b):
    return jnp.dot(a.astype(BF16), b.astype(BF16), preferred_element_type=F32)


def _dot_nt(a, b):
    return lax.dot_general(a.astype(BF16), b.astype(BF16), (((1,), (1,)), ((), ())), preferred_element_type=F32)


def _split2(x):
    hi = x.astype(BF16)
    lo = (x - hi.astype(F32)).astype(BF16)
    return hi, lo


def _dot_exact_lhs(m_bf16, x):
    hi, lo = _split2(x)
    return (jnp.dot(m_bf16, hi, preferred_element_type=F32) + jnp.dot(m_bf16, lo, preferred_element_type=F32))


def _mm_kernel(*refs, n_x, n_w, has_res, m_ctx, tm, cast_w):
    it = iter(refs)
    x_refs = [next(it) for _ in range(n_x)]
    w_refs = [next(it) for _ in range(n_w)]
    res_ref = next(it) if has_res else None
    gate_ref = next(it) if has_res else None
    o_ref = next(it)
    wb_refs = [next(it) for _ in range(n_w)] if cast_w else w_refs
    i = pl.program_id(1)

    if cast_w:
        @pl.when(i == 0)
        def _():
            for w_ref, wb_ref in zip(w_refs, wb_refs):
                wb_ref[...] = w_ref[...].astype(BF16)
    x = jnp.concatenate([x_ref[...].astype(BF16) for x_ref in x_refs], axis=1)

    acc = jnp.dot(x, wb_refs[0][...], preferred_element_type=F32)
    if n_w == 2:
        acc3 = jnp.dot(x, wb_refs[1][...], preferred_element_type=F32)
        acc = acc * jax.nn.sigmoid(acc) * acc3
    if has_res:
        rows = i * tm + lax.broadcasted_iota(jnp.int32, acc.shape, 0)
        g = jnp.where(rows < m_ctx, gate_ref[0:1, :], gate_ref[1:2, :])
        acc = res_ref[...] + g * acc
    o_ref[...] = acc.astype(o_ref.dtype)


def pmatmul(x, w, widx=(), *, w3=None, tm, tn, out_dtype=F32, res=None, gates=None, m_ctx=0, name="mm"):
    xs = list(x) if isinstance(x, (list, tuple)) else [x]
    m = xs[0].shape[0]
    k = sum(xi.shape[1] for xi in xs)
    n = w.shape[-1]
    tm = math.gcd(m, tm)
    assert w.shape[-2] == k and tm % 16 == 0
    nj, ni = pl.cdiv(n, tn), m // tm
    lead = (None,) * len(widx)
    w_spec = pl.BlockSpec(lead + (k, tn), lambda j, i: tuple(widx) + (0, j))
    ws = [w] if w3 is None else [w, w3]
    in_specs = [pl.BlockSpec((tm, xi.shape[1]), lambda j, i: (i, 0)) for xi in xs] + [w_spec] * len(ws)
    args = xs + ws
    if res is not None:
        gates = _tab(gates)
        gi = gates.idx
        in_specs += [pl.BlockSpec((tm, tn), lambda j, i: (i, j)),
                     pl.BlockSpec((None,) * len(gi) + (2, tn), lambda j, i: gi + (0, j))]
        args += [res, gates.arr]
    cast_w = w.dtype != BF16
    kern = functools.partial(_mm_kernel, n_x=len(xs), n_w=len(ws), has_res=res is not None, m_ctx=m_ctx, tm=tm,
                             cast_w=cast_w)
    return pl.pallas_call(
        kern,
        out_shape=jax.ShapeDtypeStruct((m, n), out_dtype),
        grid=(nj, ni),
        in_specs=in_specs,
        out_specs=pl.BlockSpec((tm, tn), lambda j, i: (i, j)),
        scratch_shapes=[pltpu.VMEM((k, tn), BF16) for _ in ws] if cast_w else [],
        compiler_params=_cparams(("arbitrary", "arbitrary")),
        name=name,
    )(*args)


def _gmm_kernel(meta_ref, x_ref, *refs, n_w, has_rowscale, n_tiles):
    it = iter(refs)
    w_refs = [next(it) for _ in range(n_w)]
    rs_ref = next(it) if has_rowscale else None
    o_ref = next(it)
    wb_refs = [next(it) for _ in range(n_w)]
    t = pl.program_id(1)
    e = meta_ref[t]
    e_prev = meta_ref[jnp.maximum(t - 1, 0)]

    @pl.when((t == 0) | (e != e_prev))
    def _():
        for w_ref, wb_ref in zip(w_refs, wb_refs):
            wb_ref[...] = w_ref[...].astype(BF16)

    @pl.when(t < meta_ref[n_tiles])
    def _():
        x = x_ref[...].astype(BF16)
        acc = jnp.dot(x, wb_refs[0][...], preferred_element_type=F32)
        if n_w == 2:
            acc3 = jnp.dot(x, wb_refs[1][...], preferred_element_type=F32)
            acc = acc * jax.nn.sigmoid(acc) * acc3
        if has_rowscale:
            acc = acc * rs_ref[...]
        o_ref[...] = acc.astype(o_ref.dtype)

    @pl.when(t >= meta_ref[n_tiles])
    def _():
        o_ref[...] = jnp.zeros(o_ref.shape, o_ref.dtype)


def gmatmul(meta, x, w, jl, *, w3=None, tn, out_dtype, rowscale=None, name):
    r, k = x.shape
    n = w.shape[-1]
    n_tiles = r // MOE_TILE
    ws = [w] if w3 is None else [w, w3]
    w_spec = pl.BlockSpec((None, None, k, tn), lambda j, t, mr: (jl, mr[t], 0, j))
    in_specs = [pl.BlockSpec((MOE_TILE, k), lambda j, t, mr: (t, 0))] + [w_spec] * len(ws)
    args = [x] + ws
    if rowscale is not None:
        in_specs.append(pl.BlockSpec((MOE_TILE, 1), lambda j, t, mr: (t, 0)))
        args.append(rowscale)
    return pl.pallas_call(
        functools.partial(_gmm_kernel, n_w=len(ws), has_rowscale=rowscale is not None, n_tiles=n_tiles),
        out_shape=jax.ShapeDtypeStruct((r, n), out_dtype),
        grid_spec=pltpu.PrefetchScalarGridSpec(
            num_scalar_prefetch=1,
            grid=(n // tn, n_tiles),
            in_specs=in_specs,
            out_specs=pl.BlockSpec((MOE_TILE, tn), lambda j, t, mr: (t, j)),
            scratch_shapes=[pltpu.VMEM((k, tn), BF16) for _ in ws]),
        compiler_params=_cparams(("arbitrary", "arbitrary")),
        name=name,
    )(meta, *args)


def _norm_kernel(x_ref, g_ref, sh_ref, sc_ref, *rest, m_ctx, tm, with_router, skip_blocks):
    i = pl.program_id(0) + skip_blocks
    x = x_ref[...]
    y = x * lax.rsqrt(jnp.mean(x * x, axis=-1, keepdims=True) + EPS) * g_ref[...]
    rows = i * tm + lax.broadcasted_iota(jnp.int32, x.shape, 0)
    is_ctx = rows < m_ctx
    sc = jnp.where(is_ctx, sc_ref[0:1, :], sc_ref[1:2, :])
    sh = jnp.where(is_ctx, sh_ref[0:1, :], sh_ref[1:2, :])
    h = y * (1.0 + sc) + sh
    if with_router:
        r_ref, o_ref, logit_ref = rest
        logit_ref[...] = jnp.dot(h, r_ref[...], preferred_element_type=F32, precision=lax.Precision.HIGHEST)
    else:
        (o_ref,) = rest
    o_ref[...] = h.astype(o_ref.dtype)


def norm_mod(x, g, shift2, scale2, *, m_ctx, out_dtype, router=None, skip_rows=0, tm=768, name="norm_mod"):
    m_in, d = x.shape
    m = m_in - skip_rows
    tm = math.gcd(math.gcd(m, tm), skip_rows) if skip_rows else math.gcd(m, tm)
    skip_blocks = skip_rows // tm
    row = pl.BlockSpec((tm, d), lambda i: (i, 0))
    tabs = [_tab(g), _tab(shift2), _tab(scale2)]
    in_specs = [pl.BlockSpec((tm, d), lambda i: (i + skip_blocks, 0))] + [t.spec for t in tabs]
    args = [x] + [t.arr for t in tabs]
    out_shape = jax.ShapeDtypeStruct((m, d), out_dtype)
    out_specs = row
    if router is not None:
        router = _tab(router)
        in_specs.append(router.spec)
        args.append(router.arr)
        out_shape = [out_shape, jax.ShapeDtypeStruct((m, router.shape[1]), F32)]
        out_specs = [row, pl.BlockSpec((tm, router.shape[1]), lambda i: (i, 0))]
    return pl.pallas_call(
        functools.partial(_norm_kernel, m_ctx=m_ctx, tm=tm, with_router=router is not None, skip_blocks=skip_blocks),
        out_shape=out_shape,
        grid=(m // tm,),
        in_specs=in_specs,
        out_specs=out_specs,
        compiler_params=_cparams(("parallel",)),
        name=name,
    )(*args)


def _attn_kernel(q_ref, k_ref, v_ref, o_ref, *, tk, n_kv, dv):
    q = q_ref[...]
    m = acc = None
    for j in range(n_kv):
        kb = k_ref[j * tk:(j + 1) * tk, :]
        vb = v_ref[j * tk:(j + 1) * tk, :]
        s = lax.dot_general(q, kb, (((1,), (1,)), ((), ())), preferred_element_type=F32)
        m_blk = jnp.max(s, axis=-1, keepdims=True)
        m_new = m_blk if j == 0 else jnp.maximum(m, m_blk)
        pv = jnp.dot(jnp.exp2(s - m_new).astype(BF16), vb, preferred_element_type=F32)
        acc = pv if j == 0 else jnp.exp2(m - m_new) * acc + pv
        m = m_new
    o_ref[...] = (acc[:, :dv] / acc[:, dv:dv + 1]).astype(o_ref.dtype)


def _attn_kv_tile(lk, cap=1408):
    return max(t for t in range(128, cap + 1, 128) if lk % t == 0)


def _attn_into_kernel(q_ref, k_ref, v_ref, prev_ref, o_ref, **kw):
    del prev_ref
    _attn_kernel(q_ref, k_ref, v_ref, o_ref, **kw)


def flash_attention(q, k, v_ext, *, dv, tq, lq=None, lk=None, into=None, name="mla_attn"):
    h, dqk = q.shape[0], q.shape[2]
    lq = q.shape[1] if lq is None else lq
    lk = k.shape[1] if lk is None else lk
    dve = v_ext.shape[2]
    tk = _attn_kv_tile(lk)
    tq = math.gcd(lq, tq)
    kw = dict(tk=tk, n_kv=lk // tk, dv=dv)
    in_specs = [pl.BlockSpec((None, tq, dqk), lambda hh, i: (hh, i, 0)),
                pl.BlockSpec((None, lk, dqk), lambda hh, i: (hh, 0, 0)),
                pl.BlockSpec((None, lk, dve), lambda hh, i: (hh, 0, 0))]
    args = [q, k, v_ext]
    if into is not None:
        in_specs.append(pl.BlockSpec(memory_space=pl.ANY))
        args.append(into)
    return pl.pallas_call(
        functools.partial(_attn_kernel if into is None else _attn_into_kernel, **kw),
        out_shape=jax.ShapeDtypeStruct((lq, h * dv) if into is None else into.shape, BF16),
        grid=(h, lq // tq),
        in_specs=in_specs,
        out_specs=pl.BlockSpec((tq, dv), lambda hh, i: (i, hh)),
        input_output_aliases={} if into is None else {3: 0},
        compiler_params=_cparams(("parallel", "parallel")),
        name=name,
    )(*args)


def _chunk_constants():
    c = CHUNK
    t = np.arange(c)
    tri = (t[None, :] <= t[:, None]).astype(np.float32)
    strict = (t[None, :] < t[:, None]).astype(np.float32)
    eye = np.eye(c, dtype=np.float32)
    seg, off = [], []
    for lv in range(N_LEVELS):
        s = c >> (lv + 1)
        blk = t // s
        same = blk[:, None] == blk[None, :]
        odd = (blk % 2 == 1)[:, None]
        seg.append(np.where(odd, same & (t[None, :] <= t[:, None]), same & (t[None, :] > t[:, None])).astype(np.float32))
        off.append((odd & (blk[None, :] == blk[:, None] - 1)).astype(np.float32))
    seg, off = np.stack(seg), np.stack(off)

    def both(a):
        return np.stack([a, a[..., ::-1, ::-1]])

    return {k: both(v) for k, v in dict(tri=tri, strict=strict, eye=eye, seg=seg, off=off).items()}


_CC = _chunk_constants()


def _chunk_pos(d, c, n_ctx_chunks, n_chunks):
    back = jnp.where(c < n_ctx_chunks, n_ctx_chunks - 1 - c, n_chunks + n_ctx_chunks - 1 - c)
    return jnp.where(d == 0, c, back)


def _log_sigmoid(x):
    return jnp.minimum(x, 0.0) - jnp.log1p(jnp.exp(-jnp.abs(x)))


def _gla_kernel(*refs, mode, dk, dv, pack, groups, n_in):
    ins = [refs[:n_in], refs[n_in:2 * n_in]]
    par_ref, mall_ref, off_ref, eye_ref, hmk_ref, hmv_ref, o0_ref, o1_ref, st_ref = refs[2 * n_in:]
    c = pl.program_id(0)

    @pl.when(c == 0)
    def _():
        st_ref[...] = jnp.zeros(st_ref.shape, F32)

    eye = eye_ref[...]
    hmk = hmk_ref[...]
    hmv = hmv_ref[...]
    wk, wv = pack * dk, pack * dv
    for d, o_ref in enumerate((o0_ref, o1_ref)):
        if mode == "hgrn":
            q_raw, f_raw, v = (r[...] for r in ins[d])
            par = par_ref[d]
            q = q_raw * jax.nn.sigmoid(q_raw)
            g = jnp.logaddexp(par[0:1], par[1:2] + _log_sigmoid(f_raw))
            k = par[2:3] * jax.nn.sigmoid(-f_raw)
        else:
            q_raw, k, v, a_dn = (r[...] for r in ins[d])
            par = par_ref[d]
            q = q_raw * dk ** -0.5
            g = _log_sigmoid(_dot(a_dn, par[0:128]) + par[128:129]) / GLA_GATE_NORM
        e_all = _dot_exact_lhs(mall_ref[d], g)
        bc = e_all[0:CHUNK]
        btot = jnp.sum(g, axis=0, keepdims=True)
        qhat = q * jnp.exp(bc)
        kt = k * jnp.exp(btot - bc)
        ebt = jnp.exp(btot)
        qw, kw = [q], [k]
        for lv in range(N_LEVELS):
            w = jnp.exp(e_all[(lv + 1) * CHUNK:(lv + 2) * CHUNK])
            qw.append(q * w)
            kw.append(k * w)
        for gi in range(groups):
            slk = slice(gi * wk, (gi + 1) * wk)
            slv = slice(gi * wv, (gi + 1) * wv)

            def stk(x):
                return jnp.concatenate([x[:, slk].astype(BF16)] * pack, axis=0) * hmk

            def tile(x):
                return jnp.concatenate([x[:, slk].astype(BF16)] * pack, axis=0)

            att = eye * _dot_nt(stk(qw[0]), tile(kw[0]))
            for lv in range(N_LEVELS):
                att = att + off_ref[d, lv] * _dot_nt(stk(qw[lv + 1]), tile(kw[lv + 1]))
            vs = jnp.concatenate([v[:, slv]] * pack, axis=0) * hmv
            st = st_ref[d, gi]
            o = _dot_nt(stk(qhat), st) + _dot(att, vs)
            o_ref[:, slv] = sum(o[h * CHUNK:(h + 1) * CHUNK] for h in range(pack))
            st_ref[d, gi] = st * ebt[:, slk] + _dot(vs.T, stk(kt))


def gla_scan(p, cols, par, *, mode, heads, dk, dv, pack, n_ctx, name):
    l = p.shape[0]
    groups = heads // pack
    nch, ncc = l // CHUNK, n_ctx // CHUNK
    nb = pack * CHUNK

    def col_spec(d, off, width):
        assert off % width == 0
        return pl.BlockSpec((CHUNK, width), lambda c: (_chunk_pos(d, c, ncc, nch), off // width))

    def const(a):
        return pl.BlockSpec(a.shape, lambda c: (0,) * a.ndim)

    bd = lambda a: np.kron(np.eye(pack, dtype=np.float32), a)
    mall = jnp.asarray(np.concatenate([_CC["tri"][:, None], _CC["seg"]], axis=1).reshape(2, -1, CHUNK), BF16)
    off = jnp.asarray(np.stack([np.stack([bd(_CC["off"][d, lv]) for lv in range(N_LEVELS)]) for d in range(2)]), F32)
    eye = jnp.asarray(np.eye(nb, dtype=np.float32))
    hmk = jnp.asarray(np.kron(np.eye(pack, dtype=np.float32), np.ones((CHUNK, dk), np.float32)), BF16)
    hmv = jnp.asarray(np.kron(np.eye(pack, dtype=np.float32), np.ones((CHUNK, dv), np.float32)))
    consts = [_tab(a) for a in (par, mall, off, eye, hmk, hmv)]
    in_specs = [col_spec(d, o, w) for d in range(2) for (o, w) in cols[d]]
    out_spec = lambda d: pl.BlockSpec((CHUNK, heads * dv), lambda c: (_chunk_pos(d, c, ncc, nch), 0))
    return pl.pallas_call(
        functools.partial(_gla_kernel, mode=mode, dk=dk, dv=dv, pack=pack, groups=groups, n_in=len(cols[0])),
        out_shape=[jax.ShapeDtypeStruct((l, heads * dv), F32)] * 2,
        grid=(nch,),
        in_specs=in_specs + [t.spec for t in consts],
        out_specs=[out_spec(0), out_spec(1)],
        scratch_shapes=[pltpu.VMEM((2, groups, pack * dv, pack * dk), F32)],
        compiler_params=_cparams(("arbitrary",)),
        name=name,
    )(*([p] * len(in_specs)), *[t.arr for t in consts])


def _post_kernel(of_ref, ob_ref, gate_ref, gn_ref, o_ref, *, heads, dv):
    o = of_ref[...] + ob_ref[...]
    gate = gate_ref[...]
    gn = gn_ref[...]
    for h in range(heads):
        sl = slice(h * dv, (h + 1) * dv)
        oh = o[:, sl]
        y = oh * lax.rsqrt(jnp.mean(oh * oh, axis=-1, keepdims=True) + EPS) * gn
        gh = gate[:, sl]
        o_ref[:, sl] = (y * gh * jax.nn.sigmoid(gh)).astype(o_ref.dtype)


def mix_post(o_f, o_b, p, gate_off, g_norm, *, heads, dv, tm=768, name):
    l, w = o_f.shape
    tm = math.gcd(l, tm)
    assert gate_off % w == 0
    row = pl.BlockSpec((tm, w), lambda i: (i, 0))
    g_norm = _tab(g_norm)
    return pl.pallas_call(
        functools.partial(_post_kernel, heads=heads, dv=dv),
        out_shape=jax.ShapeDtypeStruct((l, w), BF16),
        grid=(l // tm,),
        in_specs=[row, row, pl.BlockSpec((tm, w), lambda i: (i, gate_off // w)), g_norm.spec],
        out_specs=row,
        compiler_params=_cparams(("parallel",)),
        name=name,
    )(o_f, o_b, p, g_norm.arr)


def _seg_sum(x, seg_bf16):
    hi, lo = _split2(x)
    return jnp.dot(hi, seg_bf16, preferred_element_type=F32) + jnp.dot(lo, seg_bf16, preferred_element_type=F32)


def _shift_mix_block(x, halo_prev, halo_next, mu, seg_start, seg_end):
    row = lax.broadcasted_iota(jnp.int32, x.shape, 0)
    first = jnp.where(seg_start, 0.0, halo_prev[7:8, :])
    last = jnp.where(seg_end, 0.0, halo_next[0:1, :])
    xp = jnp.where(row == 0, first, pltpu.roll(x, 1, 0))
    xn = jnp.where(row == CHUNK - 1, last, pltpu.roll(x, CHUNK - 1, 0))
    return x + mu[0:1] * (xp - x) + mu[1:2] * (xn - x)


def _rwkv_kernel(*refs, groups, ncc, nch):
    ins = [refs[0:6], refs[6:12]]
    (mu_rkv_ref, mu_lo_ref, vec_ref, wup_ref, aup_ref, gup_ref, seg_ref,
     tri_ref, strict_ref, incl_ref, off_ref, eye_ref, hm_ref,
     o0_ref, o1_ref, bo0_ref, bo1_ref, g_ref, st_ref) = refs[12:]
    c = pl.program_id(0)

    @pl.when(c == 0)
    def _():
        st_ref[...] = jnp.zeros(st_ref.shape, F32)

    hm = hm_ref[...]
    eye = eye_ref[...]
    seg = seg_ref[...]
    w = RWKV_PACK * RWKV_HEAD
    nb = RWKV_PACK * CHUNK
    gw = GROUP_W

    def stack(x):
        return jnp.concatenate([x.astype(BF16)] * RWKV_PACK, axis=0) * hm

    streams = []
    for d, (o_ref, bo_ref) in enumerate(((o0_ref, bo0_ref), (o1_ref, bo1_ref))):
        rkv_ref, rkv_p, rkv_n, lo_ref, lo_p, lo_n = ins[d]
        pos = _chunk_pos(d, c, ncc, nch)
        seg_start = (pos == 0) | (pos == ncc)
        seg_end = (pos == ncc - 1) | (pos == nch - 1)
        rkv = _shift_mix_block(rkv_ref[...], rkv_p[...], rkv_n[...], mu_rkv_ref[...], seg_start, seg_end)
        lora = _shift_mix_block(lo_ref[...], lo_p[...], lo_n[...], mu_lo_ref[...], seg_start, seg_end)
        r, k, v = rkv[:, 0:gw], rkv[:, gw:2 * gw], rkv[:, 2 * gw:3 * gw]
        w_dn, a_dn, g_dn = lora[:, 0:128], lora[:, 128:256], lora[:, 256:384]
        vec = vec_ref[d]
        wl = vec[0:1] + _dot(jnp.tanh(w_dn), wup_ref[d])
        lw = -jnp.exp(-(jnp.maximum(-wl, 0.0) + jnp.log1p(jnp.exp(-jnp.abs(wl)))) - 0.5)
        a = jax.nn.sigmoid(vec[1:2] + _dot(a_dn, aup_ref[d]))
        kk = k * vec[2:3]
        kk = kk * lax.rsqrt(jnp.maximum(_seg_sum(kk * kk, seg), 1e-24))
        k = k * (1.0 + (a - 1.0) * vec[3:4])
        b = kk * a
        bo_ref[...] = _seg_sum(r * k * vec[4:5], seg) * v
        if d == 0:
            g_ref[...] = _dot(jax.nn.sigmoid(g_dn), gup_ref[...])
        bc = _dot_exact_lhs(tri_ref[d], lw)
        ebt = jnp.exp(jnp.sum(lw, axis=0, keepdims=True))
        einv = jnp.exp(-bc)
        khat = kk * jnp.exp(bc - lw)
        rhat = r * jnp.exp(bc)
        ks = k * einv
        bs = b * einv
        for g in range(groups):
            sl = slice(g * w, (g + 1) * w)
            streams.append(dict(
                d=d, g=g, sl=sl, o_ref=o_ref, ebt=ebt[:, sl],
                kr=jnp.concatenate([stack(khat[:, sl]), stack(rhat[:, sl])], axis=0),
                bk=jnp.concatenate([stack(bs[:, sl]), stack(ks[:, sl])], axis=0),
                vs=stack(v[:, sl])))

    for s in streams:
        aa = _dot_nt(s["kr"], s["bk"])
        strict, incl = strict_ref[s["d"]], incl_ref[s["d"]]
        s["akb"] = strict.astype(F32) * aa[:nb, :nb]
        s["akb_b"] = s["akb"].astype(BF16)
        s["arb"] = incl * aa[nb:, :nb].astype(BF16)
        s["ak_v"] = jnp.concatenate([strict * aa[:nb, nb:].astype(BF16), incl * aa[nb:, nb:].astype(BF16)], axis=0)
        s["minv"] = eye - off_ref[s["d"], N_LEVELS - 1].astype(F32) * s["akb"]
    for lv in range(N_LEVELS - 2, -1, -1):
        for s in streams:
            s["minv_b"] = s["minv"].astype(BF16)
            s["t1"] = _dot(off_ref[s["d"], lv] * s["akb_b"], s["minv_b"])
        for s in streams:
            s["minv"] = s["minv"] - _dot(s["minv_b"], s["t1"])
    for s in streams:
        st = st_ref[s["d"], s["g"]]
        from_state = _dot_nt(s["kr"], st)
        from_v = _dot(s["ak_v"], s["vs"])
        u = _dot(s["minv"], from_state[:nb] + from_v[:nb])
        o = from_state[nb:] + from_v[nb:] - _dot(s["arb"], u)
        s["o_ref"][:, s["sl"]] = sum(o[h * CHUNK:(h + 1) * CHUNK] for h in range(RWKV_PACK))
        upd = _dot(s["vs"].T, s["bk"][nb:]) - _dot(u.T, s["bk"][:nb])
        st_ref[s["d"], s["g"]] = (st + upd) * s["ebt"]


def rwkv_scan(p, rkv_off, lora_off, params, *, n_ctx, name="rwkv_scan"):
    l = p.shape[0]
    hw = GROUP_W
    w = RWKV_PACK * RWKV_HEAD
    groups = hw // w
    nch, ncc = l // CHUNK, n_ctx // CHUNK
    nb = RWKV_PACK * CHUNK
    halo = 8
    per_chunk = CHUNK // halo

    def shared(d):
        return pl.BlockSpec((CHUNK, hw), lambda c: (_chunk_pos(d, c, ncc, nch), 0))

    def piece(d, off, width):
        assert off % width == 0
        cb = off // width
        pos = lambda c: _chunk_pos(d, c, ncc, nch)
        return [pl.BlockSpec((CHUNK, width), lambda c: (pos(c), cb)),
                pl.BlockSpec((halo, width), lambda c: (jnp.maximum(pos(c) * per_chunk - 1, 0), cb)),
                pl.BlockSpec((halo, width), lambda c: (jnp.minimum((pos(c) + 1) * per_chunk, l // halo - 1), cb))]

    def const(a):
        return pl.BlockSpec(a.shape, lambda c: (0,) * a.ndim)

    bd = lambda a: np.kron(np.eye(RWKV_PACK, dtype=np.float32), a)
    strict = jnp.asarray(np.stack([bd(_CC["strict"][d]) for d in range(2)]), BF16)
    incl = jnp.asarray(np.stack([bd(_CC["strict"][d] + _CC["eye"][d]) for d in range(2)]), BF16)
    off = jnp.asarray(np.stack([np.stack([bd(_CC["off"][d, lv]) for lv in range(N_LEVELS)]) for d in range(2)]), BF16)
    eye = jnp.asarray(np.eye(nb, dtype=np.float32))
    hm = jnp.asarray(np.kron(np.eye(RWKV_PACK, dtype=np.float32), np.ones((CHUNK, RWKV_HEAD), np.float32)), BF16)
    tri = jnp.asarray(_CC["tri"], BF16)
    seg = jnp.asarray(np.kron(np.eye(RWKV_HEADS, dtype=np.float32), np.ones((RWKV_HEAD, RWKV_HEAD), np.float32)), BF16)
    consts = [_tab(a) for a in list(params) + [seg, tri, strict, incl, off, eye, hm]]
    in_specs = []
    for d in range(2):
        in_specs += piece(d, rkv_off, 3 * hw) + piece(d, lora_off, 3 * 128)
    return pl.pallas_call(
        functools.partial(_rwkv_kernel, groups=groups, ncc=ncc, nch=nch),
        out_shape=[jax.ShapeDtypeStruct((l, hw), F32)] * 5,
        grid=(nch,),
        in_specs=in_specs + [t.spec for t in consts],
        out_specs=[shared(0), shared(1), shared(0), shared(1), shared(0)],
        scratch_shapes=[pltpu.VMEM((2, groups, nb, nb), F32)],
        compiler_params=_cparams(("arbitrary",)),
        name=name,
    )(*([p] * len(in_specs)), *[t.arr for t in consts])


def _rwkv_post_kernel(of_ref, ob_ref, bf_ref, bb_ref, g_ref, ln_ref, seg_ref, o_ref):
    seg = seg_ref[...]
    o = of_ref[...] + ob_ref[...]
    mean = _seg_sum(o, seg) * (1.0 / RWKV_HEAD)
    oc = o - mean
    var = _seg_sum(oc * oc, seg) * (1.0 / RWKV_HEAD)
    y = oc * lax.rsqrt(var + RWKV_LN_EPS) * ln_ref[0:1] + ln_ref[1:2] + bf_ref[...] + bb_ref[...]
    o_ref[...] = (y * g_ref[...]).astype(o_ref.dtype)


def rwkv_post(o_f, o_b, bo_f, bo_b, g, ln, *, tm=768, name="rwkv_post"):
    l, w = o_f.shape
    tm = math.gcd(l, tm)
    row = pl.BlockSpec((tm, w), lambda i: (i, 0))
    ln = _tab(ln)
    seg = jnp.asarray(np.kron(np.eye(RWKV_HEADS, dtype=np.float32), np.ones((RWKV_HEAD, RWKV_HEAD), np.float32)), BF16)
    return pl.pallas_call(
        _rwkv_post_kernel,
        out_shape=jax.ShapeDtypeStruct((l, w), BF16),
        grid=(l // tm,),
        in_specs=[row] * 5 + [ln.spec, pl.BlockSpec((w, w), lambda i: (0, 0))],
        out_specs=row,
        compiler_params=_cparams(("parallel",)),
        name=name,
    )(o_f, o_b, bo_f, bo_b, g, ln.arr, seg)


@functools.lru_cache(maxsize=None)
def _rope_lane_tables(m, n):
    rows = n // GRID_W
    row = np.repeat(np.arange(rows, dtype=np.float32), GRID_W)
    col = np.tile(np.arange(GRID_W, dtype=np.float32), rows)
    n_freq = MLA_ROPE // 4
    freqs = (np.float32(ROPE_BASE) ** (-np.arange(n_freq, dtype=np.float32) / np.float32(n_freq))).astype(np.float32)
    ang = np.stack([row[:, None] * freqs, col[:, None] * freqs], axis=1).astype(np.float32)
    cos64 = np.repeat(np.cos(ang), 2, axis=1).reshape(n, MLA_ROPE)
    sin64 = (np.repeat(np.sin(ang), 2, axis=1) * np.array([-1.0, 1.0, -1.0, 1.0])[None, :, None]).reshape(n, MLA_ROPE)
    cos_t = np.tile(np.concatenate([np.ones((m, MLA_ROPE)), cos64], axis=0), (1, MLA_HEADS)).astype(np.float32)
    sin_t = np.tile(np.concatenate([np.zeros((m, MLA_ROPE)), sin64], axis=0), (1, MLA_HEADS)).astype(np.float32)
    return cos_t, sin_t


def _rope_lanes(x, cos_t, sin_t):
    n = x.shape[1]
    lane = lax.broadcasted_iota(jnp.int32, x.shape, 1)
    swapped = jnp.where(lane % 32 < 16, pltpu.roll(x, n - 16, 1), pltpu.roll(x, 16, 1))
    return x * cos_t + swapped * sin_t


def _mla_prep_kernel(cq_ref, ckv_ref, kr_ref, cos_ref, sin_ref, qn_ref, kvn_ref, wq_ref, wkv_ref,
                     q_ref, k_ref, v_ref):
    cq = cq_ref[...]
    ms = jnp.sum(cq * cq, axis=-1, keepdims=True) * (1.0 / MLA_Q_LORA)
    q = _dot(cq * lax.rsqrt(ms + EPS) * qn_ref[...], wq_ref[...])
    ckv = ckv_ref[...]
    kvn = ckv * lax.rsqrt(jnp.mean(ckv * ckv, axis=-1, keepdims=True) + EPS) * kvn_ref[...]
    kv = _dot(kvn, wkv_ref[...])
    cos_t, sin_t = cos_ref[...], sin_ref[...]
    hn = MLA_HEADS * MLA_NOPE
    scale = (MLA_NOPE + MLA_ROPE) ** -0.5 * math.log2(math.e)
    q_rope = _rope_lanes(q[:, hn:], cos_t, sin_t)
    k_rope = _rope_lanes(kr_ref[...], cos_t[:, :128], sin_t[:, :128])
    k_rope_hi = pltpu.roll(k_rope, MLA_ROPE, 1)
    lane = lax.broadcasted_iota(jnp.int32, k_rope.shape, 1)
    ones_col = jnp.where(lane == 0, 1.0, 0.0)
    for h in range(MLA_HEADS):
        pair = q_rope[:, (h // 2) * 128:(h // 2 + 1) * 128]
        q_ref[h] = (jnp.concatenate([q[:, h * MLA_NOPE:(h + 1) * MLA_NOPE], pair], axis=1) * scale).astype(q_ref.dtype)
        k_ref[h] = jnp.concatenate([kv[:, h * MLA_NOPE:(h + 1) * MLA_NOPE], k_rope if h % 2 == 0 else k_rope_hi],
                                   axis=1).astype(k_ref.dtype)
        v_ref[h] = jnp.concatenate([kv[:, hn + h * MLA_V:hn + (h + 1) * MLA_V], ones_col], axis=1).astype(v_ref.dtype)


def mla_tables(q_norm, w_q_up, kv_norm, w_kv_up):
    nl = w_q_up.shape[0]
    dq = MLA_NOPE + MLA_ROPE
    pad = PACKED["cq"][3] - MLA_Q_LORA
    wq = w_q_up.reshape(nl, MLA_Q_LORA, MLA_HEADS, dq)
    wq = jnp.concatenate([wq[..., :MLA_NOPE].reshape(nl, MLA_Q_LORA, -1), wq[..., MLA_NOPE:].reshape(nl, MLA_Q_LORA, -1)],
                         axis=2)
    wq = jnp.pad(wq, ((0, 0), (0, pad), (0, 0)))
    qn = jnp.pad(q_norm, ((0, 0), (0, pad)))[:, None, :]
    wkv = w_kv_up.reshape(nl, MLA_KV_LORA, MLA_HEADS, MLA_NOPE + MLA_V)
    wkv = jnp.concatenate([wkv[..., :MLA_NOPE].reshape(nl, MLA_KV_LORA, -1), wkv[..., MLA_NOPE:].reshape(nl, MLA_KV_LORA, -1)],
                          axis=2)
    return qn, kv_norm[:, None, :], wq, wkv


def mla_prep(p, m, tables, *, tm=256, name="mla_prep"):
    l = p.shape[0]
    cos_t, sin_t = (jnp.asarray(t) for t in _rope_lane_tables(m, l - m))
    tabs = [_tab(t) for t in tables]

    def col(nm):
        _, _, off, wp = PACKED[nm]
        return pl.BlockSpec((tm, wp), lambda i: (i, off // wp))

    row = lambda wd: pl.BlockSpec((tm, wd), lambda i: (i, 0))
    out_spec = pl.BlockSpec((MLA_HEADS, tm, 256), lambda i: (0, i, 0))
    return pl.pallas_call(
        _mla_prep_kernel,
        out_shape=[jax.ShapeDtypeStruct((MLA_HEADS, l, 256), BF16)] * 3,
        grid=(l // tm,),
        in_specs=[col("cq"), col("ckv"), col("k_rope"), row(256), row(256)] + [t.spec for t in tabs],
        out_specs=[out_spec] * 3,
        compiler_params=_cparams(("parallel",)),
        name=name,
    )(p, p, p, cos_t, sin_t, *[t.arr for t in tabs])


def mla_mixer(p, m, tables):
    qh, kh, vh = mla_prep(p, m, tables)
    o_all = flash_attention(qh, kh, vh, dv=MLA_V, tq=1408, name="mla_attn_lat")
    return flash_attention(qh, kh, vh, dv=MLA_V, tq=m, lq=m, lk=m, into=o_all, name="mla_attn_ctx")


def _pad_rows(t, before, total):
    cfg = [(0, 0)] * t.ndim
    cfg[-2] = (before, total - before - t.shape[-2])
    return jnp.pad(t, cfg)


def rwkv_tables(mu, w0, w_up, a0, a_up, k_k, k_a, u, g_up, ln_g, ln_b):
    zeros = jnp.zeros_like(w0)
    vec = jnp.stack([w0, a0, k_k, k_a, u, zeros, zeros, zeros], axis=2)
    wup = jnp.stack([_pad_rows(w_up[:, d], d * RWKV_W_LORA, 128) for d in range(2)], axis=1)
    aup = jnp.stack([_pad_rows(a_up[:, d], d * RWKV_A_LORA, 128) for d in range(2)], axis=1)
    ln = jnp.stack([ln_g, ln_b] + [jnp.zeros_like(ln_g)] * 6, axis=1)
    return (mu[:, :, :3 * GROUP_W], mu[:, :, 3 * GROUP_W:], vec, wup, aup, g_up), ln


def rwkv7_mixer(p, m, params, ln):
    outs = rwkv_scan(p, PACKED["rwkv_r"][2], PACKED["w_dn"][2], params, n_ctx=m)
    return rwkv_post(*outs, ln)


PACKED = {}


def _build_packed():
    orig = dict(cq=(0, MLA_Q_LORA), ckv=(MLA_Q_LORA, MLA_KV_LORA), k_rope=(MLA_Q_LORA + MLA_KV_LORA, MLA_ROPE))
    b = MLA_COLS
    for i, nm in enumerate(("rwkv_r", "rwkv_k", "rwkv_v")):
        orig[nm] = (b + i * GROUP_W, GROUP_W)
    b += 3 * GROUP_W
    orig.update(w_dn=(b, 2 * RWKV_W_LORA), a_dn=(b + 2 * RWKV_W_LORA, 2 * RWKV_A_LORA),
                g_dn=(b + 2 * RWKV_W_LORA + 2 * RWKV_A_LORA, RWKV_G_LORA))
    b = IN_SPLITS[1]
    orig.update(gla_q=(b, GLA_KD), gla_k=(b + GLA_KD, GLA_KD), gla_v=(b + 2 * GLA_KD, GROUP_W),
                gla_a=(b + GLA_SPLITS[2], 2 * GLA_GATE_RANK), gla_r=(b + GLA_SPLITS[3], GROUP_W))
    b = IN_SPLITS[2]
    for i, nm in enumerate(("hgrn_q", "hgrn_f0", "hgrn_f1", "hgrn_i", "hgrn_g")):
        orig[nm] = (b + i * GROUP_W, GROUP_W)
    order = [(nm, 512) for nm in ("rwkv_r", "rwkv_k", "rwkv_v", "gla_v", "gla_r", "hgrn_q", "hgrn_f0", "hgrn_f1",
                                  "hgrn_i", "hgrn_g", "cq")]
    orig["pad"] = (0, 0)
    order += [(nm, 256) for nm in ("gla_q", "gla_k")]
    order += [(nm, 128) for nm in ("w_dn", "a_dn", "g_dn", "gla_a", "k_rope", "pad")]
    order += [("ckv", 256)]
    off = 0
    for nm, wp in order:
        PACKED[nm] = (orig[nm][0], orig[nm][1], off, wp)
        off += wp
    return off


N_PACKED = _build_packed()


def pack_w_in(w):
    pieces = []
    for o, wd, _, wp in PACKED.values():
        pieces.append(w[..., o:o + wd])
        if wp > wd:
            pieces.append(jnp.zeros(w.shape[:-1] + (wp - wd,), w.dtype))
    return jnp.concatenate(pieces, axis=-1)


def _pcol(p, nm):
    _, wd, off, _ = PACKED[nm]
    return p[:, off:off + wd]


def _blk(nm):
    return (PACKED[nm][2], PACKED[nm][3])


def gla_tables(a_up, a_bias):
    return jnp.stack([jnp.concatenate([_pad_rows(a_up[:, d], d * GLA_GATE_RANK, 128),
                                       _pad_rows(a_bias[:, d][:, None, :], 0, 8)], axis=1) for d in range(2)], axis=1)


def hgrn_tables(lb_all):
    lb = jnp.swapaxes(lb_all, 0, 1)
    zeros = jnp.zeros_like(lb)
    return jnp.stack([jnp.log(lb), jnp.log1p(-lb), 1.0 - lb] + [zeros] * 5, axis=2)


def gla_mixer(p, m, par, g_norm):
    cols = [[_blk("gla_q"), _blk("gla_k"), _blk("gla_v"), _blk("gla_a")]] * 2
    o_f, o_b = gla_scan(p, cols, par, mode="gla", heads=GLA_HEADS, dk=GLA_DK, dv=GLA_DV, pack=4, n_ctx=m,
                        name="gla_scan")
    return mix_post(o_f, o_b, p, PACKED["gla_r"][2], g_norm, heads=GLA_HEADS, dv=GLA_DV, name="gla_post")


def hgrn2_mixer(p, m, par, g_norm):
    cols = [[_blk("hgrn_q"), _blk("hgrn_f%d" % d), _blk("hgrn_i")] for d in range(2)]
    o_f, o_b = gla_scan(p, cols, par, mode="hgrn", heads=HGRN_HEADS, dk=HGRN_EXPAND, dv=HGRN_DV, pack=2, n_ctx=m,
                        name="hgrn_scan")
    return mix_post(o_f, o_b, p, PACKED["hgrn_g"][2], g_norm, heads=HGRN_HEADS, dv=HGRN_DV, name="hgrn_post")


def dense_ffn(xs, h2, w1, w3, w2, j, gates2, m):
    act = pmatmul(h2, w1, (j,), w3=w3, tm=1408, tn=512, out_dtype=BF16, name="ffn_up")
    return pmatmul(act, w2, (j,), tm=704, tn=512, res=xs, gates=gates2, m_ctx=m, name="ffn_down")


def _route(top_idx, weights):
    n_pairs = top_idx.size
    r = n_pairs + N_EXPERTS * MOE_TILE
    n_tiles = r // MOE_TILE
    e_flat = top_idx.reshape(-1)
    onehot = (e_flat[:, None] == jnp.arange(N_EXPERTS)[None, :]).astype(jnp.int32)
    csum = jnp.cumsum(onehot, axis=0)
    counts = csum[-1]
    padded = (counts + MOE_TILE - 1) // MOE_TILE * MOE_TILE
    ends = jnp.cumsum(padded)
    dest = jnp.sum(onehot * (csum - 1 + (ends - padded)[None, :]), axis=1)
    put = dict(mode="promise_in_bounds", unique_indices=True)
    src_token = jnp.zeros((r,), jnp.int32).at[dest].set(jnp.arange(n_pairs, dtype=jnp.int32) // TOP_K, **put)
    row_gate = jnp.zeros((r,), F32).at[dest].set(weights.reshape(-1), **put)
    tile_start = jnp.arange(n_tiles, dtype=jnp.int32) * MOE_TILE
    tile_expert = jnp.minimum(jnp.sum(tile_start[:, None] >= ends[None, :], axis=1), N_EXPERTS - 1)
    meta = jnp.concatenate([tile_expert.astype(jnp.int32), (ends[-1:] // MOE_TILE).astype(jnp.int32)])
    return src_token, row_gate, dest.reshape(top_idx.shape), meta


def _combine_kernel(x_ref, y_ref, gate_ref, o_ref, *, m_ctx, tm, d):
    i = pl.program_id(0)
    rows = i * tm + lax.broadcasted_iota(jnp.int32, (tm, d), 0)
    g = jnp.where(rows < m_ctx, gate_ref[0:1, :], gate_ref[1:2, :])
    o_ref[...] = x_ref[...] + g * (y_ref[0].astype(F32) + y_ref[1].astype(F32))


def moe_combine(xs, y_tok, gates2, *, m_ctx, tm=256, name="moe_combine"):
    l, d = xs.shape
    gates2 = _tab(gates2)
    return pl.pallas_call(
        functools.partial(_combine_kernel, m_ctx=m_ctx, tm=tm, d=d),
        out_shape=jax.ShapeDtypeStruct((l, d), F32),
        grid=(l // tm,),
        in_specs=[pl.BlockSpec((tm, d), lambda i: (i, 0)), pl.BlockSpec((TOP_K, tm, d), lambda i: (0, i, 0)),
                  gates2.spec],
        out_specs=pl.BlockSpec((tm, d), lambda i: (i, 0)),
        compiler_params=_cparams(("parallel",)),
        name=name,
    )(xs, y_tok, gates2.arr)


def moe_ffn(xs, h2, logits, w1, w3, w2, j, gates2, m):
    l, d = h2.shape
    top_vals, top_idx = lax.top_k(logits[:, :N_EXPERTS], TOP_K)
    weights = jax.nn.softmax(top_vals, axis=-1)
    src_token, row_gate, dest, meta = _route(top_idx, weights)
    x_sorted = jnp.concatenate([h2, h2, h2], axis=0).at[src_token].get(mode="promise_in_bounds")
    act = gmatmul(meta, x_sorted, w1, j, w3=w3, tn=512, out_dtype=BF16, rowscale=row_gate[:, None], name="moe_up")
    y = gmatmul(meta, act, w2, j, tn=512, out_dtype=BF16, name="moe_down")
    y_tok = y.at[dest.T.reshape(-1)].get(mode="promise_in_bounds").reshape(TOP_K, l, d)
    return moe_combine(xs, y_tok, gates2, m_ctx=m)


def kernel(x, c, ctx, c_ctx, norm1_g, norm2_g, w_mod, b_mod, w_in, w_out, mla_q_norm, mla_w_q_up, mla_kv_norm, mla_w_kv_up, rwkv_mu, rwkv_w0, rwkv_w_up, rwkv_a0, rwkv_a_up, rwkv_k_k, rwkv_k_a, rwkv_u, rwkv_g_up, rwkv_ln_g, rwkv_ln_b, gla_a_up, gla_a_bias, gla_norm, hgrn_lb, hgrn_norm, ffn_w1, ffn_w3, ffn_w2, moe_router, moe_w1, moe_w3, moe_w2, final_norm_g):
    m, n, d = ctx.shape[1], x.shape[1], x.shape[2]
    lb_all = jnp.cumsum(jax.nn.softmax(hgrn_lb.astype(F32), axis=1), axis=1)
    lb_all = lb_all - lb_all[:, :1]
    xs = jnp.concatenate([ctx[0], x[0]], axis=0)

    cvec = jnp.zeros((16, d), F32).at[0].set(jax.nn.silu(c[0])).at[1].set(jax.nn.silu(c_ctx))
    mod = jnp.stack([pmatmul(cvec, w_mod, (l,), tm=16, tn=2048, name="mod")[:2] for l in range(DEPTH)]) + b_mod[:, None, :]
    mods = jnp.swapaxes(mod[:, ::-1].reshape(DEPTH, 2, 6, d), 1, 2)
    zmods = jnp.zeros((2, d), F32)
    g1, g2 = norm1_g[:, None, :], norm2_g[:, None, :]
    w_in_p = pack_w_in(w_in)
    mla_tabs = mla_tables(mla_q_norm, mla_w_q_up, mla_kv_norm, mla_w_kv_up)
    rwkv_tabs, rwkv_ln = rwkv_tables(rwkv_mu, rwkv_w0, rwkv_w_up, rwkv_a0, rwkv_a_up, rwkv_k_k, rwkv_k_a, rwkv_u,
                                      rwkv_g_up, rwkv_ln_g, rwkv_ln_b)
    gla_par, hgrn_par = gla_tables(gla_a_up, gla_a_bias), hgrn_tables(lb_all)
    gla_g, hgrn_g = gla_norm[:, None, :], hgrn_norm[:, None, :]
    router_p = jnp.pad(moe_router, ((0, 0), (0, 0), (0, 128 - N_EXPERTS)))

    for l in range(DEPTH):
        sh1, sc1, gt1, sh2, sc2, gt2 = (_Tab(mods, (l, i)) for i in range(6))
        h = norm_mod(xs, _Tab(g1, (l,)), sh1, sc1, m_ctx=m, out_dtype=BF16)
        p = pmatmul(h, w_in_p, (l,), tm=1408, tn=1024, name="w_in")
        o = [
            mla_mixer(p, m, [_Tab(t, (l,)) for t in mla_tabs]),
            rwkv7_mixer(p, m, [_Tab(t, (l,)) for t in rwkv_tabs], _Tab(rwkv_ln, (l,))),
            gla_mixer(p, m, _Tab(gla_par, (l,)), _Tab(gla_g, (l,))),
            hgrn2_mixer(p, m, _Tab(hgrn_par, (l,)), _Tab(hgrn_g, (l,))),
        ]
        xs = pmatmul(o, w_out, (l,), tm=1408, tn=512, res=xs, gates=gt1, m_ctx=m, name="w_out")

        j = l // 2
        if l % 2 == 0:
            h2 = norm_mod(xs, _Tab(g2, (l,)), sh2, sc2, m_ctx=m, out_dtype=BF16)
            xs = dense_ffn(xs, h2, ffn_w1, ffn_w3, ffn_w2, j, gt2, m)
        else:
            h2, logits = norm_mod(xs, _Tab(g2, (l,)), sh2, sc2, m_ctx=m, out_dtype=BF16, router=_Tab(router_p, (j,)),
                                  name="norm_mod_router")
            xs = moe_ffn(xs, h2, logits, moe_w1, moe_w3, moe_w2, j, gt2, m)
    out = norm_mod(xs, final_norm_g[None, :], zmods, zmods, m_ctx=m, out_dtype=F32, skip_rows=m, name="final_norm")
    return out[None]
```

```python
import functools
import math

import numpy as np
import jax
import jax.numpy as jnp
from jax import lax
from jax.experimental import pallas as pl
from jax.experimental.pallas import tpu as pltpu

F32 = jnp.float32
BF16 = jnp.bfloat16

DEPTH = 4
GRID_W = 64
EPS = 1e-6
GROUP_W = 512
MLA_HEADS, MLA_NOPE, MLA_ROPE, MLA_V = 4, 128, 64, 128
MLA_Q_LORA, MLA_KV_LORA = 384, 256
ROPE_BASE = 10000.0
RWKV_HEADS, RWKV_HEAD = 8, 64
RWKV_W_LORA, RWKV_A_LORA, RWKV_G_LORA = 64, 64, 128
RWKV_LN_EPS = 64e-5
GLA_HEADS, GLA_DK, GLA_DV = 4, 64, 128
GLA_GATE_RANK, GLA_GATE_NORM = 16, 16.0
HGRN_HEADS, HGRN_EXPAND, HGRN_DV = 4, 128, 128
CHUNK = 64
N_EXPERTS, TOP_K = 8, 2
MLA_COLS = MLA_Q_LORA + MLA_KV_LORA + MLA_ROPE
RWKV_COLS = 3 * GROUP_W + 2 * RWKV_W_LORA + 2 * RWKV_A_LORA + RWKV_G_LORA
GLA_KD = GLA_HEADS * GLA_DK
GLA_COLS = 2 * GLA_KD + GROUP_W + 2 * GLA_GATE_RANK + GROUP_W
IN_SPLITS = (MLA_COLS, MLA_COLS + RWKV_COLS, MLA_COLS + RWKV_COLS + GLA_COLS)
RWKV_SPLITS = (GROUP_W, 2 * GROUP_W, 3 * GROUP_W, 3 * GROUP_W + 2 * RWKV_W_LORA,
               3 * GROUP_W + 2 * RWKV_W_LORA + 2 * RWKV_A_LORA)
GLA_SPLITS = (GLA_KD, 2 * GLA_KD, 2 * GLA_KD + GROUP_W, 2 * GLA_KD + GROUP_W + 2 * GLA_GATE_RANK)

V7X_VMEM_LIMIT = 56 * 1024 * 1024
V7X_MXU = 256
N_LEVELS = 6
RWKV_PACK = V7X_MXU // RWKV_HEAD
MOE_TILE = 512


def _cparams(sem, vmem=V7X_VMEM_LIMIT):
    return pltpu.CompilerParams(dimension_semantics=sem, vmem_limit_bytes=vmem)


class _Tab:
    def __init__(self, arr, idx=()):
        self.arr, self.idx = arr, tuple(idx)

    @property
    def shape(self):
        return self.arr.shape[len(self.idx):]

    @property
    def spec(self):
        idx, rest = self.idx, self.shape
        return pl.BlockSpec((None,) * len(idx) + rest, lambda *_: idx + (0,) * len(rest))


def _tab(a):
    return a if isinstance(a, _Tab) else _Tab(a)


def _dot(a, b):
    return jnp.dot(a.astype(BF16), b.astype(BF16), preferred_element_type=F32)


def _dot_nt(a, b):
    return lax.dot_general(a.astype(BF16), b.astype(BF16), (((1,), (1,)), ((), ())), preferred_element_type=F32)


def _split2(x):
    hi = x.astype(BF16)
    lo = (x - hi.astype(F32)).astype(BF16)
    return hi, lo


def _dot_exact_lhs(m_bf16, x):
    hi, lo = _split2(x)
    return (jnp.dot(m_bf16, hi, preferred_element_type=F32) + jnp.dot(m_bf16, lo, preferred_element_type=F32))


def _mm_kernel(*refs, n_x, n_w, has_res, m_ctx, tm, cast_w):
    it = iter(refs)
    x_refs = [next(it) for _ in range(n_x)]
    w_refs = [next(it) for _ in range(n_w)]
    res_ref = next(it) if has_res else None
    gate_ref = next(it) if has_res else None
    o_ref = next(it)
    wb_refs = [next(it) for _ in range(n_w)] if cast_w else w_refs
    i = pl.program_id(1)

    if cast_w:
        @pl.when(i == 0)
        def _():
            for w_ref, wb_ref in zip(w_refs, wb_refs):
                wb_ref[...] = w_ref[...].astype(BF16)
    x = jnp.concatenate([x_ref[...].astype(BF16) for x_ref in x_refs], axis=1)

    acc = jnp.dot(x, wb_refs[0][...], preferred_element_type=F32)
    if n_w == 2:
        acc3 = jnp.dot(x, wb_refs[1][...], preferred_element_type=F32)
        acc = acc * jax.nn.sigmoid(acc) * acc3
    if has_res:
        rows = i * tm + lax.broadcasted_iota(jnp.int32, acc.shape, 0)
        g = jnp.where(rows < m_ctx, gate_ref[0:1, :], gate_ref[1:2, :])
        acc = res_ref[...] + g * acc
    o_ref[...] = acc.astype(o_ref.dtype)


def pmatmul(x, w, widx=(), *, w3=None, tm, tn, out_dtype=F32, res=None, gates=None, m_ctx=0, name="mm"):
    xs = list(x) if isinstance(x, (list, tuple)) else [x]
    m = xs[0].shape[0]
    k = sum(xi.shape[1] for xi in xs)
    n = w.shape[-1]
    tm = math.gcd(m, tm)
    assert w.shape[-2] == k and tm % 16 == 0
    nj, ni = pl.cdiv(n, tn), m // tm
    lead = (None,) * len(widx)
    w_spec = pl.BlockSpec(lead + (k, tn), lambda j, i: tuple(widx) + (0, j))
    ws = [w] if w3 is None else [w, w3]
    in_specs = [pl.BlockSpec((tm, xi.shape[1]), lambda j, i: (i, 0)) for xi in xs] + [w_spec] * len(ws)
    args = xs + ws
    if res is not None:
        gates = _tab(gates)
        gi = gates.idx
        in_specs += [pl.BlockSpec((tm, tn), lambda j, i: (i, j)),
                     pl.BlockSpec((None,) * len(gi) + (2, tn), lambda j, i: gi + (0, j))]
        args += [res, gates.arr]
    cast_w = w.dtype != BF16
    kern = functools.partial(_mm_kernel, n_x=len(xs), n_w=len(ws), has_res=res is not None, m_ctx=m_ctx, tm=tm,
                             cast_w=cast_w)
    return pl.pallas_call(
        kern,
        out_shape=jax.ShapeDtypeStruct((m, n), out_dtype),
        grid=(nj, ni),
        in_specs=in_specs,
        out_specs=pl.BlockSpec((tm, tn), lambda j, i: (i, j)),
        scratch_shapes=[pltpu.VMEM((k, tn), BF16) for _ in ws] if cast_w else [],
        compiler_params=_cparams(("arbitrary", "arbitrary")),
        name=name,
    )(*args)


def _gmm_kernel(meta_ref, x_ref, *refs, n_w, has_rowscale, n_tiles):
    it = iter(refs)
    w_refs = [next(it) for _ in range(n_w)]
    rs_ref = next(it) if has_rowscale else None
    o_ref = next(it)
    wb_refs = [next(it) for _ in range(n_w)]
    t = pl.program_id(1)
    e = meta_ref[t]
    e_prev = meta_ref[jnp.maximum(t - 1, 0)]

    @pl.when((t == 0) | (e != e_prev))
    def _():
        for w_ref, wb_ref in zip(w_refs, wb_refs):
            wb_ref[...] = w_ref[...].astype(BF16)

    @pl.when(t < meta_ref[n_tiles])
    def _():
        x = x_ref[...].astype(BF16)
        acc = jnp.dot(x, wb_refs[0][...], preferred_element_type=F32)
        if n_w == 2:
            acc3 = jnp.dot(x, wb_refs[1][...], preferred_element_type=F32)
            acc = acc * jax.nn.sigmoid(acc) * acc3
        if has_rowscale:
            acc = acc * rs_ref[...]
        o_ref[...] = acc.astype(o_ref.dtype)

    @pl.when(t >= meta_ref[n_tiles])
    def _():
        o_ref[...] = jnp.zeros(o_ref.shape, o_ref.dtype)


def gmatmul(meta, x, w, jl, *, w3=None, tn, out_dtype, rowscale=None, name):
    r, k = x.shape
    n = w.shape[-1]
    n_tiles = r // MOE_TILE
    ws = [w] if w3 is None else [w, w3]
    w_spec = pl.BlockSpec((None, None, k, tn), lambda j, t, mr: (jl, mr[t], 0, j))
    in_specs = [pl.BlockSpec((MOE_TILE, k), lambda j, t, mr: (t, 0))] + [w_spec] * len(ws)
    args = [x] + ws
    if rowscale is not None:
        in_specs.append(pl.BlockSpec((MOE_TILE, 1), lambda j, t, mr: (t, 0)))
        args.append(rowscale)
    return pl.pallas_call(
        functools.partial(_gmm_kernel, n_w=len(ws), has_rowscale=rowscale is not None, n_tiles=n_tiles),
        out_shape=jax.ShapeDtypeStruct((r, n), out_dtype),
        grid_spec=pltpu.PrefetchScalarGridSpec(
            num_scalar_prefetch=1,
            grid=(n // tn, n_tiles),
            in_specs=in_specs,
            out_specs=pl.BlockSpec((MOE_TILE, tn), lambda j, t, mr: (t, j)),
            scratch_shapes=[pltpu.VMEM((k, tn), BF16) for _ in ws]),
        compiler_params=_cparams(("arbitrary", "arbitrary")),
        name=name,
    )(meta, *args)


def _norm_kernel(x_ref, g_ref, sh_ref, sc_ref, *rest, m_ctx, tm, with_router, skip_blocks):
    i = pl.program_id(0) + skip_blocks
    x = x_ref[...]
    y = x * lax.rsqrt(jnp.mean(x * x, axis=-1, keepdims=True) + EPS) * g_ref[...]
    rows = i * tm + lax.broadcasted_iota(jnp.int32, x.shape, 0)
    is_ctx = rows < m_ctx
    sc = jnp.where(is_ctx, sc_ref[0:1, :], sc_ref[1:2, :])
    sh = jnp.where(is_ctx, sh_ref[0:1, :], sh_ref[1:2, :])
    h = y * (1.0 + sc) + sh
    if with_router:
        r_ref, o_ref, logit_ref = rest
        logit_ref[...] = jnp.dot(h, r_ref[...], preferred_element_type=F32, precision=lax.Precision.HIGHEST)
    else:
        (o_ref,) = rest
    o_ref[...] = h.astype(o_ref.dtype)


def norm_mod(x, g, shift2, scale2, *, m_ctx, out_dtype, router=None, skip_rows=0, tm=768, name="norm_mod"):
    m_in, d = x.shape
    m = m_in - skip_rows
    tm = math.gcd(math.gcd(m, tm), skip_rows) if skip_rows else math.gcd(m, tm)
    skip_blocks = skip_rows // tm
    row = pl.BlockSpec((tm, d), lambda i: (i, 0))
    tabs = [_tab(g), _tab(shift2), _tab(scale2)]
    in_specs = [pl.BlockSpec((tm, d), lambda i: (i + skip_blocks, 0))] + [t.spec for t in tabs]
    args = [x] + [t.arr for t in tabs]
    out_shape = jax.ShapeDtypeStruct((m, d), out_dtype)
    out_specs = row
    if router is not None:
        router = _tab(router)
        in_specs.append(router.spec)
        args.append(router.arr)
        out_shape = [out_shape, jax.ShapeDtypeStruct((m, router.shape[1]), F32)]
        out_specs = [row, pl.BlockSpec((tm, router.shape[1]), lambda i: (i, 0))]
    return pl.pallas_call(
        functools.partial(_norm_kernel, m_ctx=m_ctx, tm=tm, with_router=router is not None, skip_blocks=skip_blocks),
        out_shape=out_shape,
        grid=(m // tm,),
        in_specs=in_specs,
        out_specs=out_specs,
        compiler_params=_cparams(("parallel",)),
        name=name,
    )(*args)


def _attn_kernel(q_ref, k_ref, v_ref, o_ref, *, tk, n_kv, dv):
    q = q_ref[...]
    m = acc = None
    for j in range(n_kv):
        kb = k_ref[j * tk:(j + 1) * tk, :]
        vb = v_ref[j * tk:(j + 1) * tk, :]
        s = lax.dot_general(q, kb, (((1,), (1,)), ((), ())), preferred_element_type=F32)
        m_blk = jnp.max(s, axis=-1, keepdims=True)
        m_new = m_blk if j == 0 else jnp.maximum(m, m_blk)
        pv = jnp.dot(jnp.exp2(s - m_new).astype(BF16), vb, preferred_element_type=F32)
        acc = pv if j == 0 else jnp.exp2(m - m_new) * acc + pv
        m = m_new
    o_ref[...] = (acc[:, :dv] / acc[:, dv:dv + 1]).astype(o_ref.dtype)


def _attn_kv_tile(lk, cap=1408):
    return max(t for t in range(128, cap + 1, 128) if lk % t == 0)


def _attn_into_kernel(q_ref, k_ref, v_ref, prev_ref, o_ref, **kw):
    del prev_ref
    _attn_kernel(q_ref, k_ref, v_ref, o_ref, **kw)


def flash_attention(q, k, v_ext, *, dv, tq, lq=None, lk=None, into=None, name="mla_attn"):
    h, dqk = q.shape[0], q.shape[2]
    lq = q.shape[1] if lq is None else lq
    lk = k.shape[1] if lk is None else lk
    dve = v_ext.shape[2]
    tk = _attn_kv_tile(lk)
    tq = math.gcd(lq, tq)
    kw = dict(tk=tk, n_kv=lk // tk, dv=dv)
    in_specs = [pl.BlockSpec((None, tq, dqk), lambda hh, i: (hh, i, 0)),
                pl.BlockSpec((None, lk, dqk), lambda hh, i: (hh, 0, 0)),
                pl.BlockSpec((None, lk, dve), lambda hh, i: (hh, 0, 0))]
    args = [q, k, v_ext]
    if into is not None:
        in_specs.append(pl.BlockSpec(memory_space=pl.ANY))
        args.append(into)
    return pl.pallas_call(
        functools.partial(_attn_kernel if into is None else _attn_into_kernel, **kw),
        out_shape=jax.ShapeDtypeStruct((lq, h * dv) if into is None else into.shape, BF16),
        grid=(h, lq // tq),
        in_specs=in_specs,
        out_specs=pl.BlockSpec((tq, dv), lambda hh, i: (i, hh)),
        input_output_aliases={} if into is None else {3: 0},
        compiler_params=_cparams(("parallel", "parallel")),
        name=name,
    )(*args)


def _chunk_constants():
    c = CHUNK
    t = np.arange(c)
    tri = (t[None, :] <= t[:, None]).astype(np.float32)
    strict = (t[None, :] < t[:, None]).astype(np.float32)
    eye = np.eye(c, dtype=np.float32)
    seg, off = [], []
    for lv in range(N_LEVELS):
        s = c >> (lv + 1)
        blk = t // s
        same = blk[:, None] == blk[None, :]
        odd = (blk % 2 == 1)[:, None]
        seg.append(np.where(odd, same & (t[None, :] <= t[:, None]), same & (t[None, :] > t[:, None])).astype(np.float32))
        off.append((odd & (blk[None, :] == blk[:, None] - 1)).astype(np.float32))
    seg, off = np.stack(seg), np.stack(off)

    def both(a):
        return np.stack([a, a[..., ::-1, ::-1]])

    return {k: both(v) for k, v in dict(tri=tri, strict=strict, eye=eye, seg=seg, off=off).items()}


_CC = _chunk_constants()


def _chunk_pos(d, c, n_ctx_chunks, n_chunks):
    back = jnp.where(c < n_ctx_chunks, n_ctx_chunks - 1 - c, n_chunks + n_ctx_chunks - 1 - c)
    return jnp.where(d == 0, c, back)


def _log_sigmoid(x):
    return jnp.minimum(x, 0.0) - jnp.log1p(jnp.exp(-jnp.abs(x)))


def _gla_kernel(*refs, mode, dk, dv, pack, groups, n_in):
    ins = [refs[:n_in], refs[n_in:2 * n_in]]
    par_ref, mall_ref, off_ref, eye_ref, hmk_ref, hmv_ref, o0_ref, o1_ref, st_ref = refs[2 * n_in:]
    c = pl.program_id(0)

    @pl.when(c == 0)
    def _():
        st_ref[...] = jnp.zeros(st_ref.shape, F32)

    eye = eye_ref[...]
    hmk = hmk_ref[...]
    hmv = hmv_ref[...]
    wk, wv = pack * dk, pack * dv
    for d, o_ref in enumerate((o0_ref, o1_ref)):
        if mode == "hgrn":
            q_raw, f_raw, v = (r[...] for r in ins[d])
            par = par_ref[d]
            q = q_raw * jax.nn.sigmoid(q_raw)
            g = jnp.logaddexp(par[0:1], par[1:2] + _log_sigmoid(f_raw))
            k = par[2:3] * jax.nn.sigmoid(-f_raw)
        else:
            q_raw, k, v, a_dn = (r[...] for r in ins[d])
            par = par_ref[d]
            q = q_raw * dk ** -0.5
            g = _log_sigmoid(_dot(a_dn, par[0:128]) + par[128:129]) / GLA_GATE_NORM
        e_all = _dot_exact_lhs(mall_ref[d], g)
        bc = e_all[0:CHUNK]
        btot = jnp.sum(g, axis=0, keepdims=True)
        qhat = q * jnp.exp(bc)
        kt = k * jnp.exp(btot - bc)
        ebt = jnp.exp(btot)
        qw, kw = [q], [k]
        for lv in range(N_LEVELS):
            w = jnp.exp(e_all[(lv + 1) * CHUNK:(lv + 2) * CHUNK])
            qw.append(q * w)
            kw.append(k * w)
        for gi in range(groups):
            slk = slice(gi * wk, (gi + 1) * wk)
            slv = slice(gi * wv, (gi + 1) * wv)

            def stk(x):
                return jnp.concatenate([x[:, slk].astype(BF16)] * pack, axis=0) * hmk

            def tile(x):
                return jnp.concatenate([x[:, slk].astype(BF16)] * pack, axis=0)

            att = eye * _dot_nt(stk(qw[0]), tile(kw[0]))
            for lv in range(N_LEVELS):
                att = att + off_ref[d, lv] * _dot_nt(stk(qw[lv + 1]), tile(kw[lv + 1]))
            vs = jnp.concatenate([v[:, slv]] * pack, axis=0) * hmv
            st = st_ref[d, gi]
            o = _dot_nt(stk(qhat), st) + _dot(att, vs)
            o_ref[:, slv] = sum(o[h * CHUNK:(h + 1) * CHUNK] for h in range(pack))
            st_ref[d, gi] = st * ebt[:, slk] + _dot(vs.T, stk(kt))


def gla_scan(p, cols, par, *, mode, heads, dk, dv, pack, n_ctx, name):
    l = p.shape[0]
    groups = heads // pack
    nch, ncc = l // CHUNK, n_ctx // CHUNK
    nb = pack * CHUNK

    def col_spec(d, off, width):
        assert off % width == 0
        return pl.BlockSpec((CHUNK, width), lambda c: (_chunk_pos(d, c, ncc, nch), off // width))

    def const(a):
        return pl.BlockSpec(a.shape, lambda c: (0,) * a.ndim)

    bd = lambda a: np.kron(np.eye(pack, dtype=np.float32), a)
    mall = jnp.asarray(np.concatenate([_CC["tri"][:, None], _CC["seg"]], axis=1).reshape(2, -1, CHUNK), BF16)
    off = jnp.asarray(np.stack([np.stack([bd(_CC["off"][d, lv]) for lv in range(N_LEVELS)]) for d in range(2)]), F32)
    eye = jnp.asarray(np.eye(nb, dtype=np.float32))
    hmk = jnp.asarray(np.kron(np.eye(pack, dtype=np.float32), np.ones((CHUNK, dk), np.float32)), BF16)
    hmv = jnp.asarray(np.kron(np.eye(pack, dtype=np.float32), np.ones((CHUNK, dv), np.float32)))
    consts = [_tab(a) for a in (par, mall, off, eye, hmk, hmv)]
    in_specs = [col_spec(d, o, w) for d in range(2) for (o, w) in cols[d]]
    out_spec = lambda d: pl.BlockSpec((CHUNK, heads * dv), lambda c: (_chunk_pos(d, c, ncc, nch), 0))
    return pl.pallas_call(
        functools.partial(_gla_kernel, mode=mode, dk=dk, dv=dv, pack=pack, groups=groups, n_in=len(cols[0])),
        out_shape=[jax.ShapeDtypeStruct((l, heads * dv), F32)] * 2,
        grid=(nch,),
        in_specs=in_specs + [t.spec for t in consts],
        out_specs=[out_spec(0), out_spec(1)],
        scratch_shapes=[pltpu.VMEM((2, groups, pack * dv, pack * dk), F32)],
        compiler_params=_cparams(("arbitrary",)),
        name=name,
    )(*([p] * len(in_specs)), *[t.arr for t in consts])


def _post_kernel(of_ref, ob_ref, gate_ref, gn_ref, o_ref, *, heads, dv):
    o = of_ref[...] + ob_ref[...]
    gate = gate_ref[...]
    gn = gn_ref[...]
    for h in range(heads):
        sl = slice(h * dv, (h + 1) * dv)
        oh = o[:, sl]
        y = oh * lax.rsqrt(jnp.mean(oh * oh, axis=-1, keepdims=True) + EPS) * gn
        gh = gate[:, sl]
        o_ref[:, sl] = (y * gh * jax.nn.sigmoid(gh)).astype(o_ref.dtype)


def mix_post(o_f, o_b, p, gate_off, g_norm, *, heads, dv, tm=768, name):
    l, w = o_f.shape
    tm = math.gcd(l, tm)
    assert gate_off % w == 0
    row = pl.BlockSpec((tm, w), lambda i: (i, 0))
    g_norm = _tab(g_norm)
    return pl.pallas_call(
        functools.partial(_post_kernel, heads=heads, dv=dv),
        out_shape=jax.ShapeDtypeStruct((l, w), BF16),
        grid=(l // tm,),
        in_specs=[row, row, pl.BlockSpec((tm, w), lambda i: (i, gate_off // w)), g_norm.spec],
        out_specs=row,
        compiler_params=_cparams(("parallel",)),
        name=name,
    )(o_f, o_b, p, g_norm.arr)


def _seg_sum(x, seg_bf16):
    hi, lo = _split2(x)
    return jnp.dot(hi, seg_bf16, preferred_element_type=F32) + jnp.dot(lo, seg_bf16, preferred_element_type=F32)


def _shift_mix_block(x, halo_prev, halo_next, mu, seg_start, seg_end):
    row = lax.broadcasted_iota(jnp.int32, x.shape, 0)
    first = jnp.where(seg_start, 0.0, halo_prev[7:8, :])
    last = jnp.where(seg_end, 0.0, halo_next[0:1, :])
    xp = jnp.where(row == 0, first, pltpu.roll(x, 1, 0))
    xn = jnp.where(row == CHUNK - 1, last, pltpu.roll(x, CHUNK - 1, 0))
    return x + mu[0:1] * (xp - x) + mu[1:2] * (xn - x)


def _rwkv_kernel(*refs, groups, ncc, nch):
    ins = [refs[0:6], refs[6:12]]
    (mu_rkv_ref, mu_lo_ref, vec_ref, wup_ref, aup_ref, gup_ref, seg_ref,
     tri_ref, strict_ref, incl_ref, off_ref, eye_ref, hm_ref,
     o0_ref, o1_ref, bo0_ref, bo1_ref, g_ref, st_ref) = refs[12:]
    c = pl.program_id(0)

    @pl.when(c == 0)
    def _():
        st_ref[...] = jnp.zeros(st_ref.shape, F32)

    hm = hm_ref[...]
    eye = eye_ref[...]
    seg = seg_ref[...]
    w = RWKV_PACK * RWKV_HEAD
    nb = RWKV_PACK * CHUNK
    gw = GROUP_W

    def stack(x):
        return jnp.concatenate([x.astype(BF16)] * RWKV_PACK, axis=0) * hm

    streams = []
    for d, (o_ref, bo_ref) in enumerate(((o0_ref, bo0_ref), (o1_ref, bo1_ref))):
        rkv_ref, rkv_p, rkv_n, lo_ref, lo_p, lo_n = ins[d]
        pos = _chunk_pos(d, c, ncc, nch)
        seg_start = (pos == 0) | (pos == ncc)
        seg_end = (pos == ncc - 1) | (pos == nch - 1)
        rkv = _shift_mix_block(rkv_ref[...], rkv_p[...], rkv_n[...], mu_rkv_ref[...], seg_start, seg_end)
        lora = _shift_mix_block(lo_ref[...], lo_p[...], lo_n[...], mu_lo_ref[...], seg_start, seg_end)
        r, k, v = rkv[:, 0:gw], rkv[:, gw:2 * gw], rkv[:, 2 * gw:3 * gw]
        w_dn, a_dn, g_dn = lora[:, 0:128], lora[:, 128:256], lora[:, 256:384]
        vec = vec_ref[d]
        wl = vec[0:1] + _dot(jnp.tanh(w_dn), wup_ref[d])
        lw = -jnp.exp(-(jnp.maximum(-wl, 0.0) + jnp.log1p(jnp.exp(-jnp.abs(wl)))) - 0.5)
        a = jax.nn.sigmoid(vec[1:2] + _dot(a_dn, aup_ref[d]))
        kk = k * vec[2:3]
        kk = kk * lax.rsqrt(jnp.maximum(_seg_sum(kk * kk, seg), 1e-24))
        k = k * (1.0 + (a - 1.0) * vec[3:4])
        b = kk * a
        bo_ref[...] = _seg_sum(r * k * vec[4:5], seg) * v
        if d == 0:
            g_ref[...] = _dot(jax.nn.sigmoid(g_dn), gup_ref[...])
        bc = _dot_exact_lhs(tri_ref[d], lw)
        ebt = jnp.exp(jnp.sum(lw, axis=0, keepdims=True))
        einv = jnp.exp(-bc)
        khat = kk * jnp.exp(bc - lw)
        rhat = r * jnp.exp(bc)
        ks = k * einv
        bs = b * einv
        for g in range(groups):
            sl = slice(g * w, (g + 1) * w)
            streams.append(dict(
                d=d, g=g, sl=sl, o_ref=o_ref, ebt=ebt[:, sl],
                kr=jnp.concatenate([stack(khat[:, sl]), stack(rhat[:, sl])], axis=0),
                bk=jnp.concatenate([stack(bs[:, sl]), stack(ks[:, sl])], axis=0),
                vs=stack(v[:, sl])))

    for s in streams:
        aa = _dot_nt(s["kr"], s["bk"])
        strict, incl = strict_ref[s["d"]], incl_ref[s["d"]]
        s["akb"] = strict.astype(F32) * aa[:nb, :nb]
        s["akb_b"] = s["akb"].astype(BF16)
        s["arb"] = incl * aa[nb:, :nb].astype(BF16)
        s["ak_v"] = jnp.concatenate([strict * aa[:nb, nb:].astype(BF16), incl * aa[nb:, nb:].astype(BF16)], axis=0)
        s["minv"] = eye - off_ref[s["d"], N_LEVELS - 1].astype(F32) * s["akb"]
    for lv in range(N_LEVELS - 2, -1, -1):
        for s in streams:
            s["minv_b"] = s["minv"].astype(BF16)
            s["t1"] = _dot(off_ref[s["d"], lv] * s["akb_b"], s["minv_b"])
        for s in streams:
            s["minv"] = s["minv"] - _dot(s["minv_b"], s["t1"])
    for s in streams:
        st = st_ref[s["d"], s["g"]]
        from_state = _dot_nt(s["kr"], st)
        from_v = _dot(s["ak_v"], s["vs"])
        u = _dot(s["minv"], from_state[:nb] + from_v[:nb])
        o = from_state[nb:] + from_v[nb:] - _dot(s["arb"], u)
        s["o_ref"][:, s["sl"]] = sum(o[h * CHUNK:(h + 1) * CHUNK] for h in range(RWKV_PACK))
        upd = _dot(s["vs"].T, s["bk"][nb:]) - _dot(u.T, s["bk"][:nb])
        st_ref[s["d"], s["g"]] = (st + upd) * s["ebt"]


def rwkv_scan(p, rkv_off, lora_off, params, *, n_ctx, name="rwkv_scan"):
    l = p.shape[0]
    hw = GROUP_W
    w = RWKV_PACK * RWKV_HEAD
    groups = hw // w
    nch, ncc = l // CHUNK, n_ctx // CHUNK
    nb = RWKV_PACK * CHUNK
    halo = 8
    per_chunk = CHUNK // halo

    def shared(d):
        return pl.BlockSpec((CHUNK, hw), lambda c: (_chunk_pos(d, c, ncc, nch), 0))

    def piece(d, off, width):
        assert off % width == 0
        cb = off // width
        pos = lambda c: _chunk_pos(d, c, ncc, nch)
        return [pl.BlockSpec((CHUNK, width), lambda c: (pos(c), cb)),
                pl.BlockSpec((halo, width), lambda c: (jnp.maximum(pos(c) * per_chunk - 1, 0), cb)),
                pl.BlockSpec((halo, width), lambda c: (jnp.minimum((pos(c) + 1) * per_chunk, l // halo - 1), cb))]

    def const(a):
        return pl.BlockSpec(a.shape, lambda c: (0,) * a.ndim)

    bd = lambda a: np.kron(np.eye(RWKV_PACK, dtype=np.float32), a)
    strict = jnp.asarray(np.stack([bd(_CC["strict"][d]) for d in range(2)]), BF16)
    incl = jnp.asarray(np.stack([bd(_CC["strict"][d] + _CC["eye"][d]) for d in range(2)]), BF16)
    off = jnp.asarray(np.stack([np.stack([bd(_CC["off"][d, lv]) for lv in range(N_LEVELS)]) for d in range(2)]), BF16)
    eye = jnp.asarray(np.eye(nb, dtype=np.float32))
    hm = jnp.asarray(np.kron(np.eye(RWKV_PACK, dtype=np.float32), np.ones((CHUNK, RWKV_HEAD), np.float32)), BF16)
    tri = jnp.asarray(_CC["tri"], BF16)
    seg = jnp.asarray(np.kron(np.eye(RWKV_HEADS, dtype=np.float32), np.ones((RWKV_HEAD, RWKV_HEAD), np.float32)), BF16)
    consts = [_tab(a) for a in list(params) + [seg, tri, strict, incl, off, eye, hm]]
    in_specs = []
    for d in range(2):
        in_specs += piece(d, rkv_off, 3 * hw) + piece(d, lora_off, 3 * 128)
    return pl.pallas_call(
        functools.partial(_rwkv_kernel, groups=groups, ncc=ncc, nch=nch),
        out_shape=[jax.ShapeDtypeStruct((l, hw), F32)] * 5,
        grid=(nch,),
        in_specs=in_specs + [t.spec for t in consts],
        out_specs=[shared(0), shared(1), shared(0), shared(1), shared(0)],
        scratch_shapes=[pltpu.VMEM((2, groups, nb, nb), F32)],
        compiler_params=_cparams(("arbitrary",)),
        name=name,
    )(*([p] * len(in_specs)), *[t.arr for t in consts])


def _rwkv_post_kernel(of_ref, ob_ref, bf_ref, bb_ref, g_ref, ln_ref, seg_ref, o_ref):
    seg = seg_ref[...]
    o = of_ref[...] + ob_ref[...]
    mean = _seg_sum(o, seg) * (1.0 / RWKV_HEAD)
    oc = o - mean
    var = _seg_sum(oc * oc, seg) * (1.0 / RWKV_HEAD)
    y = oc * lax.rsqrt(var + RWKV_LN_EPS) * ln_ref[0:1] + ln_ref[1:2] + bf_ref[...] + bb_ref[...]
    o_ref[...] = (y * g_ref[...]).astype(o_ref.dtype)


def rwkv_post(o_f, o_b, bo_f, bo_b, g, ln, *, tm=768, name="rwkv_post"):
    l, w = o_f.shape
    tm = math.gcd(l, tm)
    row = pl.BlockSpec((tm, w), lambda i: (i, 0))
    ln = _tab(ln)
    seg = jnp.asarray(np.kron(np.eye(RWKV_HEADS, dtype=np.float32), np.ones((RWKV_HEAD, RWKV_HEAD), np.float32)), BF16)
    return pl.pallas_call(
        _rwkv_post_kernel,
        out_shape=jax.ShapeDtypeStruct((l, w), BF16),
        grid=(l // tm,),
        in_specs=[row] * 5 + [ln.spec, pl.BlockSpec((w, w), lambda i: (0, 0))],
        out_specs=row,
        compiler_params=_cparams(("parallel",)),
        name=name,
    )(o_f, o_b, bo_f, bo_b, g, ln.arr, seg)


@functools.lru_cache(maxsize=None)
def _rope_lane_tables(m, n):
    rows = n // GRID_W
    row = np.repeat(np.arange(rows, dtype=np.float32), GRID_W)
    col = np.tile(np.arange(GRID_W, dtype=np.float32), rows)
    n_freq = MLA_ROPE // 4
    freqs = (np.float32(ROPE_BASE) ** (-np.arange(n_freq, dtype=np.float32) / np.float32(n_freq))).astype(np.float32)
    ang = np.stack([row[:, None] * freqs, col[:, None] * freqs], axis=1).astype(np.float32)
    cos64 = np.repeat(np.cos(ang), 2, axis=1).reshape(n, MLA_ROPE)
    sin64 = (np.repeat(np.sin(ang), 2, axis=1) * np.array([-1.0, 1.0, -1.0, 1.0])[None, :, None]).reshape(n, MLA_ROPE)
    cos_t = np.tile(np.concatenate([np.ones((m, MLA_ROPE)), cos64], axis=0), (1, MLA_HEADS)).astype(np.float32)
    sin_t = np.tile(np.concatenate([np.zeros((m, MLA_ROPE)), sin64], axis=0), (1, MLA_HEADS)).astype(np.float32)
    return cos_t, sin_t


def _rope_lanes(x, cos_t, sin_t):
    n = x.shape[1]
    lane = lax.broadcasted_iota(jnp.int32, x.shape, 1)
    swapped = jnp.where(lane % 32 < 16, pltpu.roll(x, n - 16, 1), pltpu.roll(x, 16, 1))
    return x * cos_t + swapped * sin_t


def _mla_prep_kernel(cq_ref, ckv_ref, kr_ref, cos_ref, sin_ref, qn_ref, kvn_ref, wq_ref, wkv_ref,
                     q_ref, k_ref, v_ref):
    cq = cq_ref[...]
    ms = jnp.sum(cq * cq, axis=-1, keepdims=True) * (1.0 / MLA_Q_LORA)
    q = _dot(cq * lax.rsqrt(ms + EPS) * qn_ref[...], wq_ref[...])
    ckv = ckv_ref[...]
    kvn = ckv * lax.rsqrt(jnp.mean(ckv * ckv, axis=-1, keepdims=True) + EPS) * kvn_ref[...]
    kv = _dot(kvn, wkv_ref[...])
    cos_t, sin_t = cos_ref[...], sin_ref[...]
    hn = MLA_HEADS * MLA_NOPE
    scale = (MLA_NOPE + MLA_ROPE) ** -0.5 * math.log2(math.e)
    q_rope = _rope_lanes(q[:, hn:], cos_t, sin_t)
    k_rope = _rope_lanes(kr_ref[...], cos_t[:, :128], sin_t[:, :128])
    k_rope_hi = pltpu.roll(k_rope, MLA_ROPE, 1)
    lane = lax.broadcasted_iota(jnp.int32, k_rope.shape, 1)
    ones_col = jnp.where(lane == 0, 1.0, 0.0)
    for h in range(MLA_HEADS):
        pair = q_rope[:, (h // 2) * 128:(h // 2 + 1) * 128]
        q_ref[h] = (jnp.concatenate([q[:, h * MLA_NOPE:(h + 1) * MLA_NOPE], pair], axis=1) * scale).astype(q_ref.dtype)
        k_ref[h] = jnp.concatenate([kv[:, h * MLA_NOPE:(h + 1) * MLA_NOPE], k_rope if h % 2 == 0 else k_rope_hi],
                                   axis=1).astype(k_ref.dtype)
        v_ref[h] = jnp.concatenate([kv[:, hn + h * MLA_V:hn + (h + 1) * MLA_V], ones_col], axis=1).astype(v_ref.dtype)


def mla_tables(q_norm, w_q_up, kv_norm, w_kv_up):
    nl = w_q_up.shape[0]
    dq = MLA_NOPE + MLA_ROPE
    pad = PACKED["cq"][3] - MLA_Q_LORA
    wq = w_q_up.reshape(nl, MLA_Q_LORA, MLA_HEADS, dq)
    wq = jnp.concatenate([wq[..., :MLA_NOPE].reshape(nl, MLA_Q_LORA, -1), wq[..., MLA_NOPE:].reshape(nl, MLA_Q_LORA, -1)],
                         axis=2)
    wq = jnp.pad(wq, ((0, 0), (0, pad), (0, 0)))
    qn = jnp.pad(q_norm, ((0, 0), (0, pad)))[:, None, :]
    wkv = w_kv_up.reshape(nl, MLA_KV_LORA, MLA_HEADS, MLA_NOPE + MLA_V)
    wkv = jnp.concatenate([wkv[..., :MLA_NOPE].reshape(nl, MLA_KV_LORA, -1), wkv[..., MLA_NOPE:].reshape(nl, MLA_KV_LORA, -1)],
                          axis=2)
    return qn, kv_norm[:, None, :], wq, wkv


def mla_prep(p, m, tables, *, tm=256, name="mla_prep"):
    l = p.shape[0]
    cos_t, sin_t = (jnp.asarray(t) for t in _rope_lane_tables(m, l - m))
    tabs = [_tab(t) for t in tables]

    def col(nm):
        _, _, off, wp = PACKED[nm]
        return pl.BlockSpec((tm, wp), lambda i: (i, off // wp))

    row = lambda wd: pl.BlockSpec((tm, wd), lambda i: (i, 0))
    out_spec = pl.BlockSpec((MLA_HEADS, tm, 256), lambda i: (0, i, 0))
    return pl.pallas_call(
        _mla_prep_kernel,
        out_shape=[jax.ShapeDtypeStruct((MLA_HEADS, l, 256), BF16)] * 3,
        grid=(l // tm,),
        in_specs=[col("cq"), col("ckv"), col("k_rope"), row(256), row(256)] + [t.spec for t in tabs],
        out_specs=[out_spec] * 3,
        compiler_params=_cparams(("parallel",)),
        name=name,
    )(p, p, p, cos_t, sin_t, *[t.arr for t in tabs])


def mla_mixer(p, m, tables):
    qh, kh, vh = mla_prep(p, m, tables)
    o_all = flash_attention(qh, kh, vh, dv=MLA_V, tq=1408, name="mla_attn_lat")
    return flash_attention(qh, kh, vh, dv=MLA_V, tq=m, lq=m, lk=m, into=o_all, name="mla_attn_ctx")


def _pad_rows(t, before, total):
    cfg = [(0, 0)] * t.ndim
    cfg[-2] = (before, total - before - t.shape[-2])
    return jnp.pad(t, cfg)


def rwkv_tables(mu, w0, w_up, a0, a_up, k_k, k_a, u, g_up, ln_g, ln_b):
    zeros = jnp.zeros_like(w0)
    vec = jnp.stack([w0, a0, k_k, k_a, u, zeros, zeros, zeros], axis=2)
    wup = jnp.stack([_pad_rows(w_up[:, d], d * RWKV_W_LORA, 128) for d in range(2)], axis=1)
    aup = jnp.stack([_pad_rows(a_up[:, d], d * RWKV_A_LORA, 128) for d in range(2)], axis=1)
    ln = jnp.stack([ln_g, ln_b] + [jnp.zeros_like(ln_g)] * 6, axis=1)
    return (mu[:, :, :3 * GROUP_W], mu[:, :, 3 * GROUP_W:], vec, wup, aup, g_up), ln


def rwkv7_mixer(p, m, params, ln):
    outs = rwkv_scan(p, PACKED["rwkv_r"][2], PACKED["w_dn"][2], params, n_ctx=m)
    return rwkv_post(*outs, ln)


PACKED = {}


def _build_packed():
    orig = dict(cq=(0, MLA_Q_LORA), ckv=(MLA_Q_LORA, MLA_KV_LORA), k_rope=(MLA_Q_LORA + MLA_KV_LORA, MLA_ROPE))
    b = MLA_COLS
    for i, nm in enumerate(("rwkv_r", "rwkv_k", "rwkv_v")):
        orig[nm] = (b + i * GROUP_W, GROUP_W)
    b += 3 * GROUP_W
    orig.update(w_dn=(b, 2 * RWKV_W_LORA), a_dn=(b + 2 * RWKV_W_LORA, 2 * RWKV_A_LORA),
                g_dn=(b + 2 * RWKV_W_LORA + 2 * RWKV_A_LORA, RWKV_G_LORA))
    b = IN_SPLITS[1]
    orig.update(gla_q=(b, GLA_KD), gla_k=(b + GLA_KD, GLA_KD), gla_v=(b + 2 * GLA_KD, GROUP_W),
                gla_a=(b + GLA_SPLITS[2], 2 * GLA_GATE_RANK), gla_r=(b + GLA_SPLITS[3], GROUP_W))
    b = IN_SPLITS[2]
    for i, nm in enumerate(("hgrn_q", "hgrn_f0", "hgrn_f1", "hgrn_i", "hgrn_g")):
        orig[nm] = (b + i * GROUP_W, GROUP_W)
    order = [(nm, 512) for nm in ("rwkv_r", "rwkv_k", "rwkv_v", "gla_v", "gla_r", "hgrn_q", "hgrn_f0", "hgrn_f1",
                                  "hgrn_i", "hgrn_g", "cq")]
    orig["pad"] = (0, 0)
    order += [(nm, 256) for nm in ("gla_q", "gla_k")]
    order += [(nm, 128) for nm in ("w_dn", "a_dn", "g_dn", "gla_a", "k_rope", "pad")]
    order += [("ckv", 256)]
    off = 0
    for nm, wp in order:
        PACKED[nm] = (orig[nm][0], orig[nm][1], off, wp)
        off += wp
    return off


N_PACKED = _build_packed()


def pack_w_in(w):
    pieces = []
    for o, wd, _, wp in PACKED.values():
        pieces.append(w[..., o:o + wd])
        if wp > wd:
            pieces.append(jnp.zeros(w.shape[:-1] + (wp - wd,), w.dtype))
    return jnp.concatenate(pieces, axis=-1)


def _pcol(p, nm):
    _, wd, off, _ = PACKED[nm]
    return p[:, off:off + wd]


def _blk(nm):
    return (PACKED[nm][2], PACKED[nm][3])


def gla_tables(a_up, a_bias):
    return jnp.stack([jnp.concatenate([_pad_rows(a_up[:, d], d * GLA_GATE_RANK, 128),
                                       _pad_rows(a_bias[:, d][:, None, :], 0, 8)], axis=1) for d in range(2)], axis=1)


def hgrn_tables(lb_all):
    lb = jnp.swapaxes(lb_all, 0, 1)
    zeros = jnp.zeros_like(lb)
    return jnp.stack([jnp.log(lb), jnp.log1p(-lb), 1.0 - lb] + [zeros] * 5, axis=2)


def gla_mixer(p, m, par, g_norm):
    cols = [[_blk("gla_q"), _blk("gla_k"), _blk("gla_v"), _blk("gla_a")]] * 2
    o_f, o_b = gla_scan(p, cols, par, mode="gla", heads=GLA_HEADS, dk=GLA_DK, dv=GLA_DV, pack=4, n_ctx=m,
                        name="gla_scan")
    return mix_post(o_f, o_b, p, PACKED["gla_r"][2], g_norm, heads=GLA_HEADS, dv=GLA_DV, name="gla_post")


def hgrn2_mixer(p, m, par, g_norm):
    cols = [[_blk("hgrn_q"), _blk("hgrn_f%d" % d), _blk("hgrn_i")] for d in range(2)]
    o_f, o_b = gla_scan(p, cols, par, mode="hgrn", heads=HGRN_HEADS, dk=HGRN_EXPAND, dv=HGRN_DV, pack=2, n_ctx=m,
                        name="hgrn_scan")
    return mix_post(o_f, o_b, p, PACKED["hgrn_g"][2], g_norm, heads=HGRN_HEADS, dv=HGRN_DV, name="hgrn_post")


def dense_ffn(xs, h2, w1, w3, w2, j, gates2, m):
    act = pmatmul(h2, w1, (j,), w3=w3, tm=1408, tn=512, out_dtype=BF16, name="ffn_up")
    return pmatmul(act, w2, (j,), tm=704, tn=512, res=xs, gates=gates2, m_ctx=m, name="ffn_down")


def _route(top_idx, weights):
    n_pairs = top_idx.size
    r = n_pairs + N_EXPERTS * MOE_TILE
    n_tiles = r // MOE_TILE
    e_flat = top_idx.reshape(-1)
    onehot = (e_flat[:, None] == jnp.arange(N_EXPERTS)[None, :]).astype(jnp.int32)
    csum = jnp.cumsum(onehot, axis=0)
    counts = csum[-1]
    padded = (counts + MOE_TILE - 1) // MOE_TILE * MOE_TILE
    ends = jnp.cumsum(padded)
    dest = jnp.sum(onehot * (csum - 1 + (ends - padded)[None, :]), axis=1)
    put = dict(mode="promise_in_bounds", unique_indices=True)
    src_token = jnp.zeros((r,), jnp.int32).at[dest].set(jnp.arange(n_pairs, dtype=jnp.int32) // TOP_K, **put)
    row_gate = jnp.zeros((r,), F32).at[dest].set(weights.reshape(-1), **put)
    tile_start = jnp.arange(n_tiles, dtype=jnp.int32) * MOE_TILE
    tile_expert = jnp.minimum(jnp.sum(tile_start[:, None] >= ends[None, :], axis=1), N_EXPERTS - 1)
    meta = jnp.concatenate([tile_expert.astype(jnp.int32), (ends[-1:] // MOE_TILE).astype(jnp.int32)])
    return src_token, row_gate, dest.reshape(top_idx.shape), meta


def _combine_kernel(x_ref, y_ref, gate_ref, o_ref, *, m_ctx, tm, d):
    i = pl.program_id(0)
    rows = i * tm + lax.broadcasted_iota(jnp.int32, (tm, d), 0)
    g = jnp.where(rows < m_ctx, gate_ref[0:1, :], gate_ref[1:2, :])
    o_ref[...] = x_ref[...] + g * (y_ref[0].astype(F32) + y_ref[1].astype(F32))


def moe_combine(xs, y_tok, gates2, *, m_ctx, tm=768, name="moe_combine"):
    l, d = xs.shape
    tm = math.gcd(l, tm)
    gates2 = _tab(gates2)
    return pl.pallas_call(
        functools.partial(_combine_kernel, m_ctx=m_ctx, tm=tm, d=d),
        out_shape=jax.ShapeDtypeStruct((l, d), F32),
        grid=(l // tm,),
        in_specs=[pl.BlockSpec((tm, d), lambda i: (i, 0)), pl.BlockSpec((TOP_K, tm, d), lambda i: (0, i, 0)),
                  gates2.spec],
        out_specs=pl.BlockSpec((tm, d), lambda i: (i, 0)),
        compiler_params=_cparams(("parallel",)),
        name=name,
    )(xs, y_tok, gates2.arr)


def moe_ffn(xs, h2, logits, w1, w3, w2, j, gates2, m):
    l, d = h2.shape
    top_vals, top_idx = lax.top_k(logits[:, :N_EXPERTS], TOP_K)
    weights = jax.nn.softmax(top_vals, axis=-1)
    src_token, row_gate, dest, meta = _route(top_idx, weights)
    x_sorted = h2.at[src_token].get(mode="promise_in_bounds")
    act = gmatmul(meta, x_sorted, w1, j, w3=w3, tn=512, out_dtype=BF16, rowscale=row_gate[:, None], name="moe_up")
    y = gmatmul(meta, act, w2, j, tn=512, out_dtype=BF16, name="moe_down")
    y_tok = y.at[dest.T.reshape(-1)].get(mode="promise_in_bounds").reshape(TOP_K, l, d)
    return moe_combine(xs, y_tok, gates2, m_ctx=m)


def kernel(x, c, ctx, c_ctx, norm1_g, norm2_g, w_mod, b_mod, w_in, w_out, mla_q_norm, mla_w_q_up, mla_kv_norm, mla_w_kv_up, rwkv_mu, rwkv_w0, rwkv_w_up, rwkv_a0, rwkv_a_up, rwkv_k_k, rwkv_k_a, rwkv_u, rwkv_g_up, rwkv_ln_g, rwkv_ln_b, gla_a_up, gla_a_bias, gla_norm, hgrn_lb, hgrn_norm, ffn_w1, ffn_w3, ffn_w2, moe_router, moe_w1, moe_w3, moe_w2, final_norm_g):
    m, n, d = ctx.shape[1], x.shape[1], x.shape[2]
    lb_all = jnp.cumsum(jax.nn.softmax(hgrn_lb.astype(F32), axis=1), axis=1)
    lb_all = lb_all - lb_all[:, :1]
    xs = jnp.concatenate([ctx[0], x[0]], axis=0)

    cvec = jnp.zeros((16, d), F32).at[0].set(jax.nn.silu(c[0])).at[1].set(jax.nn.silu(c_ctx))
    mod = jnp.stack([pmatmul(cvec, w_mod, (l,), tm=16, tn=2048, name="mod")[:2] for l in range(DEPTH)]) + b_mod[:, None, :]
    mods = jnp.swapaxes(mod[:, ::-1].reshape(DEPTH, 2, 6, d), 1, 2)
    zmods = jnp.zeros((2, d), F32)
    g1, g2 = norm1_g[:, None, :], norm2_g[:, None, :]
    w_in_p = pack_w_in(w_in)
    mla_tabs = mla_tables(mla_q_norm, mla_w_q_up, mla_kv_norm, mla_w_kv_up)
    rwkv_tabs, rwkv_ln = rwkv_tables(rwkv_mu, rwkv_w0, rwkv_w_up, rwkv_a0, rwkv_a_up, rwkv_k_k, rwkv_k_a, rwkv_u,
                                      rwkv_g_up, rwkv_ln_g, rwkv_ln_b)
    gla_par, hgrn_par = gla_tables(gla_a_up, gla_a_bias), hgrn_tables(lb_all)
    gla_g, hgrn_g = gla_norm[:, None, :], hgrn_norm[:, None, :]
    router_p = jnp.pad(moe_router, ((0, 0), (0, 0), (0, 128 - N_EXPERTS)))

    for l in range(DEPTH):
        sh1, sc1, gt1, sh2, sc2, gt2 = (_Tab(mods, (l, i)) for i in range(6))
        h = norm_mod(xs, _Tab(g1, (l,)), sh1, sc1, m_ctx=m, out_dtype=BF16)
        p = pmatmul(h, w_in_p, (l,), tm=1408, tn=1024, name="w_in")
        o = [
            mla_mixer(p, m, [_Tab(t, (l,)) for t in mla_tabs]),
            rwkv7_mixer(p, m, [_Tab(t, (l,)) for t in rwkv_tabs], _Tab(rwkv_ln, (l,))),
            gla_mixer(p, m, _Tab(gla_par, (l,)), _Tab(gla_g, (l,))),
            hgrn2_mixer(p, m, _Tab(hgrn_par, (l,)), _Tab(hgrn_g, (l,))),
        ]
        xs = pmatmul(o, w_out, (l,), tm=1408, tn=512, res=xs, gates=gt1, m_ctx=m, name="w_out")

        j = l // 2
        if l % 2 == 0:
            h2 = norm_mod(xs, _Tab(g2, (l,)), sh2, sc2, m_ctx=m, out_dtype=BF16)
            xs = dense_ffn(xs, h2, ffn_w1, ffn_w3, ffn_w2, j, gt2, m)
        else:
            h2, logits = norm_mod(xs, _Tab(g2, (l,)), sh2, sc2, m_ctx=m, out_dtype=BF16, router=_Tab(router_p, (j,)),
                                  name="norm_mod_router")
            xs = moe_ffn(xs, h2, logits, moe_w1, moe_w3, moe_w2, j, gt2, m)
    out = norm_mod(xs, final_norm_g[None, :], zmods, zmods, m_ctx=m, out_dtype=F32, skip_rows=m, name="final_norm")
    return out[None]
```

```python
import functools
import math

import numpy as np
import jax
import jax.numpy as jnp
from jax import lax
from jax.experimental import pallas as pl
from jax.experimental.pallas import tpu as pltpu

F32 = jnp.float32
BF16 = jnp.bfloat16

DEPTH = 4
GRID_W = 64
EPS = 1e-6
GROUP_W = 512
MLA_HEADS, MLA_NOPE, MLA_ROPE, MLA_V = 4, 128, 64, 128
MLA_Q_LORA, MLA_KV_LORA = 384, 256
ROPE_BASE = 10000.0
RWKV_HEADS, RWKV_HEAD = 8, 64
RWKV_W_LORA, RWKV_A_LORA, RWKV_G_LORA = 64, 64, 128
RWKV_LN_EPS = 64e-5
GLA_HEADS, GLA_DK, GLA_DV = 4, 64, 128
GLA_GATE_RANK, GLA_GATE_NORM = 16, 16.0
HGRN_HEADS, HGRN_EXPAND, HGRN_DV = 4, 128, 128
CHUNK = 64
N_EXPERTS, TOP_K = 8, 2
MLA_COLS = MLA_Q_LORA + MLA_KV_LORA + MLA_ROPE
RWKV_COLS = 3 * GROUP_W + 2 * RWKV_W_LORA + 2 * RWKV_A_LORA + RWKV_G_LORA
GLA_KD = GLA_HEADS * GLA_DK
GLA_COLS = 2 * GLA_KD + GROUP_W + 2 * GLA_GATE_RANK + GROUP_W
IN_SPLITS = (MLA_COLS, MLA_COLS + RWKV_COLS, MLA_COLS + RWKV_COLS + GLA_COLS)
RWKV_SPLITS = (GROUP_W, 2 * GROUP_W, 3 * GROUP_W, 3 * GROUP_W + 2 * RWKV_W_LORA,
               3 * GROUP_W + 2 * RWKV_W_LORA + 2 * RWKV_A_LORA)
GLA_SPLITS = (GLA_KD, 2 * GLA_KD, 2 * GLA_KD + GROUP_W, 2 * GLA_KD + GROUP_W + 2 * GLA_GATE_RANK)

V7X_VMEM_LIMIT = 56 * 1024 * 1024
V7X_MXU = 256
N_LEVELS = 6
RWKV_PACK = V7X_MXU // RWKV_HEAD
MOE_TILE = 512


def _cparams(sem, vmem=V7X_VMEM_LIMIT):
    return pltpu.CompilerParams(dimension_semantics=sem, vmem_limit_bytes=vmem)


class _Tab:
    def __init__(self, arr, idx=()):
        self.arr, self.idx = arr, tuple(idx)

    @property
    def shape(self):
        return self.arr.shape[len(self.idx):]

    @property
    def spec(self):
        idx, rest = self.idx, self.shape
        return pl.BlockSpec((None,) * len(idx) + rest, lambda *_: idx + (0,) * len(rest))


def _tab(a):
    return a if isinstance(a, _Tab) else _Tab(a)


def _dot(a, b):
    return jnp.dot(a.astype(BF16), b.astype(BF16), preferred_element_type=F32)


def _dot_nt(a, b):
    return lax.dot_general(a.astype(BF16), b.astype(BF16), (((1,), (1,)), ((), ())), preferred_element_type=F32)


def _split2(x):
    hi = x.astype(BF16)
    lo = (x - hi.astype(F32)).astype(BF16)
    return hi, lo


def _dot_exact_lhs(m_bf16, x):
    hi, lo = _split2(x)
    return (jnp.dot(m_bf16, hi, preferred_element_type=F32) + jnp.dot(m_bf16, lo, preferred_element_type=F32))


def _mm_kernel(*refs, n_x, n_w, has_res, m_ctx, tm, cast_w):
    it = iter(refs)
    x_refs = [next(it) for _ in range(n_x)]
    w_refs = [next(it) for _ in range(n_w)]
    res_ref = next(it) if has_res else None
    gate_ref = next(it) if has_res else None
    o_ref = next(it)
    wb_refs = [next(it) for _ in range(n_w)] if cast_w else w_refs
    i = pl.program_id(1)

    if cast_w:
        @pl.when(i == 0)
        def _():
            for w_ref, wb_ref in zip(w_refs, wb_refs):
                wb_ref[...] = w_ref[...].astype(BF16)
    x = jnp.concatenate([x_ref[...].astype(BF16) for x_ref in x_refs], axis=1)

    acc = jnp.dot(x, wb_refs[0][...], preferred_element_type=F32)
    if n_w == 2:
        acc3 = jnp.dot(x, wb_refs[1][...], preferred_element_type=F32)
        acc = acc * jax.nn.sigmoid(acc) * acc3
    if has_res:
        rows = i * tm + lax.broadcasted_iota(jnp.int32, acc.shape, 0)
        g = jnp.where(rows < m_ctx, gate_ref[0:1, :], gate_ref[1:2, :])
        acc = res_ref[...] + g * acc
    o_ref[...] = acc.astype(o_ref.dtype)


def pmatmul(x, w, widx=(), *, w3=None, tm, tn, out_dtype=F32, res=None, gates=None, m_ctx=0, name="mm"):
    xs = list(x) if isinstance(x, (list, tuple)) else [x]
    m = xs[0].shape[0]
    k = sum(xi.shape[1] for xi in xs)
    n = w.shape[-1]
    tm = math.gcd(m, tm)
    assert w.shape[-2] == k and tm % 16 == 0
    nj, ni = pl.cdiv(n, tn), m // tm
    lead = (None,) * len(widx)
    w_spec = pl.BlockSpec(lead + (k, tn), lambda j, i: tuple(widx) + (0, j))
    ws = [w] if w3 is None else [w, w3]
    in_specs = [pl.BlockSpec((tm, xi.shape[1]), lambda j, i: (i, 0)) for xi in xs] + [w_spec] * len(ws)
    args = xs + ws
    if res is not None:
        gates = _tab(gates)
        gi = gates.idx
        in_specs += [pl.BlockSpec((tm, tn), lambda j, i: (i, j)),
                     pl.BlockSpec((None,) * len(gi) + (2, tn), lambda j, i: gi + (0, j))]
        args += [res, gates.arr]
    cast_w = w.dtype != BF16
    kern = functools.partial(_mm_kernel, n_x=len(xs), n_w=len(ws), has_res=res is not None, m_ctx=m_ctx, tm=tm,
                             cast_w=cast_w)
    return pl.pallas_call(
        kern,
        out_shape=jax.ShapeDtypeStruct((m, n), out_dtype),
        grid=(nj, ni),
        in_specs=in_specs,
        out_specs=pl.BlockSpec((tm, tn), lambda j, i: (i, j)),
        scratch_shapes=[pltpu.VMEM((k, tn), BF16) for _ in ws] if cast_w else [],
        compiler_params=_cparams(("arbitrary", "arbitrary")),
        name=name,
    )(*args)


def _gmm_kernel(meta_ref, x_ref, *refs, n_w, has_rowscale, n_tiles):
    it = iter(refs)
    w_refs = [next(it) for _ in range(n_w)]
    rs_ref = next(it) if has_rowscale else None
    o_ref = next(it)
    wb_refs = [next(it) for _ in range(n_w)]
    t = pl.program_id(1)
    e = meta_ref[t]
    e_prev = meta_ref[jnp.maximum(t - 1, 0)]

    @pl.when((t == 0) | (e != e_prev))
    def _():
        for w_ref, wb_ref in zip(w_refs, wb_refs):
            wb_ref[...] = w_ref[...].astype(BF16)

    @pl.when(t < meta_ref[n_tiles])
    def _():
        x = x_ref[...].astype(BF16)
        acc = jnp.dot(x, wb_refs[0][...], preferred_element_type=F32)
        if n_w == 2:
            acc3 = jnp.dot(x, wb_refs[1][...], preferred_element_type=F32)
            acc = acc * jax.nn.sigmoid(acc) * acc3
        if has_rowscale:
            acc = acc * rs_ref[...]
        o_ref[...] = acc.astype(o_ref.dtype)

    @pl.when(t >= meta_ref[n_tiles])
    def _():
        o_ref[...] = jnp.zeros(o_ref.shape, o_ref.dtype)


def gmatmul(meta, x, w, jl, *, w3=None, tn, out_dtype, rowscale=None, name):
    r, k = x.shape
    n = w.shape[-1]
    n_tiles = r // MOE_TILE
    ws = [w] if w3 is None else [w, w3]
    w_spec = pl.BlockSpec((None, None, k, tn), lambda j, t, mr: (jl, mr[t], 0, j))
    in_specs = [pl.BlockSpec((MOE_TILE, k), lambda j, t, mr: (t, 0))] + [w_spec] * len(ws)
    args = [x] + ws
    if rowscale is not None:
        in_specs.append(pl.BlockSpec((MOE_TILE, 1), lambda j, t, mr: (t, 0)))
        args.append(rowscale)
    return pl.pallas_call(
        functools.partial(_gmm_kernel, n_w=len(ws), has_rowscale=rowscale is not None, n_tiles=n_tiles),
        out_shape=jax.ShapeDtypeStruct((r, n), out_dtype),
        grid_spec=pltpu.PrefetchScalarGridSpec(
            num_scalar_prefetch=1,
            grid=(pl.cdiv(n, tn), n_tiles),
            in_specs=in_specs,
            out_specs=pl.BlockSpec((MOE_TILE, tn), lambda j, t, mr: (t, j)),
            scratch_shapes=[pltpu.VMEM((k, tn), BF16) for _ in ws]),
        compiler_params=_cparams(("arbitrary", "arbitrary")),
        name=name,
    )(meta, *args)


def _norm_kernel(x_ref, g_ref, sh_ref, sc_ref, *rest, m_ctx, tm, with_router, skip_blocks):
    i = pl.program_id(0) + skip_blocks
    x = x_ref[...]
    y = x * lax.rsqrt(jnp.mean(x * x, axis=-1, keepdims=True) + EPS) * g_ref[...]
    rows = i * tm + lax.broadcasted_iota(jnp.int32, x.shape, 0)
    is_ctx = rows < m_ctx
    sc = jnp.where(is_ctx, sc_ref[0:1, :], sc_ref[1:2, :])
    sh = jnp.where(is_ctx, sh_ref[0:1, :], sh_ref[1:2, :])
    h = y * (1.0 + sc) + sh
    if with_router:
        r_ref, o_ref, logit_ref = rest
        logit_ref[...] = jnp.dot(h, r_ref[...], preferred_element_type=F32, precision=lax.Precision.HIGHEST)
    else:
        (o_ref,) = rest
    o_ref[...] = h.astype(o_ref.dtype)


def norm_mod(x, g, shift2, scale2, *, m_ctx, out_dtype, router=None, skip_rows=0, tm=768, name="norm_mod"):
    m_in, d = x.shape
    m = m_in - skip_rows
    tm = math.gcd(math.gcd(m, tm), skip_rows) if skip_rows else math.gcd(m, tm)
    skip_blocks = skip_rows // tm
    row = pl.BlockSpec((tm, d), lambda i: (i, 0))
    tabs = [_tab(g), _tab(shift2), _tab(scale2)]
    in_specs = [pl.BlockSpec((tm, d), lambda i: (i + skip_blocks, 0))] + [t.spec for t in tabs]
    args = [x] + [t.arr for t in tabs]
    out_shape = jax.ShapeDtypeStruct((m, d), out_dtype)
    out_specs = row
    if router is not None:
        router = _tab(router)
        in_specs.append(router.spec)
        args.append(router.arr)
        out_shape = [out_shape, jax.ShapeDtypeStruct((m, router.shape[1]), F32)]
        out_specs = [row, pl.BlockSpec((tm, router.shape[1]), lambda i: (i, 0))]
    return pl.pallas_call(
        functools.partial(_norm_kernel, m_ctx=m_ctx, tm=tm, with_router=router is not None, skip_blocks=skip_blocks),
        out_shape=out_shape,
        grid=(m // tm,),
        in_specs=in_specs,
        out_specs=out_specs,
        compiler_params=_cparams(("parallel",)),
        name=name,
    )(*args)


def _attn_kernel(q_ref, k_ref, v_ref, o_ref, *, tk, n_kv, dv):
    q = q_ref[...]
    m = acc = None
    for j in range(n_kv):
        kb = k_ref[j * tk:(j + 1) * tk, :]
        vb = v_ref[j * tk:(j + 1) * tk, :]
        s = lax.dot_general(q, kb, (((1,), (1,)), ((), ())), preferred_element_type=F32)
        m_blk = jnp.max(s, axis=-1, keepdims=True)
        m_new = m_blk if j == 0 else jnp.maximum(m, m_blk)
        pv = jnp.dot(jnp.exp2(s - m_new).astype(BF16), vb, preferred_element_type=F32)
        acc = pv if j == 0 else jnp.exp2(m - m_new) * acc + pv
        m = m_new
    o_ref[...] = (acc[:, :dv] / acc[:, dv:dv + 1]).astype(o_ref.dtype)


def _attn_kv_tile(lk, cap=1408):
    return max(t for t in range(128, cap + 1, 128) if lk % t == 0)


def _attn_into_kernel(q_ref, k_ref, v_ref, prev_ref, o_ref, **kw):
    del prev_ref
    _attn_kernel(q_ref, k_ref, v_ref, o_ref, **kw)


def flash_attention(q, k, v_ext, *, dv, tq, lq=None, lk=None, into=None, name="mla_attn"):
    h, dqk = q.shape[0], q.shape[2]
    lq = q.shape[1] if lq is None else lq
    lk = k.shape[1] if lk is None else lk
    dve = v_ext.shape[2]
    tk = _attn_kv_tile(lk)
    tq = math.gcd(lq, tq)
    kw = dict(tk=tk, n_kv=lk // tk, dv=dv)
    in_specs = [pl.BlockSpec((None, tq, dqk), lambda hh, i: (hh, i, 0)),
                pl.BlockSpec((None, lk, dqk), lambda hh, i: (hh, 0, 0)),
                pl.BlockSpec((None, lk, dve), lambda hh, i: (hh, 0, 0))]
    args = [q, k, v_ext]
    if into is not None:
        in_specs.append(pl.BlockSpec(memory_space=pl.ANY))
        args.append(into)
    return pl.pallas_call(
        functools.partial(_attn_kernel if into is None else _attn_into_kernel, **kw),
        out_shape=jax.ShapeDtypeStruct((lq, h * dv) if into is None else into.shape, BF16),
        grid=(h, lq // tq),
        in_specs=in_specs,
        out_specs=pl.BlockSpec((tq, dv), lambda hh, i: (i, hh)),
        input_output_aliases={} if into is None else {3: 0},
        compiler_params=_cparams(("parallel", "parallel")),
        name=name,
    )(*args)


def _chunk_constants():
    c = CHUNK
    t = np.arange(c)
    tri = (t[None, :] <= t[:, None]).astype(np.float32)
    strict = (t[None, :] < t[:, None]).astype(np.float32)
    eye = np.eye(c, dtype=np.float32)
    seg, off = [], []
    for lv in range(N_LEVELS):
        s = c >> (lv + 1)
        blk = t // s
        same = blk[:, None] == blk[None, :]
        odd = (blk % 2 == 1)[:, None]
        seg.append(np.where(odd, same & (t[None, :] <= t[:, None]), same & (t[None, :] > t[:, None])).astype(np.float32))
        off.append((odd & (blk[None, :] == blk[:, None] - 1)).astype(np.float32))
    seg, off = np.stack(seg), np.stack(off)

    def both(a):
        return np.stack([a, a[..., ::-1, ::-1]])

    return {k: both(v) for k, v in dict(tri=tri, strict=strict, eye=eye, seg=seg, off=off).items()}


_CC = _chunk_constants()


def _chunk_pos(d, c, n_ctx_chunks, n_chunks):
    back = jnp.where(c < n_ctx_chunks, n_ctx_chunks - 1 - c, n_chunks + n_ctx_chunks - 1 - c)
    return jnp.where(d == 0, c, back)


def _log_sigmoid(x):
    return jnp.minimum(x, 0.0) - jnp.log1p(jnp.exp(-jnp.abs(x)))


def _gla_kernel(*refs, mode, dk, dv, pack, groups, n_in):
    ins = [refs[:n_in], refs[n_in:2 * n_in]]
    par_ref, mall_ref, off_ref, eye_ref, hmk_ref, hmv_ref, o0_ref, o1_ref, st_ref = refs[2 * n_in:]
    c = pl.program_id(0)

    @pl.when(c == 0)
    def _():
        st_ref[...] = jnp.zeros(st_ref.shape, F32)

    eye = eye_ref[...]
    hmk = hmk_ref[...]
    hmv = hmv_ref[...]
    wk, wv = pack * dk, pack * dv
    for d, o_ref in enumerate((o0_ref, o1_ref)):
        if mode == "hgrn":
            q_raw, f_raw, v = (r[...] for r in ins[d])
            par = par_ref[d]
            q = q_raw * jax.nn.sigmoid(q_raw)
            g = jnp.logaddexp(par[0:1], par[1:2] + _log_sigmoid(f_raw))
            k = par[2:3] * jax.nn.sigmoid(-f_raw)
        else:
            q_raw, k, v, a_dn = (r[...] for r in ins[d])
            par = par_ref[d]
            q = q_raw * dk ** -0.5
            g = _log_sigmoid(_dot(a_dn, par[0:128]) + par[128:129]) / GLA_GATE_NORM
        e_all = _dot_exact_lhs(mall_ref[d], g)
        bc = e_all[0:CHUNK]
        btot = jnp.sum(g, axis=0, keepdims=True)
        qhat = q * jnp.exp(bc)
        kt = k * jnp.exp(btot - bc)
        ebt = jnp.exp(btot)
        qw, kw = [q], [k]
        for lv in range(N_LEVELS):
            w = jnp.exp(e_all[(lv + 1) * CHUNK:(lv + 2) * CHUNK])
            qw.append(q * w)
            kw.append(k * w)
        for gi in range(groups):
            slk = slice(gi * wk, (gi + 1) * wk)
            slv = slice(gi * wv, (gi + 1) * wv)

            def stk(x):
                return jnp.concatenate([x[:, slk].astype(BF16)] * pack, axis=0) * hmk

            def tile(x):
                return jnp.concatenate([x[:, slk].astype(BF16)] * pack, axis=0)

            att = eye * _dot_nt(stk(qw[0]), tile(kw[0]))
            for lv in range(N_LEVELS):
                att = att + off_ref[d, lv] * _dot_nt(stk(qw[lv + 1]), tile(kw[lv + 1]))
            vs = jnp.concatenate([v[:, slv]] * pack, axis=0) * hmv
            st = st_ref[d, gi]
            o = _dot_nt(stk(qhat), st) + _dot(att, vs)
            o_ref[:, slv] = sum(o[h * CHUNK:(h + 1) * CHUNK] for h in range(pack))
            st_ref[d, gi] = st * ebt[:, slk] + _dot(vs.T, stk(kt))


def gla_scan(p, cols, par, *, mode, heads, dk, dv, pack, n_ctx, name):
    l = p.shape[0]
    groups = heads // pack
    nch, ncc = l // CHUNK, n_ctx // CHUNK
    nb = pack * CHUNK

    def col_spec(d, off, width):
        assert off % width == 0
        return pl.BlockSpec((CHUNK, width), lambda c: (_chunk_pos(d, c, ncc, nch), off // width))

    def const(a):
        return pl.BlockSpec(a.shape, lambda c: (0,) * a.ndim)

    bd = lambda a: np.kron(np.eye(pack, dtype=np.float32), a)
    mall = jnp.asarray(np.concatenate([_CC["tri"][:, None], _CC["seg"]], axis=1).reshape(2, -1, CHUNK), BF16)
    off = jnp.asarray(np.stack([np.stack([bd(_CC["off"][d, lv]) for lv in range(N_LEVELS)]) for d in range(2)]), F32)
    eye = jnp.asarray(np.eye(nb, dtype=np.float32))
    hmk = jnp.asarray(np.kron(np.eye(pack, dtype=np.float32), np.ones((CHUNK, dk), np.float32)), BF16)
    hmv = jnp.asarray(np.kron(np.eye(pack, dtype=np.float32), np.ones((CHUNK, dv), np.float32)))
    consts = [_tab(a) for a in (par, mall, off, eye, hmk, hmv)]
    in_specs = [col_spec(d, o, w) for d in range(2) for (o, w) in cols[d]]
    out_spec = lambda d: pl.BlockSpec((CHUNK, heads * dv), lambda c: (_chunk_pos(d, c, ncc, nch), 0))
    return pl.pallas_call(
        functools.partial(_gla_kernel, mode=mode, dk=dk, dv=dv, pack=pack, groups=groups, n_in=len(cols[0])),
        out_shape=[jax.ShapeDtypeStruct((l, heads * dv), F32)] * 2,
        grid=(nch,),
        in_specs=in_specs + [t.spec for t in consts],
        out_specs=[out_spec(0), out_spec(1)],
        scratch_shapes=[pltpu.VMEM((2, groups, pack * dv, pack * dk), F32)],
        compiler_params=_cparams(("arbitrary",)),
        name=name,
    )(*([p] * len(in_specs)), *[t.arr for t in consts])


def _post_kernel(of_ref, ob_ref, gate_ref, gn_ref, o_ref, *, heads, dv):
    o = of_ref[...] + ob_ref[...]
    gate = gate_ref[...]
    gn = gn_ref[...]
    for h in range(heads):
        sl = slice(h * dv, (h + 1) * dv)
        oh = o[:, sl]
        y = oh * lax.rsqrt(jnp.mean(oh * oh, axis=-1, keepdims=True) + EPS) * gn
        gh = gate[:, sl]
        o_ref[:, sl] = (y * gh * jax.nn.sigmoid(gh)).astype(o_ref.dtype)


def mix_post(o_f, o_b, p, gate_off, g_norm, *, heads, dv, tm=768, name):
    l, w = o_f.shape
    tm = math.gcd(l, tm)
    assert gate_off % w == 0
    row = pl.BlockSpec((tm, w), lambda i: (i, 0))
    g_norm = _tab(g_norm)
    return pl.pallas_call(
        functools.partial(_post_kernel, heads=heads, dv=dv),
        out_shape=jax.ShapeDtypeStruct((l, w), BF16),
        grid=(l // tm,),
        in_specs=[row, row, pl.BlockSpec((tm, w), lambda i: (i, gate_off // w)), g_norm.spec],
        out_specs=row,
        compiler_params=_cparams(("parallel",)),
        name=name,
    )(o_f, o_b, p, g_norm.arr)


def _seg_sum(x, seg_bf16):
    hi, lo = _split2(x)
    return jnp.dot(hi, seg_bf16, preferred_element_type=F32) + jnp.dot(lo, seg_bf16, preferred_element_type=F32)


def _shift_mix_block(x, halo_prev, halo_next, mu, seg_start, seg_end):
    row = lax.broadcasted_iota(jnp.int32, x.shape, 0)
    first = jnp.where(seg_start, 0.0, halo_prev[7:8, :])
    last = jnp.where(seg_end, 0.0, halo_next[0:1, :])
    xp = jnp.where(row == 0, first, pltpu.roll(x, 1, 0))
    xn = jnp.where(row == CHUNK - 1, last, pltpu.roll(x, CHUNK - 1, 0))
    return x + mu[0:1] * (xp - x) + mu[1:2] * (xn - x)


def _rwkv_kernel(*refs, groups, ncc, nch):
    ins = [refs[0:6], refs[6:12]]
    (mu_rkv_ref, mu_lo_ref, vec_ref, wup_ref, aup_ref, gup_ref, seg_ref,
     tri_ref, strict_ref, incl_ref, off_ref, eye_ref, hm_ref,
     o0_ref, o1_ref, bo0_ref, bo1_ref, g_ref, st_ref) = refs[12:]
    c = pl.program_id(0)

    @pl.when(c == 0)
    def _():
        st_ref[...] = jnp.zeros(st_ref.shape, F32)

    hm = hm_ref[...]
    eye = eye_ref[...]
    seg = seg_ref[...]
    w = RWKV_PACK * RWKV_HEAD
    nb = RWKV_PACK * CHUNK
    gw = GROUP_W

    def stack(x):
        return jnp.concatenate([x.astype(BF16)] * RWKV_PACK, axis=0) * hm

    streams = []
    for d, (o_ref, bo_ref) in enumerate(((o0_ref, bo0_ref), (o1_ref, bo1_ref))):
        rkv_ref, rkv_p, rkv_n, lo_ref, lo_p, lo_n = ins[d]
        pos = _chunk_pos(d, c, ncc, nch)
        seg_start = (pos == 0) | (pos == ncc)
        seg_end = (pos == ncc - 1) | (pos == nch - 1)
        rkv = _shift_mix_block(rkv_ref[...], rkv_p[...], rkv_n[...], mu_rkv_ref[...], seg_start, seg_end)
        lora = _shift_mix_block(lo_ref[...], lo_p[...], lo_n[...], mu_lo_ref[...], seg_start, seg_end)
        r, k, v = rkv[:, 0:gw], rkv[:, gw:2 * gw], rkv[:, 2 * gw:3 * gw]
        w_dn, a_dn, g_dn = lora[:, 0:128], lora[:, 128:256], lora[:, 256:384]
        vec = vec_ref[d]
        wl = vec[0:1] + _dot(jnp.tanh(w_dn), wup_ref[d])
        lw = -jnp.exp(-(jnp.maximum(-wl, 0.0) + jnp.log1p(jnp.exp(-jnp.abs(wl)))) - 0.5)
        a = jax.nn.sigmoid(vec[1:2] + _dot(a_dn, aup_ref[d]))
        kk = k * vec[2:3]
        kk = kk * lax.rsqrt(jnp.maximum(_seg_sum(kk * kk, seg), 1e-24))
        k = k * (1.0 + (a - 1.0) * vec[3:4])
        b = kk * a
        bo_ref[...] = _seg_sum(r * k * vec[4:5], seg) * v
        if d == 0:
            g_ref[...] = _dot(jax.nn.sigmoid(g_dn), gup_ref[...])
        bc = _dot_exact_lhs(tri_ref[d], lw)
        ebt = jnp.exp(jnp.sum(lw, axis=0, keepdims=True))
        einv = jnp.exp(-bc)
        khat = kk * jnp.exp(bc - lw)
        rhat = r * jnp.exp(bc)
        ks = k * einv
        bs = b * einv
        for g in range(groups):
            sl = slice(g * w, (g + 1) * w)
            streams.append(dict(
                d=d, g=g, sl=sl, o_ref=o_ref, ebt=ebt[:, sl],
                kr=jnp.concatenate([stack(khat[:, sl]), stack(rhat[:, sl])], axis=0),
                bk=jnp.concatenate([stack(bs[:, sl]), stack(ks[:, sl])], axis=0),
                vs=stack(v[:, sl])))

    for s in streams:
        aa = _dot_nt(s["kr"], s["bk"])
        strict, incl = strict_ref[s["d"]], incl_ref[s["d"]]
        s["akb"] = strict.astype(F32) * aa[:nb, :nb]
        s["akb_b"] = s["akb"].astype(BF16)
        s["arb"] = incl * aa[nb:, :nb].astype(BF16)
        s["ak_v"] = jnp.concatenate([strict * aa[:nb, nb:].astype(BF16), incl * aa[nb:, nb:].astype(BF16)], axis=0)
        s["minv"] = eye - off_ref[s["d"], N_LEVELS - 1].astype(F32) * s["akb"]
    for lv in range(N_LEVELS - 2, -1, -1):
        for s in streams:
            s["minv_b"] = s["minv"].astype(BF16)
            s["t1"] = _dot(off_ref[s["d"], lv] * s["akb_b"], s["minv_b"])
        for s in streams:
            s["minv"] = s["minv"] - _dot(s["minv_b"], s["t1"])
    for s in streams:
        st = st_ref[s["d"], s["g"]]
        from_state = _dot_nt(s["kr"], st)
        from_v = _dot(s["ak_v"], s["vs"])
        u = _dot(s["minv"], from_state[:nb] + from_v[:nb])
        o = from_state[nb:] + from_v[nb:] - _dot(s["arb"], u)
        s["o_ref"][:, s["sl"]] = sum(o[h * CHUNK:(h + 1) * CHUNK] for h in range(RWKV_PACK))
        upd = _dot(s["vs"].T, s["bk"][nb:]) - _dot(u.T, s["bk"][:nb])
        st_ref[s["d"], s["g"]] = (st + upd) * s["ebt"]


def rwkv_scan(p, rkv_off, lora_off, params, *, n_ctx, name="rwkv_scan"):
    l = p.shape[0]
    hw = GROUP_W
    w = RWKV_PACK * RWKV_HEAD
    groups = hw // w
    nch, ncc = l // CHUNK, n_ctx // CHUNK
    nb = RWKV_PACK * CHUNK
    halo = 8
    per_chunk = CHUNK // halo

    def shared(d):
        return pl.BlockSpec((CHUNK, hw), lambda c: (_chunk_pos(d, c, ncc, nch), 0))

    def piece(d, off, width):
        assert off % width == 0
        cb = off // width
        pos = lambda c: _chunk_pos(d, c, ncc, nch)
        return [pl.BlockSpec((CHUNK, width), lambda c: (pos(c), cb)),
                pl.BlockSpec((halo, width), lambda c: (jnp.maximum(pos(c) * per_chunk - 1, 0), cb)),
                pl.BlockSpec((halo, width), lambda c: (jnp.minimum((pos(c) + 1) * per_chunk, l // halo - 1), cb))]

    def const(a):
        return pl.BlockSpec(a.shape, lambda c: (0,) * a.ndim)

    bd = lambda a: np.kron(np.eye(RWKV_PACK, dtype=np.float32), a)
    strict = jnp.asarray(np.stack([bd(_CC["strict"][d]) for d in range(2)]), BF16)
    incl = jnp.asarray(np.stack([bd(_CC["strict"][d] + _CC["eye"][d]) for d in range(2)]), BF16)
    off = jnp.asarray(np.stack([np.stack([bd(_CC["off"][d, lv]) for lv in range(N_LEVELS)]) for d in range(2)]), BF16)
    eye = jnp.asarray(np.eye(nb, dtype=np.float32))
    hm = jnp.asarray(np.kron(np.eye(RWKV_PACK, dtype=np.float32), np.ones((CHUNK, RWKV_HEAD), np.float32)), BF16)
    tri = jnp.asarray(_CC["tri"], BF16)
    seg = jnp.asarray(np.kron(np.eye(RWKV_HEADS, dtype=np.float32), np.ones((RWKV_HEAD, RWKV_HEAD), np.float32)), BF16)
    consts = [_tab(a) for a in list(params) + [seg, tri, strict, incl, off, eye, hm]]
    in_specs = []
    for d in range(2):
        in_specs += piece(d, rkv_off, 3 * hw) + piece(d, lora_off, 3 * 128)
    return pl.pallas_call(
        functools.partial(_rwkv_kernel, groups=groups, ncc=ncc, nch=nch),
        out_shape=[jax.ShapeDtypeStruct((l, hw), F32)] * 5,
        grid=(nch,),
        in_specs=in_specs + [t.spec for t in consts],
        out_specs=[shared(0), shared(1), shared(0), shared(1), shared(0)],
        scratch_shapes=[pltpu.VMEM((2, groups, nb, nb), F32)],
        compiler_params=_cparams(("arbitrary",)),
        name=name,
    )(*([p] * len(in_specs)), *[t.arr for t in consts])


def _rwkv_post_kernel(of_ref, ob_ref, bf_ref, bb_ref, g_ref, ln_ref, seg_ref, o_ref):
    seg = seg_ref[...]
    o = of_ref[...] + ob_ref[...]
    mean = _seg_sum(o, seg) * (1.0 / RWKV_HEAD)
    oc = o - mean
    var = _seg_sum(oc * oc, seg) * (1.0 / RWKV_HEAD)
    y = oc * lax.rsqrt(var + RWKV_LN_EPS) * ln_ref[0:1] + ln_ref[1:2] + bf_ref[...] + bb_ref[...]
    o_ref[...] = (y * g_ref[...]).astype(o_ref.dtype)


def rwkv_post(o_f, o_b, bo_f, bo_b, g, ln, *, tm=768, name="rwkv_post"):
    l, w = o_f.shape
    tm = math.gcd(l, tm)
    row = pl.BlockSpec((tm, w), lambda i: (i, 0))
    ln = _tab(ln)
    seg = jnp.asarray(np.kron(np.eye(RWKV_HEADS, dtype=np.float32), np.ones((RWKV_HEAD, RWKV_HEAD), np.float32)), BF16)
    return pl.pallas_call(
        _rwkv_post_kernel,
        out_shape=jax.ShapeDtypeStruct((l, w), BF16),
        grid=(l // tm,),
        in_specs=[row] * 5 + [ln.spec, pl.BlockSpec((w, w), lambda i: (0, 0))],
        out_specs=row,
        compiler_params=_cparams(("parallel",)),
        name=name,
    )(o_f, o_b, bo_f, bo_b, g, ln.arr, seg)


@functools.lru_cache(maxsize=None)
def _rope_lane_tables(m, n):
    rows = n // GRID_W
    row = np.repeat(np.arange(rows, dtype=np.float32), GRID_W)
    col = np.tile(np.arange(GRID_W, dtype=np.float32), rows)
    n_freq = MLA_ROPE // 4
    freqs = (np.float32(ROPE_BASE) ** (-np.arange(n_freq, dtype=np.float32) / np.float32(n_freq))).astype(np.float32)
    ang = np.stack([row[:, None] * freqs, col[:, None] * freqs], axis=1).astype(np.float32)
    cos64 = np.repeat(np.cos(ang), 2, axis=1).reshape(n, MLA_ROPE)
    sin64 = (np.repeat(np.sin(ang), 2, axis=1) * np.array([-1.0, 1.0, -1.0, 1.0])[None, :, None]).reshape(n, MLA_ROPE)
    cos_t = np.tile(np.concatenate([np.ones((m, MLA_ROPE)), cos64], axis=0), (1, MLA_HEADS)).astype(np.float32)
    sin_t = np.tile(np.concatenate([np.zeros((m, MLA_ROPE)), sin64], axis=0), (1, MLA_HEADS)).astype(np.float32)
    return cos_t, sin_t


def _rope_lanes(x, cos_t, sin_t):
    n = x.shape[1]
    lane = lax.broadcasted_iota(jnp.int32, x.shape, 1)
    swapped = jnp.where(lane % 32 < 16, pltpu.roll(x, n - 16, 1), pltpu.roll(x, 16, 1))
    return x * cos_t + swapped * sin_t


def _mla_prep_kernel(cq_ref, ckv_ref, kr_ref, cos_ref, sin_ref, qn_ref, kvn_ref, wq_ref, wkv_ref,
                     q_ref, k_ref, v_ref):
    cq = cq_ref[...]
    ms = jnp.sum(cq * cq, axis=-1, keepdims=True) * (1.0 / MLA_Q_LORA)
    q = _dot(cq * lax.rsqrt(ms + EPS) * qn_ref[...], wq_ref[...])
    ckv = ckv_ref[...]
    kvn = ckv * lax.rsqrt(jnp.mean(ckv * ckv, axis=-1, keepdims=True) + EPS) * kvn_ref[...]
    kv = _dot(kvn, wkv_ref[...])
    cos_t, sin_t = cos_ref[...], sin_ref[...]
    hn = MLA_HEADS * MLA_NOPE
    scale = (MLA_NOPE + MLA_ROPE) ** -0.5 * math.log2(math.e)
    q_rope = _rope_lanes(q[:, hn:], cos_t, sin_t)
    k_rope = _rope_lanes(kr_ref[...], cos_t[:, :128], sin_t[:, :128])
    k_rope_hi = pltpu.roll(k_rope, MLA_ROPE, 1)
    lane = lax.broadcasted_iota(jnp.int32, k_rope.shape, 1)
    ones_col = jnp.where(lane == 0, 1.0, 0.0)
    for h in range(MLA_HEADS):
        pair = q_rope[:, (h // 2) * 128:(h // 2 + 1) * 128]
        q_ref[h] = (jnp.concatenate([q[:, h * MLA_NOPE:(h + 1) * MLA_NOPE], pair], axis=1) * scale).astype(q_ref.dtype)
        k_ref[h] = jnp.concatenate([kv[:, h * MLA_NOPE:(h + 1) * MLA_NOPE], k_rope if h % 2 == 0 else k_rope_hi],
                                   axis=1).astype(k_ref.dtype)
        v_ref[h] = jnp.concatenate([kv[:, hn + h * MLA_V:hn + (h + 1) * MLA_V], ones_col], axis=1).astype(v_ref.dtype)


def mla_tables(q_norm, w_q_up, kv_norm, w_kv_up):
    nl = w_q_up.shape[0]
    dq = MLA_NOPE + MLA_ROPE
    pad = PACKED["cq"][3] - MLA_Q_LORA
    wq = w_q_up.reshape(nl, MLA_Q_LORA, MLA_HEADS, dq)
    wq = jnp.concatenate([wq[..., :MLA_NOPE].reshape(nl, MLA_Q_LORA, -1), wq[..., MLA_NOPE:].reshape(nl, MLA_Q_LORA, -1)],
                         axis=2)
    wq = jnp.pad(wq, ((0, 0), (0, pad), (0, 0)))
    qn = jnp.pad(q_norm, ((0, 0), (0, pad)))[:, None, :]
    wkv = w_kv_up.reshape(nl, MLA_KV_LORA, MLA_HEADS, MLA_NOPE + MLA_V)
    wkv = jnp.concatenate([wkv[..., :MLA_NOPE].reshape(nl, MLA_KV_LORA, -1), wkv[..., MLA_NOPE:].reshape(nl, MLA_KV_LORA, -1)],
                          axis=2)
    return qn, kv_norm[:, None, :], wq, wkv


def mla_prep(p, m, tables, *, tm=256, name="mla_prep"):
    l = p.shape[0]
    cos_t, sin_t = (jnp.asarray(t) for t in _rope_lane_tables(m, l - m))
    tabs = [_tab(t) for t in tables]

    def col(nm):
        _, _, off, wp = PACKED[nm]
        return pl.BlockSpec((tm, wp), lambda i: (i, off // wp))

    row = lambda wd: pl.BlockSpec((tm, wd), lambda i: (i, 0))
    out_spec = pl.BlockSpec((MLA_HEADS, tm, 256), lambda i: (0, i, 0))
    return pl.pallas_call(
        _mla_prep_kernel,
        out_shape=[jax.ShapeDtypeStruct((MLA_HEADS, l, 256), BF16)] * 3,
        grid=(l // tm,),
        in_specs=[col("cq"), col("ckv"), col("k_rope"), row(256), row(256)] + [t.spec for t in tabs],
        out_specs=[out_spec] * 3,
        compiler_params=_cparams(("parallel",)),
        name=name,
    )(p, p, p, cos_t, sin_t, *[t.arr for t in tabs])


def mla_mixer(p, m, tables):
    qh, kh, vh = mla_prep(p, m, tables)
    o_all = flash_attention(qh, kh, vh, dv=MLA_V, tq=1408, name="mla_attn_lat")
    return flash_attention(qh, kh, vh, dv=MLA_V, tq=m, lq=m, lk=m, into=o_all, name="mla_attn_ctx")


def _pad_rows(t, before, total):
    cfg = [(0, 0)] * t.ndim
    cfg[-2] = (before, total - before - t.shape[-2])
    return jnp.pad(t, cfg)


def rwkv_tables(mu, w0, w_up, a0, a_up, k_k, k_a, u, g_up, ln_g, ln_b):
    zeros = jnp.zeros_like(w0)
    vec = jnp.stack([w0, a0, k_k, k_a, u, zeros, zeros, zeros], axis=2)
    wup = jnp.stack([_pad_rows(w_up[:, d], d * RWKV_W_LORA, 128) for d in range(2)], axis=1)
    aup = jnp.stack([_pad_rows(a_up[:, d], d * RWKV_A_LORA, 128) for d in range(2)], axis=1)
    ln = jnp.stack([ln_g, ln_b] + [jnp.zeros_like(ln_g)] * 6, axis=1)
    return (mu[:, :, :3 * GROUP_W], mu[:, :, 3 * GROUP_W:], vec, wup, aup, g_up), ln


def rwkv7_mixer(p, m, params, ln):
    outs = rwkv_scan(p, PACKED["rwkv_r"][2], PACKED["w_dn"][2], params, n_ctx=m)
    return rwkv_post(*outs, ln)


PACKED = {}


def _build_packed():
    orig = dict(cq=(0, MLA_Q_LORA), ckv=(MLA_Q_LORA, MLA_KV_LORA), k_rope=(MLA_Q_LORA + MLA_KV_LORA, MLA_ROPE))
    b = MLA_COLS
    for i, nm in enumerate(("rwkv_r", "rwkv_k", "rwkv_v")):
        orig[nm] = (b + i * GROUP_W, GROUP_W)
    b += 3 * GROUP_W
    orig.update(w_dn=(b, 2 * RWKV_W_LORA), a_dn=(b + 2 * RWKV_W_LORA, 2 * RWKV_A_LORA),
                g_dn=(b + 2 * RWKV_W_LORA + 2 * RWKV_A_LORA, RWKV_G_LORA))
    b = IN_SPLITS[1]
    orig.update(gla_q=(b, GLA_KD), gla_k=(b + GLA_KD, GLA_KD), gla_v=(b + 2 * GLA_KD, GROUP_W),
                gla_a=(b + GLA_SPLITS[2], 2 * GLA_GATE_RANK), gla_r=(b + GLA_SPLITS[3], GROUP_W))
    b = IN_SPLITS[2]
    for i, nm in enumerate(("hgrn_q", "hgrn_f0", "hgrn_f1", "hgrn_i", "hgrn_g")):
        orig[nm] = (b + i * GROUP_W, GROUP_W)
    order = [(nm, 512) for nm in ("rwkv_r", "rwkv_k", "rwkv_v", "gla_v", "gla_r", "hgrn_q", "hgrn_f0", "hgrn_f1",
                                  "hgrn_i", "hgrn_g", "cq")]
    orig["pad"] = (0, 0)
    order += [(nm, 256) for nm in ("gla_q", "gla_k")]
    order += [(nm, 128) for nm in ("w_dn", "a_dn", "g_dn", "gla_a", "k_rope", "pad")]
    order += [("ckv", 256)]
    off = 0
    for nm, wp in order:
        PACKED[nm] = (orig[nm][0], orig[nm][1], off, wp)
        off += wp
    return off


N_PACKED = _build_packed()


def pack_w_in(w):
    pieces = []
    for o, wd, _, wp in PACKED.values():
        pieces.append(w[..., o:o + wd])
        if wp > wd:
            pieces.append(jnp.zeros(w.shape[:-1] + (wp - wd,), w.dtype))
    return jnp.concatenate(pieces, axis=-1)


def _pcol(p, nm):
    _, wd, off, _ = PACKED[nm]
    return p[:, off:off + wd]


def _blk(nm):
    return (PACKED[nm][2], PACKED[nm][3])


def gla_tables(a_up, a_bias):
    return jnp.stack([jnp.concatenate([_pad_rows(a_up[:, d], d * GLA_GATE_RANK, 128),
                                       _pad_rows(a_bias[:, d][:, None, :], 0, 8)], axis=1) for d in range(2)], axis=1)


def hgrn_tables(lb_all):
    lb = jnp.swapaxes(lb_all, 0, 1)
    zeros = jnp.zeros_like(lb)
    return jnp.stack([jnp.log(lb), jnp.log1p(-lb), 1.0 - lb] + [zeros] * 5, axis=2)


def gla_mixer(p, m, par, g_norm):
    cols = [[_blk("gla_q"), _blk("gla_k"), _blk("gla_v"), _blk("gla_a")]] * 2
    o_f, o_b = gla_scan(p, cols, par, mode="gla", heads=GLA_HEADS, dk=GLA_DK, dv=GLA_DV, pack=4, n_ctx=m,
                        name="gla_scan")
    return mix_post(o_f, o_b, p, PACKED["gla_r"][2], g_norm, heads=GLA_HEADS, dv=GLA_DV, name="gla_post")


def hgrn2_mixer(p, m, par, g_norm):
    cols = [[_blk("hgrn_q"), _blk("hgrn_f%d" % d), _blk("hgrn_i")] for d in range(2)]
    o_f, o_b = gla_scan(p, cols, par, mode="hgrn", heads=HGRN_HEADS, dk=HGRN_EXPAND, dv=HGRN_DV, pack=2, n_ctx=m,
                        name="hgrn_scan")
    return mix_post(o_f, o_b, p, PACKED["hgrn_g"][2], g_norm, heads=HGRN_HEADS, dv=HGRN_DV, name="hgrn_post")


def dense_ffn(xs, h2, w1, w3, w2, j, gates2, m):
    act = pmatmul(h2, w1, (j,), w3=w3, tm=1408, tn=512, out_dtype=BF16, name="ffn_up")
    return pmatmul(act, w2, (j,), tm=704, tn=512, res=xs, gates=gates2, m_ctx=m, name="ffn_down")


def _route(top_idx, weights):
    n_pairs = top_idx.size
    r = n_pairs + N_EXPERTS * MOE_TILE
    n_tiles = r // MOE_TILE
    e_flat = top_idx.reshape(-1)
    onehot = (e_flat[:, None] == jnp.arange(N_EXPERTS)[None, :]).astype(jnp.int32)
    csum = jnp.cumsum(onehot, axis=0)
    counts = csum[-1]
    padded = (counts + MOE_TILE - 1) // MOE_TILE * MOE_TILE
    ends = jnp.cumsum(padded)
    dest = jnp.sum(onehot * (csum - 1 + (ends - padded)[None, :]), axis=1)
    put = dict(mode="promise_in_bounds", unique_indices=True)
    src_token = jnp.zeros((r,), jnp.int32).at[dest].set(jnp.arange(n_pairs, dtype=jnp.int32) // TOP_K, **put)
    row_gate = jnp.zeros((r,), F32).at[dest].set(weights.reshape(-1), **put)
    tile_start = jnp.arange(n_tiles, dtype=jnp.int32) * MOE_TILE
    tile_expert = jnp.minimum(jnp.sum(tile_start[:, None] >= ends[None, :], axis=1), N_EXPERTS - 1)
    meta = jnp.concatenate([tile_expert.astype(jnp.int32), (ends[-1:] // MOE_TILE).astype(jnp.int32)])
    return src_token, row_gate, dest.reshape(top_idx.shape), meta


def _combine_kernel(x_ref, y_ref, gate_ref, o_ref, *, m_ctx, tm, d):
    i = pl.program_id(0)
    rows = i * tm + lax.broadcasted_iota(jnp.int32, (tm, d), 0)
    g = jnp.where(rows < m_ctx, gate_ref[0:1, :], gate_ref[1:2, :])
    o_ref[...] = x_ref[...] + g * (y_ref[0].astype(F32) + y_ref[1].astype(F32))


def moe_combine(xs, y_tok, gates2, *, m_ctx, tm=768, name="moe_combine"):
    l, d = xs.shape
    tm = math.gcd(l, tm)
    gates2 = _tab(gates2)
    return pl.pallas_call(
        functools.partial(_combine_kernel, m_ctx=m_ctx, tm=tm, d=d),
        out_shape=jax.ShapeDtypeStruct((l, d), F32),
        grid=(l // tm,),
        in_specs=[pl.BlockSpec((tm, d), lambda i: (i, 0)), pl.BlockSpec((TOP_K, tm, d), lambda i: (0, i, 0)),
                  gates2.spec],
        out_specs=pl.BlockSpec((tm, d), lambda i: (i, 0)),
        compiler_params=_cparams(("parallel",)),
        name=name,
    )(xs, y_tok, gates2.arr)


def moe_ffn(xs, h2, logits, w1, w3, w2, j, gates2, m):
    l, d = h2.shape
    top_vals, top_idx = lax.top_k(logits[:, :N_EXPERTS], TOP_K)
    weights = jax.nn.softmax(top_vals, axis=-1)
    src_token, row_gate, dest, meta = _route(top_idx, weights)
    x_sorted = h2.at[src_token].get(mode="promise_in_bounds")
    act = gmatmul(meta, x_sorted, w1, j, w3=w3, tn=1024, out_dtype=BF16, rowscale=row_gate[:, None], name="moe_up")
    y = gmatmul(meta, act, w2, j, tn=512, out_dtype=BF16, name="moe_down")
    y_tok = y.at[dest.T.reshape(-1)].get(mode="promise_in_bounds").reshape(TOP_K, l, d)
    return moe_combine(xs, y_tok, gates2, m_ctx=m)


def kernel(x, c, ctx, c_ctx, norm1_g, norm2_g, w_mod, b_mod, w_in, w_out, mla_q_norm, mla_w_q_up, mla_kv_norm, mla_w_kv_up, rwkv_mu, rwkv_w0, rwkv_w_up, rwkv_a0, rwkv_a_up, rwkv_k_k, rwkv_k_a, rwkv_u, rwkv_g_up, rwkv_ln_g, rwkv_ln_b, gla_a_up, gla_a_bias, gla_norm, hgrn_lb, hgrn_norm, ffn_w1, ffn_w3, ffn_w2, moe_router, moe_w1, moe_w3, moe_w2, final_norm_g):
    m, n, d = ctx.shape[1], x.shape[1], x.shape[2]
    lb_all = jnp.cumsum(jax.nn.softmax(hgrn_lb.astype(F32), axis=1), axis=1)
    lb_all = lb_all - lb_all[:, :1]
    xs = jnp.concatenate([ctx[0], x[0]], axis=0)

    cvec = jnp.zeros((16, d), F32).at[0].set(jax.nn.silu(c[0])).at[1].set(jax.nn.silu(c_ctx))
    mod = jnp.stack([pmatmul(cvec, w_mod, (l,), tm=16, tn=2048, name="mod")[:2] for l in range(DEPTH)]) + b_mod[:, None, :]
    mods = jnp.swapaxes(mod[:, ::-1].reshape(DEPTH, 2, 6, d), 1, 2)
    zmods = jnp.zeros((2, d), F32)
    g1, g2 = norm1_g[:, None, :], norm2_g[:, None, :]
    w_in_p = pack_w_in(w_in)
    mla_tabs = mla_tables(mla_q_norm, mla_w_q_up, mla_kv_norm, mla_w_kv_up)
    rwkv_tabs, rwkv_ln = rwkv_tables(rwkv_mu, rwkv_w0, rwkv_w_up, rwkv_a0, rwkv_a_up, rwkv_k_k, rwkv_k_a, rwkv_u,
                                      rwkv_g_up, rwkv_ln_g, rwkv_ln_b)
    gla_par, hgrn_par = gla_tables(gla_a_up, gla_a_bias), hgrn_tables(lb_all)
    gla_g, hgrn_g = gla_norm[:, None, :], hgrn_norm[:, None, :]
    router_p = jnp.pad(moe_router, ((0, 0), (0, 0), (0, 128 - N_EXPERTS)))

    for l in range(DEPTH):
        sh1, sc1, gt1, sh2, sc2, gt2 = (_Tab(mods, (l, i)) for i in range(6))
        h = norm_mod(xs, _Tab(g1, (l,)), sh1, sc1, m_ctx=m, out_dtype=BF16)
        p = pmatmul(h, w_in_p, (l,), tm=1408, tn=1024, name="w_in")
        o = [
            mla_mixer(p, m, [_Tab(t, (l,)) for t in mla_tabs]),
            rwkv7_mixer(p, m, [_Tab(t, (l,)) for t in rwkv_tabs], _Tab(rwkv_ln, (l,))),
            gla_mixer(p, m, _Tab(gla_par, (l,)), _Tab(gla_g, (l,))),
            hgrn2_mixer(p, m, _Tab(hgrn_par, (l,)), _Tab(hgrn_g, (l,))),
        ]
        xs = pmatmul(o, w_out, (l,), tm=1408, tn=512, res=xs, gates=gt1, m_ctx=m, name="w_out")

        j = l // 2
        if l % 2 == 0:
            h2 = norm_mod(xs, _Tab(g2, (l,)), sh2, sc2, m_ctx=m, out_dtype=BF16)
            xs = dense_ffn(xs, h2, ffn_w1, ffn_w3, ffn_w2, j, gt2, m)
        else:
            h2, logits = norm_mod(xs, _Tab(g2, (l,)), sh2, sc2, m_ctx=m, out_dtype=BF16, router=_Tab(router_p, (j,)),
                                  name="norm_mod_router")
            xs = moe_ffn(xs, h2, logits, moe_w1, moe_w3, moe_w2, j, gt2, m)
    out = norm_mod(xs, final_norm_g[None, :], zmods, zmods, m_ctx=m, out_dtype=F32, skip_rows=m, name="final_norm")
    return out[None]
```

```python
import functools
import math

import numpy as np
import jax
import jax.numpy as jnp
from jax import lax
from jax.experimental import pallas as pl
from jax.experimental.pallas import tpu as pltpu

F32 = jnp.float32
BF16 = jnp.bfloat16

DEPTH = 4
GRID_W = 64
EPS = 1e-6
GROUP_W = 512
MLA_HEADS, MLA_NOPE, MLA_ROPE, MLA_V = 4, 128, 64, 128
MLA_Q_LORA, MLA_KV_LORA = 384, 256
ROPE_BASE = 10000.0
RWKV_HEADS, RWKV_HEAD = 8, 64
RWKV_W_LORA, RWKV_A_LORA, RWKV_G_LORA = 64, 64, 128
RWKV_LN_EPS = 64e-5
GLA_HEADS, GLA_DK, GLA_DV = 4, 64, 128
GLA_GATE_RANK, GLA_GATE_NORM = 16, 16.0
HGRN_HEADS, HGRN_EXPAND, HGRN_DV = 4, 128, 128
CHUNK = 64
N_EXPERTS, TOP_K = 8, 2
MLA_COLS = MLA_Q_LORA + MLA_KV_LORA + MLA_ROPE
RWKV_COLS = 3 * GROUP_W + 2 * RWKV_W_LORA + 2 * RWKV_A_LORA + RWKV_G_LORA
GLA_KD = GLA_HEADS * GLA_DK
GLA_COLS = 2 * GLA_KD + GROUP_W + 2 * GLA_GATE_RANK + GROUP_W
IN_SPLITS = (MLA_COLS, MLA_COLS + RWKV_COLS, MLA_COLS + RWKV_COLS + GLA_COLS)
RWKV_SPLITS = (GROUP_W, 2 * GROUP_W, 3 * GROUP_W, 3 * GROUP_W + 2 * RWKV_W_LORA,
               3 * GROUP_W + 2 * RWKV_W_LORA + 2 * RWKV_A_LORA)
GLA_SPLITS = (GLA_KD, 2 * GLA_KD, 2 * GLA_KD + GROUP_W, 2 * GLA_KD + GROUP_W + 2 * GLA_GATE_RANK)

V7X_VMEM_LIMIT = 56 * 1024 * 1024
V7X_MXU = 256
N_LEVELS = 6
RWKV_PACK = V7X_MXU // RWKV_HEAD
MOE_TILE = 512


def _cparams(sem, vmem=V7X_VMEM_LIMIT):
    return pltpu.CompilerParams(dimension_semantics=sem, vmem_limit_bytes=vmem)


class _Tab:
    def __init__(self, arr, idx=()):
        self.arr, self.idx = arr, tuple(idx)

    @property
    def shape(self):
        return self.arr.shape[len(self.idx):]

    @property
    def spec(self):
        idx, rest = self.idx, self.shape
        return pl.BlockSpec((None,) * len(idx) + rest, lambda *_: idx + (0,) * len(rest))


def _tab(a):
    return a if isinstance(a, _Tab) else _Tab(a)


def _dot(a, b):
    return jnp.dot(a.astype(BF16), b.astype(BF16), preferred_element_type=F32)


def _dot_nt(a, b):
    return lax.dot_general(a.astype(BF16), b.astype(BF16), (((1,), (1,)), ((), ())), preferred_element_type=F32)


def _split2(x):
    hi = x.astype(BF16)
    lo = (x - hi.astype(F32)).astype(BF16)
    return hi, lo


def _dot_exact_lhs(m_bf16, x):
    hi, lo = _split2(x)
    return (jnp.dot(m_bf16, hi, preferred_element_type=F32) + jnp.dot(m_bf16, lo, preferred_element_type=F32))


def _mm_kernel(*refs, n_x, n_w, has_res, m_ctx, tm, cast_w):
    it = iter(refs)
    x_refs = [next(it) for _ in range(n_x)]
    w_refs = [next(it) for _ in range(n_w)]
    res_ref = next(it) if has_res else None
    gate_ref = next(it) if has_res else None
    o_ref = next(it)
    wb_refs = [next(it) for _ in range(n_w)] if cast_w else w_refs
    i = pl.program_id(1)

    if cast_w:
        @pl.when(i == 0)
        def _():
            for w_ref, wb_ref in zip(w_refs, wb_refs):
                wb_ref[...] = w_ref[...].astype(BF16)
    x = jnp.concatenate([x_ref[...].astype(BF16) for x_ref in x_refs], axis=1)

    acc = jnp.dot(x, wb_refs[0][...], preferred_element_type=F32)
    if n_w == 2:
        acc3 = jnp.dot(x, wb_refs[1][...], preferred_element_type=F32)
        acc = acc * jax.nn.sigmoid(acc) * acc3
    if has_res:
        rows = i * tm + lax.broadcasted_iota(jnp.int32, acc.shape, 0)
        g = jnp.where(rows < m_ctx, gate_ref[0:1, :], gate_ref[1:2, :])
        acc = res_ref[...] + g * acc
    o_ref[...] = acc.astype(o_ref.dtype)


def pmatmul(x, w, widx=(), *, w3=None, tm, tn, out_dtype=F32, res=None, gates=None, m_ctx=0, name="mm"):
    xs = list(x) if isinstance(x, (list, tuple)) else [x]
    m = xs[0].shape[0]
    k = sum(xi.shape[1] for xi in xs)
    n = w.shape[-1]
    tm = math.gcd(m, tm)
    assert w.shape[-2] == k and tm % 16 == 0
    nj, ni = pl.cdiv(n, tn), m // tm
    lead = (None,) * len(widx)
    w_spec = pl.BlockSpec(lead + (k, tn), lambda j, i: tuple(widx) + (0, j))
    ws = [w] if w3 is None else [w, w3]
    in_specs = [pl.BlockSpec((tm, xi.shape[1]), lambda j, i: (i, 0)) for xi in xs] + [w_spec] * len(ws)
    args = xs + ws
    if res is not None:
        gates = _tab(gates)
        gi = gates.idx
        in_specs += [pl.BlockSpec((tm, tn), lambda j, i: (i, j)),
                     pl.BlockSpec((None,) * len(gi) + (2, tn), lambda j, i: gi + (0, j))]
        args += [res, gates.arr]
    cast_w = w.dtype != BF16
    kern = functools.partial(_mm_kernel, n_x=len(xs), n_w=len(ws), has_res=res is not None, m_ctx=m_ctx, tm=tm,
                             cast_w=cast_w)
    return pl.pallas_call(
        kern,
        out_shape=jax.ShapeDtypeStruct((m, n), out_dtype),
        grid=(nj, ni),
        in_specs=in_specs,
        out_specs=pl.BlockSpec((tm, tn), lambda j, i: (i, j)),
        scratch_shapes=[pltpu.VMEM((k, tn), BF16) for _ in ws] if cast_w else [],
        compiler_params=_cparams(("arbitrary", "arbitrary")),
        name=name,
    )(*args)


def _gmm_kernel(meta_ref, x_ref, *refs, n_w, has_rowscale, n_tiles):
    it = iter(refs)
    w_refs = [next(it) for _ in range(n_w)]
    rs_ref = next(it) if has_rowscale else None
    o_ref = next(it)
    wb_refs = [next(it) for _ in range(n_w)]
    t = pl.program_id(1)
    e = meta_ref[t]
    e_prev = meta_ref[jnp.maximum(t - 1, 0)]

    @pl.when((t == 0) | (e != e_prev))
    def _():
        for w_ref, wb_ref in zip(w_refs, wb_refs):
            wb_ref[...] = w_ref[...].astype(BF16)

    @pl.when(t < meta_ref[n_tiles])
    def _():
        x = x_ref[...].astype(BF16)
        acc = jnp.dot(x, wb_refs[0][...], preferred_element_type=F32)
        if n_w == 2:
            acc3 = jnp.dot(x, wb_refs[1][...], preferred_element_type=F32)
            acc = acc * jax.nn.sigmoid(acc) * acc3
        if has_rowscale:
            acc = acc * rs_ref[...]
        o_ref[...] = acc.astype(o_ref.dtype)

    @pl.when(t >= meta_ref[n_tiles])
    def _():
        o_ref[...] = jnp.zeros(o_ref.shape, o_ref.dtype)


def gmatmul(meta, x, w, jl, *, w3=None, tn, out_dtype, rowscale=None, name):
    r, k = x.shape
    n = w.shape[-1]
    n_tiles = r // MOE_TILE
    ws = [w] if w3 is None else [w, w3]
    w_spec = pl.BlockSpec((None, None, k, tn), lambda j, t, mr: (jl, mr[t], 0, j))
    in_specs = [pl.BlockSpec((MOE_TILE, k), lambda j, t, mr: (t, 0))] + [w_spec] * len(ws)
    args = [x] + ws
    if rowscale is not None:
        in_specs.append(pl.BlockSpec((MOE_TILE, 1), lambda j, t, mr: (t, 0)))
        args.append(rowscale)
    return pl.pallas_call(
        functools.partial(_gmm_kernel, n_w=len(ws), has_rowscale=rowscale is not None, n_tiles=n_tiles),
        out_shape=jax.ShapeDtypeStruct((r, n), out_dtype),
        grid_spec=pltpu.PrefetchScalarGridSpec(
            num_scalar_prefetch=1,
            grid=(pl.cdiv(n, tn), n_tiles),
            in_specs=in_specs,
            out_specs=pl.BlockSpec((MOE_TILE, tn), lambda j, t, mr: (t, j)),
            scratch_shapes=[pltpu.VMEM((k, tn), BF16) for _ in ws]),
        compiler_params=_cparams(("arbitrary", "arbitrary")),
        name=name,
    )(meta, *args)


def _norm_kernel(x_ref, g_ref, sh_ref, sc_ref, *rest, m_ctx, tm, with_router, skip_blocks):
    i = pl.program_id(0) + skip_blocks
    x = x_ref[...]
    y = x * lax.rsqrt(jnp.mean(x * x, axis=-1, keepdims=True) + EPS) * g_ref[...]
    rows = i * tm + lax.broadcasted_iota(jnp.int32, x.shape, 0)
    is_ctx = rows < m_ctx
    sc = jnp.where(is_ctx, sc_ref[0:1, :], sc_ref[1:2, :])
    sh = jnp.where(is_ctx, sh_ref[0:1, :], sh_ref[1:2, :])
    h = y * (1.0 + sc) + sh
    if with_router:
        r_ref, o_ref, logit_ref = rest
        logit_ref[...] = jnp.dot(h, r_ref[...], preferred_element_type=F32, precision=lax.Precision.HIGHEST)
    else:
        (o_ref,) = rest
    o_ref[...] = h.astype(o_ref.dtype)


def norm_mod(x, g, shift2, scale2, *, m_ctx, out_dtype, router=None, skip_rows=0, tm=768, name="norm_mod"):
    m_in, d = x.shape
    m = m_in - skip_rows
    tm = math.gcd(math.gcd(m, tm), skip_rows) if skip_rows else math.gcd(m, tm)
    skip_blocks = skip_rows // tm
    row = pl.BlockSpec((tm, d), lambda i: (i, 0))
    tabs = [_tab(g), _tab(shift2), _tab(scale2)]
    in_specs = [pl.BlockSpec((tm, d), lambda i: (i + skip_blocks, 0))] + [t.spec for t in tabs]
    args = [x] + [t.arr for t in tabs]
    out_shape = jax.ShapeDtypeStruct((m, d), out_dtype)
    out_specs = row
    if router is not None:
        router = _tab(router)
        in_specs.append(router.spec)
        args.append(router.arr)
        out_shape = [out_shape, jax.ShapeDtypeStruct((m, router.shape[1]), F32)]
        out_specs = [row, pl.BlockSpec((tm, router.shape[1]), lambda i: (i, 0))]
    return pl.pallas_call(
        functools.partial(_norm_kernel, m_ctx=m_ctx, tm=tm, with_router=router is not None, skip_blocks=skip_blocks),
        out_shape=out_shape,
        grid=(m // tm,),
        in_specs=in_specs,
        out_specs=out_specs,
        compiler_params=_cparams(("parallel",)),
        name=name,
    )(*args)


def _attn_kernel(q_ref, k_ref, v_ref, o_ref, *, tk, n_kv, dv):
    q = q_ref[...]
    m = acc = None
    for j in range(n_kv):
        kb = k_ref[j * tk:(j + 1) * tk, :]
        vb = v_ref[j * tk:(j + 1) * tk, :]
        s = lax.dot_general(q, kb, (((1,), (1,)), ((), ())), preferred_element_type=F32)
        m_blk = jnp.max(s, axis=-1, keepdims=True)
        m_new = m_blk if j == 0 else jnp.maximum(m, m_blk)
        pv = jnp.dot(jnp.exp2(s - m_new).astype(BF16), vb, preferred_element_type=F32)
        acc = pv if j == 0 else jnp.exp2(m - m_new) * acc + pv
        m = m_new
    o_ref[...] = (acc[:, :dv] / acc[:, dv:dv + 1]).astype(o_ref.dtype)


def _attn_kv_tile(lk, cap=1408):
    return max(t for t in range(128, cap + 1, 128) if lk % t == 0)


def _attn_into_kernel(q_ref, k_ref, v_ref, prev_ref, o_ref, **kw):
    del prev_ref
    _attn_kernel(q_ref, k_ref, v_ref, o_ref, **kw)


def flash_attention(q, k, v_ext, *, dv, tq, lq=None, lk=None, into=None, name="mla_attn"):
    h, dqk = q.shape[0], q.shape[2]
    lq = q.shape[1] if lq is None else lq
    lk = k.shape[1] if lk is None else lk
    dve = v_ext.shape[2]
    tk = _attn_kv_tile(lk)
    tq = math.gcd(lq, tq)
    kw = dict(tk=tk, n_kv=lk // tk, dv=dv)
    in_specs = [pl.BlockSpec((None, tq, dqk), lambda hh, i: (hh, i, 0)),
                pl.BlockSpec((None, lk, dqk), lambda hh, i: (hh, 0, 0)),
                pl.BlockSpec((None, lk, dve), lambda hh, i: (hh, 0, 0))]
    args = [q, k, v_ext]
    if into is not None:
        in_specs.append(pl.BlockSpec(memory_space=pl.ANY))
        args.append(into)
    return pl.pallas_call(
        functools.partial(_attn_kernel if into is None else _attn_into_kernel, **kw),
        out_shape=jax.ShapeDtypeStruct((lq, h * dv) if into is None else into.shape, BF16),
        grid=(h, lq // tq),
        in_specs=in_specs,
        out_specs=pl.BlockSpec((tq, dv), lambda hh, i: (i, hh)),
        input_output_aliases={} if into is None else {3: 0},
        compiler_params=_cparams(("parallel", "parallel")),
        name=name,
    )(*args)


def _chunk_constants():
    c = CHUNK
    t = np.arange(c)
    tri = (t[None, :] <= t[:, None]).astype(np.float32)
    strict = (t[None, :] < t[:, None]).astype(np.float32)
    eye = np.eye(c, dtype=np.float32)
    seg, off = [], []
    for lv in range(N_LEVELS):
        s = c >> (lv + 1)
        blk = t // s
        same = blk[:, None] == blk[None, :]
        odd = (blk % 2 == 1)[:, None]
        seg.append(np.where(odd, same & (t[None, :] <= t[:, None]), same & (t[None, :] > t[:, None])).astype(np.float32))
        off.append((odd & (blk[None, :] == blk[:, None] - 1)).astype(np.float32))
    seg, off = np.stack(seg), np.stack(off)

    def both(a):
        return np.stack([a, a[..., ::-1, ::-1]])

    return {k: both(v) for k, v in dict(tri=tri, strict=strict, eye=eye, seg=seg, off=off).items()}


_CC = _chunk_constants()


def _chunk_pos(d, c, n_ctx_chunks, n_chunks):
    back = jnp.where(c < n_ctx_chunks, n_ctx_chunks - 1 - c, n_chunks + n_ctx_chunks - 1 - c)
    return jnp.where(d == 0, c, back)


def _log_sigmoid(x):
    return jnp.minimum(x, 0.0) - jnp.log1p(jnp.exp(-jnp.abs(x)))


def _gla_kernel(*refs, mode, dk, dv, pack, groups, n_in):
    ins = [refs[:n_in], refs[n_in:2 * n_in]]
    par_ref, mall_ref, off_ref, eye_ref, hmk_ref, hmv_ref, o0_ref, o1_ref, st_ref = refs[2 * n_in:]
    c = pl.program_id(0)

    @pl.when(c == 0)
    def _():
        st_ref[...] = jnp.zeros(st_ref.shape, F32)

    eye = eye_ref[...]
    hmk = hmk_ref[...]
    hmv = hmv_ref[...]
    wk, wv = pack * dk, pack * dv
    for d, o_ref in enumerate((o0_ref, o1_ref)):
        if mode == "hgrn":
            q_raw, f_raw, v = (r[...] for r in ins[d])
            par = par_ref[d]
            q = q_raw * jax.nn.sigmoid(q_raw)
            g = jnp.logaddexp(par[0:1], par[1:2] + _log_sigmoid(f_raw))
            k = par[2:3] * jax.nn.sigmoid(-f_raw)
        else:
            q_raw, k, v, a_dn = (r[...] for r in ins[d])
            par = par_ref[d]
            q = q_raw * dk ** -0.5
            g = _log_sigmoid(_dot(a_dn, par[0:128]) + par[128:129]) / GLA_GATE_NORM
        e_all = _dot_exact_lhs(mall_ref[d], g)
        bc = e_all[0:CHUNK]
        btot = jnp.sum(g, axis=0, keepdims=True)
        qhat = q * jnp.exp(bc)
        kt = k * jnp.exp(btot - bc)
        ebt = jnp.exp(btot)
        qw, kw = [q], [k]
        for lv in range(N_LEVELS):
            w = jnp.exp(e_all[(lv + 1) * CHUNK:(lv + 2) * CHUNK])
            qw.append(q * w)
            kw.append(k * w)
        for gi in range(groups):
            slk = slice(gi * wk, (gi + 1) * wk)
            slv = slice(gi * wv, (gi + 1) * wv)

            def stk(x):
                return jnp.concatenate([x[:, slk].astype(BF16)] * pack, axis=0) * hmk

            def tile(x):
                return jnp.concatenate([x[:, slk].astype(BF16)] * pack, axis=0)

            att = eye * _dot_nt(stk(qw[0]), tile(kw[0]))
            for lv in range(N_LEVELS):
                att = att + off_ref[d, lv] * _dot_nt(stk(qw[lv + 1]), tile(kw[lv + 1]))
            vs = jnp.concatenate([v[:, slv]] * pack, axis=0) * hmv
            st = st_ref[d, gi]
            o = _dot_nt(stk(qhat), st) + _dot(att, vs)
            o_ref[:, slv] = sum(o[h * CHUNK:(h + 1) * CHUNK] for h in range(pack))
            st_ref[d, gi] = st * ebt[:, slk] + _dot(vs.T, stk(kt))


def gla_scan(p, cols, par, *, mode, heads, dk, dv, pack, n_ctx, name):
    l = p.shape[0]
    groups = heads // pack
    nch, ncc = l // CHUNK, n_ctx // CHUNK
    nb = pack * CHUNK

    def col_spec(d, off, width):
        assert off % width == 0
        return pl.BlockSpec((CHUNK, width), lambda c: (_chunk_pos(d, c, ncc, nch), off // width))

    def const(a):
        return pl.BlockSpec(a.shape, lambda c: (0,) * a.ndim)

    bd = lambda a: np.kron(np.eye(pack, dtype=np.float32), a)
    mall = jnp.asarray(np.concatenate([_CC["tri"][:, None], _CC["seg"]], axis=1).reshape(2, -1, CHUNK), BF16)
    off = jnp.asarray(np.stack([np.stack([bd(_CC["off"][d, lv]) for lv in range(N_LEVELS)]) for d in range(2)]), F32)
    eye = jnp.asarray(np.eye(nb, dtype=np.float32))
    hmk = jnp.asarray(np.kron(np.eye(pack, dtype=np.float32), np.ones((CHUNK, dk), np.float32)), BF16)
    hmv = jnp.asarray(np.kron(np.eye(pack, dtype=np.float32), np.ones((CHUNK, dv), np.float32)))
    consts = [_tab(a) for a in (par, mall, off, eye, hmk, hmv)]
    in_specs = [col_spec(d, o, w) for d in range(2) for (o, w) in cols[d]]
    out_spec = lambda d: pl.BlockSpec((CHUNK, heads * dv), lambda c: (_chunk_pos(d, c, ncc, nch), 0))
    return pl.pallas_call(
        functools.partial(_gla_kernel, mode=mode, dk=dk, dv=dv, pack=pack, groups=groups, n_in=len(cols[0])),
        out_shape=[jax.ShapeDtypeStruct((l, heads * dv), F32)] * 2,
        grid=(nch,),
        in_specs=in_specs + [t.spec for t in consts],
        out_specs=[out_spec(0), out_spec(1)],
        scratch_shapes=[pltpu.VMEM((2, groups, pack * dv, pack * dk), F32)],
        compiler_params=_cparams(("arbitrary",)),
        name=name,
    )(*([p] * len(in_specs)), *[t.arr for t in consts])


def _post_kernel(of_ref, ob_ref, gate_ref, gn_ref, o_ref, *, heads, dv):
    o = of_ref[...] + ob_ref[...]
    gate = gate_ref[...]
    gn = gn_ref[...]
    for h in range(heads):
        sl = slice(h * dv, (h + 1) * dv)
        oh = o[:, sl]
        y = oh * lax.rsqrt(jnp.mean(oh * oh, axis=-1, keepdims=True) + EPS) * gn
        gh = gate[:, sl]
        o_ref[:, sl] = (y * gh * jax.nn.sigmoid(gh)).astype(o_ref.dtype)


def mix_post(o_f, o_b, p, gate_off, g_norm, *, heads, dv, tm=768, name):
    l, w = o_f.shape
    tm = math.gcd(l, tm)
    assert gate_off % w == 0
    row = pl.BlockSpec((tm, w), lambda i: (i, 0))
    g_norm = _tab(g_norm)
    return pl.pallas_call(
        functools.partial(_post_kernel, heads=heads, dv=dv),
        out_shape=jax.ShapeDtypeStruct((l, w), BF16),
        grid=(l // tm,),
        in_specs=[row, row, pl.BlockSpec((tm, w), lambda i: (i, gate_off // w)), g_norm.spec],
        out_specs=row,
        compiler_params=_cparams(("parallel",)),
        name=name,
    )(o_f, o_b, p, g_norm.arr)


def _seg_sum(x, seg_bf16):
    hi, lo = _split2(x)
    return jnp.dot(hi, seg_bf16, preferred_element_type=F32) + jnp.dot(lo, seg_bf16, preferred_element_type=F32)


def _shift_mix_block(x, halo_prev, halo_next, mu, seg_start, seg_end):
    row = lax.broadcasted_iota(jnp.int32, x.shape, 0)
    first = jnp.where(seg_start, 0.0, halo_prev[7:8, :])
    last = jnp.where(seg_end, 0.0, halo_next[0:1, :])
    xp = jnp.where(row == 0, first, pltpu.roll(x, 1, 0))
    xn = jnp.where(row == CHUNK - 1, last, pltpu.roll(x, CHUNK - 1, 0))
    return x + mu[0:1] * (xp - x) + mu[1:2] * (xn - x)


def _rwkv_kernel(*refs, groups, ncc, nch):
    ins = [refs[0:6], refs[6:12]]
    (mu_rkv_ref, mu_lo_ref, vec_ref, wup_ref, aup_ref, gup_ref, seg_ref,
     tri_ref, strict_ref, incl_ref, off_ref, eye_ref, hm_ref,
     o0_ref, o1_ref, bo0_ref, bo1_ref, g_ref, st_ref) = refs[12:]
    c = pl.program_id(0)

    @pl.when(c == 0)
    def _():
        st_ref[...] = jnp.zeros(st_ref.shape, F32)

    hm = hm_ref[...]
    eye = eye_ref[...]
    seg = seg_ref[...]
    w = RWKV_PACK * RWKV_HEAD
    nb = RWKV_PACK * CHUNK
    gw = GROUP_W

    def stack(x):
        return jnp.concatenate([x.astype(BF16)] * RWKV_PACK, axis=0) * hm

    streams = []
    for d, (o_ref, bo_ref) in enumerate(((o0_ref, bo0_ref), (o1_ref, bo1_ref))):
        rkv_ref, rkv_p, rkv_n, lo_ref, lo_p, lo_n = ins[d]
        pos = _chunk_pos(d, c, ncc, nch)
        seg_start = (pos == 0) | (pos == ncc)
        seg_end = (pos == ncc - 1) | (pos == nch - 1)
        rkv = _shift_mix_block(rkv_ref[...], rkv_p[...], rkv_n[...], mu_rkv_ref[...], seg_start, seg_end)
        lora = _shift_mix_block(lo_ref[...], lo_p[...], lo_n[...], mu_lo_ref[...], seg_start, seg_end)
        r, k, v = rkv[:, 0:gw], rkv[:, gw:2 * gw], rkv[:, 2 * gw:3 * gw]
        w_dn, a_dn, g_dn = lora[:, 0:128], lora[:, 128:256], lora[:, 256:384]
        vec = vec_ref[d]
        wl = vec[0:1] + _dot(jnp.tanh(w_dn), wup_ref[d])
        lw = -jnp.exp(-(jnp.maximum(-wl, 0.0) + jnp.log1p(jnp.exp(-jnp.abs(wl)))) - 0.5)
        a = jax.nn.sigmoid(vec[1:2] + _dot(a_dn, aup_ref[d]))
        kk = k * vec[2:3]
        kk = kk * lax.rsqrt(jnp.maximum(_seg_sum(kk * kk, seg), 1e-24))
        k = k * (1.0 + (a - 1.0) * vec[3:4])
        b = kk * a
        bo_ref[...] = _seg_sum(r * k * vec[4:5], seg) * v
        if d == 0:
            g_ref[...] = _dot(jax.nn.sigmoid(g_dn), gup_ref[...])
        bc = _dot_exact_lhs(tri_ref[d], lw)
        ebt = jnp.exp(jnp.sum(lw, axis=0, keepdims=True))
        einv = jnp.exp(-bc)
        khat = kk * jnp.exp(bc - lw)
        rhat = r * jnp.exp(bc)
        ks = k * einv
        bs = b * einv
        for g in range(groups):
            sl = slice(g * w, (g + 1) * w)
            streams.append(dict(
                d=d, g=g, sl=sl, o_ref=o_ref, ebt=ebt[:, sl],
                kr=jnp.concatenate([stack(khat[:, sl]), stack(rhat[:, sl])], axis=0),
                bk=jnp.concatenate([stack(bs[:, sl]), stack(ks[:, sl])], axis=0),
                vs=stack(v[:, sl])))

    for s in streams:
        aa = _dot_nt(s["kr"], s["bk"])
        strict, incl = strict_ref[s["d"]], incl_ref[s["d"]]
        s["akb"] = strict.astype(F32) * aa[:nb, :nb]
        s["akb_b"] = s["akb"].astype(BF16)
        s["arb"] = incl * aa[nb:, :nb].astype(BF16)
        s["ak_v"] = jnp.concatenate([strict * aa[:nb, nb:].astype(BF16), incl * aa[nb:, nb:].astype(BF16)], axis=0)
        s["minv"] = eye - off_ref[s["d"], N_LEVELS - 1].astype(F32) * s["akb"]
    for lv in range(N_LEVELS - 2, -1, -1):
        for s in streams:
            s["minv_b"] = s["minv"].astype(BF16)
            s["t1"] = _dot(off_ref[s["d"], lv] * s["akb_b"], s["minv_b"])
        for s in streams:
            s["minv"] = s["minv"] - _dot(s["minv_b"], s["t1"])
    for s in streams:
        st = st_ref[s["d"], s["g"]]
        from_state = _dot_nt(s["kr"], st)
        from_v = _dot(s["ak_v"], s["vs"])
        u = _dot(s["minv"], from_state[:nb] + from_v[:nb])
        o = from_state[nb:] + from_v[nb:] - _dot(s["arb"], u)
        s["o_ref"][:, s["sl"]] = sum(o[h * CHUNK:(h + 1) * CHUNK] for h in range(RWKV_PACK))
        upd = _dot(s["vs"].T, s["bk"][nb:]) - _dot(u.T, s["bk"][:nb])
        st_ref[s["d"], s["g"]] = (st + upd) * s["ebt"]


def rwkv_scan(p, rkv_off, lora_off, params, *, n_ctx, name="rwkv_scan"):
    l = p.shape[0]
    hw = GROUP_W
    w = RWKV_PACK * RWKV_HEAD
    groups = hw // w
    nch, ncc = l // CHUNK, n_ctx // CHUNK
    nb = RWKV_PACK * CHUNK
    halo = 8
    per_chunk = CHUNK // halo

    def shared(d):
        return pl.BlockSpec((CHUNK, hw), lambda c: (_chunk_pos(d, c, ncc, nch), 0))

    def piece(d, off, width):
        assert off % width == 0
        cb = off // width
        pos = lambda c: _chunk_pos(d, c, ncc, nch)
        return [pl.BlockSpec((CHUNK, width), lambda c: (pos(c), cb)),
                pl.BlockSpec((halo, width), lambda c: (jnp.maximum(pos(c) * per_chunk - 1, 0), cb)),
                pl.BlockSpec((halo, width), lambda c: (jnp.minimum((pos(c) + 1) * per_chunk, l // halo - 1), cb))]

    def const(a):
        return pl.BlockSpec(a.shape, lambda c: (0,) * a.ndim)

    bd = lambda a: np.kron(np.eye(RWKV_PACK, dtype=np.float32), a)
    strict = jnp.asarray(np.stack([bd(_CC["strict"][d]) for d in range(2)]), BF16)
    incl = jnp.asarray(np.stack([bd(_CC["strict"][d] + _CC["eye"][d]) for d in range(2)]), BF16)
    off = jnp.asarray(np.stack([np.stack([bd(_CC["off"][d, lv]) for lv in range(N_LEVELS)]) for d in range(2)]), BF16)
    eye = jnp.asarray(np.eye(nb, dtype=np.float32))
    hm = jnp.asarray(np.kron(np.eye(RWKV_PACK, dtype=np.float32), np.ones((CHUNK, RWKV_HEAD), np.float32)), BF16)
    tri = jnp.asarray(_CC["tri"], BF16)
    seg = jnp.asarray(np.kron(np.eye(RWKV_HEADS, dtype=np.float32), np.ones((RWKV_HEAD, RWKV_HEAD), np.float32)), BF16)
    consts = [_tab(a) for a in list(params) + [seg, tri, strict, incl, off, eye, hm]]
    in_specs = []
    for d in range(2):
        in_specs += piece(d, rkv_off, 3 * hw) + piece(d, lora_off, 3 * 128)
    return pl.pallas_call(
        functools.partial(_rwkv_kernel, groups=groups, ncc=ncc, nch=nch),
        out_shape=[jax.ShapeDtypeStruct((l, hw), F32)] * 5,
        grid=(nch,),
        in_specs=in_specs + [t.spec for t in consts],
        out_specs=[shared(0), shared(1), shared(0), shared(1), shared(0)],
        scratch_shapes=[pltpu.VMEM((2, groups, nb, nb), F32)],
        compiler_params=_cparams(("arbitrary",)),
        name=name,
    )(*([p] * len(in_specs)), *[t.arr for t in consts])


def _rwkv_post_kernel(of_ref, ob_ref, bf_ref, bb_ref, g_ref, ln_ref, seg_ref, o_ref):
    seg = seg_ref[...]
    o = of_ref[...] + ob_ref[...]
    mean = _seg_sum(o, seg) * (1.0 / RWKV_HEAD)
    oc = o - mean
    var = _seg_sum(oc * oc, seg) * (1.0 / RWKV_HEAD)
    y = oc * lax.rsqrt(var + RWKV_LN_EPS) * ln_ref[0:1] + ln_ref[1:2] + bf_ref[...] + bb_ref[...]
    o_ref[...] = (y * g_ref[...]).astype(o_ref.dtype)


def rwkv_post(o_f, o_b, bo_f, bo_b, g, ln, *, tm=768, name="rwkv_post"):
    l, w = o_f.shape
    tm = math.gcd(l, tm)
    row = pl.BlockSpec((tm, w), lambda i: (i, 0))
    ln = _tab(ln)
    seg = jnp.asarray(np.kron(np.eye(RWKV_HEADS, dtype=np.float32), np.ones((RWKV_HEAD, RWKV_HEAD), np.float32)), BF16)
    return pl.pallas_call(
        _rwkv_post_kernel,
        out_shape=jax.ShapeDtypeStruct((l, w), BF16),
        grid=(l // tm,),
        in_specs=[row] * 5 + [ln.spec, pl.BlockSpec((w, w), lambda i: (0, 0))],
        out_specs=row,
        compiler_params=_cparams(("parallel",)),
        name=name,
    )(o_f, o_b, bo_f, bo_b, g, ln.arr, seg)


@functools.lru_cache(maxsize=None)
def _rope_lane_tables(m, n):
    rows = n // GRID_W
    row = np.repeat(np.arange(rows, dtype=np.float32), GRID_W)
    col = np.tile(np.arange(GRID_W, dtype=np.float32), rows)
    n_freq = MLA_ROPE // 4
    freqs = (np.float32(ROPE_BASE) ** (-np.arange(n_freq, dtype=np.float32) / np.float32(n_freq))).astype(np.float32)
    ang = np.stack([row[:, None] * freqs, col[:, None] * freqs], axis=1).astype(np.float32)
    cos64 = np.repeat(np.cos(ang), 2, axis=1).reshape(n, MLA_ROPE)
    sin64 = (np.repeat(np.sin(ang), 2, axis=1) * np.array([-1.0, 1.0, -1.0, 1.0])[None, :, None]).reshape(n, MLA_ROPE)
    cos_t = np.tile(np.concatenate([np.ones((m, MLA_ROPE)), cos64], axis=0), (1, MLA_HEADS)).astype(np.float32)
    sin_t = np.tile(np.concatenate([np.zeros((m, MLA_ROPE)), sin64], axis=0), (1, MLA_HEADS)).astype(np.float32)
    return cos_t, sin_t


def _rope_lanes(x, cos_t, sin_t):
    n = x.shape[1]
    lane = lax.broadcasted_iota(jnp.int32, x.shape, 1)
    swapped = jnp.where(lane % 32 < 16, pltpu.roll(x, n - 16, 1), pltpu.roll(x, 16, 1))
    return x * cos_t + swapped * sin_t


def _mla_prep_kernel(cq_ref, ckv_ref, kr_ref, cos_ref, sin_ref, qn_ref, kvn_ref, wq_ref, wkv_ref,
                     q_ref, k_ref, v_ref):
    cq = cq_ref[...]
    ms = jnp.sum(cq * cq, axis=-1, keepdims=True) * (1.0 / MLA_Q_LORA)
    q = _dot(cq * lax.rsqrt(ms + EPS) * qn_ref[...], wq_ref[...])
    ckv = ckv_ref[...]
    kvn = ckv * lax.rsqrt(jnp.mean(ckv * ckv, axis=-1, keepdims=True) + EPS) * kvn_ref[...]
    kv = _dot(kvn, wkv_ref[...])
    cos_t, sin_t = cos_ref[...], sin_ref[...]
    hn = MLA_HEADS * MLA_NOPE
    scale = (MLA_NOPE + MLA_ROPE) ** -0.5 * math.log2(math.e)
    q_rope = _rope_lanes(q[:, hn:], cos_t, sin_t)
    k_rope = _rope_lanes(kr_ref[...], cos_t[:, :128], sin_t[:, :128])
    k_rope_hi = pltpu.roll(k_rope, MLA_ROPE, 1)
    lane = lax.broadcasted_iota(jnp.int32, k_rope.shape, 1)
    ones_col = jnp.where(lane == 0, 1.0, 0.0)
    for h in range(MLA_HEADS):
        pair = q_rope[:, (h // 2) * 128:(h // 2 + 1) * 128]
        q_ref[h] = (jnp.concatenate([q[:, h * MLA_NOPE:(h + 1) * MLA_NOPE], pair], axis=1) * scale).astype(q_ref.dtype)
        k_ref[h] = jnp.concatenate([kv[:, h * MLA_NOPE:(h + 1) * MLA_NOPE], k_rope if h % 2 == 0 else k_rope_hi],
                                   axis=1).astype(k_ref.dtype)
        v_ref[h] = jnp.concatenate([kv[:, hn + h * MLA_V:hn + (h + 1) * MLA_V], ones_col], axis=1).astype(v_ref.dtype)


def mla_tables(q_norm, w_q_up, kv_norm, w_kv_up):
    nl = w_q_up.shape[0]
    dq = MLA_NOPE + MLA_ROPE
    pad = PACKED["cq"][3] - MLA_Q_LORA
    wq = w_q_up.reshape(nl, MLA_Q_LORA, MLA_HEADS, dq)
    wq = jnp.concatenate([wq[..., :MLA_NOPE].reshape(nl, MLA_Q_LORA, -1), wq[..., MLA_NOPE:].reshape(nl, MLA_Q_LORA, -1)],
                         axis=2)
    wq = jnp.pad(wq, ((0, 0), (0, pad), (0, 0)))
    qn = jnp.pad(q_norm, ((0, 0), (0, pad)))[:, None, :]
    wkv = w_kv_up.reshape(nl, MLA_KV_LORA, MLA_HEADS, MLA_NOPE + MLA_V)
    wkv = jnp.concatenate([wkv[..., :MLA_NOPE].reshape(nl, MLA_KV_LORA, -1), wkv[..., MLA_NOPE:].reshape(nl, MLA_KV_LORA, -1)],
                          axis=2)
    return qn, kv_norm[:, None, :], wq, wkv


def mla_prep(p, m, tables, *, tm=256, name="mla_prep"):
    l = p.shape[0]
    cos_t, sin_t = (jnp.asarray(t) for t in _rope_lane_tables(m, l - m))
    tabs = [_tab(t) for t in tables]

    def col(nm):
        _, _, off, wp = PACKED[nm]
        return pl.BlockSpec((tm, wp), lambda i: (i, off // wp))

    row = lambda wd: pl.BlockSpec((tm, wd), lambda i: (i, 0))
    out_spec = pl.BlockSpec((MLA_HEADS, tm, 256), lambda i: (0, i, 0))
    return pl.pallas_call(
        _mla_prep_kernel,
        out_shape=[jax.ShapeDtypeStruct((MLA_HEADS, l, 256), BF16)] * 3,
        grid=(l // tm,),
        in_specs=[col("cq"), col("ckv"), col("k_rope"), row(256), row(256)] + [t.spec for t in tabs],
        out_specs=[out_spec] * 3,
        compiler_params=_cparams(("parallel",)),
        name=name,
    )(p, p, p, cos_t, sin_t, *[t.arr for t in tabs])


def mla_mixer(p, m, tables):
    qh, kh, vh = mla_prep(p, m, tables)
    o_all = flash_attention(qh, kh, vh, dv=MLA_V, tq=1408, name="mla_attn_lat")
    return flash_attention(qh, kh, vh, dv=MLA_V, tq=m, lq=m, lk=m, into=o_all, name="mla_attn_ctx")


def _pad_rows(t, before, total):
    cfg = [(0, 0)] * t.ndim
    cfg[-2] = (before, total - before - t.shape[-2])
    return jnp.pad(t, cfg)


def rwkv_tables(mu, w0, w_up, a0, a_up, k_k, k_a, u, g_up, ln_g, ln_b):
    zeros = jnp.zeros_like(w0)
    vec = jnp.stack([w0, a0, k_k, k_a, u, zeros, zeros, zeros], axis=2)
    wup = jnp.stack([_pad_rows(w_up[:, d], d * RWKV_W_LORA, 128) for d in range(2)], axis=1)
    aup = jnp.stack([_pad_rows(a_up[:, d], d * RWKV_A_LORA, 128) for d in range(2)], axis=1)
    ln = jnp.stack([ln_g, ln_b] + [jnp.zeros_like(ln_g)] * 6, axis=1)
    return (mu[:, :, :3 * GROUP_W], mu[:, :, 3 * GROUP_W:], vec, wup, aup, g_up), ln


def rwkv7_mixer(p, m, params, ln):
    outs = rwkv_scan(p, PACKED["rwkv_r"][2], PACKED["w_dn"][2], params, n_ctx=m)
    return rwkv_post(*outs, ln)


PACKED = {}


def _build_packed():
    orig = dict(cq=(0, MLA_Q_LORA), ckv=(MLA_Q_LORA, MLA_KV_LORA), k_rope=(MLA_Q_LORA + MLA_KV_LORA, MLA_ROPE))
    b = MLA_COLS
    for i, nm in enumerate(("rwkv_r", "rwkv_k", "rwkv_v")):
        orig[nm] = (b + i * GROUP_W, GROUP_W)
    b += 3 * GROUP_W
    orig.update(w_dn=(b, 2 * RWKV_W_LORA), a_dn=(b + 2 * RWKV_W_LORA, 2 * RWKV_A_LORA),
                g_dn=(b + 2 * RWKV_W_LORA + 2 * RWKV_A_LORA, RWKV_G_LORA))
    b = IN_SPLITS[1]
    orig.update(gla_q=(b, GLA_KD), gla_k=(b + GLA_KD, GLA_KD), gla_v=(b + 2 * GLA_KD, GROUP_W),
                gla_a=(b + GLA_SPLITS[2], 2 * GLA_GATE_RANK), gla_r=(b + GLA_SPLITS[3], GROUP_W))
    b = IN_SPLITS[2]
    for i, nm in enumerate(("hgrn_q", "hgrn_f0", "hgrn_f1", "hgrn_i", "hgrn_g")):
        orig[nm] = (b + i * GROUP_W, GROUP_W)
    order = [(nm, 512) for nm in ("rwkv_r", "rwkv_k", "rwkv_v", "gla_v", "gla_r", "hgrn_q", "hgrn_f0", "hgrn_f1",
                                  "hgrn_i", "hgrn_g", "cq")]
    orig["pad"] = (0, 0)
    order += [(nm, 256) for nm in ("gla_q", "gla_k")]
    order += [(nm, 128) for nm in ("w_dn", "a_dn", "g_dn", "gla_a", "k_rope", "pad")]
    order += [("ckv", 256)]
    off = 0
    for nm, wp in order:
        PACKED[nm] = (orig[nm][0], orig[nm][1], off, wp)
        off += wp
    return off


N_PACKED = _build_packed()


def pack_w_in(w):
    pieces = []
    for o, wd, _, wp in PACKED.values():
        pieces.append(w[..., o:o + wd])
        if wp > wd:
            pieces.append(jnp.zeros(w.shape[:-1] + (wp - wd,), w.dtype))
    return jnp.concatenate(pieces, axis=-1)


def _pcol(p, nm):
    _, wd, off, _ = PACKED[nm]
    return p[:, off:off + wd]


def _blk(nm):
    return (PACKED[nm][2], PACKED[nm][3])


def gla_tables(a_up, a_bias):
    return jnp.stack([jnp.concatenate([_pad_rows(a_up[:, d], d * GLA_GATE_RANK, 128),
                                       _pad_rows(a_bias[:, d][:, None, :], 0, 8)], axis=1) for d in range(2)], axis=1)


def hgrn_tables(lb_all):
    lb = jnp.swapaxes(lb_all, 0, 1)
    zeros = jnp.zeros_like(lb)
    return jnp.stack([jnp.log(lb), jnp.log1p(-lb), 1.0 - lb] + [zeros] * 5, axis=2)


def gla_mixer(p, m, par, g_norm):
    cols = [[_blk("gla_q"), _blk("gla_k"), _blk("gla_v"), _blk("gla_a")]] * 2
    o_f, o_b = gla_scan(p, cols, par, mode="gla", heads=GLA_HEADS, dk=GLA_DK, dv=GLA_DV, pack=4, n_ctx=m,
                        name="gla_scan")
    return mix_post(o_f, o_b, p, PACKED["gla_r"][2], g_norm, heads=GLA_HEADS, dv=GLA_DV, name="gla_post")


def hgrn2_mixer(p, m, par, g_norm):
    cols = [[_blk("hgrn_q"), _blk("hgrn_f%d" % d), _blk("hgrn_i")] for d in range(2)]
    o_f, o_b = gla_scan(p, cols, par, mode="hgrn", heads=HGRN_HEADS, dk=HGRN_EXPAND, dv=HGRN_DV, pack=2, n_ctx=m,
                        name="hgrn_scan")
    return mix_post(o_f, o_b, p, PACKED["hgrn_g"][2], g_norm, heads=HGRN_HEADS, dv=HGRN_DV, name="hgrn_post")


def dense_ffn(xs, h2, w1, w3, w2, j, gates2, m):
    act = pmatmul(h2, w1, (j,), w3=w3, tm=1408, tn=512, out_dtype=BF16, name="ffn_up")
    return pmatmul(act, w2, (j,), tm=704, tn=512, res=xs, gates=gates2, m_ctx=m, name="ffn_down")


def _route(top_idx, weights):
    n_pairs = top_idx.size
    r = n_pairs + N_EXPERTS * MOE_TILE
    n_tiles = r // MOE_TILE
    e_flat = top_idx.reshape(-1)
    onehot = (e_flat[:, None] == jnp.arange(N_EXPERTS)[None, :]).astype(jnp.int32)
    csum = jnp.cumsum(onehot, axis=0)
    counts = csum[-1]
    padded = (counts + MOE_TILE - 1) // MOE_TILE * MOE_TILE
    ends = jnp.cumsum(padded)
    dest = jnp.sum(onehot * (csum - 1 + (ends - padded)[None, :]), axis=1)
    put = dict(mode="promise_in_bounds", unique_indices=True)
    src_token = jnp.zeros((r,), jnp.int32).at[dest].set(jnp.arange(n_pairs, dtype=jnp.int32) // TOP_K, **put)
    row_gate = jnp.zeros((r,), F32).at[dest].set(weights.reshape(-1), **put)
    tile_start = jnp.arange(n_tiles, dtype=jnp.int32) * MOE_TILE
    tile_expert = jnp.minimum(jnp.sum(tile_start[:, None] >= ends[None, :], axis=1), N_EXPERTS - 1)
    meta = jnp.concatenate([tile_expert.astype(jnp.int32), (ends[-1:] // MOE_TILE).astype(jnp.int32)])
    return src_token, row_gate, dest.reshape(top_idx.shape), meta


def _combine_kernel(x_ref, y_ref, gate_ref, *rest, m_ctx, tm, d, with_norm):
    i = pl.program_id(0)
    is_ctx = i * tm + lax.broadcasted_iota(jnp.int32, (tm, d), 0) < m_ctx
    g = jnp.where(is_ctx, gate_ref[0:1, :], gate_ref[1:2, :])
    x = x_ref[...] + g * (y_ref[0].astype(F32) + y_ref[1].astype(F32))
    if with_norm:
        g_ref, sh_ref, sc_ref, o_ref, h_ref = rest
        y = x * lax.rsqrt(jnp.mean(x * x, axis=-1, keepdims=True) + EPS) * g_ref[...]
        sc = jnp.where(is_ctx, sc_ref[0:1, :], sc_ref[1:2, :])
        sh = jnp.where(is_ctx, sh_ref[0:1, :], sh_ref[1:2, :])
        h_ref[...] = (y * (1.0 + sc) + sh).astype(h_ref.dtype)
    else:
        (o_ref,) = rest
    o_ref[...] = x


def moe_combine(xs, y_tok, gates2, *, m_ctx, next_norm=None, tm=384, name="moe_combine"):
    l, d = xs.shape
    tm = math.gcd(l, tm)
    row = pl.BlockSpec((tm, d), lambda i: (i, 0))
    tabs = [_tab(gates2)] + [_tab(t) for t in (next_norm or ())]
    out_shape = jax.ShapeDtypeStruct((l, d), F32)
    return pl.pallas_call(
        functools.partial(_combine_kernel, m_ctx=m_ctx, tm=tm, d=d, with_norm=next_norm is not None),
        out_shape=[out_shape, jax.ShapeDtypeStruct((l, d), BF16)] if next_norm else out_shape,
        grid=(l // tm,),
        in_specs=[row, pl.BlockSpec((TOP_K, tm, d), lambda i: (0, i, 0))] + [t.spec for t in tabs],
        out_specs=[row, row] if next_norm else row,
        compiler_params=_cparams(("parallel",)),
        name=name,
    )(xs, y_tok, *[t.arr for t in tabs])


def moe_ffn(xs, h2, logits, w1, w3, w2, j, gates2, m, next_norm=None):
    l, d = h2.shape
    top_vals, top_idx = lax.top_k(logits[:, :N_EXPERTS], TOP_K)
    weights = jax.nn.softmax(top_vals, axis=-1)
    src_token, row_gate, dest, meta = _route(top_idx, weights)
    x_sorted = h2.at[src_token].get(mode="promise_in_bounds")
    act = gmatmul(meta, x_sorted, w1, j, w3=w3, tn=1024, out_dtype=BF16, rowscale=row_gate[:, None], name="moe_up")
    y = gmatmul(meta, act, w2, j, tn=512, out_dtype=BF16, name="moe_down")
    y_tok = y.at[dest.T.reshape(-1)].get(mode="promise_in_bounds").reshape(TOP_K, l, d)
    return moe_combine(xs, y_tok, gates2, m_ctx=m, next_norm=next_norm)


def kernel(x, c, ctx, c_ctx, norm1_g, norm2_g, w_mod, b_mod, w_in, w_out, mla_q_norm, mla_w_q_up, mla_kv_norm, mla_w_kv_up, rwkv_mu, rwkv_w0, rwkv_w_up, rwkv_a0, rwkv_a_up, rwkv_k_k, rwkv_k_a, rwkv_u, rwkv_g_up, rwkv_ln_g, rwkv_ln_b, gla_a_up, gla_a_bias, gla_norm, hgrn_lb, hgrn_norm, ffn_w1, ffn_w3, ffn_w2, moe_router, moe_w1, moe_w3, moe_w2, final_norm_g):
    m, n, d = ctx.shape[1], x.shape[1], x.shape[2]
    lb_all = jnp.cumsum(jax.nn.softmax(hgrn_lb.astype(F32), axis=1), axis=1)
    lb_all = lb_all - lb_all[:, :1]
    xs = jnp.concatenate([ctx[0], x[0]], axis=0)

    cvec = jnp.zeros((16, d), F32).at[0].set(jax.nn.silu(c[0])).at[1].set(jax.nn.silu(c_ctx))
    mod = jnp.stack([pmatmul(cvec, w_mod, (l,), tm=16, tn=2048, name="mod")[:2] for l in range(DEPTH)]) + b_mod[:, None, :]
    mods = jnp.swapaxes(mod[:, ::-1].reshape(DEPTH, 2, 6, d), 1, 2)
    zmods = jnp.zeros((2, d), F32)
    g1, g2 = norm1_g[:, None, :], norm2_g[:, None, :]
    w_in_p = pack_w_in(w_in)
    mla_tabs = mla_tables(mla_q_norm, mla_w_q_up, mla_kv_norm, mla_w_kv_up)
    rwkv_tabs, rwkv_ln = rwkv_tables(rwkv_mu, rwkv_w0, rwkv_w_up, rwkv_a0, rwkv_a_up, rwkv_k_k, rwkv_k_a, rwkv_u,
                                      rwkv_g_up, rwkv_ln_g, rwkv_ln_b)
    gla_par, hgrn_par = gla_tables(gla_a_up, gla_a_bias), hgrn_tables(lb_all)
    gla_g, hgrn_g = gla_norm[:, None, :], hgrn_norm[:, None, :]
    router_p = jnp.pad(moe_router, ((0, 0), (0, 0), (0, 128 - N_EXPERTS)))

    h = None
    for l in range(DEPTH):
        sh1, sc1, gt1, sh2, sc2, gt2 = (_Tab(mods, (l, i)) for i in range(6))
        if h is None:
            h = norm_mod(xs, _Tab(g1, (l,)), sh1, sc1, m_ctx=m, out_dtype=BF16)
        p = pmatmul(h, w_in_p, (l,), tm=1408, tn=1024, name="w_in")
        o = [
            mla_mixer(p, m, [_Tab(t, (l,)) for t in mla_tabs]),
            rwkv7_mixer(p, m, [_Tab(t, (l,)) for t in rwkv_tabs], _Tab(rwkv_ln, (l,))),
            gla_mixer(p, m, _Tab(gla_par, (l,)), _Tab(gla_g, (l,))),
            hgrn2_mixer(p, m, _Tab(hgrn_par, (l,)), _Tab(hgrn_g, (l,))),
        ]
        xs = pmatmul(o, w_out, (l,), tm=1408, tn=512, res=xs, gates=gt1, m_ctx=m, name="w_out")

        j = l // 2
        h = None
        if l % 2 == 0:
            h2 = norm_mod(xs, _Tab(g2, (l,)), sh2, sc2, m_ctx=m, out_dtype=BF16)
            xs = dense_ffn(xs, h2, ffn_w1, ffn_w3, ffn_w2, j, gt2, m)
        else:
            h2, logits = norm_mod(xs, _Tab(g2, (l,)), sh2, sc2, m_ctx=m, out_dtype=BF16, router=_Tab(router_p, (j,)),
                                  name="norm_mod_router")
            nxt = (_Tab(g1, (l + 1,)), _Tab(mods, (l + 1, 0)), _Tab(mods, (l + 1, 1))) if l + 1 < DEPTH else None
            res = moe_ffn(xs, h2, logits, moe_w1, moe_w3, moe_w2, j, gt2, m, next_norm=nxt)
            xs, h = res if nxt else (res, None)
    out = norm_mod(xs, final_norm_g[None, :], zmods, zmods, m_ctx=m, out_dtype=F32, skip_rows=m, name="final_norm")
    return out[None]
```
